```python
import math
import jax, jax.numpy as jnp
from jax import lax
import numpy as np

D_MODEL = 1024
BATCH = 4
SEQ = 4096
DEPTH = 1
DEC_BATCH = 16
DEC_SEQ = 2048
PAST_LEN = 128

DIL_PAIRS = ((128, 1), (512, 4), (2048, 16))
N_DIL_GROUPS = 3
A_HEADS_PER_GROUP = 4
A_HEADS = N_DIL_GROUPS * A_HEADS_PER_GROUP
A_HEAD_DIM = 64
A_QKV = A_HEADS * A_HEAD_DIM
A_OUT = A_HEADS_PER_GROUP * A_HEAD_DIM
Q_BLOCK = 128
NEG_INF = -1e30
N_REL_BUCKETS = 32
REL_MAX_DISTANCE = 1024
GLA_HEADS = 4
GLA_K_WIDTH = D_MODEL // 2
GLA_V_WIDTH = D_MODEL
GLA_DK = GLA_K_WIDTH // GLA_HEADS
GLA_DV = GLA_V_WIDTH // GLA_HEADS
GATE_RANK = 16
GATE_TAU = 16.0
GLA_CHUNK = 64
N_GROUPS = 4
EXPERTS_PER_GROUP = 8
N_EXPERTS = N_GROUPS * EXPERTS_PER_GROUP
EXPERT_FF = 256
TOP_K_INNER = 2
EPS = 1e-6
SPLITS = (A_QKV, A_QKV, A_QKV, GLA_K_WIDTH, GLA_K_WIDTH, GLA_V_WIDTH, GLA_V_WIDTH, GATE_RANK, GATE_RANK, D_MODEL, D_MODEL)
IN_WIDTH = sum(SPLITS)

kernel_name = 'hybrid_dilated_gla_hmoe_encoder'


def rmsnorm(x, g):
    xf = x.astype(jnp.float32)
    r = xf * lax.rsqrt(jnp.mean(xf * xf, axis=-1, keepdims=True) + EPS)
    return (r * g.astype(jnp.float32)).astype(x.dtype)


def t5_buckets(rel):
    nb = N_REL_BUCKETS // 2
    max_exact = nb // 2
    ret = (rel > 0).astype(np.int64) * nb
    n = np.abs(rel)
    large = max_exact + (np.log(np.maximum(n, 1) / max_exact) / math.log(REL_MAX_DISTANCE / max_exact) * (nb - max_exact)).astype(np.int64)
    large = np.minimum(large, nb - 1)
    return (ret + np.where(n < max_exact, n, large)).astype(np.int32)


def dilated_group_attention(q, k, v, bias, rel):
    B, S, H, Dh = q.shape
    rel_j = jnp.asarray(rel, jnp.int32)
    nblk = S // Q_BLOCK
    q_blocks = q.reshape(B, nblk, Q_BLOCK, H, Dh).swapaxes(0, 1)
    scale = A_HEAD_DIM ** -0.5
    bias_f = bias.astype(jnp.float32)[None, :, None, :]

    def one_block(args):
        i, q_blk = args
        pos = i * Q_BLOCK + jnp.arange(Q_BLOCK, dtype=jnp.int32)
        kpos = pos[:, None] + rel_j[None, :]
        valid = (kpos >= 0) & (kpos < S)
        idx = jnp.clip(kpos, 0, S - 1)
        k_g = k[:, idx]
        v_g = v[:, idx]
        s = jnp.einsum('bqhd,bqjhd->bhqj', q_blk, k_g).astype(jnp.float32) * scale + bias_f
        s = jnp.where(valid[None, None], s, NEG_INF)
        lse = jax.nn.logsumexp(s, axis=-1)
        p = jnp.exp(s - lse[..., None])
        o = jnp.einsum('bhqj,bqjhd->bqhd', p.astype(v.dtype), v_g)
        return o, lse.swapaxes(1, 2)

    o, lse = lax.map(one_block, (jnp.arange(nblk, dtype=jnp.int32), q_blocks))
    o = o.swapaxes(0, 1).reshape(B, S, H, Dh)
    lse = lse.swapaxes(0, 1).reshape(B, S, H)
    return o, lse


def gla_chunked(q, k, v, logg, strict):
    B, S, H, dk = q.shape
    dv = v.shape[-1]
    n = S // GLA_CHUNK
    q, k, v, logg = [t.reshape(B, n, GLA_CHUNK, H, t.shape[-1]) for t in (q, k, v, logg)]
    b = jnp.cumsum(logg, axis=2)
    b_last = b[:, :, -1:]
    q_in = q * jnp.exp(b)
    k_in = k * jnp.exp(-b)
    k_out = k * jnp.exp(b_last - b)
    scores = jnp.einsum('bnihd,bnjhd->bnhij', q_in, k_in)
    mask = jnp.tril(jnp.ones((GLA_CHUNK, GLA_CHUNK), dtype=bool), -1 if strict else 0)
    scores = jnp.where(mask, scores, 0.0)
    o_intra = jnp.einsum('bnhij,bnjhv->bnihv', scores, v)

    def step(state, xs):
        q_c, k_c, v_c, decay_c = xs
        o_c = jnp.einsum('bihd,bhdv->bihv', q_c, state)
        state = decay_c[..., None] * state + jnp.einsum('bjhd,bjhv->bhdv', k_c, v_c)
        return state, o_c

    xs = (q_in.swapaxes(0, 1), k_out.swapaxes(0, 1), v.swapaxes(0, 1), jnp.exp(b_last[:, :, 0]).swapaxes(0, 1))
    state0 = jnp.zeros((B, H, dk, dv), jnp.float32)
    _, o_inter = lax.scan(step, state0, xs)
    return (o_intra + o_inter.swapaxes(0, 1)).reshape(B, S, H, dv)


def hier_moe(h, w_rg, b_rg, w_re, b_re, w_eg, w_eu, w_ed):
    T = h.shape[0]
    g_logits = (h @ w_rg + b_rg).astype(jnp.float32)
    pg = jax.nn.softmax(g_logits, axis=-1)
    pg_top, g_sel = lax.top_k(pg, 1)
    e_logits = (h @ w_re + b_re).astype(jnp.float32).reshape(T, N_GROUPS, EXPERTS_PER_GROUP)
    e_logits = jnp.take_along_axis(e_logits, g_sel[:, :, None], axis=1)[:, 0]
    pe = jax.nn.softmax(e_logits, axis=-1)
    pe_top, e_sel = lax.top_k(pe, TOP_K_INNER)
    pe_top = pe_top / jnp.sum(pe_top, axis=-1, keepdims=True)
    weights = pg_top * pe_top
    gidx = g_sel * EXPERTS_PER_GROUP + e_sel
    gate = jnp.sum(jax.nn.one_hot(gidx, N_EXPERTS, dtype=jnp.float32) * weights[..., None], axis=1).astype(h.dtype)
    out = jnp.zeros_like(h)
    for e in range(N_EXPERTS):
        a = jax.nn.silu(h @ w_eg[e]) * (h @ w_eu[e])
        out = out + gate[:, e:e + 1] * (a @ w_ed[e])
    return out


def encoder_layer(x, norm_mix, w_in, rel_bias, w_gate_f, b_gate_f, w_gate_b, b_gate_b, gla_norm,
                  w_proj_a, w_proj_b, w_out, norm_moe, w_rg, b_rg, w_re, b_re, w_eg, w_eu, w_ed):
    B, S, _ = x.shape
    dt = x.dtype
    h = rmsnorm(x, norm_mix)
    proj = h @ w_in
    split_points = np.cumsum(SPLITS)[:-1].tolist()
    qa, ka, va, qb, kb, vb, og, gf, gb, ma, mb = jnp.split(proj, split_points, axis=-1)

    qa = qa.reshape(B, S, A_HEADS, A_HEAD_DIM)
    ka = ka.reshape(B, S, A_HEADS, A_HEAD_DIM)
    va = va.reshape(B, S, A_HEADS, A_HEAD_DIM)
    outs, lses = [], []
    for g, (win, dil) in enumerate(DIL_PAIRS):
        half = win // (2 * dil)
        rel = dil * np.arange(-half, half + 1)
        bucket = jnp.asarray(t5_buckets(rel))
        hs = slice(g * A_HEADS_PER_GROUP, (g + 1) * A_HEADS_PER_GROUP)
        bias = rel_bias[bucket][:, hs].T
        o, l = dilated_group_attention(qa[:, :, hs], ka[:, :, hs], va[:, :, hs], bias, rel)
        outs.append(o)
        lses.append(l)
    w_grp = jax.nn.softmax(jnp.stack(lses), axis=0)
    oa = jnp.sum(w_grp[..., None].astype(dt) * jnp.stack(outs), axis=0)
    ya = oa.reshape(B, S, A_OUT) @ w_proj_a

    f32 = jnp.float32
    qb = qb.reshape(B, S, GLA_HEADS, GLA_DK).astype(f32) * (GLA_DK ** -0.5)
    kb = kb.reshape(B, S, GLA_HEADS, GLA_DK).astype(f32)
    vb = vb.reshape(B, S, GLA_HEADS, GLA_DV).astype(f32)
    logg_f = (jax.nn.log_sigmoid((gf @ w_gate_f + b_gate_f).astype(f32)) / GATE_TAU).reshape(B, S, GLA_HEADS, GLA_DK)
    logg_b = (jax.nn.log_sigmoid((gb @ w_gate_b + b_gate_b).astype(f32)) / GATE_TAU).reshape(B, S, GLA_HEADS, GLA_DK)
    o_f = gla_chunked(qb, kb, vb, logg_f, False)
    flip = lambda t: jnp.flip(t, axis=1)
    o_b = flip(gla_chunked(flip(qb), flip(kb), flip(vb), flip(logg_b), True))
    ob = rmsnorm(o_f + o_b, gla_norm).reshape(B, S, GLA_V_WIDTH).astype(dt) * jax.nn.silu(og)
    yb = ob @ w_proj_b

    mix = jax.nn.sigmoid(ma) * ya + jax.nn.sigmoid(mb) * yb
    x = x + mix @ w_out

    hm = rmsnorm(x, norm_moe).reshape(B * S, D_MODEL)
    x = x + hier_moe(hm, w_rg, b_rg, w_re, b_re, w_eg, w_eu, w_ed).reshape(B, S, D_MODEL)
    return x


def setup_inputs(seed: int = 0) -> dict:
    key = jax.random.key(seed)
    ks = jax.random.split(key, 24)
    nrm = lambda k, shape, s: jax.random.normal(k, shape, jnp.float32) * s
    L = DEPTH
    return {
        'x_prompt': nrm(ks[0], (BATCH, SEQ, D_MODEL), 1.0),
        'x_sample': nrm(ks[1], (DEC_BATCH, DEC_SEQ, D_MODEL), 1.0),
        'norm_mix': 1.0 + nrm(ks[2], (L, D_MODEL), 0.02),
        'w_in': nrm(ks[3], (L, D_MODEL, IN_WIDTH), D_MODEL ** -0.5),
        'rel_bias': nrm(ks[4], (N_REL_BUCKETS, A_HEADS), 0.5),
        'w_gate_f': nrm(ks[5], (L, GATE_RANK, GLA_K_WIDTH), GATE_RANK ** -0.5),
        'b_gate_f': nrm(ks[6], (L, GLA_K_WIDTH), 0.1),
        'w_gate_b': nrm(ks[7], (L, GATE_RANK, GLA_K_WIDTH), GATE_RANK ** -0.5),
        'b_gate_b': nrm(ks[8], (L, GLA_K_WIDTH), 0.1),
        'gla_norm': 1.0 + nrm(ks[9], (L, GLA_DV), 0.02),
        'w_proj_a': nrm(ks[10], (L, A_OUT, D_MODEL), A_OUT ** -0.5),
        'w_proj_b': nrm(ks[11], (L, GLA_V_WIDTH, D_MODEL), GLA_V_WIDTH ** -0.5),
        'w_out': nrm(ks[12], (L, D_MODEL, D_MODEL), D_MODEL ** -0.5),
        'norm_moe': 1.0 + nrm(ks[13], (L, D_MODEL), 0.02),
        'w_router_group': nrm(ks[14], (L, D_MODEL, N_GROUPS), D_MODEL ** -0.5),
        'b_router_group': nrm(ks[15], (L, N_GROUPS), 0.01),
        'w_router_expert': nrm(ks[16], (L, D_MODEL, N_EXPERTS), D_MODEL ** -0.5),
        'b_router_expert': nrm(ks[17], (L, N_EXPERTS), 0.01),
        'w_exp_gate': nrm(ks[18], (L, N_EXPERTS, D_MODEL, EXPERT_FF), D_MODEL ** -0.5),
        'w_exp_up': nrm(ks[19], (L, N_EXPERTS, D_MODEL, EXPERT_FF), D_MODEL ** -0.5),
        'w_exp_down': nrm(ks[20], (L, N_EXPERTS, EXPERT_FF, D_MODEL), EXPERT_FF ** -0.5),
        'norm_final': 1.0 + nrm(ks[21], (D_MODEL,), 0.02),
    }


def reference(x_prompt, x_sample, norm_mix, w_in, rel_bias, w_gate_f, b_gate_f, w_gate_b, b_gate_b,
              gla_norm, w_proj_a, w_proj_b, w_out, norm_moe, w_router_group, b_router_group,
              w_router_expert, b_router_expert, w_exp_gate, w_exp_up, w_exp_down, norm_final):
    def trunk(x):
        for l in range(DEPTH):
            x = encoder_layer(x, norm_mix[l], w_in[l], rel_bias, w_gate_f[l], b_gate_f[l], w_gate_b[l],
                              b_gate_b[l], gla_norm[l], w_proj_a[l], w_proj_b[l], w_out[l], norm_moe[l],
                              w_router_group[l], b_router_group[l], w_router_expert[l], b_router_expert[l],
                              w_exp_gate[l], w_exp_up[l], w_exp_down[l])
        return rmsnorm(x, norm_final)

    y_prompt = trunk(x_prompt)
    y_sample = trunk(x_sample)
    return (y_prompt, y_sample)
```

```python
import functools
import math

import numpy as np
import jax
import jax.numpy as jnp
from jax import lax
from jax.experimental import pallas as pl
from jax.experimental.pallas import tpu as pltpu

F32 = jnp.float32
BF16 = jnp.bfloat16
HIGHEST = lax.Precision.HIGHEST

D_MODEL = 1024
EPS = 1e-6
NEG_INF = -1e30

DIL_PAIRS = ((128, 1), (512, 4), (2048, 16))
A_HPG = 4
A_DH = 64
A_GW = A_HPG * A_DH
A_QKV = 3 * A_GW
HALF_WIN = 64
Q_BLK = 128
N_REL_BUCKETS = 32
REL_MAX_DISTANCE = 1024
GLA_H = 4
GLA_DK = 128
GLA_DV = 256
GATE_RANK = 16
GATE_TAU = 16.0
GLA_CHUNK = 64
N_GROUPS = 4
EPG = 8
N_EXPERTS = 32
EXPERT_FF = 256

C_MA, C_MB = 0, 1024
C_QA, C_KA, C_VA = 2048, 2816, 3584
C_QB, C_KB, C_VB, C_OG = 4352, 4864, 5376, 6400
C_G = 7424
PW = 7680

VMEM_LIMIT = 56 * 1024 * 1024


def _cparams(sem):
    return pltpu.CompilerParams(dimension_semantics=sem, vmem_limit_bytes=VMEM_LIMIT)


def _inproj_kernel(x_ref, g_ref, w_ref, o_ref, h_scr):
    @pl.when(pl.program_id(1) == 0)
    def _():
        x = x_ref[...]
        r = x * lax.rsqrt(jnp.mean(x * x, axis=-1, keepdims=True) + EPS)
        h_scr[...] = (r * g_ref[...]).astype(BF16)

    o_ref[...] = jnp.dot(h_scr[...], w_ref[...], preferred_element_type=F32).astype(BF16)


def _inproj(x2d, norm_g, w_all, tm=1024, tn=1536):
    T = x2d.shape[0]
    return pl.pallas_call(
        _inproj_kernel,
        grid=(T // tm, PW // tn),
        in_specs=[
            pl.BlockSpec((tm, D_MODEL), lambda i, j: (i, 0)),
            pl.BlockSpec((1, D_MODEL), lambda i, j: (0, 0)),
            pl.BlockSpec((D_MODEL, tn), lambda i, j: (0, j)),
        ],
        out_specs=pl.BlockSpec((tm, tn), lambda i, j: (i, j)),
        out_shape=jax.ShapeDtypeStruct((T, PW), BF16),
        scratch_shapes=[pltpu.VMEM((tm, D_MODEL), BF16)],
        compiler_params=_cparams(("parallel", "arbitrary")),
        name="inproj",
    )(x2d, norm_g, w_all)


def _attn_kernel(q_ref, k_ref, v_ref, bias_ref, o_ref, lse_ref, *, L, wk):
    i = pl.program_id(2)
    start = pl.multiple_of(jnp.clip(i * Q_BLK - HALF_WIN, 0, L - wk), HALF_WIN)
    q = q_ref[0]
    kw = k_ref[0, pl.ds(start, wk), :]
    vw = v_ref[0, pl.ds(start, wk), :]
    scale = A_DH ** -0.5
    lane = lax.broadcasted_iota(jnp.int32, (Q_BLK, 128), 1)
    outs = []
    lse_tile = jnp.zeros((Q_BLK, 128), F32)
    for h in range(A_HPG):
        sl = slice(h * A_DH, (h + 1) * A_DH)
        s = lax.dot_general(q[:, sl], kw[:, sl], (((1,), (1,)), ((), ())), preferred_element_type=F32)
        s = s * scale + bias_ref[0, h]
        m = jnp.max(s, axis=-1, keepdims=True)
        p = jnp.exp(s - m)
        l = jnp.sum(p, axis=-1, keepdims=True)
        o = jnp.dot(p.astype(BF16), vw[:, sl], preferred_element_type=F32) / l
        outs.append(o)
        lse = m + jnp.log(l)
        lse_tile = jnp.where((lane >= 32 * h) & (lane < 32 * (h + 1)), lse, lse_tile)
    o_ref[0] = jnp.concatenate(outs, axis=1).astype(BF16)
    lse_ref[0] = lse_tile


def _attn_group(proj, bias_tiles, g, d, B, S):
    L = S // d
    nblk = L // Q_BLK
    wk = min(2 * Q_BLK, L)
    pv = proj.reshape(B, L, d * PW)
    nc = PW // A_GW
    cq, ck, cv = C_QA // A_GW + g, C_KA // A_GW + g, C_VA // A_GW + g

    def variant(i):
        return jnp.where(i == 0, 0, jnp.where(i == nblk - 1, 2, 1))

    return pl.pallas_call(
        functools.partial(_attn_kernel, L=L, wk=wk),
        grid=(B, d, nblk),
        in_specs=[
            pl.BlockSpec((1, Q_BLK, A_GW), lambda b, r, i: (b, i, r * nc + cq)),
            pl.BlockSpec((1, L, A_GW), lambda b, r, i: (b, 0, r * nc + ck)),
            pl.BlockSpec((1, L, A_GW), lambda b, r, i: (b, 0, r * nc + cv)),
            pl.BlockSpec((1, A_HPG, Q_BLK, wk), lambda b, r, i: (variant(i), 0, 0, 0)),
        ],
        out_specs=[
            pl.BlockSpec((1, Q_BLK, A_GW), lambda b, r, i: (b, i, r)),
            pl.BlockSpec((1, Q_BLK, 128), lambda b, r, i: (b, i, r)),
        ],
        out_shape=[
            jax.ShapeDtypeStruct((B, L, d * A_GW), BF16),
            jax.ShapeDtypeStruct((B, L, d * 128), F32),
        ],
        compiler_params=_cparams(("parallel", "parallel", "arbitrary")),
        name=f"attn_d{d}",
    )(pv, pv, pv, bias_tiles)


def _t5_buckets(rel):
    nb = N_REL_BUCKETS // 2
    max_exact = nb // 2
    ret = (rel > 0).astype(np.int64) * nb
    n = np.abs(rel)
    large = max_exact + (np.log(np.maximum(n, 1) / max_exact) / math.log(REL_MAX_DISTANCE / max_exact)
                         * (nb - max_exact)).astype(np.int64)
    large = np.minimum(large, nb - 1)
    return (ret + np.where(n < max_exact, n, large)).astype(np.int32)


def _bias_tiles(rel_bias, g, d, wk):
    rel = d * np.arange(-HALF_WIN, HALF_WIN + 1)
    bucket = _t5_buckets(rel)
    bias = rel_bias[jnp.asarray(bucket)][:, g * A_HPG:(g + 1) * A_HPG].T.astype(F32)
    qi = np.arange(Q_BLK)[:, None]
    ci = np.arange(wk)[None, :]
    tiles = []
    for off in (0, HALF_WIN, wk - Q_BLK):
        j = ci - off - qi + HALF_WIN
        ok = (j >= 0) & (j <= 2 * HALF_WIN)
        jj = np.clip(j, 0, 2 * HALF_WIN)
        tiles.append(jnp.where(jnp.asarray(ok)[None], bias[:, jj], NEG_INF))
    return jnp.stack(tiles)


def _log_sigmoid(z):
    return jnp.minimum(z, 0.0) - jnp.log1p(jnp.exp(-jnp.abs(z)))


def _gla_kernel(q_ref, k_ref, v_ref, og_ref, g_ref, wgf_ref, wgb_ref, bgf_ref, bgb_ref, gn_ref, o_ref,
                qif, kif, kof, qib, kib, kob, vt, etf, etb, of_scr, *, S):
    PB = 256
    C = GLA_CHUNK
    rr = lax.broadcasted_iota(jnp.int32, (PB, PB), 0)
    cc = lax.broadcasted_iota(jnp.int32, (PB, PB), 1)
    same = (rr >> 6) == (cc >> 6)
    tl = jnp.where(same & (cc <= rr), 1.0, 0.0).astype(F32)
    ones_bd = jnp.where(same, 1.0, 0.0).astype(F32)
    qscale = GLA_DK ** -0.5
    inv_tau = 1.0 / GATE_TAU

    def prep(i, carry):
        rows = pl.ds(pl.multiple_of(i * PB, PB), PB)
        g = g_ref[0, rows, :]
        q = q_ref[0, rows, :].astype(F32) * qscale
        k = k_ref[0, rows, :].astype(F32)
        zf = jnp.dot(g, wgf_ref[...], preferred_element_type=F32) + bgf_ref[...]
        lf = _log_sigmoid(zf) * inv_tau
        bf = jnp.dot(tl, lf, precision=HIGHEST, preferred_element_type=F32)
        totf = jnp.dot(ones_bd, lf, precision=HIGHEST, preferred_element_type=F32)
        qif[rows, :] = (q * jnp.exp(bf)).astype(BF16)
        kif[rows, :] = (k * jnp.exp(-bf)).astype(BF16)
        kof[rows, :] = (k * jnp.exp(totf - bf)).astype(BF16)
        etf[rows, :] = jnp.exp(totf)
        zb = jnp.dot(g, wgb_ref[...], preferred_element_type=F32) + bgb_ref[...]
        lb = _log_sigmoid(zb) * inv_tau
        pb = jnp.dot(tl, lb, precision=HIGHEST, preferred_element_type=F32)
        totb = jnp.dot(ones_bd, lb, precision=HIGHEST, preferred_element_type=F32)
        bs = totb - pb + lb
        qib[rows, :] = (q * jnp.exp(bs)).astype(BF16)
        kib[rows, :] = (k * jnp.exp(-bs)).astype(BF16)
        kob[rows, :] = (k * jnp.exp(totb - bs)).astype(BF16)
        etb[rows, :] = jnp.exp(totb)
        vtb = v_ref[0, rows, :].astype(F32).T.astype(BF16)
        vt[2 * i] = vtb[:, :128]
        vt[2 * i + 1] = vtb[:, 128:]
        return carry

    lax.fori_loop(0, S // PB, prep, 0)

    r64 = lax.broadcasted_iota(jnp.int32, (C, C), 0)
    c64 = lax.broadcasted_iota(jnp.int32, (C, C), 1)
    mask_f = c64 <= r64
    mask_b = c64 > r64
    zeros_half = jnp.zeros((C, GLA_DK), BF16)
    nt = (((1,), (1,)), ((), ()))

    def chunk_step(qi_ref, ki_ref, ko_ref, et_ref, mask, p, half, state):
        r0 = pl.multiple_of(p * (2 * C) + half * C, C)
        rows = pl.ds(r0, C)
        qi = qi_ref[rows, :]
        ki = ki_ref[rows, :]
        ko = ko_ref[rows, :]
        v = v_ref[0, rows, :]
        sc = lax.dot_general(qi, ki, nt, preferred_element_type=F32)
        sc = jnp.where(mask, sc, 0.0).astype(BF16)
        o = jnp.dot(sc, v, preferred_element_type=F32)
        o = o + lax.dot_general(qi, state.astype(BF16), nt, preferred_element_type=F32)
        decay = et_ref[pl.ds(r0, 1), :]
        ko_pad = jnp.concatenate([ko, zeros_half] if half == 0 else [zeros_half, ko], axis=0)
        state = state * decay + jnp.dot(vt[p], ko_pad, preferred_element_type=F32)
        return rows, o, state

    def fwd(p, state):
        for half in (0, 1):
            rows, o, state = chunk_step(qif, kif, kof, etf, mask_f, p, half, state)
            of_scr[rows, :] = o
        return state

    lax.fori_loop(0, S // (2 * C), fwd, jnp.zeros((GLA_DV, GLA_DK), F32))

    gn = gn_ref[...]

    def bwd(t, state):
        p = S // (2 * C) - 1 - t
        for half in (1, 0):
            rows, o, state = chunk_step(qib, kib, kob, etb, mask_b, p, half, state)
            tot = of_scr[rows, :] + o
            nrm = tot * lax.rsqrt(jnp.mean(tot * tot, axis=-1, keepdims=True) + EPS) * gn
            og = og_ref[0, rows, :].astype(F32)
            o_ref[0, rows, :] = (nrm * (og * jax.nn.sigmoid(og))).astype(BF16)
        return state

    lax.fori_loop(0, S // (2 * C), bwd, jnp.zeros((GLA_DV, GLA_DK), F32))


def _gla(proj, wgf, wgb, bgf, bgb, gla_norm, B, S):
    pv = proj.reshape(B, S, PW)
    cq, ck, cv, cog, cg = C_QB // GLA_DK, C_KB // GLA_DK, C_VB // GLA_DV, C_OG // GLA_DV, C_G // 128
    return pl.pallas_call(
        functools.partial(_gla_kernel, S=S),
        grid=(B, GLA_H),
        in_specs=[
            pl.BlockSpec((1, S, GLA_DK), lambda b, h: (b, 0, cq + h)),
            pl.BlockSpec((1, S, GLA_DK), lambda b, h: (b, 0, ck + h)),
            pl.BlockSpec((1, S, GLA_DV), lambda b, h: (b, 0, cv + h)),
            pl.BlockSpec((1, S, GLA_DV), lambda b, h: (b, 0, cog + h)),
            pl.BlockSpec((1, S, 128), lambda b, h: (b, 0, cg)),
            pl.BlockSpec((128, GLA_DK), lambda b, h: (0, h)),
            pl.BlockSpec((128, GLA_DK), lambda b, h: (0, h)),
            pl.BlockSpec((1, GLA_DK), lambda b, h: (0, h)),
            pl.BlockSpec((1, GLA_DK), lambda b, h: (0, h)),
            pl.BlockSpec((1, GLA_DV), lambda b, h: (0, 0)),
        ],
        out_specs=pl.BlockSpec((1, S, GLA_DV), lambda b, h: (b, 0, h)),
        out_shape=jax.ShapeDtypeStruct((B, S, GLA_H * GLA_DV), BF16),
        scratch_shapes=[
            pltpu.VMEM((S, GLA_DK), BF16), pltpu.VMEM((S, GLA_DK), BF16), pltpu.VMEM((S, GLA_DK), BF16),
            pltpu.VMEM((S, GLA_DK), BF16), pltpu.VMEM((S, GLA_DK), BF16), pltpu.VMEM((S, GLA_DK), BF16),
            pltpu.VMEM((S // 128, GLA_DV, 128), BF16),
            pltpu.VMEM((S, GLA_DK), F32), pltpu.VMEM((S, GLA_DK), F32),
            pltpu.VMEM((S, GLA_DV), F32),
        ],
        compiler_params=_cparams(("parallel", "arbitrary")),
        name="gla",
    )(pv, pv, pv, pv, pv, wgf, wgb, bgf, bgb, gla_norm)


def _outproj_kernel(x_ref, ma_ref, mb_ref, o0_ref, o1_ref, o2_ref, l0_ref, l1_ref, l2_ref, ob_ref,
                    wpa_ref, wpb_ref, wo_ref, nm_ref, wr_ref, br_ref, exp_ref,
                    x1_ref, hm_ref, gate_ref):
    l0, l1, l2 = l0_ref[...], l1_ref[...], l2_ref[...]
    m = jnp.maximum(jnp.maximum(l0, l1), l2)
    e0, e1, e2 = jnp.exp(l0 - m), jnp.exp(l1 - m), jnp.exp(l2 - m)
    den = e0 + e1 + e2
    ex = exp_ref[...]
    oa = jnp.zeros(o0_ref.shape, F32)
    for e, o_ref in ((e0, o0_ref), (e1, o1_ref), (e2, o2_ref)):
        w = jnp.dot(e / den, ex, precision=HIGHEST, preferred_element_type=F32)
        oa = oa + w * o_ref[...].astype(F32)
    ya = jnp.dot(oa.astype(BF16), wpa_ref[...], preferred_element_type=F32)
    yb = jnp.dot(ob_ref[...], wpb_ref[...], preferred_element_type=F32)
    mix = jax.nn.sigmoid(ma_ref[...].astype(F32)) * ya + jax.nn.sigmoid(mb_ref[...].astype(F32)) * yb
    x1 = x_ref[...] + jnp.dot(mix.astype(BF16), wo_ref[...], preferred_element_type=F32)
    x1_ref[...] = x1
    hm = x1 * lax.rsqrt(jnp.mean(x1 * x1, axis=-1, keepdims=True) + EPS) * nm_ref[...]
    hm_ref[...] = hm.astype(BF16)

    logits = jnp.dot(hm, wr_ref[...], precision=HIGHEST, preferred_element_type=F32) + br_ref[...]
    tm = logits.shape[0]
    lane = lax.broadcasted_iota(jnp.int32, (tm, 128), 1)
    is_g = (lane >= N_EXPERTS) & (lane < 2 * N_EXPERTS)
    is_e = lane < N_EXPERTS
    grp_of_lane = jnp.where(is_g, (lane - N_EXPERTS) >> 3, lane >> 3)
    gl = jnp.where(is_g, logits, NEG_INF)
    gmax = jnp.max(gl, axis=-1, keepdims=True)
    gsum = jnp.sum(jnp.where(is_g, jnp.exp(gl - gmax), 0.0), axis=-1, keepdims=True) * (1.0 / EPG)
    pg_top = 1.0 / gsum
    g_sel = jnp.min(jnp.where(is_g & (gl == gmax), grp_of_lane, N_GROUPS), axis=-1, keepdims=True)
    in_grp = is_e & (grp_of_lane == g_sel)
    el = jnp.where(in_grp, logits, NEG_INF)
    emax = jnp.max(el, axis=-1, keepdims=True)
    ee = jnp.where(in_grp, jnp.exp(el - emax), 0.0)
    pe = ee / jnp.sum(ee, axis=-1, keepdims=True)
    p1 = jnp.max(pe, axis=-1, keepdims=True)
    i1 = jnp.min(jnp.where(in_grp & (pe == p1), lane, 128), axis=-1, keepdims=True)
    rest = in_grp & (lane != i1)
    pe2 = jnp.where(rest, pe, -1.0)
    p2 = jnp.max(pe2, axis=-1, keepdims=True)
    i2 = jnp.min(jnp.where(rest & (pe2 == p2), lane, 128), axis=-1, keepdims=True)
    psum = p1 + p2
    gate = jnp.where(lane == i1, pg_top * (p1 / psum), jnp.where(lane == i2, pg_top * (p2 / psum), 0.0))
    gate_ref[...] = gate


def _outproj(x2d, proj, o_list, lse_list, ob2d, wpa, wpb, wo, norm_moe, wr, br, expand, tm=512):
    T = x2d.shape[0]
    row = lambda w: pl.BlockSpec((tm, w), lambda i: (i, 0))
    full = lambda a: pl.BlockSpec(a.shape, lambda i: (0,) * a.ndim)
    return pl.pallas_call(
        _outproj_kernel,
        grid=(T // tm,),
        in_specs=[
            row(D_MODEL),
            pl.BlockSpec((tm, D_MODEL), lambda i: (i, C_MA // D_MODEL)),
            pl.BlockSpec((tm, D_MODEL), lambda i: (i, C_MB // D_MODEL)),
            row(A_GW), row(A_GW), row(A_GW), row(128), row(128), row(128),
            row(D_MODEL),
            full(wpa), full(wpb), full(wo), full(norm_moe), full(wr), full(br), full(expand),
        ],
        out_specs=[row(D_MODEL), row(D_MODEL), row(128)],
        out_shape=[
            jax.ShapeDtypeStruct((T, D_MODEL), F32),
            jax.ShapeDtypeStruct((T, D_MODEL), BF16),
            jax.ShapeDtypeStruct((T, 128), F32),
        ],
        compiler_params=_cparams(("parallel",)),
        name="outproj",
    )(x2d, proj, proj, *o_list, *lse_list, ob2d, wpa, wpb, wo, norm_moe, wr, br, expand)


def _moe_kernel(hm_ref, gate_ref, x1_ref, wgu_ref, wd_ref, nf_ref, y_ref, acc):
    e = pl.program_id(1)

    @pl.when(e == 0)
    def _():
        acc[...] = jnp.zeros_like(acc)

    gu = jnp.dot(hm_ref[...], wgu_ref[0], preferred_element_type=F32)
    gt, up = gu[:, :EXPERT_FF], gu[:, EXPERT_FF:]
    a = (gt * jax.nn.sigmoid(gt)) * up
    ye = jnp.dot(a.astype(BF16), wd_ref[0], preferred_element_type=F32)
    gate = gate_ref[...]
    lane = lax.broadcasted_iota(jnp.int32, gate.shape, 1)
    ge = jnp.sum(jnp.where(lane == e, gate, 0.0), axis=-1, keepdims=True)
    acc[...] += ge * ye

    @pl.when(e == N_EXPERTS - 1)
    def _():
        x2 = x1_ref[...] + acc[...]
        y_ref[...] = x2 * lax.rsqrt(jnp.mean(x2 * x2, axis=-1, keepdims=True) + EPS) * nf_ref[...]


def _moe(hm, gate, x1, wgu, wd, norm_final, tm=1024):
    T = hm.shape[0]
    return pl.pallas_call(
        _moe_kernel,
        grid=(T // tm, N_EXPERTS),
        in_specs=[
            pl.BlockSpec((tm, D_MODEL), lambda i, e: (i, 0)),
            pl.BlockSpec((tm, 128), lambda i, e: (i, 0)),
            pl.BlockSpec((tm, D_MODEL), lambda i, e: (i, 0)),
            pl.BlockSpec((1, D_MODEL, 2 * EXPERT_FF), lambda i, e: (e, 0, 0)),
            pl.BlockSpec((1, EXPERT_FF, D_MODEL), lambda i, e: (e, 0, 0)),
            pl.BlockSpec((1, D_MODEL), lambda i, e: (0, 0)),
        ],
        out_specs=pl.BlockSpec((tm, D_MODEL), lambda i, e: (i, 0)),
        out_shape=jax.ShapeDtypeStruct((T, D_MODEL), F32),
        scratch_shapes=[pltpu.VMEM((tm, D_MODEL), F32)],
        compiler_params=_cparams(("parallel", "arbitrary")),
        name="moe",
    )(hm, gate, x1, wgu, wd, norm_final)


def _prep_weights(norm_mix, w_in, rel_bias, w_gate_f, b_gate_f, w_gate_b, b_gate_b, gla_norm, w_proj_a,
                  w_proj_b, w_out, norm_moe, w_rg, b_rg, w_re, b_re, w_eg, w_eu, w_ed, norm_final):
    splits = (A_QKV, A_QKV, A_QKV, 512, 512, 1024, 1024, GATE_RANK, GATE_RANK, D_MODEL, D_MODEL)
    qa, ka, va, qb, kb, vb, og, gf, gb, ma, mb = jnp.split(w_in, np.cumsum(splits)[:-1].tolist(), axis=1)
    gpad = jnp.zeros((D_MODEL, PW - C_G - 2 * GATE_RANK), w_in.dtype)
    w_all = jnp.concatenate([ma, mb, qa, ka, va, qb, kb, vb, og, gf, gb, gpad], axis=1).astype(BF16)
    kw = GLA_H * GLA_DK
    wgf = jnp.zeros((128, kw), F32).at[:GATE_RANK].set(w_gate_f).astype(BF16)
    wgb = jnp.zeros((128, kw), F32).at[GATE_RANK:2 * GATE_RANK].set(w_gate_b).astype(BF16)
    wr = jnp.zeros((D_MODEL, 128), F32)
    wr = wr.at[:, :N_EXPERTS].set(w_re).at[:, N_EXPERTS:2 * N_EXPERTS].set(jnp.repeat(w_rg, EPG, axis=1))
    br = jnp.zeros((1, 128), F32)
    br = br.at[0, :N_EXPERTS].set(b_re).at[0, N_EXPERTS:2 * N_EXPERTS].set(jnp.repeat(b_rg, EPG))
    expand = (np.arange(128)[:, None] == 32 * (np.arange(A_GW)[None, :] // A_DH)).astype(np.float32)
    return dict(
        norm_mix=norm_mix.reshape(1, D_MODEL), w_all=w_all, rel_bias=rel_bias,
        wgf=wgf, wgb=wgb, bgf=b_gate_f.reshape(1, kw), bgb=b_gate_b.reshape(1, kw),
        gla_norm=gla_norm.reshape(1, GLA_DV),
        wpa=w_proj_a.astype(BF16), wpb=w_proj_b.astype(BF16), wo=w_out.astype(BF16),
        norm_moe=norm_moe.reshape(1, D_MODEL), wr=wr, br=br, expand=jnp.asarray(expand),
        wgu=jnp.concatenate([w_eg, w_eu], axis=-1).astype(BF16), wd=w_ed.astype(BF16),
        norm_final=norm_final.reshape(1, D_MODEL),
    )


def _trunk(x, w):
    B, S, _ = x.shape
    T = B * S
    x2d = x.reshape(T, D_MODEL)
    proj = _inproj(x2d, w["norm_mix"], w["w_all"])
    o_list, lse_list = [], []
    for g, (_, d) in enumerate(DIL_PAIRS):
        L = S // d
        tiles = _bias_tiles(w["rel_bias"], g, d, min(2 * Q_BLK, L))
        o, lse = _attn_group(proj, tiles, g, d, B, S)
        o_list.append(o.reshape(T, A_GW))
        lse_list.append(lse.reshape(T, 128))
    ob = _gla(proj, w["wgf"], w["wgb"], w["bgf"], w["bgb"], w["gla_norm"], B, S)
    x1, hm, gate = _outproj(x2d, proj, o_list, lse_list, ob.reshape(T, D_MODEL), w["wpa"], w["wpb"], w["wo"],
                            w["norm_moe"], w["wr"], w["br"], w["expand"])
    y = _moe(hm, gate, x1, w["wgu"], w["wd"], w["norm_final"])
    return y.reshape(B, S, D_MODEL)


def kernel(x_prompt, x_sample, norm_mix, w_in, rel_bias, w_gate_f, b_gate_f, w_gate_b, b_gate_b, gla_norm,
           w_proj_a, w_proj_b, w_out, norm_moe, w_router_group, b_router_group, w_router_expert,
           b_router_expert, w_exp_gate, w_exp_up, w_exp_down, norm_final):
    w = _prep_weights(norm_mix[0], w_in[0], rel_bias, w_gate_f[0], b_gate_f[0], w_gate_b[0], b_gate_b[0],
                      gla_norm[0], w_proj_a[0], w_proj_b[0], w_out[0], norm_moe[0], w_router_group[0],
                      b_router_group[0], w_router_expert[0], b_router_expert[0], w_exp_gate[0], w_exp_up[0],
                      w_exp_down[0], norm_final)
    return (_trunk(x_prompt, w), _trunk(x_sample, w))
```

```python
import functools
import math

import numpy as np
import jax
import jax.numpy as jnp
from jax import lax
from jax.experimental import pallas as pl
from jax.experimental.pallas import tpu as pltpu

F32 = jnp.float32
BF16 = jnp.bfloat16
HIGHEST = lax.Precision.HIGHEST

D_MODEL = 1024
EPS = 1e-6
NEG_INF = -1e30

DIL_PAIRS = ((128, 1), (512, 4), (2048, 16))
A_HPG = 4
A_DH = 64
A_GW = A_HPG * A_DH
A_QKV = 3 * A_GW
HALF_WIN = 64
Q_BLK = 128
N_REL_BUCKETS = 32
REL_MAX_DISTANCE = 1024
GLA_H = 4
GLA_DK = 128
GLA_DV = 256
GATE_RANK = 16
GATE_TAU = 16.0
GLA_CHUNK = 64
N_GROUPS = 4
EPG = 8
N_EXPERTS = 32
EXPERT_FF = 256

PA = 3 * A_QKV
C_MA, C_MB = 0, 1024
C_QB, C_KB, C_VB, C_OG = 2048, 2560, 3072, 4096
C_G = 5120
PR = 5376
PROJ_TN = 768

VMEM_LIMIT = 56 * 1024 * 1024


def _cparams(sem):
    return pltpu.CompilerParams(dimension_semantics=sem, vmem_limit_bytes=VMEM_LIMIT)


def _inproj_kernel(x_ref, g_ref, w_ref, oa_ref, or_ref, h_scr):
    j = pl.program_id(1)

    @pl.when(j == 0)
    def _():
        x = x_ref[...]
        r = x * lax.rsqrt(jnp.mean(x * x, axis=-1, keepdims=True) + EPS)
        h_scr[...] = (r * g_ref[...]).astype(BF16)

    y = jnp.dot(h_scr[...], w_ref[...], preferred_element_type=F32).astype(BF16)

    @pl.when(j < PA // PROJ_TN)
    def _():
        oa_ref[...] = y

    @pl.when(j >= PA // PROJ_TN)
    def _():
        or_ref[...] = y


def _inproj(x2d, norm_g, w_all, tm=1024):
    T = x2d.shape[0]
    na = PA // PROJ_TN
    return pl.pallas_call(
        _inproj_kernel,
        grid=(T // tm, (PA + PR) // PROJ_TN),
        in_specs=[
            pl.BlockSpec((tm, D_MODEL), lambda i, j: (i, 0)),
            pl.BlockSpec((1, D_MODEL), lambda i, j: (0, 0)),
            pl.BlockSpec((D_MODEL, PROJ_TN), lambda i, j: (0, j)),
        ],
        out_specs=[
            pl.BlockSpec((tm, PROJ_TN), lambda i, j: (i, jnp.minimum(j, na - 1))),
            pl.BlockSpec((tm, PROJ_TN), lambda i, j: (i, jnp.maximum(j - na, 0))),
        ],
        out_shape=[jax.ShapeDtypeStruct((T, PA), BF16), jax.ShapeDtypeStruct((T, PR), BF16)],
        scratch_shapes=[pltpu.VMEM((tm, D_MODEL), BF16)],
        compiler_params=_cparams(("parallel", "arbitrary")),
        name="inproj",
    )(x2d, norm_g, w_all)


def _attn_kernel(q_ref, k_ref, v_ref, bias_ref, o_ref, lse_ref, *, L, wk):
    i = pl.program_id(2)
    start = pl.multiple_of(jnp.clip(i * Q_BLK - HALF_WIN, 0, L - wk), HALF_WIN)
    q = q_ref[0]
    kw = k_ref[0, pl.ds(start, wk), :]
    vw = v_ref[0, pl.ds(start, wk), :]
    scale = A_DH ** -0.5
    lane = lax.broadcasted_iota(jnp.int32, (Q_BLK, 128), 1)
    outs = []
    lse_tile = jnp.zeros((Q_BLK, 128), F32)
    for h in range(A_HPG):
        sl = slice(h * A_DH, (h + 1) * A_DH)
        s = lax.dot_general(q[:, sl], kw[:, sl], (((1,), (1,)), ((), ())), preferred_element_type=F32)
        s = s * scale + bias_ref[0, h]
        m = jnp.max(s, axis=-1, keepdims=True)
        p = jnp.exp(s - m)
        l = jnp.sum(p, axis=-1, keepdims=True)
        o = jnp.dot(p.astype(BF16), vw[:, sl], preferred_element_type=F32) / l
        outs.append(o)
        lse = m + jnp.log(l)
        lse_tile = jnp.where((lane >= 32 * h) & (lane < 32 * (h + 1)), lse, lse_tile)
    o_ref[0] = jnp.concatenate(outs, axis=1).astype(BF16)
    lse_ref[0] = lse_tile


def _attn_group(proj_a, bias_tiles, g, d, B, S):
    L = S // d
    nblk = L // Q_BLK
    wk = min(2 * Q_BLK, L)
    pv = proj_a.reshape(B, L, d * PA)
    nc = PA // A_GW
    cq, ck, cv = g, 3 + g, 6 + g

    def variant(i):
        return jnp.where(i == 0, 0, jnp.where(i == nblk - 1, 2, 1))

    return pl.pallas_call(
        functools.partial(_attn_kernel, L=L, wk=wk),
        grid=(B, d, nblk),
        in_specs=[
            pl.BlockSpec((1, Q_BLK, A_GW), lambda b, r, i: (b, i, r * nc + cq)),
            pl.BlockSpec((1, L, A_GW), lambda b, r, i: (b, 0, r * nc + ck)),
            pl.BlockSpec((1, L, A_GW), lambda b, r, i: (b, 0, r * nc + cv)),
            pl.BlockSpec((1, A_HPG, Q_BLK, wk), lambda b, r, i: (variant(i), 0, 0, 0)),
        ],
        out_specs=[
            pl.BlockSpec((1, Q_BLK, A_GW), lambda b, r, i: (b, i, r)),
            pl.BlockSpec((1, Q_BLK, 128), lambda b, r, i: (b, i, r)),
        ],
        out_shape=[
            jax.ShapeDtypeStruct((B, L, d * A_GW), BF16),
            jax.ShapeDtypeStruct((B, L, d * 128), F32),
        ],
        compiler_params=_cparams(("parallel", "parallel", "arbitrary")),
        name=f"attn_d{d}",
    )(pv, pv, pv, bias_tiles)


def _t5_buckets(rel):
    nb = N_REL_BUCKETS // 2
    max_exact = nb // 2
    ret = (rel > 0).astype(np.int64) * nb
    n = np.abs(rel)
    large = max_exact + (np.log(np.maximum(n, 1) / max_exact) / math.log(REL_MAX_DISTANCE / max_exact)
                         * (nb - max_exact)).astype(np.int64)
    large = np.minimum(large, nb - 1)
    return (ret + np.where(n < max_exact, n, large)).astype(np.int32)


def _bias_tiles(rel_bias, g, d, wk):
    rel = d * np.arange(-HALF_WIN, HALF_WIN + 1)
    bucket = _t5_buckets(rel)
    bias = rel_bias[jnp.asarray(bucket)][:, g * A_HPG:(g + 1) * A_HPG].T.astype(F32)
    pad = wk + Q_BLK
    neg = jnp.full((A_HPG, pad), NEG_INF, F32)
    val = jnp.concatenate([neg, bias, neg], axis=1)
    zero = pad + HALF_WIN
    m = wk + Q_BLK
    tiles = []
    for off in (0, HALF_WIN, wk - Q_BLK):
        u = jnp.concatenate([val[:, zero - off:zero - off + wk], val[:, zero - off - Q_BLK:zero - off]], axis=1)
        flat = jnp.tile(u, (1, Q_BLK))[:, :Q_BLK * (m - 1)]
        tiles.append(flat.reshape(A_HPG, Q_BLK, m - 1)[:, :, :wk])
    return jnp.stack(tiles)


def _log_sigmoid(z):
    return jnp.minimum(z, 0.0) - jnp.log1p(jnp.exp(-jnp.abs(z)))


def _gla_kernel(q_ref, k_ref, v_ref, og_ref, g_ref, wgf_ref, wgb_ref, bgf_ref, bgb_ref, gn_ref, o_ref,
                qif, kif, kof, qib, kib, kob, vt, etf, etb, of_scr, *, S):
    PB = 256
    C = GLA_CHUNK
    rr = lax.broadcasted_iota(jnp.int32, (PB, PB), 0)
    cc = lax.broadcasted_iota(jnp.int32, (PB, PB), 1)
    same = (rr >> 6) == (cc >> 6)
    tl = jnp.where(same & (cc <= rr), 1.0, 0.0).astype(F32)
    ones_bd = jnp.where(same, 1.0, 0.0).astype(F32)
    qscale = GLA_DK ** -0.5
    inv_tau = 1.0 / GATE_TAU

    def prep(i, carry):
        rows = pl.ds(pl.multiple_of(i * PB, PB), PB)
        g = g_ref[0, rows, :]
        q = q_ref[0, rows, :].astype(F32) * qscale
        k = k_ref[0, rows, :].astype(F32)
        zf = jnp.dot(g, wgf_ref[...], preferred_element_type=F32) + bgf_ref[...]
        lf = _log_sigmoid(zf) * inv_tau
        bf = jnp.dot(tl, lf, precision=HIGHEST, preferred_element_type=F32)
        totf = jnp.dot(ones_bd, lf, precision=HIGHEST, preferred_element_type=F32)
        qif[rows, :] = (q * jnp.exp(bf)).astype(BF16)
        kif[rows, :] = (k * jnp.exp(-bf)).astype(BF16)
        kof[rows, :] = (k * jnp.exp(totf - bf)).astype(BF16)
        etf[rows, :] = jnp.exp(totf)
        zb = jnp.dot(g, wgb_ref[...], preferred_element_type=F32) + bgb_ref[...]
        lb = _log_sigmoid(zb) * inv_tau
        pb = jnp.dot(tl, lb, precision=HIGHEST, preferred_element_type=F32)
        totb = jnp.dot(ones_bd, lb, precision=HIGHEST, preferred_element_type=F32)
        bs = totb - pb + lb
        qib[rows, :] = (q * jnp.exp(bs)).astype(BF16)
        kib[rows, :] = (k * jnp.exp(-bs)).astype(BF16)
        kob[rows, :] = (k * jnp.exp(totb - bs)).astype(BF16)
        etb[rows, :] = jnp.exp(totb)
        vtb = v_ref[0, rows, :].astype(F32).T.astype(BF16)
        vt[2 * i] = vtb[:, :128]
        vt[2 * i + 1] = vtb[:, 128:]
        return carry

    lax.fori_loop(0, S // PB, prep, 0)

    r64 = lax.broadcasted_iota(jnp.int32, (C, C), 0)
    c64 = lax.broadcasted_iota(jnp.int32, (C, C), 1)
    mask_f = c64 <= r64
    mask_b = c64 > r64
    zeros_half = jnp.zeros((C, GLA_DK), BF16)
    nt = (((1,), (1,)), ((), ()))

    def chunk_step(qi_ref, ki_ref, ko_ref, et_ref, mask, p, half, state):
        r0 = pl.multiple_of(p * (2 * C) + half * C, C)
        rows = pl.ds(r0, C)
        qi = qi_ref[rows, :]
        ki = ki_ref[rows, :]
        ko = ko_ref[rows, :]
        v = v_ref[0, rows, :]
        sc = lax.dot_general(qi, ki, nt, preferred_element_type=F32)
        sc = jnp.where(mask, sc, 0.0).astype(BF16)
        o = jnp.dot(sc, v, preferred_element_type=F32)
        o = o + lax.dot_general(qi, state.astype(BF16), nt, preferred_element_type=F32)
        decay = et_ref[pl.ds(r0, 1), :]
        ko_pad = jnp.concatenate([ko, zeros_half] if half == 0 else [zeros_half, ko], axis=0)
        state = state * decay + jnp.dot(vt[p], ko_pad, preferred_element_type=F32)
        return rows, o, state

    def fwd(p, state):
        for half in (0, 1):
            rows, o, state = chunk_step(qif, kif, kof, etf, mask_f, p, half, state)
            of_scr[rows, :] = o
        return state

    lax.fori_loop(0, S // (2 * C), fwd, jnp.zeros((GLA_DV, GLA_DK), F32))

    gn = gn_ref[...]

    def bwd(t, state):
        p = S // (2 * C) - 1 - t
        for half in (1, 0):
            rows, o, state = chunk_step(qib, kib, kob, etb, mask_b, p, half, state)
            tot = of_scr[rows, :] + o
            nrm = tot * lax.rsqrt(jnp.mean(tot * tot, axis=-1, keepdims=True) + EPS) * gn
            og = og_ref[0, rows, :].astype(F32)
            o_ref[0, rows, :] = (nrm * (og * jax.nn.sigmoid(og))).astype(BF16)
        return state

    lax.fori_loop(0, S // (2 * C), bwd, jnp.zeros((GLA_DV, GLA_DK), F32))


def _gla(proj, wgf, wgb, bgf, bgb, gla_norm, B, S):
    pv = proj.reshape(B, S, PR)
    cq, ck, cv, cog, cg = C_QB // GLA_DK, C_KB // GLA_DK, C_VB // GLA_DV, C_OG // GLA_DV, C_G // 128
    return pl.pallas_call(
        functools.partial(_gla_kernel, S=S),
        grid=(B, GLA_H),
        in_specs=[
            pl.BlockSpec((1, S, GLA_DK), lambda b, h: (b, 0, cq + h)),
            pl.BlockSpec((1, S, GLA_DK), lambda b, h: (b, 0, ck + h)),
            pl.BlockSpec((1, S, GLA_DV), lambda b, h: (b, 0, cv + h)),
            pl.BlockSpec((1, S, GLA_DV), lambda b, h: (b, 0, cog + h)),
            pl.BlockSpec((1, S, 128), lambda b, h: (b, 0, cg)),
            pl.BlockSpec((128, GLA_DK), lambda b, h: (0, h)),
            pl.BlockSpec((128, GLA_DK), lambda b, h: (0, h)),
            pl.BlockSpec((1, GLA_DK), lambda b, h: (0, h)),
            pl.BlockSpec((1, GLA_DK), lambda b, h: (0, h)),
            pl.BlockSpec((1, GLA_DV), lambda b, h: (0, 0)),
        ],
        out_specs=pl.BlockSpec((1, S, GLA_DV), lambda b, h: (b, 0, h)),
        out_shape=jax.ShapeDtypeStruct((B, S, GLA_H * GLA_DV), BF16),
        scratch_shapes=[
            pltpu.VMEM((S, GLA_DK), BF16), pltpu.VMEM((S, GLA_DK), BF16), pltpu.VMEM((S, GLA_DK), BF16),
            pltpu.VMEM((S, GLA_DK), BF16), pltpu.VMEM((S, GLA_DK), BF16), pltpu.VMEM((S, GLA_DK), BF16),
            pltpu.VMEM((S // 128, GLA_DV, 128), BF16),
            pltpu.VMEM((S, GLA_DK), F32), pltpu.VMEM((S, GLA_DK), F32),
            pltpu.VMEM((S, GLA_DV), F32),
        ],
        compiler_params=_cparams(("parallel", "arbitrary")),
        name="gla",
    )(pv, pv, pv, pv, pv, wgf, wgb, bgf, bgb, gla_norm)


def _outproj_kernel(x_ref, ma_ref, mb_ref, o0_ref, o1_ref, o2_ref, l0_ref, l1_ref, l2_ref, ob_ref,
                    wpa_ref, wpb_ref, wo_ref, nm_ref, wr_ref, br_ref, exp_ref,
                    x1_ref, hm_ref, gate_ref):
    l0, l1, l2 = l0_ref[...], l1_ref[...], l2_ref[...]
    m = jnp.maximum(jnp.maximum(l0, l1), l2)
    e0, e1, e2 = jnp.exp(l0 - m), jnp.exp(l1 - m), jnp.exp(l2 - m)
    den = e0 + e1 + e2
    ex = exp_ref[...]
    oa = jnp.zeros(o0_ref.shape, F32)
    for e, o_ref in ((e0, o0_ref), (e1, o1_ref), (e2, o2_ref)):
        w = jnp.dot(e / den, ex, precision=HIGHEST, preferred_element_type=F32)
        oa = oa + w * o_ref[...].astype(F32)
    ya = jnp.dot(oa.astype(BF16), wpa_ref[...], preferred_element_type=F32)
    yb = jnp.dot(ob_ref[...], wpb_ref[...], preferred_element_type=F32)
    mix = jax.nn.sigmoid(ma_ref[...].astype(F32)) * ya + jax.nn.sigmoid(mb_ref[...].astype(F32)) * yb
    x1 = x_ref[...] + jnp.dot(mix.astype(BF16), wo_ref[...], preferred_element_type=F32)
    x1_ref[...] = x1
    hm = x1 * lax.rsqrt(jnp.mean(x1 * x1, axis=-1, keepdims=True) + EPS) * nm_ref[...]
    hm_ref[...] = hm.astype(BF16)

    logits = jnp.dot(hm, wr_ref[...], precision=HIGHEST, preferred_element_type=F32) + br_ref[...]
    tm = logits.shape[0]
    lane = lax.broadcasted_iota(jnp.int32, (tm, 128), 1)
    is_g = (lane >= N_EXPERTS) & (lane < 2 * N_EXPERTS)
    is_e = lane < N_EXPERTS
    grp_of_lane = jnp.where(is_g, (lane - N_EXPERTS) >> 3, lane >> 3)
    gl = jnp.where(is_g, logits, NEG_INF)
    gmax = jnp.max(gl, axis=-1, keepdims=True)
    gsum = jnp.sum(jnp.where(is_g, jnp.exp(gl - gmax), 0.0), axis=-1, keepdims=True) * (1.0 / EPG)
    pg_top = 1.0 / gsum
    g_sel = jnp.min(jnp.where(is_g & (gl == gmax), grp_of_lane, N_GROUPS), axis=-1, keepdims=True)
    in_grp = is_e & (grp_of_lane == g_sel)
    el = jnp.where(in_grp, logits, NEG_INF)
    emax = jnp.max(el, axis=-1, keepdims=True)
    ee = jnp.where(in_grp, jnp.exp(el - emax), 0.0)
    pe = ee / jnp.sum(ee, axis=-1, keepdims=True)
    p1 = jnp.max(pe, axis=-1, keepdims=True)
    i1 = jnp.min(jnp.where(in_grp & (pe == p1), lane, 128), axis=-1, keepdims=True)
    rest = in_grp & (lane != i1)
    pe2 = jnp.where(rest, pe, -1.0)
    p2 = jnp.max(pe2, axis=-1, keepdims=True)
    i2 = jnp.min(jnp.where(rest & (pe2 == p2), lane, 128), axis=-1, keepdims=True)
    psum = p1 + p2
    gate = jnp.where(lane == i1, pg_top * (p1 / psum), jnp.where(lane == i2, pg_top * (p2 / psum), 0.0))
    gate_ref[...] = gate


def _outproj(x2d, proj, o_list, lse_list, ob2d, wpa, wpb, wo, norm_moe, wr, br, expand, tm=512):
    T = x2d.shape[0]
    row = lambda w: pl.BlockSpec((tm, w), lambda i: (i, 0))
    full = lambda a: pl.BlockSpec(a.shape, lambda i: (0,) * a.ndim)
    return pl.pallas_call(
        _outproj_kernel,
        grid=(T // tm,),
        in_specs=[
            row(D_MODEL),
            pl.BlockSpec((tm, D_MODEL), lambda i: (i, C_MA // D_MODEL)),
            pl.BlockSpec((tm, D_MODEL), lambda i: (i, C_MB // D_MODEL)),
            row(A_GW), row(A_GW), row(A_GW), row(128), row(128), row(128),
            row(D_MODEL),
            full(wpa), full(wpb), full(wo), full(norm_moe), full(wr), full(br), full(expand),
        ],
        out_specs=[row(D_MODEL), row(D_MODEL), row(128)],
        out_shape=[
            jax.ShapeDtypeStruct((T, D_MODEL), F32),
            jax.ShapeDtypeStruct((T, D_MODEL), BF16),
            jax.ShapeDtypeStruct((T, 128), F32),
        ],
        compiler_params=_cparams(("parallel",)),
        name="outproj",
    )(x2d, proj, proj, *o_list, *lse_list, ob2d, wpa, wpb, wo, norm_moe, wr, br, expand)


def _moe_kernel(hm_ref, gate_ref, x1_ref, wgu_ref, wd_ref, nf_ref, y_ref, acc):
    e = pl.program_id(1)

    @pl.when(e == 0)
    def _():
        acc[...] = jnp.zeros_like(acc)

    gu = jnp.dot(hm_ref[...], wgu_ref[0], preferred_element_type=F32)
    gt, up = gu[:, :EXPERT_FF], gu[:, EXPERT_FF:]
    a = (gt * jax.nn.sigmoid(gt)) * up
    ye = jnp.dot(a.astype(BF16), wd_ref[0], preferred_element_type=F32)
    gate = gate_ref[...]
    lane = lax.broadcasted_iota(jnp.int32, gate.shape, 1)
    ge = jnp.sum(jnp.where(lane == e, gate, 0.0), axis=-1, keepdims=True)
    acc[...] += ge * ye

    @pl.when(e == N_EXPERTS - 1)
    def _():
        x2 = x1_ref[...] + acc[...]
        y_ref[...] = x2 * lax.rsqrt(jnp.mean(x2 * x2, axis=-1, keepdims=True) + EPS) * nf_ref[...]


def _moe(hm, gate, x1, wgu, wd, norm_final, tm=1024):
    T = hm.shape[0]
    return pl.pallas_call(
        _moe_kernel,
        grid=(T // tm, N_EXPERTS),
        in_specs=[
            pl.BlockSpec((tm, D_MODEL), lambda i, e: (i, 0)),
            pl.BlockSpec((tm, 128), lambda i, e: (i, 0)),
            pl.BlockSpec((tm, D_MODEL), lambda i, e: (i, 0)),
            pl.BlockSpec((1, D_MODEL, 2 * EXPERT_FF), lambda i, e: (e, 0, 0)),
            pl.BlockSpec((1, EXPERT_FF, D_MODEL), lambda i, e: (e, 0, 0)),
            pl.BlockSpec((1, D_MODEL), lambda i, e: (0, 0)),
        ],
        out_specs=pl.BlockSpec((tm, D_MODEL), lambda i, e: (i, 0)),
        out_shape=jax.ShapeDtypeStruct((T, D_MODEL), F32),
        scratch_shapes=[pltpu.VMEM((tm, D_MODEL), F32)],
        compiler_params=_cparams(("parallel", "arbitrary")),
        name="moe",
    )(hm, gate, x1, wgu, wd, norm_final)


def _prep_weights(norm_mix, w_in, rel_bias, w_gate_f, b_gate_f, w_gate_b, b_gate_b, gla_norm, w_proj_a,
                  w_proj_b, w_out, norm_moe, w_rg, b_rg, w_re, b_re, w_eg, w_eu, w_ed, norm_final):
    splits = (A_QKV, A_QKV, A_QKV, 512, 512, 1024, 1024, GATE_RANK, GATE_RANK, D_MODEL, D_MODEL)
    qa, ka, va, qb, kb, vb, og, gf, gb, ma, mb = jnp.split(w_in, np.cumsum(splits)[:-1].tolist(), axis=1)
    gpad = jnp.zeros((D_MODEL, PR - C_G - 2 * GATE_RANK), w_in.dtype)
    w_all = jnp.concatenate([qa, ka, va, ma, mb, qb, kb, vb, og, gf, gb, gpad], axis=1).astype(BF16)
    kw = GLA_H * GLA_DK
    wgf = jnp.zeros((128, kw), F32).at[:GATE_RANK].set(w_gate_f).astype(BF16)
    wgb = jnp.zeros((128, kw), F32).at[GATE_RANK:2 * GATE_RANK].set(w_gate_b).astype(BF16)
    wr = jnp.zeros((D_MODEL, 128), F32)
    wr = wr.at[:, :N_EXPERTS].set(w_re).at[:, N_EXPERTS:2 * N_EXPERTS].set(jnp.repeat(w_rg, EPG, axis=1))
    br = jnp.zeros((1, 128), F32)
    br = br.at[0, :N_EXPERTS].set(b_re).at[0, N_EXPERTS:2 * N_EXPERTS].set(jnp.repeat(b_rg, EPG))
    expand = (np.arange(128)[:, None] == 32 * (np.arange(A_GW)[None, :] // A_DH)).astype(np.float32)
    return dict(
        norm_mix=norm_mix.reshape(1, D_MODEL), w_all=w_all, rel_bias=rel_bias,
        wgf=wgf, wgb=wgb, bgf=b_gate_f.reshape(1, kw), bgb=b_gate_b.reshape(1, kw),
        gla_norm=gla_norm.reshape(1, GLA_DV),
        wpa=w_proj_a.astype(BF16), wpb=w_proj_b.astype(BF16), wo=w_out.astype(BF16),
        norm_moe=norm_moe.reshape(1, D_MODEL), wr=wr, br=br, expand=jnp.asarray(expand),
        wgu=jnp.concatenate([w_eg, w_eu], axis=-1).astype(BF16), wd=w_ed.astype(BF16),
        norm_final=norm_final.reshape(1, D_MODEL),
    )


def _trunk(x, w):
    B, S, _ = x.shape
    T = B * S
    x2d = x.reshape(T, D_MODEL)
    proj_a, proj = _inproj(x2d, w["norm_mix"], w["w_all"])
    o_list, lse_list = [], []
    for g, (_, d) in enumerate(DIL_PAIRS):
        L = S // d
        tiles = _bias_tiles(w["rel_bias"], g, d, min(2 * Q_BLK, L))
        o, lse = _attn_group(proj_a, tiles, g, d, B, S)
        o_list.append(o.reshape(T, A_GW))
        lse_list.append(lse.reshape(T, 128))
    ob = _gla(proj, w["wgf"], w["wgb"], w["bgf"], w["bgb"], w["gla_norm"], B, S)
    x1, hm, gate = _outproj(x2d, proj, o_list, lse_list, ob.reshape(T, D_MODEL), w["wpa"], w["wpb"], w["wo"],
                            w["norm_moe"], w["wr"], w["br"], w["expand"])
    y = _moe(hm, gate, x1, w["wgu"], w["wd"], w["norm_final"])
    return y.reshape(B, S, D_MODEL)


def kernel(x_prompt, x_sample, norm_mix, w_in, rel_bias, w_gate_f, b_gate_f, w_gate_b, b_gate_b, gla_norm,
           w_proj_a, w_proj_b, w_out, norm_moe, w_router_group, b_router_group, w_router_expert,
           b_router_expert, w_exp_gate, w_exp_up, w_exp_down, norm_final):
    w = _prep_weights(norm_mix[0], w_in[0], rel_bias, w_gate_f[0], b_gate_f[0], w_gate_b[0], b_gate_b[0],
                      gla_norm[0], w_proj_a[0], w_proj_b[0], w_out[0], norm_moe[0], w_router_group[0],
                      b_router_group[0], w_router_expert[0], b_router_expert[0], w_exp_gate[0], w_exp_up[0],
                      w_exp_down[0], norm_final)
    return (_trunk(x_prompt, w), _trunk(x_sample, w))
```

```python
import functools
import math

import numpy as np
import jax
import jax.numpy as jnp
from jax import lax
from jax.experimental import pallas as pl
from jax.experimental.pallas import tpu as pltpu

F32 = jnp.float32
BF16 = jnp.bfloat16
HIGHEST = lax.Precision.HIGHEST

D_MODEL = 1024
EPS = 1e-6
NEG_INF = -1e30

DIL_PAIRS = ((128, 1), (512, 4), (2048, 16))
A_HPG = 4
A_DH = 64
A_GW = A_HPG * A_DH
A_QKV = 3 * A_GW
HALF_WIN = 64
Q_BLK = 128
N_REL_BUCKETS = 32
REL_MAX_DISTANCE = 1024
GLA_H = 4
GLA_DK = 128
GLA_DV = 256
GATE_RANK = 16
GATE_TAU = 16.0
GLA_CHUNK = 64
N_GROUPS = 4
EPG = 8
N_EXPERTS = 32
EXPERT_FF = 256

PA = 3 * A_QKV
C_MA, C_MB = 0, 1024
C_QB, C_KB, C_VB, C_OG = 2048, 2560, 3072, 4096
C_G = 5120
PR = 5376
PROJ_TN = 768

VMEM_LIMIT = 56 * 1024 * 1024


def _cparams(sem):
    return pltpu.CompilerParams(dimension_semantics=sem, vmem_limit_bytes=VMEM_LIMIT)


def _inproj_kernel(x_ref, g_ref, w_ref, oa_ref, or_ref, h_scr):
    j = pl.program_id(1)

    @pl.when(j == 0)
    def _():
        x = x_ref[...]
        r = x * lax.rsqrt(jnp.mean(x * x, axis=-1, keepdims=True) + EPS)
        h_scr[...] = (r * g_ref[...]).astype(BF16)

    y = jnp.dot(h_scr[...], w_ref[...], preferred_element_type=F32).astype(BF16)

    @pl.when(j < PA // PROJ_TN)
    def _():
        oa_ref[...] = y

    @pl.when(j >= PA // PROJ_TN)
    def _():
        or_ref[...] = y


def _inproj(x2d, norm_g, w_all, tm=1024):
    T = x2d.shape[0]
    na = PA // PROJ_TN
    return pl.pallas_call(
        _inproj_kernel,
        grid=(T // tm, (PA + PR) // PROJ_TN),
        in_specs=[
            pl.BlockSpec((tm, D_MODEL), lambda i, j: (i, 0)),
            pl.BlockSpec((1, D_MODEL), lambda i, j: (0, 0)),
            pl.BlockSpec((D_MODEL, PROJ_TN), lambda i, j: (0, j)),
        ],
        out_specs=[
            pl.BlockSpec((tm, PROJ_TN), lambda i, j: (i, jnp.minimum(j, na - 1))),
            pl.BlockSpec((tm, PROJ_TN), lambda i, j: (i, jnp.maximum(j - na, 0))),
        ],
        out_shape=[jax.ShapeDtypeStruct((T, PA), BF16), jax.ShapeDtypeStruct((T, PR), BF16)],
        scratch_shapes=[pltpu.VMEM((tm, D_MODEL), BF16)],
        compiler_params=_cparams(("parallel", "arbitrary")),
        name="inproj",
    )(x2d, norm_g, w_all)


def _attn_kernel(q_ref, k_ref, v_ref, bias_ref, o_ref, lse_ref, *, L, wk):
    i = pl.program_id(2)
    start = pl.multiple_of(jnp.clip(i * Q_BLK - HALF_WIN, 0, L - wk), HALF_WIN)
    q = q_ref[0]
    kw = k_ref[0, pl.ds(start, wk), :]
    vw = v_ref[0, pl.ds(start, wk), :]
    scale = A_DH ** -0.5
    lane = lax.broadcasted_iota(jnp.int32, (Q_BLK, 128), 1)
    outs = []
    lse_tile = jnp.zeros((Q_BLK, 128), F32)
    for h in range(A_HPG):
        sl = slice(h * A_DH, (h + 1) * A_DH)
        s = lax.dot_general(q[:, sl], kw[:, sl], (((1,), (1,)), ((), ())), preferred_element_type=F32)
        s = s * scale + bias_ref[0, h]
        m = jnp.max(s, axis=-1, keepdims=True)
        p = jnp.exp(s - m)
        l = jnp.sum(p, axis=-1, keepdims=True)
        o = jnp.dot(p.astype(BF16), vw[:, sl], preferred_element_type=F32) / l
        outs.append(o)
        lse = m + jnp.log(l)
        lse_tile = jnp.where((lane >= 32 * h) & (lane < 32 * (h + 1)), lse, lse_tile)
    o_ref[0] = jnp.concatenate(outs, axis=1).astype(BF16)
    lse_ref[0] = lse_tile


def _attn_group(proj_a, bias_tiles, g, d, B, S):
    L = S // d
    nblk = L // Q_BLK
    wk = min(2 * Q_BLK, L)
    pv = proj_a.reshape(B, L, d * PA)
    nc = PA // A_GW
    cq, ck, cv = g, 3 + g, 6 + g

    def variant(i):
        return jnp.where(i == 0, 0, jnp.where(i == nblk - 1, 2, 1))

    return pl.pallas_call(
        functools.partial(_attn_kernel, L=L, wk=wk),
        grid=(B, d, nblk),
        in_specs=[
            pl.BlockSpec((1, Q_BLK, A_GW), lambda b, r, i: (b, i, r * nc + cq)),
            pl.BlockSpec((1, L, A_GW), lambda b, r, i: (b, 0, r * nc + ck)),
            pl.BlockSpec((1, L, A_GW), lambda b, r, i: (b, 0, r * nc + cv)),
            pl.BlockSpec((1, A_HPG, Q_BLK, wk), lambda b, r, i: (variant(i), 0, 0, 0)),
        ],
        out_specs=[
            pl.BlockSpec((1, Q_BLK, A_GW), lambda b, r, i: (b, i, r)),
            pl.BlockSpec((1, Q_BLK, 128), lambda b, r, i: (b, i, r)),
        ],
        out_shape=[
            jax.ShapeDtypeStruct((B, L, d * A_GW), BF16),
            jax.ShapeDtypeStruct((B, L, d * 128), F32),
        ],
        compiler_params=_cparams(("parallel", "parallel", "arbitrary")),
        name=f"attn_d{d}",
    )(pv, pv, pv, bias_tiles)


def _t5_buckets(rel):
    nb = N_REL_BUCKETS // 2
    max_exact = nb // 2
    ret = (rel > 0).astype(np.int64) * nb
    n = np.abs(rel)
    large = max_exact + (np.log(np.maximum(n, 1) / max_exact) / math.log(REL_MAX_DISTANCE / max_exact)
                         * (nb - max_exact)).astype(np.int64)
    large = np.minimum(large, nb - 1)
    return (ret + np.where(n < max_exact, n, large)).astype(np.int32)


def _bias_tiles(rel_bias, g, d, wk):
    rel = d * np.arange(-HALF_WIN, HALF_WIN + 1)
    bucket = _t5_buckets(rel)
    bias = rel_bias[jnp.asarray(bucket)][:, g * A_HPG:(g + 1) * A_HPG].T.astype(F32)
    pad = wk + Q_BLK
    neg = jnp.full((A_HPG, pad), NEG_INF, F32)
    val = jnp.concatenate([neg, bias, neg], axis=1)
    zero = pad + HALF_WIN
    m = wk + Q_BLK
    tiles = []
    for off in (0, HALF_WIN, wk - Q_BLK):
        u = jnp.concatenate([val[:, zero - off:zero - off + wk], val[:, zero - off - Q_BLK:zero - off]], axis=1)
        flat = jnp.tile(u, (1, Q_BLK))[:, :Q_BLK * (m - 1)]
        tiles.append(flat.reshape(A_HPG, Q_BLK, m - 1)[:, :, :wk])
    return jnp.stack(tiles)


def _log_sigmoid(z):
    return jnp.minimum(z, 0.0) - jnp.log1p(jnp.exp(-jnp.abs(z)))


def _gla_kernel(q_ref, k_ref, v_ref, og_ref, g_ref, wgf_ref, wgb_ref, bgf_ref, bgb_ref, gn_ref, o_ref,
                qif, kif, kof, qib, kib, kob, vt, etf, etb, of_scr, *, S):
    PB = 256
    C = GLA_CHUNK
    rr = lax.broadcasted_iota(jnp.int32, (PB, PB), 0)
    cc = lax.broadcasted_iota(jnp.int32, (PB, PB), 1)
    same = (rr >> 6) == (cc >> 6)
    tl = jnp.where(same & (cc <= rr), 1.0, 0.0).astype(F32)
    ones_bd = jnp.where(same, 1.0, 0.0).astype(F32)
    qscale = GLA_DK ** -0.5
    inv_tau = 1.0 / GATE_TAU

    def prep(i, carry):
        rows = pl.ds(pl.multiple_of(i * PB, PB), PB)
        g = g_ref[0, rows, :]
        q = q_ref[0, rows, :].astype(F32) * qscale
        k = k_ref[0, rows, :].astype(F32)
        zf = jnp.dot(g, wgf_ref[...], preferred_element_type=F32) + bgf_ref[...]
        lf = _log_sigmoid(zf) * inv_tau
        bf = jnp.dot(tl, lf, precision=HIGHEST, preferred_element_type=F32)
        totf = jnp.dot(ones_bd, lf, precision=HIGHEST, preferred_element_type=F32)
        qif[rows, :] = (q * jnp.exp(bf)).astype(BF16)
        kif[rows, :] = (k * jnp.exp(-bf)).astype(BF16)
        kof[rows, :] = (k * jnp.exp(totf - bf)).astype(BF16)
        etf[rows, :] = jnp.exp(totf)
        zb = jnp.dot(g, wgb_ref[...], preferred_element_type=F32) + bgb_ref[...]
        lb = _log_sigmoid(zb) * inv_tau
        pb = jnp.dot(tl, lb, precision=HIGHEST, preferred_element_type=F32)
        totb = jnp.dot(ones_bd, lb, precision=HIGHEST, preferred_element_type=F32)
        bs = totb - pb + lb
        qib[rows, :] = (q * jnp.exp(bs)).astype(BF16)
        kib[rows, :] = (k * jnp.exp(-bs)).astype(BF16)
        kob[rows, :] = (k * jnp.exp(totb - bs)).astype(BF16)
        etb[rows, :] = jnp.exp(totb)
        vtb = v_ref[0, rows, :].astype(F32).T.astype(BF16)
        vt[2 * i] = vtb[:, :128]
        vt[2 * i + 1] = vtb[:, 128:]
        return carry

    lax.fori_loop(0, S // PB, prep, 0)

    r64 = lax.broadcasted_iota(jnp.int32, (C, C), 0)
    c64 = lax.broadcasted_iota(jnp.int32, (C, C), 1)
    mask_f = c64 <= r64
    mask_b = c64 > r64
    zeros_half = jnp.zeros((C, GLA_DK), BF16)
    nt = (((1,), (1,)), ((), ()))

    def chunk_step(qi_ref, ki_ref, ko_ref, et_ref, mask, p, half, state):
        r0 = pl.multiple_of(p * (2 * C) + half * C, C)
        rows = pl.ds(r0, C)
        qi = qi_ref[rows, :]
        ki = ki_ref[rows, :]
        ko = ko_ref[rows, :]
        v = v_ref[0, rows, :]
        sc = lax.dot_general(qi, ki, nt, preferred_element_type=F32)
        sc = jnp.where(mask, sc, 0.0).astype(BF16)
        o = jnp.dot(sc, v, preferred_element_type=F32)
        o = o + lax.dot_general(qi, state.astype(BF16), nt, preferred_element_type=F32)
        decay = et_ref[pl.ds(r0, 1), :]
        ko_pad = jnp.concatenate([ko, zeros_half] if half == 0 else [zeros_half, ko], axis=0)
        state = state * decay + jnp.dot(vt[p], ko_pad, preferred_element_type=F32)
        return rows, o, state

    def fwd(p, state):
        for half in (0, 1):
            rows, o, state = chunk_step(qif, kif, kof, etf, mask_f, p, half, state)
            of_scr[rows, :] = o
        return state

    lax.fori_loop(0, S // (2 * C), fwd, jnp.zeros((GLA_DV, GLA_DK), F32))

    gn = gn_ref[...]

    def bwd(t, state):
        p = S // (2 * C) - 1 - t
        for half in (1, 0):
            rows, o, state = chunk_step(qib, kib, kob, etb, mask_b, p, half, state)
            tot = of_scr[rows, :] + o
            nrm = tot * lax.rsqrt(jnp.mean(tot * tot, axis=-1, keepdims=True) + EPS) * gn
            og = og_ref[0, rows, :].astype(F32)
            o_ref[0, rows, :] = (nrm * (og * jax.nn.sigmoid(og))).astype(BF16)
        return state

    lax.fori_loop(0, S // (2 * C), bwd, jnp.zeros((GLA_DV, GLA_DK), F32))


def _gla(proj, wgf, wgb, bgf, bgb, gla_norm, B, S):
    pv = proj.reshape(B, S, PR)
    cq, ck, cv, cog, cg = C_QB // GLA_DK, C_KB // GLA_DK, C_VB // GLA_DV, C_OG // GLA_DV, C_G // 128
    return pl.pallas_call(
        functools.partial(_gla_kernel, S=S),
        grid=(B, GLA_H),
        in_specs=[
            pl.BlockSpec((1, S, GLA_DK), lambda b, h: (b, 0, cq + h)),
            pl.BlockSpec((1, S, GLA_DK), lambda b, h: (b, 0, ck + h)),
            pl.BlockSpec((1, S, GLA_DV), lambda b, h: (b, 0, cv + h)),
            pl.BlockSpec((1, S, GLA_DV), lambda b, h: (b, 0, cog + h)),
            pl.BlockSpec((1, S, 128), lambda b, h: (b, 0, cg)),
            pl.BlockSpec((128, GLA_DK), lambda b, h: (0, h)),
            pl.BlockSpec((128, GLA_DK), lambda b, h: (0, h)),
            pl.BlockSpec((1, GLA_DK), lambda b, h: (0, h)),
            pl.BlockSpec((1, GLA_DK), lambda b, h: (0, h)),
            pl.BlockSpec((1, GLA_DV), lambda b, h: (0, 0)),
        ],
        out_specs=pl.BlockSpec((1, S, GLA_DV), lambda b, h: (b, 0, h)),
        out_shape=jax.ShapeDtypeStruct((B, S, GLA_H * GLA_DV), BF16),
        scratch_shapes=[
            pltpu.VMEM((S, GLA_DK), BF16), pltpu.VMEM((S, GLA_DK), BF16), pltpu.VMEM((S, GLA_DK), BF16),
            pltpu.VMEM((S, GLA_DK), BF16), pltpu.VMEM((S, GLA_DK), BF16), pltpu.VMEM((S, GLA_DK), BF16),
            pltpu.VMEM((S // 128, GLA_DV, 128), BF16),
            pltpu.VMEM((S, GLA_DK), F32), pltpu.VMEM((S, GLA_DK), F32),
            pltpu.VMEM((S, GLA_DV), F32),
        ],
        compiler_params=_cparams(("parallel", "arbitrary")),
        name="gla",
    )(pv, pv, pv, pv, pv, wgf, wgb, bgf, bgb, gla_norm)


def _outproj_kernel(x_ref, ma_ref, mb_ref, o0_ref, o1_ref, o2_ref, l0_ref, l1_ref, l2_ref, ob_ref,
                    wpa_ref, wpb_ref, wo_ref, nm_ref, wr_ref, br_ref, exp_ref,
                    x1_ref, hm_ref, gate_ref, gatet_ref):
    l0, l1, l2 = l0_ref[...], l1_ref[...], l2_ref[...]
    m = jnp.maximum(jnp.maximum(l0, l1), l2)
    e0, e1, e2 = jnp.exp(l0 - m), jnp.exp(l1 - m), jnp.exp(l2 - m)
    den = e0 + e1 + e2
    ex = exp_ref[...]
    oa = jnp.zeros(o0_ref.shape, F32)
    for e, o_ref in ((e0, o0_ref), (e1, o1_ref), (e2, o2_ref)):
        w = jnp.dot(e / den, ex, precision=HIGHEST, preferred_element_type=F32)
        oa = oa + w * o_ref[...].astype(F32)
    ya = jnp.dot(oa.astype(BF16), wpa_ref[...], preferred_element_type=F32)
    yb = jnp.dot(ob_ref[...], wpb_ref[...], preferred_element_type=F32)
    mix = jax.nn.sigmoid(ma_ref[...].astype(F32)) * ya + jax.nn.sigmoid(mb_ref[...].astype(F32)) * yb
    x1 = x_ref[...] + jnp.dot(mix.astype(BF16), wo_ref[...], preferred_element_type=F32)
    x1_ref[...] = x1
    hm = x1 * lax.rsqrt(jnp.mean(x1 * x1, axis=-1, keepdims=True) + EPS) * nm_ref[...]
    hm_ref[...] = hm.astype(BF16)

    logits = jnp.dot(hm, wr_ref[...], precision=HIGHEST, preferred_element_type=F32) + br_ref[...]
    tm = logits.shape[0]
    lane = lax.broadcasted_iota(jnp.int32, (tm, 128), 1)
    is_g = (lane >= N_EXPERTS) & (lane < 2 * N_EXPERTS)
    is_e = lane < N_EXPERTS
    grp_of_lane = jnp.where(is_g, (lane - N_EXPERTS) >> 3, lane >> 3)
    gl = jnp.where(is_g, logits, NEG_INF)
    gmax = jnp.max(gl, axis=-1, keepdims=True)
    gsum = jnp.sum(jnp.where(is_g, jnp.exp(gl - gmax), 0.0), axis=-1, keepdims=True) * (1.0 / EPG)
    pg_top = 1.0 / gsum
    g_sel = jnp.min(jnp.where(is_g & (gl == gmax), grp_of_lane, N_GROUPS), axis=-1, keepdims=True)
    in_grp = is_e & (grp_of_lane == g_sel)
    el = jnp.where(in_grp, logits, NEG_INF)
    emax = jnp.max(el, axis=-1, keepdims=True)
    ee = jnp.where(in_grp, jnp.exp(el - emax), 0.0)
    pe = ee / jnp.sum(ee, axis=-1, keepdims=True)
    p1 = jnp.max(pe, axis=-1, keepdims=True)
    i1 = jnp.min(jnp.where(in_grp & (pe == p1), lane, 128), axis=-1, keepdims=True)
    rest = in_grp & (lane != i1)
    pe2 = jnp.where(rest, pe, -1.0)
    p2 = jnp.max(pe2, axis=-1, keepdims=True)
    i2 = jnp.min(jnp.where(rest & (pe2 == p2), lane, 128), axis=-1, keepdims=True)
    psum = p1 + p2
    gate = jnp.where(lane == i1, pg_top * (p1 / psum), jnp.where(lane == i2, pg_top * (p2 / psum), 0.0))
    gate_ref[...] = gate
    gatet_ref[...] = gate.T


def _outproj(x2d, proj, o_list, lse_list, ob2d, wpa, wpb, wo, norm_moe, wr, br, expand, tm=512):
    T = x2d.shape[0]
    row = lambda w: pl.BlockSpec((tm, w), lambda i: (i, 0))
    full = lambda a: pl.BlockSpec(a.shape, lambda i: (0,) * a.ndim)
    return pl.pallas_call(
        _outproj_kernel,
        grid=(T // tm,),
        in_specs=[
            row(D_MODEL),
            pl.BlockSpec((tm, D_MODEL), lambda i: (i, C_MA // D_MODEL)),
            pl.BlockSpec((tm, D_MODEL), lambda i: (i, C_MB // D_MODEL)),
            row(A_GW), row(A_GW), row(A_GW), row(128), row(128), row(128),
            row(D_MODEL),
            full(wpa), full(wpb), full(wo), full(norm_moe), full(wr), full(br), full(expand),
        ],
        out_specs=[row(D_MODEL), row(D_MODEL), row(128), pl.BlockSpec((128, tm), lambda i: (0, i))],
        out_shape=[
            jax.ShapeDtypeStruct((T, D_MODEL), F32),
            jax.ShapeDtypeStruct((T, D_MODEL), BF16),
            jax.ShapeDtypeStruct((T, 128), F32),
            jax.ShapeDtypeStruct((128, T), F32),
        ],
        compiler_params=_cparams(("parallel",)),
        name="outproj",
    )(x2d, proj, proj, *o_list, *lse_list, ob2d, wpa, wpb, wo, norm_moe, wr, br, expand)


MOE_TB = 1024
MOE_ALIGN = 16
MOE_CH = 128
MOE_R = 2 * MOE_TB + N_EXPERTS * MOE_ALIGN
MOE_RS = MOE_R + MOE_CH
MOE_SUB = 256


def _moe_route(hm_ref, gate_ref, gatet_ref, xs, ys, pt, wrow, rinfo):
    TB, R, SUB = MOE_TB, MOE_R, MOE_SUB
    big = float(4 * TB)
    gate = gate_ref[...]
    gt = gatet_ref[...]
    a = gate > 0.0
    at = gt > 0.0
    a_b = jnp.where(a, 1.0, 0.0).astype(BF16)
    at_b = jnp.where(at, 1.0, 0.0).astype(BF16)

    def pad(c):
        return jnp.floor((c + (MOE_ALIGN - 1)) * (1.0 / MOE_ALIGN)) * MOE_ALIGN

    er = lax.broadcasted_iota(jnp.int32, (128, 128), 0)
    ec = lax.broadcasted_iota(jnp.int32, (128, 128), 1)
    pad_row = pad(jnp.sum(jnp.where(a, 1.0, 0.0), axis=0, keepdims=True))
    off_row = jnp.dot(jnp.broadcast_to(pad_row, (8, 128)).astype(BF16), jnp.where(er < ec, 1.0, 0.0).astype(BF16),
                      preferred_element_type=F32)[0:1]
    pad_col = pad(jnp.sum(jnp.where(at, 1.0, 0.0), axis=1, keepdims=True))
    off_col = jnp.dot(jnp.where(ec < er, 1.0, 0.0).astype(BF16), jnp.broadcast_to(pad_col, (128, 128)).astype(BF16),
                      preferred_element_type=F32)[:, 0:1]

    rl = lax.broadcasted_iota(jnp.int32, (SUB, R), 1).astype(F32)
    for rc in range(TB // SUB):
        rows = slice(rc * SUB, (rc + 1) * SUB)
        tr = lax.broadcasted_iota(jnp.int32, (SUB, TB), 0) + rc * SUB
        tc = lax.broadcasted_iota(jnp.int32, (SUB, TB), 1)
        rank = jnp.dot(jnp.where(tc < tr, 1.0, 0.0).astype(BF16), a_b, preferred_element_type=F32)
        pos = off_row + rank
        a_c = a[rows]
        plo = jnp.min(jnp.where(a_c, pos, big), axis=1, keepdims=True)
        phi = jnp.max(jnp.where(a_c, pos, -1.0), axis=1, keepdims=True)
        phi = jnp.where(phi == plo, -1.0, phi)
        pt[rows, :] = jnp.where((rl == plo) | (rl == phi), 1.0, 0.0).astype(BF16)

    for cc in range(TB // SUB):
        cols = slice(cc * SUB, (cc + 1) * SUB)
        tr = lax.broadcasted_iota(jnp.int32, (TB, SUB), 0)
        tc = lax.broadcasted_iota(jnp.int32, (TB, SUB), 1) + cc * SUB
        rank_t = jnp.dot(at_b, jnp.where(tr < tc, 1.0, 0.0).astype(BF16), preferred_element_type=F32)
        pos_t = off_col + rank_t
        at_c = at[:, cols]
        gt_c = gt[:, cols]
        plo = jnp.min(jnp.where(at_c, pos_t, big), axis=0, keepdims=True)
        phi = jnp.max(jnp.where(at_c, pos_t, -1.0), axis=0, keepdims=True)
        phi = jnp.where(phi == plo, -1.0, phi)
        rinfo[0:1, cols] = plo
        rinfo[1:2, cols] = phi
        rinfo[2:3, cols] = jnp.sum(jnp.where(at_c & (pos_t == plo), gt_c, 0.0), axis=0, keepdims=True)
        rinfo[3:4, cols] = jnp.sum(jnp.where(at_c & (pos_t == phi), gt_c, 0.0), axis=0, keepdims=True)

    def gather(k, carry):
        r0 = pl.multiple_of(k * SUB, SUB)
        ri = (lax.broadcasted_iota(jnp.int32, (SUB, TB), 0) + r0).astype(F32)
        mlo = ri == rinfo[0:1, :]
        mhi = ri == rinfo[1:2, :]
        p = jnp.where(mlo | mhi, 1.0, 0.0).astype(BF16)
        xs[pl.ds(r0, SUB), :] = jnp.dot(p, hm_ref[...], preferred_element_type=F32).astype(BF16)
        w = jnp.sum(jnp.where(mlo, rinfo[2:3, :], 0.0) + jnp.where(mhi, rinfo[3:4, :], 0.0), axis=1, keepdims=True)
        wrow[pl.ds(r0, SUB), :] = jnp.broadcast_to(w, (SUB, 128))
        return carry

    lax.fori_loop(0, R // SUB, gather, 0)
    xs[R:, :] = jnp.zeros((MOE_RS - R, D_MODEL), BF16)
    wrow[R:, :] = jnp.zeros((MOE_RS - R, 128), F32)
    ys[...] = jnp.zeros_like(ys)


def _moe_kernel(meta_ref, hm_ref, gate_ref, gatet_ref, x1_ref, wgu_ref, wd_ref, nf_ref, y_ref,
                xs, ys, pt, wrow, rinfo):
    i = pl.program_id(0)
    e = pl.program_id(1)

    @pl.when(e == 0)
    def _():
        _moe_route(hm_ref, gate_ref, gatet_ref, xs, ys, pt, wrow, rinfo)

    off = meta_ref[i, e]
    nch = meta_ref[i, N_EXPERTS + e]

    def chunk(c, carry):
        rows = pl.ds(pl.multiple_of(off + c * MOE_CH, MOE_ALIGN), MOE_CH)
        gu = jnp.dot(xs[rows, :], wgu_ref[0], preferred_element_type=F32)
        gt, up = gu[:, :EXPERT_FF], gu[:, EXPERT_FF:]
        act = (gt * jax.nn.sigmoid(gt)) * up
        w = wrow[rows, :]
        act = jnp.concatenate([act[:, :128] * w, act[:, 128:] * w], axis=1)
        ys[rows, :] = jnp.dot(act.astype(BF16), wd_ref[0], preferred_element_type=F32).astype(BF16)
        return carry

    lax.fori_loop(0, nch, chunk, 0)

    @pl.when(e == N_EXPERTS - 1)
    def _():
        x2 = x1_ref[...] + jnp.dot(pt[...], ys[0:MOE_R, :], preferred_element_type=F32)
        y_ref[...] = x2 * lax.rsqrt(jnp.mean(x2 * x2, axis=-1, keepdims=True) + EPS) * nf_ref[...]


def _moe(hm, gate, gatet, x1, wgu, wd, norm_final):
    T = hm.shape[0]
    tb = MOE_TB
    nb = T // tb
    cnt = jnp.sum((gate.reshape(nb, tb, 128)[:, :, :N_EXPERTS] > 0.0).astype(jnp.int32), axis=1)
    padded = (cnt + (MOE_ALIGN - 1)) // MOE_ALIGN * MOE_ALIGN
    off = jnp.cumsum(padded, axis=1) - padded
    meta = jnp.concatenate([off, (padded + (MOE_CH - 1)) // MOE_CH], axis=1).astype(jnp.int32)
    grid_spec = pltpu.PrefetchScalarGridSpec(
        num_scalar_prefetch=1,
        grid=(nb, N_EXPERTS),
        in_specs=[
            pl.BlockSpec((tb, D_MODEL), lambda i, e, m: (i, 0)),
            pl.BlockSpec((tb, 128), lambda i, e, m: (i, 0)),
            pl.BlockSpec((128, tb), lambda i, e, m: (0, i)),
            pl.BlockSpec((tb, D_MODEL), lambda i, e, m: (i, 0)),
            pl.BlockSpec((1, D_MODEL, 2 * EXPERT_FF), lambda i, e, m: (e, 0, 0)),
            pl.BlockSpec((1, EXPERT_FF, D_MODEL), lambda i, e, m: (e, 0, 0)),
            pl.BlockSpec((1, D_MODEL), lambda i, e, m: (0, 0)),
        ],
        out_specs=pl.BlockSpec((tb, D_MODEL), lambda i, e, m: (i, 0)),
        scratch_shapes=[
            pltpu.VMEM((MOE_RS, D_MODEL), BF16),
            pltpu.VMEM((MOE_RS, D_MODEL), BF16),
            pltpu.VMEM((tb, MOE_R), BF16),
            pltpu.VMEM((MOE_RS, 128), F32),
            pltpu.VMEM((8, tb), F32),
        ],
    )
    return pl.pallas_call(
        _moe_kernel,
        grid_spec=grid_spec,
        out_shape=jax.ShapeDtypeStruct((T, D_MODEL), F32),
        compiler_params=_cparams(("parallel", "arbitrary")),
        name="moe",
    )(meta, hm, gate, gatet, x1, wgu, wd, norm_final)


def _prep_weights(norm_mix, w_in, rel_bias, w_gate_f, b_gate_f, w_gate_b, b_gate_b, gla_norm, w_proj_a,
                  w_proj_b, w_out, norm_moe, w_rg, b_rg, w_re, b_re, w_eg, w_eu, w_ed, norm_final):
    splits = (A_QKV, A_QKV, A_QKV, 512, 512, 1024, 1024, GATE_RANK, GATE_RANK, D_MODEL, D_MODEL)
    qa, ka, va, qb, kb, vb, og, gf, gb, ma, mb = jnp.split(w_in, np.cumsum(splits)[:-1].tolist(), axis=1)
    gpad = jnp.zeros((D_MODEL, PR - C_G - 2 * GATE_RANK), w_in.dtype)
    w_all = jnp.concatenate([qa, ka, va, ma, mb, qb, kb, vb, og, gf, gb, gpad], axis=1).astype(BF16)
    kw = GLA_H * GLA_DK
    wgf = jnp.zeros((128, kw), F32).at[:GATE_RANK].set(w_gate_f).astype(BF16)
    wgb = jnp.zeros((128, kw), F32).at[GATE_RANK:2 * GATE_RANK].set(w_gate_b).astype(BF16)
    wr = jnp.zeros((D_MODEL, 128), F32)
    wr = wr.at[:, :N_EXPERTS].set(w_re).at[:, N_EXPERTS:2 * N_EXPERTS].set(jnp.repeat(w_rg, EPG, axis=1))
    br = jnp.zeros((1, 128), F32)
    br = br.at[0, :N_EXPERTS].set(b_re).at[0, N_EXPERTS:2 * N_EXPERTS].set(jnp.repeat(b_rg, EPG))
    expand = (np.arange(128)[:, None] == 32 * (np.arange(A_GW)[None, :] // A_DH)).astype(np.float32)
    return dict(
        norm_mix=norm_mix.reshape(1, D_MODEL), w_all=w_all, rel_bias=rel_bias,
        wgf=wgf, wgb=wgb, bgf=b_gate_f.reshape(1, kw), bgb=b_gate_b.reshape(1, kw),
        gla_norm=gla_norm.reshape(1, GLA_DV),
        wpa=w_proj_a.astype(BF16), wpb=w_proj_b.astype(BF16), wo=w_out.astype(BF16),
        norm_moe=norm_moe.reshape(1, D_MODEL), wr=wr, br=br, expand=jnp.asarray(expand),
        wgu=jnp.concatenate([w_eg, w_eu], axis=-1).astype(BF16), wd=w_ed.astype(BF16),
        norm_final=norm_final.reshape(1, D_MODEL),
    )


def _trunk(x, w):
    B, S, _ = x.shape
    T = B * S
    x2d = x.reshape(T, D_MODEL)
    proj_a, proj = _inproj(x2d, w["norm_mix"], w["w_all"])
    o_list, lse_list = [], []
    for g, (_, d) in enumerate(DIL_PAIRS):
        L = S // d
        tiles = _bias_tiles(w["rel_bias"], g, d, min(2 * Q_BLK, L))
        o, lse = _attn_group(proj_a, tiles, g, d, B, S)
        o_list.append(o.reshape(T, A_GW))
        lse_list.append(lse.reshape(T, 128))
    ob = _gla(proj, w["wgf"], w["wgb"], w["bgf"], w["bgb"], w["gla_norm"], B, S)
    x1, hm, gate, gatet = _outproj(x2d, proj, o_list, lse_list, ob.reshape(T, D_MODEL), w["wpa"], w["wpb"],
                                   w["wo"], w["norm_moe"], w["wr"], w["br"], w["expand"])
    y = _moe(hm, gate, gatet, x1, w["wgu"], w["wd"], w["norm_final"])
    return y.reshape(B, S, D_MODEL)


def kernel(x_prompt, x_sample, norm_mix, w_in, rel_bias, w_gate_f, b_gate_f, w_gate_b, b_gate_b, gla_norm,
           w_proj_a, w_proj_b, w_out, norm_moe, w_router_group, b_router_group, w_router_expert,
           b_router_expert, w_exp_gate, w_exp_up, w_exp_down, norm_final):
    w = _prep_weights(norm_mix[0], w_in[0], rel_bias, w_gate_f[0], b_gate_f[0], w_gate_b[0], b_gate_b[0],
                      gla_norm[0], w_proj_a[0], w_proj_b[0], w_out[0], norm_moe[0], w_router_group[0],
                      b_router_group[0], w_router_expert[0], b_router_expert[0], w_exp_gate[0], w_exp_up[0],
                      w_exp_down[0], norm_final)
    return (_trunk(x_prompt, w), _trunk(x_sample, w))
```

```python
import functools
import math

import numpy as np
import jax
import jax.numpy as jnp
from jax import lax
from jax.experimental import pallas as pl
from jax.experimental.pallas import tpu as pltpu

F32 = jnp.float32
BF16 = jnp.bfloat16
HIGHEST = lax.Precision.HIGHEST

D_MODEL = 1024
EPS = 1e-6
NEG_INF = -1e30

DIL_PAIRS = ((128, 1), (512, 4), (2048, 16))
A_HPG = 4
A_DH = 64
A_GW = A_HPG * A_DH
A_QKV = 3 * A_GW
HALF_WIN = 64
Q_BLK = 128
N_REL_BUCKETS = 32
REL_MAX_DISTANCE = 1024
GLA_H = 4
GLA_DK = 128
GLA_DV = 256
GATE_RANK = 16
GATE_TAU = 16.0
GLA_CHUNK = 64
N_GROUPS = 4
EPG = 8
N_EXPERTS = 32
EXPERT_FF = 256

PA = 3 * A_QKV
C_MA, C_MB = 0, 1024
C_QB, C_KB, C_VB, C_OG = 2048, 2560, 3072, 4096
C_G = 5120
PR = 5376
PROJ_TN = 768

VMEM_LIMIT = 58 * 1024 * 1024


def _cparams(sem):
    return pltpu.CompilerParams(dimension_semantics=sem, vmem_limit_bytes=VMEM_LIMIT)


def _inproj_kernel(x_ref, g_ref, w_ref, oa_ref, or_ref, h_scr):
    j = pl.program_id(1)

    @pl.when(j == 0)
    def _():
        x = x_ref[...]
        r = x * lax.rsqrt(jnp.mean(x * x, axis=-1, keepdims=True) + EPS)
        h_scr[...] = (r * g_ref[...]).astype(BF16)

    y = jnp.dot(h_scr[...], w_ref[...], preferred_element_type=F32).astype(BF16)

    @pl.when(j < PA // PROJ_TN)
    def _():
        oa_ref[...] = y

    @pl.when(j >= PA // PROJ_TN)
    def _():
        or_ref[...] = y


def _inproj(x2d, norm_g, w_all, tm=1024):
    T = x2d.shape[0]
    na = PA // PROJ_TN
    return pl.pallas_call(
        _inproj_kernel,
        grid=(T // tm, (PA + PR) // PROJ_TN),
        in_specs=[
            pl.BlockSpec((tm, D_MODEL), lambda i, j: (i, 0)),
            pl.BlockSpec((1, D_MODEL), lambda i, j: (0, 0)),
            pl.BlockSpec((D_MODEL, PROJ_TN), lambda i, j: (0, j)),
        ],
        out_specs=[
            pl.BlockSpec((tm, PROJ_TN), lambda i, j: (i, jnp.minimum(j, na - 1))),
            pl.BlockSpec((tm, PROJ_TN), lambda i, j: (i, jnp.maximum(j - na, 0))),
        ],
        out_shape=[jax.ShapeDtypeStruct((T, PA), BF16), jax.ShapeDtypeStruct((T, PR), BF16)],
        scratch_shapes=[pltpu.VMEM((tm, D_MODEL), BF16)],
        compiler_params=_cparams(("parallel", "arbitrary")),
        name="inproj",
    )(x2d, norm_g, w_all)


def _attn_kernel(q_ref, k_ref, v_ref, bias_ref, o_ref, lse_ref, *, L, wk):
    i = pl.program_id(2)
    start = pl.multiple_of(jnp.clip(i * Q_BLK - HALF_WIN, 0, L - wk), HALF_WIN)
    q = q_ref[0]
    kw = k_ref[0, pl.ds(start, wk), :]
    vw = v_ref[0, pl.ds(start, wk), :]
    scale = A_DH ** -0.5
    lane = lax.broadcasted_iota(jnp.int32, (Q_BLK, 128), 1)
    outs = []
    lse_tile = jnp.zeros((Q_BLK, 128), F32)
    for h in range(A_HPG):
        sl = slice(h * A_DH, (h + 1) * A_DH)
        s = lax.dot_general(q[:, sl], kw[:, sl], (((1,), (1,)), ((), ())), preferred_element_type=F32)
        s = s * scale + bias_ref[0, h]
        m = jnp.max(s, axis=-1, keepdims=True)
        p = jnp.exp(s - m)
        l = jnp.sum(p, axis=-1, keepdims=True)
        o = jnp.dot(p.astype(BF16), vw[:, sl], preferred_element_type=F32) / l
        outs.append(o)
        lse = m + jnp.log(l)
        lse_tile = jnp.where((lane >= 32 * h) & (lane < 32 * (h + 1)), lse, lse_tile)
    o_ref[0] = jnp.concatenate(outs, axis=1).astype(BF16)
    lse_ref[0] = lse_tile


def _attn_group(proj_a, bias_tiles, g, d, B, S):
    L = S // d
    nblk = L // Q_BLK
    wk = min(2 * Q_BLK, L)
    pv = proj_a.reshape(B, L, d * PA)
    nc = PA // A_GW
    cq, ck, cv = g, 3 + g, 6 + g

    def variant(i):
        return jnp.where(i == 0, 0, jnp.where(i == nblk - 1, 2, 1))

    return pl.pallas_call(
        functools.partial(_attn_kernel, L=L, wk=wk),
        grid=(B, d, nblk),
        in_specs=[
            pl.BlockSpec((1, Q_BLK, A_GW), lambda b, r, i: (b, i, r * nc + cq)),
            pl.BlockSpec((1, L, A_GW), lambda b, r, i: (b, 0, r * nc + ck)),
            pl.BlockSpec((1, L, A_GW), lambda b, r, i: (b, 0, r * nc + cv)),
            pl.BlockSpec((1, A_HPG, Q_BLK, wk), lambda b, r, i: (variant(i), 0, 0, 0)),
        ],
        out_specs=[
            pl.BlockSpec((1, Q_BLK, A_GW), lambda b, r, i: (b, i, r)),
            pl.BlockSpec((1, Q_BLK, 128), lambda b, r, i: (b, i, r)),
        ],
        out_shape=[
            jax.ShapeDtypeStruct((B, L, d * A_GW), BF16),
            jax.ShapeDtypeStruct((B, L, d * 128), F32),
        ],
        compiler_params=_cparams(("parallel", "parallel", "arbitrary")),
        name=f"attn_d{d}",
    )(pv, pv, pv, bias_tiles)


def _t5_buckets(rel):
    nb = N_REL_BUCKETS // 2
    max_exact = nb // 2
    ret = (rel > 0).astype(np.int64) * nb
    n = np.abs(rel)
    large = max_exact + (np.log(np.maximum(n, 1) / max_exact) / math.log(REL_MAX_DISTANCE / max_exact)
                         * (nb - max_exact)).astype(np.int64)
    large = np.minimum(large, nb - 1)
    return (ret + np.where(n < max_exact, n, large)).astype(np.int32)


def _bias_tiles(rel_bias, g, d, wk):
    rel = d * np.arange(-HALF_WIN, HALF_WIN + 1)
    bucket = _t5_buckets(rel)
    bias = rel_bias[jnp.asarray(bucket)][:, g * A_HPG:(g + 1) * A_HPG].T.astype(F32)
    pad = wk + Q_BLK
    neg = jnp.full((A_HPG, pad), NEG_INF, F32)
    val = jnp.concatenate([neg, bias, neg], axis=1)
    zero = pad + HALF_WIN
    m = wk + Q_BLK
    tiles = []
    for off in (0, HALF_WIN, wk - Q_BLK):
        u = jnp.concatenate([val[:, zero - off:zero - off + wk], val[:, zero - off - Q_BLK:zero - off]], axis=1)
        flat = jnp.tile(u, (1, Q_BLK))[:, :Q_BLK * (m - 1)]
        tiles.append(flat.reshape(A_HPG, Q_BLK, m - 1)[:, :, :wk])
    return jnp.stack(tiles)


def _log_sigmoid(z):
    return jnp.minimum(z, 0.0) - jnp.log1p(jnp.exp(-jnp.abs(z)))


def _gla_kernel(q_ref, k_ref, v_ref, og_ref, g_ref, wgf_ref, wgb_ref, bgf_ref, bgb_ref, gn_ref, o_ref,
                qif, kif, kof, qib, kib, kob, vt, etf, etb, of_scr, *, S):
    PB = 256
    C = GLA_CHUNK
    rr = lax.broadcasted_iota(jnp.int32, (PB, PB), 0)
    cc = lax.broadcasted_iota(jnp.int32, (PB, PB), 1)
    same = (rr >> 6) == (cc >> 6)
    tl = jnp.where(same & (cc <= rr), 1.0, 0.0).astype(F32)
    ones_bd = jnp.where(same, 1.0, 0.0).astype(F32)
    qscale = GLA_DK ** -0.5
    inv_tau = 1.0 / GATE_TAU

    def prep(i, carry):
        rows = pl.ds(pl.multiple_of(i * PB, PB), PB)
        g = g_ref[0, rows, :]
        q = q_ref[0, rows, :].astype(F32) * qscale
        k = k_ref[0, rows, :].astype(F32)
        zf = jnp.dot(g, wgf_ref[...], preferred_element_type=F32) + bgf_ref[...]
        lf = _log_sigmoid(zf) * inv_tau
        bf = jnp.dot(tl, lf, precision=HIGHEST, preferred_element_type=F32)
        totf = jnp.dot(ones_bd, lf, precision=HIGHEST, preferred_element_type=F32)
        qif[rows, :] = (q * jnp.exp(bf)).astype(BF16)
        kif[rows, :] = (k * jnp.exp(-bf)).astype(BF16)
        kof[rows, :] = (k * jnp.exp(totf - bf)).astype(BF16)
        etf[rows, :] = jnp.exp(totf)
        zb = jnp.dot(g, wgb_ref[...], preferred_element_type=F32) + bgb_ref[...]
        lb = _log_sigmoid(zb) * inv_tau
        pb = jnp.dot(tl, lb, precision=HIGHEST, preferred_element_type=F32)
        totb = jnp.dot(ones_bd, lb, precision=HIGHEST, preferred_element_type=F32)
        bs = totb - pb + lb
        qib[rows, :] = (q * jnp.exp(bs)).astype(BF16)
        kib[rows, :] = (k * jnp.exp(-bs)).astype(BF16)
        kob[rows, :] = (k * jnp.exp(totb - bs)).astype(BF16)
        etb[rows, :] = jnp.exp(totb)
        vtb = v_ref[0, rows, :].astype(F32).T.astype(BF16)
        vt[2 * i] = vtb[:, :128]
        vt[2 * i + 1] = vtb[:, 128:]
        return carry

    lax.fori_loop(0, S // PB, prep, 0)

    r64 = lax.broadcasted_iota(jnp.int32, (C, C), 0)
    c64 = lax.broadcasted_iota(jnp.int32, (C, C), 1)
    mask_f = c64 <= r64
    mask_b = c64 > r64
    zeros_half = jnp.zeros((C, GLA_DK), BF16)
    nt = (((1,), (1,)), ((), ()))

    def chunk_step(qi_ref, ki_ref, ko_ref, et_ref, mask, p, half, state):
        r0 = pl.multiple_of(p * (2 * C) + half * C, C)
        rows = pl.ds(r0, C)
        qi = qi_ref[rows, :]
        ki = ki_ref[rows, :]
        ko = ko_ref[rows, :]
        v = v_ref[0, rows, :]
        sc = lax.dot_general(qi, ki, nt, preferred_element_type=F32)
        sc = jnp.where(mask, sc, 0.0).astype(BF16)
        o = jnp.dot(sc, v, preferred_element_type=F32)
        o = o + lax.dot_general(qi, state.astype(BF16), nt, preferred_element_type=F32)
        decay = et_ref[pl.ds(r0, 1), :]
        ko_pad = jnp.concatenate([ko, zeros_half] if half == 0 else [zeros_half, ko], axis=0)
        state = state * decay + jnp.dot(vt[p], ko_pad, preferred_element_type=F32)
        return rows, o, state

    def fwd(p, state):
        for half in (0, 1):
            rows, o, state = chunk_step(qif, kif, kof, etf, mask_f, p, half, state)
            of_scr[rows, :] = o
        return state

    lax.fori_loop(0, S // (2 * C), fwd, jnp.zeros((GLA_DV, GLA_DK), F32))

    gn = gn_ref[...]

    def bwd(t, state):
        p = S // (2 * C) - 1 - t
        for half in (1, 0):
            rows, o, state = chunk_step(qib, kib, kob, etb, mask_b, p, half, state)
            tot = of_scr[rows, :] + o
            nrm = tot * lax.rsqrt(jnp.mean(tot * tot, axis=-1, keepdims=True) + EPS) * gn
            og = og_ref[0, rows, :].astype(F32)
            o_ref[0, rows, :] = (nrm * (og * jax.nn.sigmoid(og))).astype(BF16)
        return state

    lax.fori_loop(0, S // (2 * C), bwd, jnp.zeros((GLA_DV, GLA_DK), F32))


def _gla(proj, wgf, wgb, bgf, bgb, gla_norm, B, S):
    pv = proj.reshape(B, S, PR)
    cq, ck, cv, cog, cg = C_QB // GLA_DK, C_KB // GLA_DK, C_VB // GLA_DV, C_OG // GLA_DV, C_G // 128
    return pl.pallas_call(
        functools.partial(_gla_kernel, S=S),
        grid=(B, GLA_H),
        in_specs=[
            pl.BlockSpec((1, S, GLA_DK), lambda b, h: (b, 0, cq + h)),
            pl.BlockSpec((1, S, GLA_DK), lambda b, h: (b, 0, ck + h)),
            pl.BlockSpec((1, S, GLA_DV), lambda b, h: (b, 0, cv + h)),
            pl.BlockSpec((1, S, GLA_DV), lambda b, h: (b, 0, cog + h)),
            pl.BlockSpec((1, S, 128), lambda b, h: (b, 0, cg)),
            pl.BlockSpec((128, GLA_DK), lambda b, h: (0, h)),
            pl.BlockSpec((128, GLA_DK), lambda b, h: (0, h)),
            pl.BlockSpec((1, GLA_DK), lambda b, h: (0, h)),
            pl.BlockSpec((1, GLA_DK), lambda b, h: (0, h)),
            pl.BlockSpec((1, GLA_DV), lambda b, h: (0, 0)),
        ],
        out_specs=pl.BlockSpec((1, S, GLA_DV), lambda b, h: (b, 0, h)),
        out_shape=jax.ShapeDtypeStruct((B, S, GLA_H * GLA_DV), BF16),
        scratch_shapes=[
            pltpu.VMEM((S, GLA_DK), BF16), pltpu.VMEM((S, GLA_DK), BF16), pltpu.VMEM((S, GLA_DK), BF16),
            pltpu.VMEM((S, GLA_DK), BF16), pltpu.VMEM((S, GLA_DK), BF16), pltpu.VMEM((S, GLA_DK), BF16),
            pltpu.VMEM((S // 128, GLA_DV, 128), BF16),
            pltpu.VMEM((S, GLA_DK), F32), pltpu.VMEM((S, GLA_DK), F32),
            pltpu.VMEM((S, GLA_DV), F32),
        ],
        compiler_params=_cparams(("parallel", "arbitrary")),
        name="gla",
    )(pv, pv, pv, pv, pv, wgf, wgb, bgf, bgb, gla_norm)


def _outproj_kernel(x_ref, ma_ref, mb_ref, o0_ref, o1_ref, o2_ref, l0_ref, l1_ref, l2_ref, ob_ref,
                    wpa_ref, wpb_ref, wo_ref, nm_ref, wr_ref, br_ref, exp_ref,
                    x1_ref, hm_ref, gate_ref, gatet_ref):
    l0, l1, l2 = l0_ref[...], l1_ref[...], l2_ref[...]
    m = jnp.maximum(jnp.maximum(l0, l1), l2)
    e0, e1, e2 = jnp.exp(l0 - m), jnp.exp(l1 - m), jnp.exp(l2 - m)
    den = e0 + e1 + e2
    ex = exp_ref[...]
    oa = jnp.zeros(o0_ref.shape, F32)
    for e, o_ref in ((e0, o0_ref), (e1, o1_ref), (e2, o2_ref)):
        w = jnp.dot(e / den, ex, precision=HIGHEST, preferred_element_type=F32)
        oa = oa + w * o_ref[...].astype(F32)
    ya = jnp.dot(oa.astype(BF16), wpa_ref[...], preferred_element_type=F32)
    yb = jnp.dot(ob_ref[...], wpb_ref[...], preferred_element_type=F32)
    mix = jax.nn.sigmoid(ma_ref[...].astype(F32)) * ya + jax.nn.sigmoid(mb_ref[...].astype(F32)) * yb
    x1 = x_ref[...] + jnp.dot(mix.astype(BF16), wo_ref[...], preferred_element_type=F32)
    x1_ref[...] = x1
    hm = x1 * lax.rsqrt(jnp.mean(x1 * x1, axis=-1, keepdims=True) + EPS) * nm_ref[...]
    hm_ref[...] = hm.astype(BF16)

    logits = jnp.dot(hm, wr_ref[...], precision=HIGHEST, preferred_element_type=F32) + br_ref[...]
    tm = logits.shape[0]
    lane = lax.broadcasted_iota(jnp.int32, (tm, 128), 1)
    is_g = (lane >= N_EXPERTS) & (lane < 2 * N_EXPERTS)
    is_e = lane < N_EXPERTS
    grp_of_lane = jnp.where(is_g, (lane - N_EXPERTS) >> 3, lane >> 3)
    gl = jnp.where(is_g, logits, NEG_INF)
    gmax = jnp.max(gl, axis=-1, keepdims=True)
    gsum = jnp.sum(jnp.where(is_g, jnp.exp(gl - gmax), 0.0), axis=-1, keepdims=True) * (1.0 / EPG)
    pg_top = 1.0 / gsum
    g_sel = jnp.min(jnp.where(is_g & (gl == gmax), grp_of_lane, N_GROUPS), axis=-1, keepdims=True)
    in_grp = is_e & (grp_of_lane == g_sel)
    el = jnp.where(in_grp, logits, NEG_INF)
    emax = jnp.max(el, axis=-1, keepdims=True)
    ee = jnp.where(in_grp, jnp.exp(el - emax), 0.0)
    pe = ee / jnp.sum(ee, axis=-1, keepdims=True)
    p1 = jnp.max(pe, axis=-1, keepdims=True)
    i1 = jnp.min(jnp.where(in_grp & (pe == p1), lane, 128), axis=-1, keepdims=True)
    rest = in_grp & (lane != i1)
    pe2 = jnp.where(rest, pe, -1.0)
    p2 = jnp.max(pe2, axis=-1, keepdims=True)
    i2 = jnp.min(jnp.where(rest & (pe2 == p2), lane, 128), axis=-1, keepdims=True)
    psum = p1 + p2
    gate = jnp.where(lane == i1, pg_top * (p1 / psum), jnp.where(lane == i2, pg_top * (p2 / psum), 0.0))
    gate_ref[...] = gate
    gatet_ref[...] = gate.T


def _outproj(x2d, proj, o_list, lse_list, ob2d, wpa, wpb, wo, norm_moe, wr, br, expand, tm=512):
    T = x2d.shape[0]
    row = lambda w: pl.BlockSpec((tm, w), lambda i: (i, 0))
    full = lambda a: pl.BlockSpec(a.shape, lambda i: (0,) * a.ndim)
    return pl.pallas_call(
        _outproj_kernel,
        grid=(T // tm,),
        in_specs=[
            row(D_MODEL),
            pl.BlockSpec((tm, D_MODEL), lambda i: (i, C_MA // D_MODEL)),
            pl.BlockSpec((tm, D_MODEL), lambda i: (i, C_MB // D_MODEL)),
            row(A_GW), row(A_GW), row(A_GW), row(128), row(128), row(128),
            row(D_MODEL),
            full(wpa), full(wpb), full(wo), full(norm_moe), full(wr), full(br), full(expand),
        ],
        out_specs=[row(D_MODEL), row(D_MODEL), row(128), pl.BlockSpec((128, tm), lambda i: (0, i))],
        out_shape=[
            jax.ShapeDtypeStruct((T, D_MODEL), F32),
            jax.ShapeDtypeStruct((T, D_MODEL), BF16),
            jax.ShapeDtypeStruct((T, 128), F32),
            jax.ShapeDtypeStruct((128, T), F32),
        ],
        compiler_params=_cparams(("parallel",)),
        name="outproj",
    )(x2d, proj, proj, *o_list, *lse_list, ob2d, wpa, wpb, wo, norm_moe, wr, br, expand)


MOE_TB = 1024
MOE_ALIGN = 16
MOE_CH = 128
MOE_R = 2 * MOE_TB + N_EXPERTS * MOE_ALIGN
MOE_RS = MOE_R + MOE_CH
MOE_SUB = 256
MOE_G = 4


def _moe_route(hm_ref, gate_ref, gatet_ref, xs, ys, pt, wrow, rinfo):
    TB, R, SUB = MOE_TB, MOE_R, MOE_SUB
    big = float(4 * TB)
    gate = gate_ref[...]
    gt = gatet_ref[...]
    a = gate > 0.0
    at = gt > 0.0
    a_b = jnp.where(a, 1.0, 0.0).astype(BF16)
    at_b = jnp.where(at, 1.0, 0.0).astype(BF16)

    def pad(c):
        return jnp.floor((c + (MOE_ALIGN - 1)) * (1.0 / MOE_ALIGN)) * MOE_ALIGN

    er = lax.broadcasted_iota(jnp.int32, (128, 128), 0)
    ec = lax.broadcasted_iota(jnp.int32, (128, 128), 1)
    pad_row = pad(jnp.sum(jnp.where(a, 1.0, 0.0), axis=0, keepdims=True))
    off_row = jnp.dot(jnp.broadcast_to(pad_row, (8, 128)).astype(BF16), jnp.where(er < ec, 1.0, 0.0).astype(BF16),
                      preferred_element_type=F32)[0:1]
    pad_col = pad(jnp.sum(jnp.where(at, 1.0, 0.0), axis=1, keepdims=True))
    off_col = jnp.dot(jnp.where(ec < er, 1.0, 0.0).astype(BF16), jnp.broadcast_to(pad_col, (128, 128)).astype(BF16),
                      preferred_element_type=F32)[:, 0:1]

    rl = lax.broadcasted_iota(jnp.int32, (SUB, R), 1).astype(F32)
    for rc in range(TB // SUB):
        rows = slice(rc * SUB, (rc + 1) * SUB)
        tr = lax.broadcasted_iota(jnp.int32, (SUB, TB), 0) + rc * SUB
        tc = lax.broadcasted_iota(jnp.int32, (SUB, TB), 1)
        rank = jnp.dot(jnp.where(tc < tr, 1.0, 0.0).astype(BF16), a_b, preferred_element_type=F32)
        pos = off_row + rank
        a_c = a[rows]
        plo = jnp.min(jnp.where(a_c, pos, big), axis=1, keepdims=True)
        phi = jnp.max(jnp.where(a_c, pos, -1.0), axis=1, keepdims=True)
        phi = jnp.where(phi == plo, -1.0, phi)
        pt[rows, :] = jnp.where((rl == plo) | (rl == phi), 1.0, 0.0).astype(BF16)

    for cc in range(TB // SUB):
        cols = slice(cc * SUB, (cc + 1) * SUB)
        tr = lax.broadcasted_iota(jnp.int32, (TB, SUB), 0)
        tc = lax.broadcasted_iota(jnp.int32, (TB, SUB), 1) + cc * SUB
        rank_t = jnp.dot(at_b, jnp.where(tr < tc, 1.0, 0.0).astype(BF16), preferred_element_type=F32)
        pos_t = off_col + rank_t
        at_c = at[:, cols]
        gt_c = gt[:, cols]
        plo = jnp.min(jnp.where(at_c, pos_t, big), axis=0, keepdims=True)
        phi = jnp.max(jnp.where(at_c, pos_t, -1.0), axis=0, keepdims=True)
        phi = jnp.where(phi == plo, -1.0, phi)
        rinfo[0:1, cols] = plo
        rinfo[1:2, cols] = phi
        rinfo[2:3, cols] = jnp.sum(jnp.where(at_c & (pos_t == plo), gt_c, 0.0), axis=0, keepdims=True)
        rinfo[3:4, cols] = jnp.sum(jnp.where(at_c & (pos_t == phi), gt_c, 0.0), axis=0, keepdims=True)

    def gather(k, carry):
        r0 = pl.multiple_of(k * SUB, SUB)
        ri = (lax.broadcasted_iota(jnp.int32, (SUB, TB), 0) + r0).astype(F32)
        mlo = ri == rinfo[0:1, :]
        mhi = ri == rinfo[1:2, :]
        p = jnp.where(mlo | mhi, 1.0, 0.0).astype(BF16)
        xs[pl.ds(r0, SUB), :] = jnp.dot(p, hm_ref[...], preferred_element_type=F32).astype(BF16)
        w = jnp.sum(jnp.where(mlo, rinfo[2:3, :], 0.0) + jnp.where(mhi, rinfo[3:4, :], 0.0), axis=1, keepdims=True)
        wrow[pl.ds(r0, SUB), :] = jnp.broadcast_to(w, (SUB, 128))
        return carry

    lax.fori_loop(0, R // SUB, gather, 0)
    xs[R:, :] = jnp.zeros((MOE_RS - R, D_MODEL), BF16)
    wrow[R:, :] = jnp.zeros((MOE_RS - R, 128), F32)
    ys[...] = jnp.zeros_like(ys)


def _moe_kernel(meta_ref, hm_ref, gate_ref, gatet_ref, x1_ref, wgu_ref, wd_ref, nf_ref, y_ref,
                xs, ys, pt, wrow, rinfo):
    i = pl.program_id(0)
    s = pl.program_id(1)

    @pl.when(s == 0)
    def _():
        _moe_route(hm_ref, gate_ref, gatet_ref, xs, ys, pt, wrow, rinfo)

    offs = [meta_ref[i, s * MOE_G + g] for g in range(MOE_G)]
    nchs = [meta_ref[i, N_EXPERTS + s * MOE_G + g] for g in range(MOE_G)]

    def chunk(g, r0):
        rows = pl.ds(pl.multiple_of(r0, MOE_ALIGN), MOE_CH)
        gu = jnp.dot(xs[rows, :], wgu_ref[g], preferred_element_type=F32)
        gt, up = gu[:, :EXPERT_FF], gu[:, EXPERT_FF:]
        act = (gt * jax.nn.sigmoid(gt)) * up
        w = wrow[rows, :]
        act = jnp.concatenate([act[:, :128] * w, act[:, 128:] * w], axis=1)
        ys[rows, :] = jnp.dot(act.astype(BF16), wd_ref[g], preferred_element_type=F32).astype(BF16)

    single = nchs[0] <= 1
    for g in range(1, MOE_G):
        single = jnp.logical_and(single, nchs[g] <= 1)

    @pl.when(single)
    def _():
        for g in range(MOE_G):
            chunk(g, offs[g])

    @pl.when(jnp.logical_not(single))
    def _():
        for g in range(MOE_G):
            def body(c, carry, g=g):
                chunk(g, offs[g] + c * MOE_CH)
                return carry
            lax.fori_loop(0, nchs[g], body, 0)

    @pl.when(s == N_EXPERTS // MOE_G - 1)
    def _():
        x2 = x1_ref[...] + jnp.dot(pt[...], ys[0:MOE_R, :], preferred_element_type=F32)
        y_ref[...] = x2 * lax.rsqrt(jnp.mean(x2 * x2, axis=-1, keepdims=True) + EPS) * nf_ref[...]


def _moe(hm, gate, gatet, x1, wgu, wd, norm_final):
    T = hm.shape[0]
    tb = MOE_TB
    nb = T // tb
    cnt = jnp.sum((gate.reshape(nb, tb, 128)[:, :, :N_EXPERTS] > 0.0).astype(jnp.int32), axis=1)
    padded = (cnt + (MOE_ALIGN - 1)) // MOE_ALIGN * MOE_ALIGN
    off = jnp.cumsum(padded, axis=1) - padded
    meta = jnp.concatenate([off, (padded + (MOE_CH - 1)) // MOE_CH], axis=1).astype(jnp.int32)
    grid_spec = pltpu.PrefetchScalarGridSpec(
        num_scalar_prefetch=1,
        grid=(nb, N_EXPERTS // MOE_G),
        in_specs=[
            pl.BlockSpec((tb, D_MODEL), lambda i, e, m: (i, 0)),
            pl.BlockSpec((tb, 128), lambda i, e, m: (i, 0)),
            pl.BlockSpec((128, tb), lambda i, e, m: (0, i)),
            pl.BlockSpec((tb, D_MODEL), lambda i, e, m: (i, 0)),
            pl.BlockSpec((MOE_G, D_MODEL, 2 * EXPERT_FF), lambda i, e, m: (e, 0, 0)),
            pl.BlockSpec((MOE_G, EXPERT_FF, D_MODEL), lambda i, e, m: (e, 0, 0)),
            pl.BlockSpec((1, D_MODEL), lambda i, e, m: (0, 0)),
        ],
        out_specs=pl.BlockSpec((tb, D_MODEL), lambda i, e, m: (i, 0)),
        scratch_shapes=[
            pltpu.VMEM((MOE_RS, D_MODEL), BF16),
            pltpu.VMEM((MOE_RS, D_MODEL), BF16),
            pltpu.VMEM((tb, MOE_R), BF16),
            pltpu.VMEM((MOE_RS, 128), F32),
            pltpu.VMEM((8, tb), F32),
        ],
    )
    return pl.pallas_call(
        _moe_kernel,
        grid_spec=grid_spec,
        out_shape=jax.ShapeDtypeStruct((T, D_MODEL), F32),
        compiler_params=_cparams(("parallel", "arbitrary")),
        name="moe",
    )(meta, hm, gate, gatet, x1, wgu, wd, norm_final)


def _prep_weights(norm_mix, w_in, rel_bias, w_gate_f, b_gate_f, w_gate_b, b_gate_b, gla_norm, w_proj_a,
                  w_proj_b, w_out, norm_moe, w_rg, b_rg, w_re, b_re, w_eg, w_eu, w_ed, norm_final):
    splits = (A_QKV, A_QKV, A_QKV, 512, 512, 1024, 1024, GATE_RANK, GATE_RANK, D_MODEL, D_MODEL)
    qa, ka, va, qb, kb, vb, og, gf, gb, ma, mb = jnp.split(w_in, np.cumsum(splits)[:-1].tolist(), axis=1)
    gpad = jnp.zeros((D_MODEL, PR - C_G - 2 * GATE_RANK), w_in.dtype)
    w_all = jnp.concatenate([qa, ka, va, ma, mb, qb, kb, vb, og, gf, gb, gpad], axis=1).astype(BF16)
    kw = GLA_H * GLA_DK
    wgf = jnp.zeros((128, kw), F32).at[:GATE_RANK].set(w_gate_f).astype(BF16)
    wgb = jnp.zeros((128, kw), F32).at[GATE_RANK:2 * GATE_RANK].set(w_gate_b).astype(BF16)
    wr = jnp.zeros((D_MODEL, 128), F32)
    wr = wr.at[:, :N_EXPERTS].set(w_re).at[:, N_EXPERTS:2 * N_EXPERTS].set(jnp.repeat(w_rg, EPG, axis=1))
    br = jnp.zeros((1, 128), F32)
    br = br.at[0, :N_EXPERTS].set(b_re).at[0, N_EXPERTS:2 * N_EXPERTS].set(jnp.repeat(b_rg, EPG))
    expand = (np.arange(128)[:, None] == 32 * (np.arange(A_GW)[None, :] // A_DH)).astype(np.float32)
    return dict(
        norm_mix=norm_mix.reshape(1, D_MODEL), w_all=w_all, rel_bias=rel_bias,
        wgf=wgf, wgb=wgb, bgf=b_gate_f.reshape(1, kw), bgb=b_gate_b.reshape(1, kw),
        gla_norm=gla_norm.reshape(1, GLA_DV),
        wpa=w_proj_a.astype(BF16), wpb=w_proj_b.astype(BF16), wo=w_out.astype(BF16),
        norm_moe=norm_moe.reshape(1, D_MODEL), wr=wr, br=br, expand=jnp.asarray(expand),
        wgu=jnp.concatenate([w_eg, w_eu], axis=-1).astype(BF16), wd=w_ed.astype(BF16),
        norm_final=norm_final.reshape(1, D_MODEL),
    )


def _trunk(x, w):
    B, S, _ = x.shape
    T = B * S
    x2d = x.reshape(T, D_MODEL)
    proj_a, proj = _inproj(x2d, w["norm_mix"], w["w_all"])
    o_list, lse_list = [], []
    for g, (_, d) in enumerate(DIL_PAIRS):
        L = S // d
        tiles = _bias_tiles(w["rel_bias"], g, d, min(2 * Q_BLK, L))
        o, lse = _attn_group(proj_a, tiles, g, d, B, S)
        o_list.append(o.reshape(T, A_GW))
        lse_list.append(lse.reshape(T, 128))
    ob = _gla(proj, w["wgf"], w["wgb"], w["bgf"], w["bgb"], w["gla_norm"], B, S)
    x1, hm, gate, gatet = _outproj(x2d, proj, o_list, lse_list, ob.reshape(T, D_MODEL), w["wpa"], w["wpb"],
                                   w["wo"], w["norm_moe"], w["wr"], w["br"], w["expand"])
    y = _moe(hm, gate, gatet, x1, w["wgu"], w["wd"], w["norm_final"])
    return y.reshape(B, S, D_MODEL)


def kernel(x_prompt, x_sample, norm_mix, w_in, rel_bias, w_gate_f, b_gate_f, w_gate_b, b_gate_b, gla_norm,
           w_proj_a, w_proj_b, w_out, norm_moe, w_router_group, b_router_group, w_router_expert,
           b_router_expert, w_exp_gate, w_exp_up, w_exp_down, norm_final):
    w = _prep_weights(norm_mix[0], w_in[0], rel_bias, w_gate_f[0], b_gate_f[0], w_gate_b[0], b_gate_b[0],
                      gla_norm[0], w_proj_a[0], w_proj_b[0], w_out[0], norm_moe[0], w_router_group[0],
                      b_router_group[0], w_router_expert[0], b_router_expert[0], w_exp_gate[0], w_exp_up[0],
                      w_exp_down[0], norm_final)
    return (_trunk(x_prompt, w), _trunk(x_sample, w))
```

```python
import functools
import math

import numpy as np
import jax
import jax.numpy as jnp
from jax import lax
from jax.experimental import pallas as pl
from jax.experimental.pallas import tpu as pltpu

F32 = jnp.float32
BF16 = jnp.bfloat16
HIGHEST = lax.Precision.HIGHEST

D_MODEL = 1024
EPS = 1e-6
NEG_INF = -1e30

DIL_PAIRS = ((128, 1), (512, 4), (2048, 16))
A_HPG = 4
A_DH = 64
A_GW = A_HPG * A_DH
A_QKV = 3 * A_GW
HALF_WIN = 64
Q_BLK = 128
N_REL_BUCKETS = 32
REL_MAX_DISTANCE = 1024
GLA_H = 4
GLA_DK = 128
GLA_DV = 256
GATE_RANK = 16
GATE_TAU = 16.0
GLA_CHUNK = 64
N_GROUPS = 4
EPG = 8
N_EXPERTS = 32
EXPERT_FF = 256

PA = 3 * A_QKV
C_MA, C_MB = 0, 1024
C_QB, C_KB, C_VB, C_OG = 2048, 2560, 3072, 4096
C_G = 5120
PR = 5376
PROJ_TN = 768

VMEM_LIMIT = 58 * 1024 * 1024


def _cparams(sem):
    return pltpu.CompilerParams(dimension_semantics=sem, vmem_limit_bytes=VMEM_LIMIT)


def _inproj_kernel(x_ref, g_ref, w_ref, oa_ref, or_ref, h_scr):
    j = pl.program_id(1)

    @pl.when(j == 0)
    def _():
        x = x_ref[...]
        r = x * lax.rsqrt(jnp.mean(x * x, axis=-1, keepdims=True) + EPS)
        h_scr[...] = (r * g_ref[...]).astype(BF16)

    y = jnp.dot(h_scr[...], w_ref[...], preferred_element_type=F32).astype(BF16)

    @pl.when(j < PA // PROJ_TN)
    def _():
        oa_ref[...] = y

    @pl.when(j >= PA // PROJ_TN)
    def _():
        or_ref[...] = y


def _inproj(x2d, norm_g, w_all, tm=1024):
    T = x2d.shape[0]
    na = PA // PROJ_TN
    return pl.pallas_call(
        _inproj_kernel,
        grid=(T // tm, (PA + PR) // PROJ_TN),
        in_specs=[
            pl.BlockSpec((tm, D_MODEL), lambda i, j: (i, 0)),
            pl.BlockSpec((1, D_MODEL), lambda i, j: (0, 0)),
            pl.BlockSpec((D_MODEL, PROJ_TN), lambda i, j: (0, j)),
        ],
        out_specs=[
            pl.BlockSpec((tm, PROJ_TN), lambda i, j: (i, jnp.minimum(j, na - 1))),
            pl.BlockSpec((tm, PROJ_TN), lambda i, j: (i, jnp.maximum(j - na, 0))),
        ],
        out_shape=[jax.ShapeDtypeStruct((T, PA), BF16), jax.ShapeDtypeStruct((T, PR), BF16)],
        scratch_shapes=[pltpu.VMEM((tm, D_MODEL), BF16)],
        compiler_params=_cparams(("parallel", "arbitrary")),
        name="inproj",
    )(x2d, norm_g, w_all)


def _attn_kernel(q_ref, k_ref, v_ref, bias_ref, o_ref, lse_ref, *, L, wk):
    i = pl.program_id(2)
    start = pl.multiple_of(jnp.clip(i * Q_BLK - HALF_WIN, 0, L - wk), HALF_WIN)
    q = q_ref[0]
    kw = k_ref[0, pl.ds(start, wk), :]
    vw = v_ref[0, pl.ds(start, wk), :]
    scale = A_DH ** -0.5
    lane = lax.broadcasted_iota(jnp.int32, (Q_BLK, 128), 1)
    outs = []
    lse_tile = jnp.zeros((Q_BLK, 128), F32)
    for h in range(A_HPG):
        sl = slice(h * A_DH, (h + 1) * A_DH)
        s = lax.dot_general(q[:, sl], kw[:, sl], (((1,), (1,)), ((), ())), preferred_element_type=F32)
        s = s * scale + bias_ref[0, h]
        m = jnp.max(s, axis=-1, keepdims=True)
        p = jnp.exp(s - m)
        l = jnp.sum(p, axis=-1, keepdims=True)
        o = jnp.dot(p.astype(BF16), vw[:, sl], preferred_element_type=F32) / l
        outs.append(o)
        lse = m + jnp.log(l)
        lse_tile = jnp.where((lane >= 32 * h) & (lane < 32 * (h + 1)), lse, lse_tile)
    o_ref[0] = jnp.concatenate(outs, axis=1).astype(BF16)
    lse_ref[0] = lse_tile


def _attn_group(proj_a, bias_tiles, g, d, B, S):
    L = S // d
    nblk = L // Q_BLK
    wk = min(2 * Q_BLK, L)
    pv = proj_a.reshape(B, L, d * PA)
    nc = PA // A_GW
    cq, ck, cv = g, 3 + g, 6 + g

    def variant(i):
        return jnp.where(i == 0, 0, jnp.where(i == nblk - 1, 2, 1))

    return pl.pallas_call(
        functools.partial(_attn_kernel, L=L, wk=wk),
        grid=(B, d, nblk),
        in_specs=[
            pl.BlockSpec((1, Q_BLK, A_GW), lambda b, r, i: (b, i, r * nc + cq)),
            pl.BlockSpec((1, L, A_GW), lambda b, r, i: (b, 0, r * nc + ck)),
            pl.BlockSpec((1, L, A_GW), lambda b, r, i: (b, 0, r * nc + cv)),
            pl.BlockSpec((1, A_HPG, Q_BLK, wk), lambda b, r, i: (variant(i), 0, 0, 0)),
        ],
        out_specs=[
            pl.BlockSpec((1, Q_BLK, A_GW), lambda b, r, i: (b, i, r)),
            pl.BlockSpec((1, Q_BLK, 128), lambda b, r, i: (b, i, r)),
        ],
        out_shape=[
            jax.ShapeDtypeStruct((B, L, d * A_GW), BF16),
            jax.ShapeDtypeStruct((B, L, d * 128), F32),
        ],
        compiler_params=_cparams(("parallel", "parallel", "arbitrary")),
        name=f"attn_d{d}",
    )(pv, pv, pv, bias_tiles)


def _t5_buckets(rel):
    nb = N_REL_BUCKETS // 2
    max_exact = nb // 2
    ret = (rel > 0).astype(np.int64) * nb
    n = np.abs(rel)
    large = max_exact + (np.log(np.maximum(n, 1) / max_exact) / math.log(REL_MAX_DISTANCE / max_exact)
                         * (nb - max_exact)).astype(np.int64)
    large = np.minimum(large, nb - 1)
    return (ret + np.where(n < max_exact, n, large)).astype(np.int32)


def _bias_tiles(rel_bias, g, d, wk):
    rel = d * np.arange(-HALF_WIN, HALF_WIN + 1)
    bucket = _t5_buckets(rel)
    bias = rel_bias[jnp.asarray(bucket)][:, g * A_HPG:(g + 1) * A_HPG].T.astype(F32)
    pad = wk + Q_BLK
    neg = jnp.full((A_HPG, pad), NEG_INF, F32)
    val = jnp.concatenate([neg, bias, neg], axis=1)
    zero = pad + HALF_WIN
    m = wk + Q_BLK
    tiles = []
    for off in (0, HALF_WIN, wk - Q_BLK):
        u = jnp.concatenate([val[:, zero - off:zero - off + wk], val[:, zero - off - Q_BLK:zero - off]], axis=1)
        flat = jnp.tile(u, (1, Q_BLK))[:, :Q_BLK * (m - 1)]
        tiles.append(flat.reshape(A_HPG, Q_BLK, m - 1)[:, :, :wk])
    return jnp.stack(tiles)


def _log_sigmoid(z):
    return jnp.minimum(z, 0.0) - jnp.log1p(jnp.exp(-jnp.abs(z)))


def _gla_kernel(q_ref, k_ref, v_ref, og_ref, g_ref, wgf_ref, wgb_ref, bgf_ref, bgb_ref, gn_ref, o_ref,
                qif, kof, qib, kob, vt, etf, etb, acc, *, S):
    PB = 256
    C = GLA_CHUNK
    NB = S // PB
    rr = lax.broadcasted_iota(jnp.int32, (PB, PB), 0)
    cc = lax.broadcasted_iota(jnp.int32, (PB, PB), 1)
    same = (rr >> 6) == (cc >> 6)
    mask_f = same & (cc <= rr)
    mask_b = same & (cc > rr)
    tl = jnp.where(mask_f, 1.0, 0.0).astype(BF16)
    qscale = GLA_DK ** -0.5
    inv_tau = 1.0 / GATE_TAU
    nt = (((1,), (1,)), ((), ()))

    def split3(x):
        hi = x.astype(BF16)
        r1 = x - hi.astype(F32)
        mid = r1.astype(BF16)
        lo = (r1 - mid.astype(F32)).astype(BF16)
        return [hi, mid, lo]

    def chunk_total(b):
        b4 = b.reshape(PB // C, C, GLA_DK)
        return jnp.broadcast_to(b4[:, C - 1:C, :], b4.shape).reshape(PB, GLA_DK)

    def prep(i, carry):
        rows = pl.ds(pl.multiple_of(i * PB, PB), PB)
        g = g_ref[0, rows, :]
        q = q_ref[0, rows, :].astype(F32) * qscale
        k = k_ref[0, rows, :].astype(F32)
        v = v_ref[0, rows, :]
        lf = _log_sigmoid(jnp.dot(g, wgf_ref[...], preferred_element_type=F32) + bgf_ref[...]) * inv_tau
        lb = _log_sigmoid(jnp.dot(g, wgb_ref[...], preferred_element_type=F32) + bgb_ref[...]) * inv_tau
        cs = jnp.dot(tl, jnp.concatenate(split3(lf) + split3(lb), axis=1), preferred_element_type=F32)
        bf = cs[:, 0:128] + cs[:, 128:256] + cs[:, 256:384]
        pb = cs[:, 384:512] + cs[:, 512:640] + cs[:, 640:768]
        totf = chunk_total(bf)
        totb = chunk_total(pb)
        bs = totb - pb + lb
        qf = (q * jnp.exp(bf)).astype(BF16)
        qb = (q * jnp.exp(bs)).astype(BF16)
        sf = lax.dot_general(qf, (k * jnp.exp(-bf)).astype(BF16), nt, preferred_element_type=F32)
        sb = lax.dot_general(qb, (k * jnp.exp(-bs)).astype(BF16), nt, preferred_element_type=F32)
        p = jnp.where(mask_f, sf, jnp.where(mask_b, sb, 0.0)).astype(BF16)
        acc[rows, :] = jnp.dot(p, v, preferred_element_type=F32)
        qif[rows, :] = qf
        qib[rows, :] = qb
        kof[rows, :] = (k * jnp.exp(totf - bf)).astype(BF16)
        kob[rows, :] = (k * jnp.exp(totb - bs)).astype(BF16)
        etf[rows, :] = jnp.exp(totf)
        etb[rows, :] = jnp.exp(totb)
        vtb = v.astype(F32).T.astype(BF16)
        vt[2 * i] = vtb[:, :128]
        vt[2 * i + 1] = vtb[:, 128:]
        return carry

    lax.fori_loop(0, NB, prep, 0)

    zeros_half = jnp.zeros((C, GLA_DK), BF16)

    def chunk_step(qi_ref, ko_ref, et_ref, blk, j, state):
        r0 = pl.multiple_of(blk * PB + j * C, C)
        rows = pl.ds(r0, C)
        acc[rows, :] += lax.dot_general(qi_ref[rows, :], state.astype(BF16), nt, preferred_element_type=F32)
        ko = ko_ref[rows, :]
        ko_pad = jnp.concatenate([ko, zeros_half] if j % 2 == 0 else [zeros_half, ko], axis=0)
        upd = jnp.dot(vt[2 * blk + j // 2], ko_pad, preferred_element_type=F32)
        return state * et_ref[pl.ds(r0, 1), :] + upd

    def serial(t, carry):
        st_f, st_b = carry
        for j in range(PB // C):
            st_f = chunk_step(qif, kof, etf, t, j, st_f)
            st_b = chunk_step(qib, kob, etb, NB - 1 - t, PB // C - 1 - j, st_b)
        return st_f, st_b

    zero_state = jnp.zeros((GLA_DV, GLA_DK), F32)
    lax.fori_loop(0, NB, serial, (zero_state, zero_state))

    gn = gn_ref[...]

    def finish(i, carry):
        rows = pl.ds(pl.multiple_of(i * PB, PB), PB)
        tot = acc[rows, :]
        nrm = tot * lax.rsqrt(jnp.mean(tot * tot, axis=-1, keepdims=True) + EPS) * gn
        og = og_ref[0, rows, :].astype(F32)
        o_ref[0, rows, :] = (nrm * (og * jax.nn.sigmoid(og))).astype(BF16)
        return carry

    lax.fori_loop(0, NB, finish, 0)


def _gla(proj, wgf, wgb, bgf, bgb, gla_norm, B, S):
    pv = proj.reshape(B, S, PR)
    cq, ck, cv, cog, cg = C_QB // GLA_DK, C_KB // GLA_DK, C_VB // GLA_DV, C_OG // GLA_DV, C_G // 128
    return pl.pallas_call(
        functools.partial(_gla_kernel, S=S),
        grid=(B, GLA_H),
        in_specs=[
            pl.BlockSpec((1, S, GLA_DK), lambda b, h: (b, 0, cq + h)),
            pl.BlockSpec((1, S, GLA_DK), lambda b, h: (b, 0, ck + h)),
            pl.BlockSpec((1, S, GLA_DV), lambda b, h: (b, 0, cv + h)),
            pl.BlockSpec((1, S, GLA_DV), lambda b, h: (b, 0, cog + h)),
            pl.BlockSpec((1, S, 128), lambda b, h: (b, 0, cg)),
            pl.BlockSpec((128, GLA_DK), lambda b, h: (0, h)),
            pl.BlockSpec((128, GLA_DK), lambda b, h: (0, h)),
            pl.BlockSpec((1, GLA_DK), lambda b, h: (0, h)),
            pl.BlockSpec((1, GLA_DK), lambda b, h: (0, h)),
            pl.BlockSpec((1, GLA_DV), lambda b, h: (0, 0)),
        ],
        out_specs=pl.BlockSpec((1, S, GLA_DV), lambda b, h: (b, 0, h)),
        out_shape=jax.ShapeDtypeStruct((B, S, GLA_H * GLA_DV), BF16),
        scratch_shapes=[
            pltpu.VMEM((S, GLA_DK), BF16), pltpu.VMEM((S, GLA_DK), BF16),
            pltpu.VMEM((S, GLA_DK), BF16), pltpu.VMEM((S, GLA_DK), BF16),
            pltpu.VMEM((S // 128, GLA_DV, 128), BF16),
            pltpu.VMEM((S, GLA_DK), F32), pltpu.VMEM((S, GLA_DK), F32),
            pltpu.VMEM((S, GLA_DV), F32),
        ],
        compiler_params=_cparams(("parallel", "arbitrary")),
        name="gla",
    )(pv, pv, pv, pv, pv, wgf, wgb, bgf, bgb, gla_norm)


def _outproj_kernel(x_ref, ma_ref, mb_ref, o0_ref, o1_ref, o2_ref, l0_ref, l1_ref, l2_ref, ob_ref,
                    wpa_ref, wpb_ref, wo_ref, nm_ref, wr_ref, br_ref, exp_ref,
                    x1_ref, hm_ref, gate_ref, gatet_ref):
    l0, l1, l2 = l0_ref[...], l1_ref[...], l2_ref[...]
    m = jnp.maximum(jnp.maximum(l0, l1), l2)
    e0, e1, e2 = jnp.exp(l0 - m), jnp.exp(l1 - m), jnp.exp(l2 - m)
    den = e0 + e1 + e2
    ex = exp_ref[...]
    oa = jnp.zeros(o0_ref.shape, F32)
    for e, o_ref in ((e0, o0_ref), (e1, o1_ref), (e2, o2_ref)):
        w = jnp.dot(e / den, ex, precision=HIGHEST, preferred_element_type=F32)
        oa = oa + w * o_ref[...].astype(F32)
    ya = jnp.dot(oa.astype(BF16), wpa_ref[...], preferred_element_type=F32)
    yb = jnp.dot(ob_ref[...], wpb_ref[...], preferred_element_type=F32)
    mix = jax.nn.sigmoid(ma_ref[...].astype(F32)) * ya + jax.nn.sigmoid(mb_ref[...].astype(F32)) * yb
    x1 = x_ref[...] + jnp.dot(mix.astype(BF16), wo_ref[...], preferred_element_type=F32)
    x1_ref[...] = x1
    hm = x1 * lax.rsqrt(jnp.mean(x1 * x1, axis=-1, keepdims=True) + EPS) * nm_ref[...]
    hm_ref[...] = hm.astype(BF16)

    logits = jnp.dot(hm, wr_ref[...], precision=HIGHEST, preferred_element_type=F32) + br_ref[...]
    tm = logits.shape[0]
    lane = lax.broadcasted_iota(jnp.int32, (tm, 128), 1)
    is_g = (lane >= N_EXPERTS) & (lane < 2 * N_EXPERTS)
    is_e = lane < N_EXPERTS
    grp_of_lane = jnp.where(is_g, (lane - N_EXPERTS) >> 3, lane >> 3)
    gl = jnp.where(is_g, logits, NEG_INF)
    gmax = jnp.max(gl, axis=-1, keepdims=True)
    gsum = jnp.sum(jnp.where(is_g, jnp.exp(gl - gmax), 0.0), axis=-1, keepdims=True) * (1.0 / EPG)
    pg_top = 1.0 / gsum
    g_sel = jnp.min(jnp.where(is_g & (gl == gmax), grp_of_lane, N_GROUPS), axis=-1, keepdims=True)
    in_grp = is_e & (grp_of_lane == g_sel)
    el = jnp.where(in_grp, logits, NEG_INF)
    emax = jnp.max(el, axis=-1, keepdims=True)
    ee = jnp.where(in_grp, jnp.exp(el - emax), 0.0)
    pe = ee / jnp.sum(ee, axis=-1, keepdims=True)
    p1 = jnp.max(pe, axis=-1, keepdims=True)
    i1 = jnp.min(jnp.where(in_grp & (pe == p1), lane, 128), axis=-1, keepdims=True)
    rest = in_grp & (lane != i1)
    pe2 = jnp.where(rest, pe, -1.0)
    p2 = jnp.max(pe2, axis=-1, keepdims=True)
    i2 = jnp.min(jnp.where(rest & (pe2 == p2), lane, 128), axis=-1, keepdims=True)
    psum = p1 + p2
    gate = jnp.where(lane == i1, pg_top * (p1 / psum), jnp.where(lane == i2, pg_top * (p2 / psum), 0.0))
    gate_ref[...] = gate
    gatet_ref[...] = gate.T


def _outproj(x2d, proj, o_list, lse_list, ob2d, wpa, wpb, wo, norm_moe, wr, br, expand, tm=512):
    T = x2d.shape[0]
    row = lambda w: pl.BlockSpec((tm, w), lambda i: (i, 0))
    full = lambda a: pl.BlockSpec(a.shape, lambda i: (0,) * a.ndim)
    return pl.pallas_call(
        _outproj_kernel,
        grid=(T // tm,),
        in_specs=[
            row(D_MODEL),
            pl.BlockSpec((tm, D_MODEL), lambda i: (i, C_MA // D_MODEL)),
            pl.BlockSpec((tm, D_MODEL), lambda i: (i, C_MB // D_MODEL)),
            row(A_GW), row(A_GW), row(A_GW), row(128), row(128), row(128),
            row(D_MODEL),
            full(wpa), full(wpb), full(wo), full(norm_moe), full(wr), full(br), full(expand),
        ],
        out_specs=[row(D_MODEL), row(D_MODEL), row(128), pl.BlockSpec((128, tm), lambda i: (0, i))],
        out_shape=[
            jax.ShapeDtypeStruct((T, D_MODEL), F32),
            jax.ShapeDtypeStruct((T, D_MODEL), BF16),
            jax.ShapeDtypeStruct((T, 128), F32),
            jax.ShapeDtypeStruct((128, T), F32),
        ],
        compiler_params=_cparams(("parallel",)),
        name="outproj",
    )(x2d, proj, proj, *o_list, *lse_list, ob2d, wpa, wpb, wo, norm_moe, wr, br, expand)


MOE_TB = 1024
MOE_ALIGN = 16
MOE_CH = 128
MOE_R = 2 * MOE_TB + N_EXPERTS * MOE_ALIGN
MOE_RS = MOE_R + MOE_CH
MOE_SUB = 256
MOE_G = 4


def _moe_route(hm_ref, gate_ref, gatet_ref, xs, ys, pt, wrow, rinfo):
    TB, R, SUB = MOE_TB, MOE_R, MOE_SUB
    big = float(4 * TB)
    gate = gate_ref[...]
    gt = gatet_ref[...]
    a = gate > 0.0
    at = gt > 0.0
    a_b = jnp.where(a, 1.0, 0.0).astype(BF16)
    at_b = jnp.where(at, 1.0, 0.0).astype(BF16)

    def pad(c):
        return jnp.floor((c + (MOE_ALIGN - 1)) * (1.0 / MOE_ALIGN)) * MOE_ALIGN

    er = lax.broadcasted_iota(jnp.int32, (128, 128), 0)
    ec = lax.broadcasted_iota(jnp.int32, (128, 128), 1)
    pad_row = pad(jnp.sum(jnp.where(a, 1.0, 0.0), axis=0, keepdims=True))
    off_row = jnp.dot(jnp.broadcast_to(pad_row, (8, 128)).astype(BF16), jnp.where(er < ec, 1.0, 0.0).astype(BF16),
                      preferred_element_type=F32)[0:1]
    pad_col = pad(jnp.sum(jnp.where(at, 1.0, 0.0), axis=1, keepdims=True))
    off_col = jnp.dot(jnp.where(ec < er, 1.0, 0.0).astype(BF16), jnp.broadcast_to(pad_col, (128, 128)).astype(BF16),
                      preferred_element_type=F32)[:, 0:1]

    rl = lax.broadcasted_iota(jnp.int32, (SUB, R), 1).astype(F32)
    for rc in range(TB // SUB):
        rows = slice(rc * SUB, (rc + 1) * SUB)
        tr = lax.broadcasted_iota(jnp.int32, (SUB, TB), 0) + rc * SUB
        tc = lax.broadcasted_iota(jnp.int32, (SUB, TB), 1)
        rank = jnp.dot(jnp.where(tc < tr, 1.0, 0.0).astype(BF16), a_b, preferred_element_type=F32)
        pos = off_row + rank
        a_c = a[rows]
        plo = jnp.min(jnp.where(a_c, pos, big), axis=1, keepdims=True)
        phi = jnp.max(jnp.where(a_c, pos, -1.0), axis=1, keepdims=True)
        phi = jnp.where(phi == plo, -1.0, phi)
        pt[rows, :] = jnp.where((rl == plo) | (rl == phi), 1.0, 0.0).astype(BF16)

    for cc in range(TB // SUB):
        cols = slice(cc * SUB, (cc + 1) * SUB)
        tr = lax.broadcasted_iota(jnp.int32, (TB, SUB), 0)
        tc = lax.broadcasted_iota(jnp.int32, (TB, SUB), 1) + cc * SUB
        rank_t = jnp.dot(at_b, jnp.where(tr < tc, 1.0, 0.0).astype(BF16), preferred_element_type=F32)
        pos_t = off_col + rank_t
        at_c = at[:, cols]
        gt_c = gt[:, cols]
        plo = jnp.min(jnp.where(at_c, pos_t, big), axis=0, keepdims=True)
        phi = jnp.max(jnp.where(at_c, pos_t, -1.0), axis=0, keepdims=True)
        phi = jnp.where(phi == plo, -1.0, phi)
        rinfo[0:1, cols] = plo
        rinfo[1:2, cols] = phi
        rinfo[2:3, cols] = jnp.sum(jnp.where(at_c & (pos_t == plo), gt_c, 0.0), axis=0, keepdims=True)
        rinfo[3:4, cols] = jnp.sum(jnp.where(at_c & (pos_t == phi), gt_c, 0.0), axis=0, keepdims=True)

    def gather(k, carry):
        r0 = pl.multiple_of(k * SUB, SUB)
        ri = (lax.broadcasted_iota(jnp.int32, (SUB, TB), 0) + r0).astype(F32)
        mlo = ri == rinfo[0:1, :]
        mhi = ri == rinfo[1:2, :]
        p = jnp.where(mlo | mhi, 1.0, 0.0).astype(BF16)
        xs[pl.ds(r0, SUB), :] = jnp.dot(p, hm_ref[...], preferred_element_type=F32).astype(BF16)
        w = jnp.sum(jnp.where(mlo, rinfo[2:3, :], 0.0) + jnp.where(mhi, rinfo[3:4, :], 0.0), axis=1, keepdims=True)
        wrow[pl.ds(r0, SUB), :] = jnp.broadcast_to(w, (SUB, 128))
        return carry

    lax.fori_loop(0, R // SUB, gather, 0)
    xs[R:, :] = jnp.zeros((MOE_RS - R, D_MODEL), BF16)
    wrow[R:, :] = jnp.zeros((MOE_RS - R, 128), F32)
    ys[...] = jnp.zeros_like(ys)


def _moe_kernel(meta_ref, hm_ref, gate_ref, gatet_ref, x1_ref, wgu_ref, wd_ref, nf_ref, y_ref,
                xs, ys, pt, wrow, rinfo):
    i = pl.program_id(0)
    s = pl.program_id(1)

    @pl.when(s == 0)
    def _():
        _moe_route(hm_ref, gate_ref, gatet_ref, xs, ys, pt, wrow, rinfo)

    offs = [meta_ref[i, s * MOE_G + g] for g in range(MOE_G)]
    nchs = [meta_ref[i, N_EXPERTS + s * MOE_G + g] for g in range(MOE_G)]

    def chunk(g, r0):
        rows = pl.ds(pl.multiple_of(r0, MOE_ALIGN), MOE_CH)
        gu = jnp.dot(xs[rows, :], wgu_ref[g], preferred_element_type=F32)
        gt, up = gu[:, :EXPERT_FF], gu[:, EXPERT_FF:]
        act = (gt * jax.nn.sigmoid(gt)) * up
        w = wrow[rows, :]
        act = jnp.concatenate([act[:, :128] * w, act[:, 128:] * w], axis=1)
        ys[rows, :] = jnp.dot(act.astype(BF16), wd_ref[g], preferred_element_type=F32).astype(BF16)

    single = nchs[0] <= 1
    for g in range(1, MOE_G):
        single = jnp.logical_and(single, nchs[g] <= 1)

    @pl.when(single)
    def _():
        for g in range(MOE_G):
            chunk(g, offs[g])

    @pl.when(jnp.logical_not(single))
    def _():
        for g in range(MOE_G):
            def body(c, carry, g=g):
                chunk(g, offs[g] + c * MOE_CH)
                return carry
            lax.fori_loop(0, nchs[g], body, 0)

    @pl.when(s == N_EXPERTS // MOE_G - 1)
    def _():
        x2 = x1_ref[...] + jnp.dot(pt[...], ys[0:MOE_R, :], preferred_element_type=F32)
        y_ref[...] = x2 * lax.rsqrt(jnp.mean(x2 * x2, axis=-1, keepdims=True) + EPS) * nf_ref[...]


def _moe(hm, gate, gatet, x1, wgu, wd, norm_final):
    T = hm.shape[0]
    tb = MOE_TB
    nb = T // tb
    cnt = jnp.sum((gate.reshape(nb, tb, 128)[:, :, :N_EXPERTS] > 0.0).astype(jnp.int32), axis=1)
    padded = (cnt + (MOE_ALIGN - 1)) // MOE_ALIGN * MOE_ALIGN
    off = jnp.cumsum(padded, axis=1) - padded
    meta = jnp.concatenate([off, (padded + (MOE_CH - 1)) // MOE_CH], axis=1).astype(jnp.int32)
    grid_spec = pltpu.PrefetchScalarGridSpec(
        num_scalar_prefetch=1,
        grid=(nb, N_EXPERTS // MOE_G),
        in_specs=[
            pl.BlockSpec((tb, D_MODEL), lambda i, e, m: (i, 0)),
            pl.BlockSpec((tb, 128), lambda i, e, m: (i, 0)),
            pl.BlockSpec((128, tb), lambda i, e, m: (0, i)),
            pl.BlockSpec((tb, D_MODEL), lambda i, e, m: (i, 0)),
            pl.BlockSpec((MOE_G, D_MODEL, 2 * EXPERT_FF), lambda i, e, m: (e, 0, 0)),
            pl.BlockSpec((MOE_G, EXPERT_FF, D_MODEL), lambda i, e, m: (e, 0, 0)),
            pl.BlockSpec((1, D_MODEL), lambda i, e, m: (0, 0)),
        ],
        out_specs=pl.BlockSpec((tb, D_MODEL), lambda i, e, m: (i, 0)),
        scratch_shapes=[
            pltpu.VMEM((MOE_RS, D_MODEL), BF16),
            pltpu.VMEM((MOE_RS, D_MODEL), BF16),
            pltpu.VMEM((tb, MOE_R), BF16),
            pltpu.VMEM((MOE_RS, 128), F32),
            pltpu.VMEM((8, tb), F32),
        ],
    )
    return pl.pallas_call(
        _moe_kernel,
        grid_spec=grid_spec,
        out_shape=jax.ShapeDtypeStruct((T, D_MODEL), F32),
        compiler_params=_cparams(("parallel", "arbitrary")),
        name="moe",
    )(meta, hm, gate, gatet, x1, wgu, wd, norm_final)


def _prep_weights(norm_mix, w_in, rel_bias, w_gate_f, b_gate_f, w_gate_b, b_gate_b, gla_norm, w_proj_a,
                  w_proj_b, w_out, norm_moe, w_rg, b_rg, w_re, b_re, w_eg, w_eu, w_ed, norm_final):
    splits = (A_QKV, A_QKV, A_QKV, 512, 512, 1024, 1024, GATE_RANK, GATE_RANK, D_MODEL, D_MODEL)
    qa, ka, va, qb, kb, vb, og, gf, gb, ma, mb = jnp.split(w_in, np.cumsum(splits)[:-1].tolist(), axis=1)
    gpad = jnp.zeros((D_MODEL, PR - C_G - 2 * GATE_RANK), w_in.dtype)
    w_all = jnp.concatenate([qa, ka, va, ma, mb, qb, kb, vb, og, gf, gb, gpad], axis=1).astype(BF16)
    kw = GLA_H * GLA_DK
    wgf = jnp.zeros((128, kw), F32).at[:GATE_RANK].set(w_gate_f).astype(BF16)
    wgb = jnp.zeros((128, kw), F32).at[GATE_RANK:2 * GATE_RANK].set(w_gate_b).astype(BF16)
    wr = jnp.zeros((D_MODEL, 128), F32)
    wr = wr.at[:, :N_EXPERTS].set(w_re).at[:, N_EXPERTS:2 * N_EXPERTS].set(jnp.repeat(w_rg, EPG, axis=1))
    br = jnp.zeros((1, 128), F32)
    br = br.at[0, :N_EXPERTS].set(b_re).at[0, N_EXPERTS:2 * N_EXPERTS].set(jnp.repeat(b_rg, EPG))
    expand = (np.arange(128)[:, None] == 32 * (np.arange(A_GW)[None, :] // A_DH)).astype(np.float32)
    return dict(
        norm_mix=norm_mix.reshape(1, D_MODEL), w_all=w_all, rel_bias=rel_bias,
        wgf=wgf, wgb=wgb, bgf=b_gate_f.reshape(1, kw), bgb=b_gate_b.reshape(1, kw),
        gla_norm=gla_norm.reshape(1, GLA_DV),
        wpa=w_proj_a.astype(BF16), wpb=w_proj_b.astype(BF16), wo=w_out.astype(BF16),
        norm_moe=norm_moe.reshape(1, D_MODEL), wr=wr, br=br, expand=jnp.asarray(expand),
        wgu=jnp.concatenate([w_eg, w_eu], axis=-1).astype(BF16), wd=w_ed.astype(BF16),
        norm_final=norm_final.reshape(1, D_MODEL),
    )


def _trunk(x, w):
    B, S, _ = x.shape
    T = B * S
    x2d = x.reshape(T, D_MODEL)
    proj_a, proj = _inproj(x2d, w["norm_mix"], w["w_all"])
    o_list, lse_list = [], []
    for g, (_, d) in enumerate(DIL_PAIRS):
        L = S // d
        tiles = _bias_tiles(w["rel_bias"], g, d, min(2 * Q_BLK, L))
        o, lse = _attn_group(proj_a, tiles, g, d, B, S)
        o_list.append(o.reshape(T, A_GW))
        lse_list.append(lse.reshape(T, 128))
    ob = _gla(proj, w["wgf"], w["wgb"], w["bgf"], w["bgb"], w["gla_norm"], B, S)
    x1, hm, gate, gatet = _outproj(x2d, proj, o_list, lse_list, ob.reshape(T, D_MODEL), w["wpa"], w["wpb"],
                                   w["wo"], w["norm_moe"], w["wr"], w["br"], w["expand"])
    y = _moe(hm, gate, gatet, x1, w["wgu"], w["wd"], w["norm_final"])
    return y.reshape(B, S, D_MODEL)


def kernel(x_prompt, x_sample, norm_mix, w_in, rel_bias, w_gate_f, b_gate_f, w_gate_b, b_gate_b, gla_norm,
           w_proj_a, w_proj_b, w_out, norm_moe, w_router_group, b_router_group, w_router_expert,
           b_router_expert, w_exp_gate, w_exp_up, w_exp_down, norm_final):
    w = _prep_weights(norm_mix[0], w_in[0], rel_bias, w_gate_f[0], b_gate_f[0], w_gate_b[0], b_gate_b[0],
                      gla_norm[0], w_proj_a[0], w_proj_b[0], w_out[0], norm_moe[0], w_router_group[0],
                      b_router_group[0], w_router_expert[0], b_router_expert[0], w_exp_gate[0], w_exp_up[0],
                      w_exp_down[0], norm_final)
    return (_trunk(x_prompt, w), _trunk(x_sample, w))
```

```python
import functools
import math

import numpy as np
import jax
import jax.numpy as jnp
from jax import lax
from jax.experimental import pallas as pl
from jax.experimental.pallas import tpu as pltpu

F32 = jnp.float32
BF16 = jnp.bfloat16
HIGHEST = lax.Precision.HIGHEST

D_MODEL = 1024
EPS = 1e-6
NEG_INF = -1e30

DIL_PAIRS = ((128, 1), (512, 4), (2048, 16))
A_HPG = 4
A_DH = 64
A_GW = A_HPG * A_DH
A_QKV = 3 * A_GW
HALF_WIN = 64
Q_BLK = 128
N_REL_BUCKETS = 32
REL_MAX_DISTANCE = 1024
GLA_H = 4
GLA_DK = 128
GLA_DV = 256
GATE_RANK = 16
GATE_TAU = 16.0
GLA_CHUNK = 64
N_GROUPS = 4
EPG = 8
N_EXPERTS = 32
EXPERT_FF = 256

N_DIL = len(DIL_PAIRS)
C_MA, C_MB = 0, 1024
C_QB, C_KB, C_VB, C_OG = 2048, 2560, 3072, 4096
C_G = 5120
PR = 5376
PROJ_TN = A_QKV

VMEM_LIMIT = 58 * 1024 * 1024


def _cparams(sem):
    return pltpu.CompilerParams(dimension_semantics=sem, vmem_limit_bytes=VMEM_LIMIT)


def _inproj_kernel(x_ref, g_ref, w_ref, a0_ref, a1_ref, a2_ref, or_ref, h_scr, y_scr):
    j = pl.program_id(1)

    @pl.when(j == 0)
    def _():
        x = x_ref[...]
        r = x * lax.rsqrt(jnp.mean(x * x, axis=-1, keepdims=True) + EPS)
        h_scr[...] = (r * g_ref[...]).astype(BF16)

    y = jnp.dot(h_scr[...], w_ref[...], preferred_element_type=F32)

    for grp, a_ref in enumerate((a0_ref, a1_ref, a2_ref)):
        d = DIL_PAIRS[grp][1]

        @pl.when(j == grp)
        def _(a_ref=a_ref, d=d):
            if d == 1:
                a_ref[0, 0] = y.astype(BF16)
            else:
                nct = PROJ_TN // 128
                for c in range(nct):
                    y_scr[c] = y[:, c * 128:(c + 1) * 128]
                n = y.shape[0] // d
                for r in range(d):
                    cls = [y_scr[c, pl.ds(r, n, stride=d), :] for c in range(nct)]
                    a_ref[0, r] = jnp.concatenate(cls, axis=1).astype(BF16)

    @pl.when(j >= N_DIL)
    def _():
        or_ref[...] = y.astype(BF16)


def _inproj(x, norm_g, w_all, tm=1024):
    B, S, _ = x.shape
    T = B * S
    spb = S // tm
    a_specs, a_shapes = [], []
    for _, d in DIL_PAIRS:
        a_specs.append(pl.BlockSpec((1, d, tm // d, PROJ_TN), lambda i, j: (i // spb, 0, i % spb, 0)))
        a_shapes.append(jax.ShapeDtypeStruct((B, d, S // d, PROJ_TN), BF16))
    return pl.pallas_call(
        _inproj_kernel,
        grid=(T // tm, N_DIL + PR // PROJ_TN),
        in_specs=[
            pl.BlockSpec((tm, D_MODEL), lambda i, j: (i, 0)),
            pl.BlockSpec((1, D_MODEL), lambda i, j: (0, 0)),
            pl.BlockSpec((D_MODEL, PROJ_TN), lambda i, j: (0, j)),
        ],
        out_specs=a_specs + [pl.BlockSpec((tm, PROJ_TN), lambda i, j: (i, jnp.maximum(j - N_DIL, 0)))],
        out_shape=a_shapes + [jax.ShapeDtypeStruct((T, PR), BF16)],
        scratch_shapes=[pltpu.VMEM((tm, D_MODEL), BF16), pltpu.VMEM((PROJ_TN // 128, tm, 128), F32)],
        compiler_params=_cparams(("parallel", "arbitrary")),
        name="inproj",
    )(x.reshape(T, D_MODEL), norm_g, w_all)


def _attn_kernel(q_ref, k_ref, v_ref, bias_ref, o_ref, lse_ref, *, L, wk):
    i = pl.program_id(2)
    start = pl.multiple_of(jnp.clip(i * Q_BLK - HALF_WIN, 0, L - wk), HALF_WIN)
    q = q_ref[0, 0]
    kw = k_ref[0, 0, pl.ds(start, wk), :]
    vw = v_ref[0, 0, pl.ds(start, wk), :]
    scale = A_DH ** -0.5
    lane = lax.broadcasted_iota(jnp.int32, (Q_BLK, 128), 1)
    outs = []
    lse_tile = jnp.zeros((Q_BLK, 128), F32)
    for h in range(A_HPG):
        sl = slice(h * A_DH, (h + 1) * A_DH)
        s = lax.dot_general(q[:, sl], kw[:, sl], (((1,), (1,)), ((), ())), preferred_element_type=F32)
        s = s * scale + bias_ref[0, h]
        m = jnp.max(s, axis=-1, keepdims=True)
        p = jnp.exp(s - m)
        l = jnp.sum(p, axis=-1, keepdims=True)
        o = jnp.dot(p.astype(BF16), vw[:, sl], preferred_element_type=F32) / l
        outs.append(o)
        lse = m + jnp.log(l)
        lse_tile = jnp.where((lane >= 32 * h) & (lane < 32 * (h + 1)), lse, lse_tile)
    o_ref[0] = jnp.concatenate(outs, axis=1).astype(BF16)
    lse_ref[0] = lse_tile


def _attn_group(qkv, bias_tiles, d, B, S):
    L = S // d
    nblk = L // Q_BLK
    wk = min(2 * Q_BLK, L)

    def variant(i):
        return jnp.where(i == 0, 0, jnp.where(i == nblk - 1, 2, 1))

    return pl.pallas_call(
        functools.partial(_attn_kernel, L=L, wk=wk),
        grid=(B, d, nblk),
        in_specs=[
            pl.BlockSpec((1, 1, Q_BLK, A_GW), lambda b, r, i: (b, r, i, 0)),
            pl.BlockSpec((1, 1, L, A_GW), lambda b, r, i: (b, r, 0, 1)),
            pl.BlockSpec((1, 1, L, A_GW), lambda b, r, i: (b, r, 0, 2)),
            pl.BlockSpec((1, A_HPG, Q_BLK, wk), lambda b, r, i: (variant(i), 0, 0, 0)),
        ],
        out_specs=[
            pl.BlockSpec((1, Q_BLK, A_GW), lambda b, r, i: (b, i, r)),
            pl.BlockSpec((1, Q_BLK, 128), lambda b, r, i: (b, i, r)),
        ],
        out_shape=[
            jax.ShapeDtypeStruct((B, L, d * A_GW), BF16),
            jax.ShapeDtypeStruct((B, L, d * 128), F32),
        ],
        compiler_params=_cparams(("parallel", "parallel", "arbitrary")),
        name=f"attn_d{d}",
    )(qkv, qkv, qkv, bias_tiles)


def _t5_buckets(rel):
    nb = N_REL_BUCKETS // 2
    max_exact = nb // 2
    ret = (rel > 0).astype(np.int64) * nb
    n = np.abs(rel)
    large = max_exact + (np.log(np.maximum(n, 1) / max_exact) / math.log(REL_MAX_DISTANCE / max_exact)
                         * (nb - max_exact)).astype(np.int64)
    large = np.minimum(large, nb - 1)
    return (ret + np.where(n < max_exact, n, large)).astype(np.int32)


def _bias_tiles(rel_bias, g, d, wk):
    rel = d * np.arange(-HALF_WIN, HALF_WIN + 1)
    bucket = _t5_buckets(rel)
    bias = rel_bias[jnp.asarray(bucket)][:, g * A_HPG:(g + 1) * A_HPG].T.astype(F32)
    pad = wk + Q_BLK
    neg = jnp.full((A_HPG, pad), NEG_INF, F32)
    val = jnp.concatenate([neg, bias, neg], axis=1)
    zero = pad + HALF_WIN
    m = wk + Q_BLK
    tiles = []
    for off in (0, HALF_WIN, wk - Q_BLK):
        u = jnp.concatenate([val[:, zero - off:zero - off + wk], val[:, zero - off - Q_BLK:zero - off]], axis=1)
        flat = jnp.tile(u, (1, Q_BLK))[:, :Q_BLK * (m - 1)]
        tiles.append(flat.reshape(A_HPG, Q_BLK, m - 1)[:, :, :wk])
    return jnp.stack(tiles)


def _log_sigmoid(z):
    return jnp.minimum(z, 0.0) - jnp.log1p(jnp.exp(-jnp.abs(z)))


def _gla_kernel(q_ref, k_ref, v_ref, og_ref, g_ref, wgf_ref, wgb_ref, bgf_ref, bgb_ref, gn_ref, o_ref,
                qif, kof, qib, kob, vt, etf, etb, acc, *, S):
    PB = 256
    C = GLA_CHUNK
    NB = S // PB
    rr = lax.broadcasted_iota(jnp.int32, (PB, PB), 0)
    cc = lax.broadcasted_iota(jnp.int32, (PB, PB), 1)
    same = (rr >> 6) == (cc >> 6)
    mask_f = same & (cc <= rr)
    mask_b = same & (cc > rr)
    tl = jnp.where(mask_f, 1.0, 0.0).astype(BF16)
    qscale = GLA_DK ** -0.5
    inv_tau = 1.0 / GATE_TAU
    nt = (((1,), (1,)), ((), ()))

    def split3(x):
        hi = x.astype(BF16)
        r1 = x - hi.astype(F32)
        mid = r1.astype(BF16)
        lo = (r1 - mid.astype(F32)).astype(BF16)
        return [hi, mid, lo]

    def chunk_total(b):
        b4 = b.reshape(PB // C, C, GLA_DK)
        return jnp.broadcast_to(b4[:, C - 1:C, :], b4.shape).reshape(PB, GLA_DK)

    def prep(i, carry):
        rows = pl.ds(pl.multiple_of(i * PB, PB), PB)
        g = g_ref[0, rows, :]
        q = q_ref[0, rows, :].astype(F32) * qscale
        k = k_ref[0, rows, :].astype(F32)
        v = v_ref[0, rows, :]
        lf = _log_sigmoid(jnp.dot(g, wgf_ref[...], preferred_element_type=F32) + bgf_ref[...]) * inv_tau
        lb = _log_sigmoid(jnp.dot(g, wgb_ref[...], preferred_element_type=F32) + bgb_ref[...]) * inv_tau
        cs = jnp.dot(tl, jnp.concatenate(split3(lf) + split3(lb), axis=1), preferred_element_type=F32)
        bf = cs[:, 0:128] + cs[:, 128:256] + cs[:, 256:384]
        pb = cs[:, 384:512] + cs[:, 512:640] + cs[:, 640:768]
        totf = chunk_total(bf)
        totb = chunk_total(pb)
        bs = totb - pb + lb
        qf = (q * jnp.exp(bf)).astype(BF16)
        qb = (q * jnp.exp(bs)).astype(BF16)
        sf = lax.dot_general(qf, (k * jnp.exp(-bf)).astype(BF16), nt, preferred_element_type=F32)
        sb = lax.dot_general(qb, (k * jnp.exp(-bs)).astype(BF16), nt, preferred_element_type=F32)
        p = jnp.where(mask_f, sf, jnp.where(mask_b, sb, 0.0)).astype(BF16)
        acc[rows, :] = jnp.dot(p, v, preferred_element_type=F32)
        qif[rows, :] = qf
        qib[rows, :] = qb
        kof[rows, :] = (k * jnp.exp(totf - bf)).astype(BF16)
        kob[rows, :] = (k * jnp.exp(totb - bs)).astype(BF16)
        etf[rows, :] = jnp.exp(totf)
        etb[rows, :] = jnp.exp(totb)
        vtb = v.astype(F32).T.astype(BF16)
        vt[2 * i] = vtb[:, :128]
        vt[2 * i + 1] = vtb[:, 128:]
        return carry

    lax.fori_loop(0, NB, prep, 0)

    zeros_half = jnp.zeros((C, GLA_DK), BF16)

    def chunk_step(qi_ref, ko_ref, et_ref, blk, j, state):
        r0 = pl.multiple_of(blk * PB + j * C, C)
        rows = pl.ds(r0, C)
        acc[rows, :] += lax.dot_general(qi_ref[rows, :], state.astype(BF16), nt, preferred_element_type=F32)
        ko = ko_ref[rows, :]
        ko_pad = jnp.concatenate([ko, zeros_half] if j % 2 == 0 else [zeros_half, ko], axis=0)
        upd = jnp.dot(vt[2 * blk + j // 2], ko_pad, preferred_element_type=F32)
        return state * et_ref[pl.ds(r0, 1), :] + upd

    def serial(t, carry):
        st_f, st_b = carry
        for j in range(PB // C):
            st_f = chunk_step(qif, kof, etf, t, j, st_f)
            st_b = chunk_step(qib, kob, etb, NB - 1 - t, PB // C - 1 - j, st_b)
        return st_f, st_b

    zero_state = jnp.zeros((GLA_DV, GLA_DK), F32)
    lax.fori_loop(0, NB, serial, (zero_state, zero_state))

    gn = gn_ref[...]

    def finish(i, carry):
        rows = pl.ds(pl.multiple_of(i * PB, PB), PB)
        tot = acc[rows, :]
        nrm = tot * lax.rsqrt(jnp.mean(tot * tot, axis=-1, keepdims=True) + EPS) * gn
        og = og_ref[0, rows, :].astype(F32)
        o_ref[0, rows, :] = (nrm * (og * jax.nn.sigmoid(og))).astype(BF16)
        return carry

    lax.fori_loop(0, NB, finish, 0)


def _gla(proj, wgf, wgb, bgf, bgb, gla_norm, B, S):
    pv = proj.reshape(B, S, PR)
    cq, ck, cv, cog, cg = C_QB // GLA_DK, C_KB // GLA_DK, C_VB // GLA_DV, C_OG // GLA_DV, C_G // 128
    return pl.pallas_call(
        functools.partial(_gla_kernel, S=S),
        grid=(B, GLA_H),
        in_specs=[
            pl.BlockSpec((1, S, GLA_DK), lambda b, h: (b, 0, cq + h)),
            pl.BlockSpec((1, S, GLA_DK), lambda b, h: (b, 0, ck + h)),
            pl.BlockSpec((1, S, GLA_DV), lambda b, h: (b, 0, cv + h)),
            pl.BlockSpec((1, S, GLA_DV), lambda b, h: (b, 0, cog + h)),
            pl.BlockSpec((1, S, 128), lambda b, h: (b, 0, cg)),
            pl.BlockSpec((128, GLA_DK), lambda b, h: (0, h)),
            pl.BlockSpec((128, GLA_DK), lambda b, h: (0, h)),
            pl.BlockSpec((1, GLA_DK), lambda b, h: (0, h)),
            pl.BlockSpec((1, GLA_DK), lambda b, h: (0, h)),
            pl.BlockSpec((1, GLA_DV), lambda b, h: (0, 0)),
        ],
        out_specs=pl.BlockSpec((1, S, GLA_DV), lambda b, h: (b, 0, h)),
        out_shape=jax.ShapeDtypeStruct((B, S, GLA_H * GLA_DV), BF16),
        scratch_shapes=[
            pltpu.VMEM((S, GLA_DK), BF16), pltpu.VMEM((S, GLA_DK), BF16),
            pltpu.VMEM((S, GLA_DK), BF16), pltpu.VMEM((S, GLA_DK), BF16),
            pltpu.VMEM((S // 128, GLA_DV, 128), BF16),
            pltpu.VMEM((S, GLA_DK), F32), pltpu.VMEM((S, GLA_DK), F32),
            pltpu.VMEM((S, GLA_DV), F32),
        ],
        compiler_params=_cparams(("parallel", "arbitrary")),
        name="gla",
    )(pv, pv, pv, pv, pv, wgf, wgb, bgf, bgb, gla_norm)


def _outproj_kernel(x_ref, ma_ref, mb_ref, o0_ref, o1_ref, o2_ref, l0_ref, l1_ref, l2_ref, ob_ref,
                    wpa_ref, wpb_ref, wo_ref, nm_ref, wr_ref, br_ref, exp_ref,
                    x1_ref, hm_ref, gate_ref, gatet_ref):
    l0, l1, l2 = l0_ref[...], l1_ref[...], l2_ref[...]
    m = jnp.maximum(jnp.maximum(l0, l1), l2)
    e0, e1, e2 = jnp.exp(l0 - m), jnp.exp(l1 - m), jnp.exp(l2 - m)
    den = e0 + e1 + e2
    ex = exp_ref[...]
    oa = jnp.zeros(o0_ref.shape, F32)
    for e, o_ref in ((e0, o0_ref), (e1, o1_ref), (e2, o2_ref)):
        w = jnp.dot(e / den, ex, precision=HIGHEST, preferred_element_type=F32)
        oa = oa + w * o_ref[...].astype(F32)
    ya = jnp.dot(oa.astype(BF16), wpa_ref[...], preferred_element_type=F32)
    yb = jnp.dot(ob_ref[...], wpb_ref[...], preferred_element_type=F32)
    mix = jax.nn.sigmoid(ma_ref[...].astype(F32)) * ya + jax.nn.sigmoid(mb_ref[...].astype(F32)) * yb
    x1 = x_ref[...] + jnp.dot(mix.astype(BF16), wo_ref[...], preferred_element_type=F32)
    x1_ref[...] = x1
    hm = x1 * lax.rsqrt(jnp.mean(x1 * x1, axis=-1, keepdims=True) + EPS) * nm_ref[...]
    hm_ref[...] = hm.astype(BF16)

    logits = jnp.dot(hm, wr_ref[...], precision=HIGHEST, preferred_element_type=F32) + br_ref[...]
    tm = logits.shape[0]
    lane = lax.broadcasted_iota(jnp.int32, (tm, 128), 1)
    is_g = (lane >= N_EXPERTS) & (lane < 2 * N_EXPERTS)
    is_e = lane < N_EXPERTS
    grp_of_lane = jnp.where(is_g, (lane - N_EXPERTS) >> 3, lane >> 3)
    gl = jnp.where(is_g, logits, NEG_INF)
    gmax = jnp.max(gl, axis=-1, keepdims=True)
    gsum = jnp.sum(jnp.where(is_g, jnp.exp(gl - gmax), 0.0), axis=-1, keepdims=True) * (1.0 / EPG)
    pg_top = 1.0 / gsum
    g_sel = jnp.min(jnp.where(is_g & (gl == gmax), grp_of_lane, N_GROUPS), axis=-1, keepdims=True)
    in_grp = is_e & (grp_of_lane == g_sel)
    el = jnp.where(in_grp, logits, NEG_INF)
    emax = jnp.max(el, axis=-1, keepdims=True)
    ee = jnp.where(in_grp, jnp.exp(el - emax), 0.0)
    pe = ee / jnp.sum(ee, axis=-1, keepdims=True)
    p1 = jnp.max(pe, axis=-1, keepdims=True)
    i1 = jnp.min(jnp.where(in_grp & (pe == p1), lane, 128), axis=-1, keepdims=True)
    rest = in_grp & (lane != i1)
    pe2 = jnp.where(rest, pe, -1.0)
    p2 = jnp.max(pe2, axis=-1, keepdims=True)
    i2 = jnp.min(jnp.where(rest & (pe2 == p2), lane, 128), axis=-1, keepdims=True)
    psum = p1 + p2
    gate = jnp.where(lane == i1, pg_top * (p1 / psum), jnp.where(lane == i2, pg_top * (p2 / psum), 0.0))
    gate_ref[...] = gate
    gatet_ref[...] = gate.T


def _outproj(x2d, proj, o_list, lse_list, ob2d, wpa, wpb, wo, norm_moe, wr, br, expand, tm=512):
    T = x2d.shape[0]
    row = lambda w: pl.BlockSpec((tm, w), lambda i: (i, 0))
    full = lambda a: pl.BlockSpec(a.shape, lambda i: (0,) * a.ndim)
    return pl.pallas_call(
        _outproj_kernel,
        grid=(T // tm,),
        in_specs=[
            row(D_MODEL),
            pl.BlockSpec((tm, D_MODEL), lambda i: (i, C_MA // D_MODEL)),
            pl.BlockSpec((tm, D_MODEL), lambda i: (i, C_MB // D_MODEL)),
            row(A_GW), row(A_GW), row(A_GW), row(128), row(128), row(128),
            row(D_MODEL),
            full(wpa), full(wpb), full(wo), full(norm_moe), full(wr), full(br), full(expand),
        ],
        out_specs=[row(D_MODEL), row(D_MODEL), row(128), pl.BlockSpec((128, tm), lambda i: (0, i))],
        out_shape=[
            jax.ShapeDtypeStruct((T, D_MODEL), F32),
            jax.ShapeDtypeStruct((T, D_MODEL), BF16),
            jax.ShapeDtypeStruct((T, 128), F32),
            jax.ShapeDtypeStruct((128, T), F32),
        ],
        compiler_params=_cparams(("parallel",)),
        name="outproj",
    )(x2d, proj, proj, *o_list, *lse_list, ob2d, wpa, wpb, wo, norm_moe, wr, br, expand)


MOE_TB = 1024
MOE_ALIGN = 16
MOE_CH = 128
MOE_R = 2 * MOE_TB + N_EXPERTS * MOE_ALIGN
MOE_RS = MOE_R + MOE_CH
MOE_SUB = 256
MOE_G = 4


def _moe_route(hm_ref, gate_ref, gatet_ref, xs, ys, pt, wrow, rinfo):
    TB, R, SUB = MOE_TB, MOE_R, MOE_SUB
    big = float(4 * TB)
    gate = gate_ref[...]
    gt = gatet_ref[...]
    a = gate > 0.0
    at = gt > 0.0
    a_b = jnp.where(a, 1.0, 0.0).astype(BF16)
    at_b = jnp.where(at, 1.0, 0.0).astype(BF16)

    def pad(c):
        return jnp.floor((c + (MOE_ALIGN - 1)) * (1.0 / MOE_ALIGN)) * MOE_ALIGN

    er = lax.broadcasted_iota(jnp.int32, (128, 128), 0)
    ec = lax.broadcasted_iota(jnp.int32, (128, 128), 1)
    pad_row = pad(jnp.sum(jnp.where(a, 1.0, 0.0), axis=0, keepdims=True))
    off_row = jnp.dot(jnp.broadcast_to(pad_row, (8, 128)).astype(BF16), jnp.where(er < ec, 1.0, 0.0).astype(BF16),
                      preferred_element_type=F32)[0:1]
    pad_col = pad(jnp.sum(jnp.where(at, 1.0, 0.0), axis=1, keepdims=True))
    off_col = jnp.dot(jnp.where(ec < er, 1.0, 0.0).astype(BF16), jnp.broadcast_to(pad_col, (128, 128)).astype(BF16),
                      preferred_element_type=F32)[:, 0:1]

    rl = lax.broadcasted_iota(jnp.int32, (SUB, R), 1).astype(F32)
    for rc in range(TB // SUB):
        rows = slice(rc * SUB, (rc + 1) * SUB)
        tr = lax.broadcasted_iota(jnp.int32, (SUB, TB), 0) + rc * SUB
        tc = lax.broadcasted_iota(jnp.int32, (SUB, TB), 1)
        rank = jnp.dot(jnp.where(tc < tr, 1.0, 0.0).astype(BF16), a_b, preferred_element_type=F32)
        pos = off_row + rank
        a_c = a[rows]
        plo = jnp.min(jnp.where(a_c, pos, big), axis=1, keepdims=True)
        phi = jnp.max(jnp.where(a_c, pos, -1.0), axis=1, keepdims=True)
        phi = jnp.where(phi == plo, -1.0, phi)
        pt[rows, :] = jnp.where((rl == plo) | (rl == phi), 1.0, 0.0).astype(BF16)

    for cc in range(TB // SUB):
        cols = slice(cc * SUB, (cc + 1) * SUB)
        tr = lax.broadcasted_iota(jnp.int32, (TB, SUB), 0)
        tc = lax.broadcasted_iota(jnp.int32, (TB, SUB), 1) + cc * SUB
        rank_t = jnp.dot(at_b, jnp.where(tr < tc, 1.0, 0.0).astype(BF16), preferred_element_type=F32)
        pos_t = off_col + rank_t
        at_c = at[:, cols]
        gt_c = gt[:, cols]
        plo = jnp.min(jnp.where(at_c, pos_t, big), axis=0, keepdims=True)
        phi = jnp.max(jnp.where(at_c, pos_t, -1.0), axis=0, keepdims=True)
        phi = jnp.where(phi == plo, -1.0, phi)
        rinfo[0:1, cols] = plo
        rinfo[1:2, cols] = phi
        rinfo[2:3, cols] = jnp.sum(jnp.where(at_c & (pos_t == plo), gt_c, 0.0), axis=0, keepdims=True)
        rinfo[3:4, cols] = jnp.sum(jnp.where(at_c & (pos_t == phi), gt_c, 0.0), axis=0, keepdims=True)

    def gather(k, carry):
        r0 = pl.multiple_of(k * SUB, SUB)
        ri = (lax.broadcasted_iota(jnp.int32, (SUB, TB), 0) + r0).astype(F32)
        mlo = ri == rinfo[0:1, :]
        mhi = ri == rinfo[1:2, :]
        p = jnp.where(mlo | mhi, 1.0, 0.0).astype(BF16)
        xs[pl.ds(r0, SUB), :] = jnp.dot(p, hm_ref[...], preferred_element_type=F32).astype(BF16)
        w = jnp.sum(jnp.where(mlo, rinfo[2:3, :], 0.0) + jnp.where(mhi, rinfo[3:4, :], 0.0), axis=1, keepdims=True)
        wrow[pl.ds(r0, SUB), :] = jnp.broadcast_to(w, (SUB, 128))
        return carry

    lax.fori_loop(0, R // SUB, gather, 0)
    xs[R:, :] = jnp.zeros((MOE_RS - R, D_MODEL), BF16)
    wrow[R:, :] = jnp.zeros((MOE_RS - R, 128), F32)
    ys[...] = jnp.zeros_like(ys)


def _moe_kernel(meta_ref, hm_ref, gate_ref, gatet_ref, x1_ref, wgu_ref, wd_ref, nf_ref, y_ref,
                xs, ys, pt, wrow, rinfo):
    i = pl.program_id(0)
    s = pl.program_id(1)

    @pl.when(s == 0)
    def _():
        _moe_route(hm_ref, gate_ref, gatet_ref, xs, ys, pt, wrow, rinfo)

    offs = [meta_ref[i, s * MOE_G + g] for g in range(MOE_G)]
    nchs = [meta_ref[i, N_EXPERTS + s * MOE_G + g] for g in range(MOE_G)]

    def chunk(g, r0):
        rows = pl.ds(pl.multiple_of(r0, MOE_ALIGN), MOE_CH)
        gu = jnp.dot(xs[rows, :], wgu_ref[g], preferred_element_type=F32)
        gt, up = gu[:, :EXPERT_FF], gu[:, EXPERT_FF:]
        act = (gt * jax.nn.sigmoid(gt)) * up
        w = wrow[rows, :]
        act = jnp.concatenate([act[:, :128] * w, act[:, 128:] * w], axis=1)
        ys[rows, :] = jnp.dot(act.astype(BF16), wd_ref[g], preferred_element_type=F32).astype(BF16)

    single = nchs[0] <= 1
    for g in range(1, MOE_G):
        single = jnp.logical_and(single, nchs[g] <= 1)

    @pl.when(single)
    def _():
        for g in range(MOE_G):
            chunk(g, offs[g])

    @pl.when(jnp.logical_not(single))
    def _():
        for g in range(MOE_G):
            def body(c, carry, g=g):
                chunk(g, offs[g] + c * MOE_CH)
                return carry
            lax.fori_loop(0, nchs[g], body, 0)

    @pl.when(s == N_EXPERTS // MOE_G - 1)
    def _():
        x2 = x1_ref[...] + jnp.dot(pt[...], ys[0:MOE_R, :], preferred_element_type=F32)
        y_ref[...] = x2 * lax.rsqrt(jnp.mean(x2 * x2, axis=-1, keepdims=True) + EPS) * nf_ref[...]


def _moe(hm, gate, gatet, x1, wgu, wd, norm_final):
    T = hm.shape[0]
    tb = MOE_TB
    nb = T // tb
    cnt = jnp.sum((gate.reshape(nb, tb, 128)[:, :, :N_EXPERTS] > 0.0).astype(jnp.int32), axis=1)
    padded = (cnt + (MOE_ALIGN - 1)) // MOE_ALIGN * MOE_ALIGN
    off = jnp.cumsum(padded, axis=1) - padded
    meta = jnp.concatenate([off, (padded + (MOE_CH - 1)) // MOE_CH], axis=1).astype(jnp.int32)
    grid_spec = pltpu.PrefetchScalarGridSpec(
        num_scalar_prefetch=1,
        grid=(nb, N_EXPERTS // MOE_G),
        in_specs=[
            pl.BlockSpec((tb, D_MODEL), lambda i, e, m: (i, 0)),
            pl.BlockSpec((tb, 128), lambda i, e, m: (i, 0)),
            pl.BlockSpec((128, tb), lambda i, e, m: (0, i)),
            pl.BlockSpec((tb, D_MODEL), lambda i, e, m: (i, 0)),
            pl.BlockSpec((MOE_G, D_MODEL, 2 * EXPERT_FF), lambda i, e, m: (e, 0, 0)),
            pl.BlockSpec((MOE_G, EXPERT_FF, D_MODEL), lambda i, e, m: (e, 0, 0)),
            pl.BlockSpec((1, D_MODEL), lambda i, e, m: (0, 0)),
        ],
        out_specs=pl.BlockSpec((tb, D_MODEL), lambda i, e, m: (i, 0)),
        scratch_shapes=[
            pltpu.VMEM((MOE_RS, D_MODEL), BF16),
            pltpu.VMEM((MOE_RS, D_MODEL), BF16),
            pltpu.VMEM((tb, MOE_R), BF16),
            pltpu.VMEM((MOE_RS, 128), F32),
            pltpu.VMEM((8, tb), F32),
        ],
    )
    return pl.pallas_call(
        _moe_kernel,
        grid_spec=grid_spec,
        out_shape=jax.ShapeDtypeStruct((T, D_MODEL), F32),
        compiler_params=_cparams(("parallel", "arbitrary")),
        name="moe",
    )(meta, hm, gate, gatet, x1, wgu, wd, norm_final)


def _prep_weights(norm_mix, w_in, rel_bias, w_gate_f, b_gate_f, w_gate_b, b_gate_b, gla_norm, w_proj_a,
                  w_proj_b, w_out, norm_moe, w_rg, b_rg, w_re, b_re, w_eg, w_eu, w_ed, norm_final):
    splits = (A_QKV, A_QKV, A_QKV, 512, 512, 1024, 1024, GATE_RANK, GATE_RANK, D_MODEL, D_MODEL)
    qa, ka, va, qb, kb, vb, og, gf, gb, ma, mb = jnp.split(w_in, np.cumsum(splits)[:-1].tolist(), axis=1)
    gpad = jnp.zeros((D_MODEL, PR - C_G - 2 * GATE_RANK), w_in.dtype)
    grp = lambda t, g: t[:, g * A_GW:(g + 1) * A_GW]
    a_cols = [grp(t, g) for g in range(N_DIL) for t in (qa, ka, va)]
    w_all = jnp.concatenate(a_cols + [ma, mb, qb, kb, vb, og, gf, gb, gpad], axis=1).astype(BF16)
    kw = GLA_H * GLA_DK
    wgf = jnp.zeros((128, kw), F32).at[:GATE_RANK].set(w_gate_f).astype(BF16)
    wgb = jnp.zeros((128, kw), F32).at[GATE_RANK:2 * GATE_RANK].set(w_gate_b).astype(BF16)
    wr = jnp.zeros((D_MODEL, 128), F32)
    wr = wr.at[:, :N_EXPERTS].set(w_re).at[:, N_EXPERTS:2 * N_EXPERTS].set(jnp.repeat(w_rg, EPG, axis=1))
    br = jnp.zeros((1, 128), F32)
    br = br.at[0, :N_EXPERTS].set(b_re).at[0, N_EXPERTS:2 * N_EXPERTS].set(jnp.repeat(b_rg, EPG))
    expand = (np.arange(128)[:, None] == 32 * (np.arange(A_GW)[None, :] // A_DH)).astype(np.float32)
    return dict(
        norm_mix=norm_mix.reshape(1, D_MODEL), w_all=w_all, rel_bias=rel_bias,
        wgf=wgf, wgb=wgb, bgf=b_gate_f.reshape(1, kw), bgb=b_gate_b.reshape(1, kw),
        gla_norm=gla_norm.reshape(1, GLA_DV),
        wpa=w_proj_a.astype(BF16), wpb=w_proj_b.astype(BF16), wo=w_out.astype(BF16),
        norm_moe=norm_moe.reshape(1, D_MODEL), wr=wr, br=br, expand=jnp.asarray(expand),
        wgu=jnp.concatenate([w_eg, w_eu], axis=-1).astype(BF16), wd=w_ed.astype(BF16),
        norm_final=norm_final.reshape(1, D_MODEL),
    )


def _trunk(x, w):
    B, S, _ = x.shape
    T = B * S
    x2d = x.reshape(T, D_MODEL)
    *qkv, proj = _inproj(x, w["norm_mix"], w["w_all"])
    o_list, lse_list = [], []
    for g, (_, d) in enumerate(DIL_PAIRS):
        L = S // d
        tiles = _bias_tiles(w["rel_bias"], g, d, min(2 * Q_BLK, L))
        o, lse = _attn_group(qkv[g], tiles, d, B, S)
        o_list.append(o.reshape(T, A_GW))
        lse_list.append(lse.reshape(T, 128))
    ob = _gla(proj, w["wgf"], w["wgb"], w["bgf"], w["bgb"], w["gla_norm"], B, S)
    x1, hm, gate, gatet = _outproj(x2d, proj, o_list, lse_list, ob.reshape(T, D_MODEL), w["wpa"], w["wpb"],
                                   w["wo"], w["norm_moe"], w["wr"], w["br"], w["expand"])
    y = _moe(hm, gate, gatet, x1, w["wgu"], w["wd"], w["norm_final"])
    return y.reshape(B, S, D_MODEL)


def kernel(x_prompt, x_sample, norm_mix, w_in, rel_bias, w_gate_f, b_gate_f, w_gate_b, b_gate_b, gla_norm,
           w_proj_a, w_proj_b, w_out, norm_moe, w_router_group, b_router_group, w_router_expert,
           b_router_expert, w_exp_gate, w_exp_up, w_exp_down, norm_final):
    w = _prep_weights(norm_mix[0], w_in[0], rel_bias, w_gate_f[0], b_gate_f[0], w_gate_b[0], b_gate_b[0],
                      gla_norm[0], w_proj_a[0], w_proj_b[0], w_out[0], norm_moe[0], w_router_group[0],
                      b_router_group[0], w_router_expert[0], b_router_expert[0], w_exp_gate[0], w_exp_up[0],
                      w_exp_down[0], norm_final)
    return (_trunk(x_prompt, w), _trunk(x_sample, w))
```

```python
import functools
import math

import numpy as np
import jax
import jax.numpy as jnp
from jax import lax
from jax.experimental import pallas as pl
from jax.experimental.pallas import tpu as pltpu

F32 = jnp.float32
BF16 = jnp.bfloat16

D_MODEL = 1024
EPS = 1e-6
NEG_INF = -1e30

DIL_PAIRS = ((128, 1), (512, 4), (2048, 16))
A_HPG = 4
A_DH = 64
A_GW = A_HPG * A_DH
A_QKV = 3 * A_GW
HALF_WIN = 64
Q_BLK = 128
N_REL_BUCKETS = 32
REL_MAX_DISTANCE = 1024
GLA_H = 4
GLA_DK = 128
GLA_DV = 256
GATE_RANK = 16
GATE_TAU = 16.0
GLA_CHUNK = 64
N_GROUPS = 4
EPG = 8
N_EXPERTS = 32
EXPERT_FF = 256

N_DIL = len(DIL_PAIRS)
C_MA, C_MB = 0, 1024
C_QB, C_KB, C_VB, C_OG = 2048, 2560, 3072, 4096
C_G = 5120
PR = 5376
PROJ_TN = A_QKV

VMEM_LIMIT = 58 * 1024 * 1024


def _cparams(sem):
    return pltpu.CompilerParams(dimension_semantics=sem, vmem_limit_bytes=VMEM_LIMIT)


def _inproj_kernel(x_ref, g_ref, w_ref, a0_ref, a1_ref, a2_ref, or_ref, h_scr, y_scr):
    j = pl.program_id(1)

    @pl.when(j == 0)
    def _():
        x = x_ref[...]
        r = x * lax.rsqrt(jnp.mean(x * x, axis=-1, keepdims=True) + EPS)
        h_scr[...] = (r * g_ref[...]).astype(BF16)

    y = jnp.dot(h_scr[...], w_ref[...], preferred_element_type=F32)

    for grp, a_ref in enumerate((a0_ref, a1_ref, a2_ref)):
        d = DIL_PAIRS[grp][1]

        @pl.when(j == grp)
        def _(a_ref=a_ref, d=d):
            if d == 1:
                a_ref[0, 0] = y.astype(BF16)
            else:
                nct = PROJ_TN // 128
                for c in range(nct):
                    y_scr[c] = y[:, c * 128:(c + 1) * 128]
                n = y.shape[0] // d
                for r in range(d):
                    cls = [y_scr[c, pl.ds(r, n, stride=d), :] for c in range(nct)]
                    a_ref[0, r] = jnp.concatenate(cls, axis=1).astype(BF16)

    @pl.when(j >= N_DIL)
    def _():
        or_ref[...] = y.astype(BF16)


def _inproj(x, norm_g, w_all, tm=1024):
    B, S, _ = x.shape
    T = B * S
    spb = S // tm
    a_specs, a_shapes = [], []
    for _, d in DIL_PAIRS:
        a_specs.append(pl.BlockSpec((1, d, tm // d, PROJ_TN), lambda i, j: (i // spb, 0, i % spb, 0)))
        a_shapes.append(jax.ShapeDtypeStruct((B, d, S // d, PROJ_TN), BF16))
    return pl.pallas_call(
        _inproj_kernel,
        grid=(T // tm, N_DIL + PR // PROJ_TN),
        in_specs=[
            pl.BlockSpec((tm, D_MODEL), lambda i, j: (i, 0)),
            pl.BlockSpec((1, D_MODEL), lambda i, j: (0, 0)),
            pl.BlockSpec((D_MODEL, PROJ_TN), lambda i, j: (0, j)),
        ],
        out_specs=a_specs + [pl.BlockSpec((tm, PROJ_TN), lambda i, j: (i, jnp.maximum(j - N_DIL, 0)))],
        out_shape=a_shapes + [jax.ShapeDtypeStruct((T, PR), BF16)],
        scratch_shapes=[pltpu.VMEM((tm, D_MODEL), BF16), pltpu.VMEM((PROJ_TN // 128, tm, 128), F32)],
        compiler_params=_cparams(("parallel", "arbitrary")),
        name="inproj",
    )(x.reshape(T, D_MODEL), norm_g, w_all)


def _attn_kernel(q_ref, k_ref, v_ref, bias_ref, o_ref, lse_ref, *, L, wk):
    i = pl.program_id(2)
    start = pl.multiple_of(jnp.clip(i * Q_BLK - HALF_WIN, 0, L - wk), HALF_WIN)
    q = q_ref[0, 0]
    kw = k_ref[0, 0, pl.ds(start, wk), :]
    vw = v_ref[0, 0, pl.ds(start, wk), :]
    scale = A_DH ** -0.5
    lane = lax.broadcasted_iota(jnp.int32, (Q_BLK, A_GW), 1)
    outs = []
    lse_tile = jnp.zeros((Q_BLK, A_GW), F32)
    for h in range(A_HPG):
        sl = slice(h * A_DH, (h + 1) * A_DH)
        s = lax.dot_general(q[:, sl], kw[:, sl], (((1,), (1,)), ((), ())), preferred_element_type=F32)
        s = s * scale + bias_ref[0, h]
        m = jnp.max(s, axis=-1, keepdims=True)
        p = jnp.exp(s - m)
        l = jnp.sum(p, axis=-1, keepdims=True)
        o = jnp.dot(p.astype(BF16), vw[:, sl], preferred_element_type=F32) / l
        outs.append(o)
        lse = m + jnp.log(l)
        lse_tile = jnp.where((lane >= A_DH * h) & (lane < A_DH * (h + 1)), lse, lse_tile)
    o_ref[0] = jnp.concatenate(outs, axis=1).astype(BF16)
    lse_ref[0] = lse_tile


def _attn_group(qkv, bias_tiles, d, B, S):
    L = S // d
    nblk = L // Q_BLK
    wk = min(2 * Q_BLK, L)

    def variant(i):
        return jnp.where(i == 0, 0, jnp.where(i == nblk - 1, 2, 1))

    return pl.pallas_call(
        functools.partial(_attn_kernel, L=L, wk=wk),
        grid=(B, d, nblk),
        in_specs=[
            pl.BlockSpec((1, 1, Q_BLK, A_GW), lambda b, r, i: (b, r, i, 0)),
            pl.BlockSpec((1, 1, L, A_GW), lambda b, r, i: (b, r, 0, 1)),
            pl.BlockSpec((1, 1, L, A_GW), lambda b, r, i: (b, r, 0, 2)),
            pl.BlockSpec((1, A_HPG, Q_BLK, wk), lambda b, r, i: (variant(i), 0, 0, 0)),
        ],
        out_specs=[
            pl.BlockSpec((1, Q_BLK, A_GW), lambda b, r, i: (b, i, r)),
            pl.BlockSpec((1, Q_BLK, A_GW), lambda b, r, i: (b, i, r)),
        ],
        out_shape=[
            jax.ShapeDtypeStruct((B, L, d * A_GW), BF16),
            jax.ShapeDtypeStruct((B, L, d * A_GW), F32),
        ],
        compiler_params=_cparams(("parallel", "parallel", "arbitrary")),
        name=f"attn_d{d}",
    )(qkv, qkv, qkv, bias_tiles)


def _t5_buckets(rel):
    nb = N_REL_BUCKETS // 2
    max_exact = nb // 2
    ret = (rel > 0).astype(np.int64) * nb
    n = np.abs(rel)
    large = max_exact + (np.log(np.maximum(n, 1) / max_exact) / math.log(REL_MAX_DISTANCE / max_exact)
                         * (nb - max_exact)).astype(np.int64)
    large = np.minimum(large, nb - 1)
    return (ret + np.where(n < max_exact, n, large)).astype(np.int32)


def _bias_tiles(rel_bias, g, d, wk):
    rel = d * np.arange(-HALF_WIN, HALF_WIN + 1)
    bucket = _t5_buckets(rel)
    bias = rel_bias[jnp.asarray(bucket)][:, g * A_HPG:(g + 1) * A_HPG].T.astype(F32)
    pad = wk + Q_BLK
    neg = jnp.full((A_HPG, pad), NEG_INF, F32)
    val = jnp.concatenate([neg, bias, neg], axis=1)
    zero = pad + HALF_WIN
    m = wk + Q_BLK
    tiles = []
    for off in (0, HALF_WIN, wk - Q_BLK):
        u = jnp.concatenate([val[:, zero - off:zero - off + wk], val[:, zero - off - Q_BLK:zero - off]], axis=1)
        flat = jnp.tile(u, (1, Q_BLK))[:, :Q_BLK * (m - 1)]
        tiles.append(flat.reshape(A_HPG, Q_BLK, m - 1)[:, :, :wk])
    return jnp.stack(tiles)


def _log_sigmoid(z):
    return jnp.minimum(z, 0.0) - jnp.log1p(jnp.exp(-jnp.abs(z)))


def _gla_kernel(q_ref, k_ref, v_ref, og_ref, g_ref, wgf_ref, wgb_ref, bgf_ref, bgb_ref, gn_ref, o_ref,
                qif, kof, qib, kob, vt, etf, etb, acc, *, S):
    PB = 256
    C = GLA_CHUNK
    NB = S // PB
    rr = lax.broadcasted_iota(jnp.int32, (PB, PB), 0)
    cc = lax.broadcasted_iota(jnp.int32, (PB, PB), 1)
    same = (rr >> 6) == (cc >> 6)
    mask_f = same & (cc <= rr)
    mask_b = same & (cc > rr)
    tl = jnp.where(mask_f, 1.0, 0.0).astype(BF16)
    qscale = GLA_DK ** -0.5
    inv_tau = 1.0 / GATE_TAU
    nt = (((1,), (1,)), ((), ()))

    def split3(x):
        hi = x.astype(BF16)
        r1 = x - hi.astype(F32)
        mid = r1.astype(BF16)
        lo = (r1 - mid.astype(F32)).astype(BF16)
        return [hi, mid, lo]

    def chunk_total(b):
        b4 = b.reshape(PB // C, C, GLA_DK)
        return jnp.broadcast_to(b4[:, C - 1:C, :], b4.shape).reshape(PB, GLA_DK)

    def prep(i, carry):
        rows = pl.ds(pl.multiple_of(i * PB, PB), PB)
        g = g_ref[0, rows, :]
        q = q_ref[0, rows, :].astype(F32) * qscale
        k = k_ref[0, rows, :].astype(F32)
        v = v_ref[0, rows, :]
        lf = _log_sigmoid(jnp.dot(g, wgf_ref[...], preferred_element_type=F32) + bgf_ref[...]) * inv_tau
        lb = _log_sigmoid(jnp.dot(g, wgb_ref[...], preferred_element_type=F32) + bgb_ref[...]) * inv_tau
        cs = jnp.dot(tl, jnp.concatenate(split3(lf) + split3(lb), axis=1), preferred_element_type=F32)
        bf = cs[:, 0:128] + cs[:, 128:256] + cs[:, 256:384]
        pb = cs[:, 384:512] + cs[:, 512:640] + cs[:, 640:768]
        totf = chunk_total(bf)
        totb = chunk_total(pb)
        bs = totb - pb + lb
        qf = (q * jnp.exp(bf)).astype(BF16)
        qb = (q * jnp.exp(bs)).astype(BF16)
        sf = lax.dot_general(qf, (k * jnp.exp(-bf)).astype(BF16), nt, preferred_element_type=F32)
        sb = lax.dot_general(qb, (k * jnp.exp(-bs)).astype(BF16), nt, preferred_element_type=F32)
        p = jnp.where(mask_f, sf, jnp.where(mask_b, sb, 0.0)).astype(BF16)
        acc[rows, :] = jnp.dot(p, v, preferred_element_type=F32)
        qif[rows, :] = qf
        qib[rows, :] = qb
        kof[rows, :] = (k * jnp.exp(totf - bf)).astype(BF16)
        kob[rows, :] = (k * jnp.exp(totb - bs)).astype(BF16)
        etf[rows, :] = jnp.exp(totf)
        etb[rows, :] = jnp.exp(totb)
        vtb = v.astype(F32).T.astype(BF16)
        vt[2 * i] = vtb[:, :128]
        vt[2 * i + 1] = vtb[:, 128:]
        return carry

    lax.fori_loop(0, NB, prep, 0)

    zeros_half = jnp.zeros((C, GLA_DK), BF16)

    def chunk_step(qi_ref, ko_ref, et_ref, blk, j, state):
        r0 = pl.multiple_of(blk * PB + j * C, C)
        rows = pl.ds(r0, C)
        acc[rows, :] += lax.dot_general(qi_ref[rows, :], state.astype(BF16), nt, preferred_element_type=F32)
        ko = ko_ref[rows, :]
        ko_pad = jnp.concatenate([ko, zeros_half] if j % 2 == 0 else [zeros_half, ko], axis=0)
        upd = jnp.dot(vt[2 * blk + j // 2], ko_pad, preferred_element_type=F32)
        return state * et_ref[pl.ds(r0, 1), :] + upd

    def serial(t, carry):
        st_f, st_b = carry
        for j in range(PB // C):
            st_f = chunk_step(qif, kof, etf, t, j, st_f)
            st_b = chunk_step(qib, kob, etb, NB - 1 - t, PB // C - 1 - j, st_b)
        return st_f, st_b

    zero_state = jnp.zeros((GLA_DV, GLA_DK), F32)
    lax.fori_loop(0, NB, serial, (zero_state, zero_state))

    gn = gn_ref[...]

    def finish(i, carry):
        rows = pl.ds(pl.multiple_of(i * PB, PB), PB)
        tot = acc[rows, :]
        nrm = tot * lax.rsqrt(jnp.mean(tot * tot, axis=-1, keepdims=True) + EPS) * gn
        og = og_ref[0, rows, :].astype(F32)
        o_ref[0, rows, :] = (nrm * (og * jax.nn.sigmoid(og))).astype(BF16)
        return carry

    lax.fori_loop(0, NB, finish, 0)


def _gla(proj, wgf, wgb, bgf, bgb, gla_norm, B, S):
    pv = proj.reshape(B, S, PR)
    cq, ck, cv, cog, cg = C_QB // GLA_DK, C_KB // GLA_DK, C_VB // GLA_DV, C_OG // GLA_DV, C_G // 128
    return pl.pallas_call(
        functools.partial(_gla_kernel, S=S),
        grid=(B, GLA_H),
        in_specs=[
            pl.BlockSpec((1, S, GLA_DK), lambda b, h: (b, 0, cq + h)),
            pl.BlockSpec((1, S, GLA_DK), lambda b, h: (b, 0, ck + h)),
            pl.BlockSpec((1, S, GLA_DV), lambda b, h: (b, 0, cv + h)),
            pl.BlockSpec((1, S, GLA_DV), lambda b, h: (b, 0, cog + h)),
            pl.BlockSpec((1, S, 128), lambda b, h: (b, 0, cg)),
            pl.BlockSpec((128, GLA_DK), lambda b, h: (0, h)),
            pl.BlockSpec((128, GLA_DK), lambda b, h: (0, h)),
            pl.BlockSpec((1, GLA_DK), lambda b, h: (0, h)),
            pl.BlockSpec((1, GLA_DK), lambda b, h: (0, h)),
            pl.BlockSpec((1, GLA_DV), lambda b, h: (0, 0)),
        ],
        out_specs=pl.BlockSpec((1, S, GLA_DV), lambda b, h: (b, 0, h)),
        out_shape=jax.ShapeDtypeStruct((B, S, GLA_H * GLA_DV), BF16),
        scratch_shapes=[
            pltpu.VMEM((S, GLA_DK), BF16), pltpu.VMEM((S, GLA_DK), BF16),
            pltpu.VMEM((S, GLA_DK), BF16), pltpu.VMEM((S, GLA_DK), BF16),
            pltpu.VMEM((S // 128, GLA_DV, 128), BF16),
            pltpu.VMEM((S, GLA_DK), F32), pltpu.VMEM((S, GLA_DK), F32),
            pltpu.VMEM((S, GLA_DV), F32),
        ],
        compiler_params=_cparams(("parallel", "arbitrary")),
        name="gla",
    )(pv, pv, pv, pv, pv, wgf, wgb, bgf, bgb, gla_norm)


def _outproj_kernel(x_ref, ma_ref, mb_ref, o0_ref, o1_ref, o2_ref, l0_ref, l1_ref, l2_ref, ob_ref,
                    wpa_ref, wpb_ref, wo_ref, nm_ref, wr_ref, br_ref,
                    x1_ref, hm_ref, gate_ref, gatet_ref):
    l0, l1, l2 = l0_ref[...], l1_ref[...], l2_ref[...]
    m = jnp.maximum(jnp.maximum(l0, l1), l2)
    e0, e1, e2 = jnp.exp(l0 - m), jnp.exp(l1 - m), jnp.exp(l2 - m)
    den = e0 + e1 + e2
    oa = ((e0 / den) * o0_ref[...].astype(F32) + (e1 / den) * o1_ref[...].astype(F32)
          + (e2 / den) * o2_ref[...].astype(F32))
    ya = jnp.dot(oa.astype(BF16), wpa_ref[...], preferred_element_type=F32)
    yb = jnp.dot(ob_ref[...], wpb_ref[...], preferred_element_type=F32)
    mix = jax.nn.sigmoid(ma_ref[...].astype(F32)) * ya + jax.nn.sigmoid(mb_ref[...].astype(F32)) * yb
    x1 = x_ref[...] + jnp.dot(mix.astype(BF16), wo_ref[...], preferred_element_type=F32)
    x1_ref[...] = x1
    hm = x1 * lax.rsqrt(jnp.mean(x1 * x1, axis=-1, keepdims=True) + EPS) * nm_ref[...]
    hm_b = hm.astype(BF16)
    hm_ref[...] = hm_b

    logits = jnp.dot(hm_b, wr_ref[...], preferred_element_type=F32) + br_ref[...]
    tm = logits.shape[0]
    lane = lax.broadcasted_iota(jnp.int32, (tm, 128), 1)
    is_g = (lane >= N_EXPERTS) & (lane < 2 * N_EXPERTS)
    is_e = lane < N_EXPERTS
    grp_of_lane = jnp.where(is_g, (lane - N_EXPERTS) >> 3, lane >> 3)
    gl = jnp.where(is_g, logits, NEG_INF)
    gmax = jnp.max(gl, axis=-1, keepdims=True)
    gsum = jnp.sum(jnp.where(is_g, jnp.exp(gl - gmax), 0.0), axis=-1, keepdims=True) * (1.0 / EPG)
    pg_top = 1.0 / gsum
    g_sel = jnp.min(jnp.where(is_g & (gl == gmax), grp_of_lane, N_GROUPS), axis=-1, keepdims=True)
    in_grp = is_e & (grp_of_lane == g_sel)
    el = jnp.where(in_grp, logits, NEG_INF)
    emax = jnp.max(el, axis=-1, keepdims=True)
    ee = jnp.where(in_grp, jnp.exp(el - emax), 0.0)
    pe = ee / jnp.sum(ee, axis=-1, keepdims=True)
    p1 = jnp.max(pe, axis=-1, keepdims=True)
    i1 = jnp.min(jnp.where(in_grp & (pe == p1), lane, 128), axis=-1, keepdims=True)
    rest = in_grp & (lane != i1)
    pe2 = jnp.where(rest, pe, -1.0)
    p2 = jnp.max(pe2, axis=-1, keepdims=True)
    i2 = jnp.min(jnp.where(rest & (pe2 == p2), lane, 128), axis=-1, keepdims=True)
    psum = p1 + p2
    gate = jnp.where(lane == i1, pg_top * (p1 / psum), jnp.where(lane == i2, pg_top * (p2 / psum), 0.0))
    gate_ref[...] = gate
    gatet_ref[...] = gate.T


def _outproj(x2d, proj, o_list, lse_list, ob2d, wpa, wpb, wo, norm_moe, wr, br, tm=512):
    T = x2d.shape[0]
    row = lambda w: pl.BlockSpec((tm, w), lambda i: (i, 0))
    full = lambda a: pl.BlockSpec(a.shape, lambda i: (0,) * a.ndim)
    return pl.pallas_call(
        _outproj_kernel,
        grid=(T // tm,),
        in_specs=[
            row(D_MODEL),
            pl.BlockSpec((tm, D_MODEL), lambda i: (i, C_MA // D_MODEL)),
            pl.BlockSpec((tm, D_MODEL), lambda i: (i, C_MB // D_MODEL)),
            row(A_GW), row(A_GW), row(A_GW), row(A_GW), row(A_GW), row(A_GW),
            row(D_MODEL),
            full(wpa), full(wpb), full(wo), full(norm_moe), full(wr), full(br),
        ],
        out_specs=[row(D_MODEL), row(D_MODEL), row(128), pl.BlockSpec((128, tm), lambda i: (0, i))],
        out_shape=[
            jax.ShapeDtypeStruct((T, D_MODEL), F32),
            jax.ShapeDtypeStruct((T, D_MODEL), BF16),
            jax.ShapeDtypeStruct((T, 128), F32),
            jax.ShapeDtypeStruct((128, T), F32),
        ],
        compiler_params=_cparams(("parallel",)),
        name="outproj",
    )(x2d, proj, proj, *o_list, *lse_list, ob2d, wpa, wpb, wo, norm_moe, wr, br)


MOE_TB = 1024
MOE_ALIGN = 16
MOE_CH = 128
MOE_R = 2 * MOE_TB + N_EXPERTS * MOE_ALIGN
MOE_RS = MOE_R + MOE_CH
MOE_SUB = 256
MOE_G = 4


def _moe_route(hm_ref, gate_ref, gatet_ref, xs, ys, pt, wrow, rinfo):
    TB, R, SUB = MOE_TB, MOE_R, MOE_SUB
    big = float(4 * TB)
    gate = gate_ref[...]
    gt = gatet_ref[...]
    a = gate > 0.0
    at = gt > 0.0
    a_b = jnp.where(a, 1.0, 0.0).astype(BF16)
    at_b = jnp.where(at, 1.0, 0.0).astype(BF16)

    def pad(c):
        return jnp.floor((c + (MOE_ALIGN - 1)) * (1.0 / MOE_ALIGN)) * MOE_ALIGN

    er = lax.broadcasted_iota(jnp.int32, (128, 128), 0)
    ec = lax.broadcasted_iota(jnp.int32, (128, 128), 1)
    pad_row = pad(jnp.sum(jnp.where(a, 1.0, 0.0), axis=0, keepdims=True))
    off_row = jnp.dot(jnp.broadcast_to(pad_row, (8, 128)).astype(BF16), jnp.where(er < ec, 1.0, 0.0).astype(BF16),
                      preferred_element_type=F32)[0:1]
    pad_col = pad(jnp.sum(jnp.where(at, 1.0, 0.0), axis=1, keepdims=True))
    off_col = jnp.dot(jnp.where(ec < er, 1.0, 0.0).astype(BF16), jnp.broadcast_to(pad_col, (128, 128)).astype(BF16),
                      preferred_element_type=F32)[:, 0:1]

    rl = lax.broadcasted_iota(jnp.int32, (SUB, R), 1).astype(F32)
    for rc in range(TB // SUB):
        rows = slice(rc * SUB, (rc + 1) * SUB)
        tr = lax.broadcasted_iota(jnp.int32, (SUB, TB), 0) + rc * SUB
        tc = lax.broadcasted_iota(jnp.int32, (SUB, TB), 1)
        rank = jnp.dot(jnp.where(tc < tr, 1.0, 0.0).astype(BF16), a_b, preferred_element_type=F32)
        pos = off_row + rank
        a_c = a[rows]
        plo = jnp.min(jnp.where(a_c, pos, big), axis=1, keepdims=True)
        phi = jnp.max(jnp.where(a_c, pos, -1.0), axis=1, keepdims=True)
        phi = jnp.where(phi == plo, -1.0, phi)
        pt[rows, :] = jnp.where((rl == plo) | (rl == phi), 1.0, 0.0).astype(BF16)

    for cc in range(TB // SUB):
        cols = slice(cc * SUB, (cc + 1) * SUB)
        tr = lax.broadcasted_iota(jnp.int32, (TB, SUB), 0)
        tc = lax.broadcasted_iota(jnp.int32, (TB, SUB), 1) + cc * SUB
        rank_t = jnp.dot(at_b, jnp.where(tr < tc, 1.0, 0.0).astype(BF16), preferred_element_type=F32)
        pos_t = off_col + rank_t
        at_c = at[:, cols]
        gt_c = gt[:, cols]
        plo = jnp.min(jnp.where(at_c, pos_t, big), axis=0, keepdims=True)
        phi = jnp.max(jnp.where(at_c, pos_t, -1.0), axis=0, keepdims=True)
        phi = jnp.where(phi == plo, -1.0, phi)
        rinfo[0:1, cols] = plo
        rinfo[1:2, cols] = phi
        rinfo[2:3, cols] = jnp.sum(jnp.where(at_c & (pos_t == plo), gt_c, 0.0), axis=0, keepdims=True)
        rinfo[3:4, cols] = jnp.sum(jnp.where(at_c & (pos_t == phi), gt_c, 0.0), axis=0, keepdims=True)

    def gather(k, carry):
        r0 = pl.multiple_of(k * SUB, SUB)
        ri = (lax.broadcasted_iota(jnp.int32, (SUB, TB), 0) + r0).astype(F32)
        mlo = ri == rinfo[0:1, :]
        mhi = ri == rinfo[1:2, :]
        p = jnp.where(mlo | mhi, 1.0, 0.0).astype(BF16)
        xs[pl.ds(r0, SUB), :] = jnp.dot(p, hm_ref[...], preferred_element_type=F32).astype(BF16)
        w = jnp.sum(jnp.where(mlo, rinfo[2:3, :], 0.0) + jnp.where(mhi, rinfo[3:4, :], 0.0), axis=1, keepdims=True)
        wrow[pl.ds(r0, SUB), :] = jnp.broadcast_to(w, (SUB, 128))
        return carry

    lax.fori_loop(0, R // SUB, gather, 0)
    xs[R:, :] = jnp.zeros((MOE_RS - R, D_MODEL), BF16)
    wrow[R:, :] = jnp.zeros((MOE_RS - R, 128), F32)
    ys[...] = jnp.zeros_like(ys)


def _moe_kernel(meta_ref, hm_ref, gate_ref, gatet_ref, x1_ref, wgu_ref, wd_ref, nf_ref, y_ref,
                xs, ys, pt, wrow, rinfo):
    i = pl.program_id(0)
    s = pl.program_id(1)

    @pl.when(s == 0)
    def _():
        _moe_route(hm_ref, gate_ref, gatet_ref, xs, ys, pt, wrow, rinfo)

    offs = [meta_ref[i, s * MOE_G + g] for g in range(MOE_G)]
    nchs = [meta_ref[i, N_EXPERTS + s * MOE_G + g] for g in range(MOE_G)]

    def chunk(g, r0):
        rows = pl.ds(pl.multiple_of(r0, MOE_ALIGN), MOE_CH)
        gu = jnp.dot(xs[rows, :], wgu_ref[g], preferred_element_type=F32)
        gt, up = gu[:, :EXPERT_FF], gu[:, EXPERT_FF:]
        act = (gt * jax.nn.sigmoid(gt)) * up
        w = wrow[rows, :]
        act = jnp.concatenate([act[:, :128] * w, act[:, 128:] * w], axis=1)
        ys[rows, :] = jnp.dot(act.astype(BF16), wd_ref[g], preferred_element_type=F32).astype(BF16)

    single = nchs[0] <= 1
    for g in range(1, MOE_G):
        single = jnp.logical_and(single, nchs[g] <= 1)

    @pl.when(single)
    def _():
        for g in range(MOE_G):
            chunk(g, offs[g])

    @pl.when(jnp.logical_not(single))
    def _():
        for g in range(MOE_G):
            def body(c, carry, g=g):
                chunk(g, offs[g] + c * MOE_CH)
                return carry
            lax.fori_loop(0, nchs[g], body, 0)

    @pl.when(s == N_EXPERTS // MOE_G - 1)
    def _():
        x2 = x1_ref[...] + jnp.dot(pt[...], ys[0:MOE_R, :], preferred_element_type=F32)
        y_ref[...] = x2 * lax.rsqrt(jnp.mean(x2 * x2, axis=-1, keepdims=True) + EPS) * nf_ref[...]


def _moe(hm, gate, gatet, x1, wgu, wd, norm_final):
    T = hm.shape[0]
    tb = MOE_TB
    nb = T // tb
    cnt = jnp.sum((gate.reshape(nb, tb, 128)[:, :, :N_EXPERTS] > 0.0).astype(jnp.int32), axis=1)
    padded = (cnt + (MOE_ALIGN - 1)) // MOE_ALIGN * MOE_ALIGN
    off = jnp.cumsum(padded, axis=1) - padded
    meta = jnp.concatenate([off, (padded + (MOE_CH - 1)) // MOE_CH], axis=1).astype(jnp.int32)
    grid_spec = pltpu.PrefetchScalarGridSpec(
        num_scalar_prefetch=1,
        grid=(nb, N_EXPERTS // MOE_G),
        in_specs=[
            pl.BlockSpec((tb, D_MODEL), lambda i, e, m: (i, 0)),
            pl.BlockSpec((tb, 128), lambda i, e, m: (i, 0)),
            pl.BlockSpec((128, tb), lambda i, e, m: (0, i)),
            pl.BlockSpec((tb, D_MODEL), lambda i, e, m: (i, 0)),
            pl.BlockSpec((MOE_G, D_MODEL, 2 * EXPERT_FF), lambda i, e, m: (e, 0, 0)),
            pl.BlockSpec((MOE_G, EXPERT_FF, D_MODEL), lambda i, e, m: (e, 0, 0)),
            pl.BlockSpec((1, D_MODEL), lambda i, e, m: (0, 0)),
        ],
        out_specs=pl.BlockSpec((tb, D_MODEL), lambda i, e, m: (i, 0)),
        scratch_shapes=[
            pltpu.VMEM((MOE_RS, D_MODEL), BF16),
            pltpu.VMEM((MOE_RS, D_MODEL), BF16),
            pltpu.VMEM((tb, MOE_R), BF16),
            pltpu.VMEM((MOE_RS, 128), F32),
            pltpu.VMEM((8, tb), F32),
        ],
    )
    return pl.pallas_call(
        _moe_kernel,
        grid_spec=grid_spec,
        out_shape=jax.ShapeDtypeStruct((T, D_MODEL), F32),
        compiler_params=_cparams(("parallel", "arbitrary")),
        name="moe",
    )(meta, hm, gate, gatet, x1, wgu, wd, norm_final)


def _prep_weights(norm_mix, w_in, rel_bias, w_gate_f, b_gate_f, w_gate_b, b_gate_b, gla_norm, w_proj_a,
                  w_proj_b, w_out, norm_moe, w_rg, b_rg, w_re, b_re, w_eg, w_eu, w_ed, norm_final):
    splits = (A_QKV, A_QKV, A_QKV, 512, 512, 1024, 1024, GATE_RANK, GATE_RANK, D_MODEL, D_MODEL)
    qa, ka, va, qb, kb, vb, og, gf, gb, ma, mb = jnp.split(w_in, np.cumsum(splits)[:-1].tolist(), axis=1)
    gpad = jnp.zeros((D_MODEL, PR - C_G - 2 * GATE_RANK), w_in.dtype)
    grp = lambda t, g: t[:, g * A_GW:(g + 1) * A_GW]
    a_cols = [grp(t, g) for g in range(N_DIL) for t in (qa, ka, va)]
    w_all = jnp.concatenate(a_cols + [ma, mb, qb, kb, vb, og, gf, gb, gpad], axis=1).astype(BF16)
    kw = GLA_H * GLA_DK
    wgf = jnp.zeros((128, kw), F32).at[:GATE_RANK].set(w_gate_f).astype(BF16)
    wgb = jnp.zeros((128, kw), F32).at[GATE_RANK:2 * GATE_RANK].set(w_gate_b).astype(BF16)
    wr = jnp.zeros((D_MODEL, 128), F32)
    wr = wr.at[:, :N_EXPERTS].set(w_re).at[:, N_EXPERTS:2 * N_EXPERTS].set(jnp.repeat(w_rg, EPG, axis=1))
    br = jnp.zeros((1, 128), F32)
    br = br.at[0, :N_EXPERTS].set(b_re).at[0, N_EXPERTS:2 * N_EXPERTS].set(jnp.repeat(b_rg, EPG))
    return dict(
        norm_mix=norm_mix.reshape(1, D_MODEL), w_all=w_all, rel_bias=rel_bias,
        wgf=wgf, wgb=wgb, bgf=b_gate_f.reshape(1, kw), bgb=b_gate_b.reshape(1, kw),
        gla_norm=gla_norm.reshape(1, GLA_DV),
        wpa=w_proj_a.astype(BF16), wpb=w_proj_b.astype(BF16), wo=w_out.astype(BF16),
        norm_moe=norm_moe.reshape(1, D_MODEL), wr=wr.astype(BF16), br=br,
        wgu=jnp.concatenate([w_eg, w_eu], axis=-1).astype(BF16), wd=w_ed.astype(BF16),
        norm_final=norm_final.reshape(1, D_MODEL),
    )


def _trunk(x, w):
    B, S, _ = x.shape
    T = B * S
    x2d = x.reshape(T, D_MODEL)
    *qkv, proj = _inproj(x, w["norm_mix"], w["w_all"])
    o_list, lse_list = [], []
    for g, (_, d) in enumerate(DIL_PAIRS):
        L = S // d
        tiles = _bias_tiles(w["rel_bias"], g, d, min(2 * Q_BLK, L))
        o, lse = _attn_group(qkv[g], tiles, d, B, S)
        o_list.append(o.reshape(T, A_GW))
        lse_list.append(lse.reshape(T, A_GW))
    ob = _gla(proj, w["wgf"], w["wgb"], w["bgf"], w["bgb"], w["gla_norm"], B, S)
    x1, hm, gate, gatet = _outproj(x2d, proj, o_list, lse_list, ob.reshape(T, D_MODEL), w["wpa"], w["wpb"],
                                   w["wo"], w["norm_moe"], w["wr"], w["br"])
    y = _moe(hm, gate, gatet, x1, w["wgu"], w["wd"], w["norm_final"])
    return y.reshape(B, S, D_MODEL)


def kernel(x_prompt, x_sample, norm_mix, w_in, rel_bias, w_gate_f, b_gate_f, w_gate_b, b_gate_b, gla_norm,
           w_proj_a, w_proj_b, w_out, norm_moe, w_router_group, b_router_group, w_router_expert,
           b_router_expert, w_exp_gate, w_exp_up, w_exp_down, norm_final):
    w = _prep_weights(norm_mix[0], w_in[0], rel_bias, w_gate_f[0], b_gate_f[0], w_gate_b[0], b_gate_b[0],
                      gla_norm[0], w_proj_a[0], w_proj_b[0], w_out[0], norm_moe[0], w_router_group[0],
                      b_router_group[0], w_router_expert[0], b_router_expert[0], w_exp_gate[0], w_exp_up[0],
                      w_exp_down[0], norm_final)
    return (_trunk(x_prompt, w), _trunk(x_sample, w))
```

```python
import functools
import math

import numpy as np
import jax
import jax.numpy as jnp
from jax import lax
from jax.experimental import pallas as pl
from jax.experimental.pallas import tpu as pltpu

F32 = jnp.float32
BF16 = jnp.bfloat16

D_MODEL = 1024
EPS = 1e-6
NEG_INF = -1e30

DIL_PAIRS = ((128, 1), (512, 4), (2048, 16))
A_HPG = 4
A_DH = 64
A_GW = A_HPG * A_DH
A_QKV = 3 * A_GW
HALF_WIN = 64
Q_BLK = 128
N_REL_BUCKETS = 32
REL_MAX_DISTANCE = 1024
GLA_H = 4
GLA_DK = 128
GLA_DV = 256
GATE_RANK = 16
GATE_TAU = 16.0
GLA_CHUNK = 64
N_GROUPS = 4
EPG = 8
N_EXPERTS = 32
EXPERT_FF = 256

N_DIL = len(DIL_PAIRS)
C_MA, C_MB = 0, 1024
C_QB, C_KB, C_VB, C_OG = 2048, 2560, 3072, 4096
C_G = 5120
PR = 5376
PROJ_TN = A_QKV

VMEM_LIMIT = 58 * 1024 * 1024


def _cparams(sem):
    return pltpu.CompilerParams(dimension_semantics=sem, vmem_limit_bytes=VMEM_LIMIT)


def _inproj_kernel(x_ref, g_ref, w_ref, a0_ref, a1_ref, a2_ref, or_ref, h_scr, y_scr):
    j = pl.program_id(1)

    @pl.when(j == 0)
    def _():
        x = x_ref[...]
        r = x * lax.rsqrt(jnp.mean(x * x, axis=-1, keepdims=True) + EPS)
        h_scr[...] = (r * g_ref[...]).astype(BF16)

    y = jnp.dot(h_scr[...], w_ref[...], preferred_element_type=F32)

    for grp, a_ref in enumerate((a0_ref, a1_ref, a2_ref)):
        d = DIL_PAIRS[grp][1]

        @pl.when(j == grp)
        def _(a_ref=a_ref, d=d):
            if d == 1:
                a_ref[0, 0] = y.astype(BF16)
            else:
                nct = PROJ_TN // 128
                for c in range(nct):
                    y_scr[c] = y[:, c * 128:(c + 1) * 128]
                n = y.shape[0] // d
                for r in range(d):
                    cls = [y_scr[c, pl.ds(r, n, stride=d), :] for c in range(nct)]
                    a_ref[0, r] = jnp.concatenate(cls, axis=1).astype(BF16)

    @pl.when(j >= N_DIL)
    def _():
        or_ref[...] = y.astype(BF16)


def _inproj(x, norm_g, w_all, tm=1024):
    B, S, _ = x.shape
    T = B * S
    spb = S // tm
    a_specs, a_shapes = [], []
    for _, d in DIL_PAIRS:
        a_specs.append(pl.BlockSpec((1, d, tm // d, PROJ_TN), lambda i, j: (i // spb, 0, i % spb, 0)))
        a_shapes.append(jax.ShapeDtypeStruct((B, d, S // d, PROJ_TN), BF16))
    return pl.pallas_call(
        _inproj_kernel,
        grid=(T // tm, N_DIL + PR // PROJ_TN),
        in_specs=[
            pl.BlockSpec((tm, D_MODEL), lambda i, j: (i, 0)),
            pl.BlockSpec((1, D_MODEL), lambda i, j: (0, 0)),
            pl.BlockSpec((D_MODEL, PROJ_TN), lambda i, j: (0, j)),
        ],
        out_specs=a_specs + [pl.BlockSpec((tm, PROJ_TN), lambda i, j: (i, jnp.maximum(j - N_DIL, 0)))],
        out_shape=a_shapes + [jax.ShapeDtypeStruct((T, PR), BF16)],
        scratch_shapes=[pltpu.VMEM((tm, D_MODEL), BF16), pltpu.VMEM((PROJ_TN // 128, tm, 128), F32)],
        compiler_params=_cparams(("parallel", "arbitrary")),
        name="inproj",
    )(x.reshape(T, D_MODEL), norm_g, w_all)


ATTN_ITEMS = 4


def _attn_kernel(q_ref, k_ref, v_ref, bias_ref, o_ref, lse_ref, *, L, wk, nc, nq, nblk):
    i = pl.program_id(2)
    lane = lax.broadcasted_iota(jnp.int32, (Q_BLK, A_GW), 1)
    qscale = jnp.asarray(A_DH ** -0.5, BF16)
    nt = (((1,), (1,)), ((), ()))
    for c in range(nc):
        for t in range(nq):
            blk = i * nq + t
            start = pl.multiple_of(jnp.clip(blk * Q_BLK - HALF_WIN, 0, L - wk), HALF_WIN)
            var = jnp.where(blk == 0, 0, jnp.where(blk == nblk - 1, 2, 1))
            rows = slice(t * Q_BLK, (t + 1) * Q_BLK)
            q = q_ref[0, c, rows, :] * qscale
            kw = k_ref[0, c, pl.ds(start, wk), :]
            vw = v_ref[0, c, pl.ds(start, wk), :]
            outs = []
            lse_tile = jnp.zeros((Q_BLK, A_GW), F32)
            for h in range(A_HPG):
                sl = slice(h * A_DH, (h + 1) * A_DH)
                s = lax.dot_general(q[:, sl], kw[:, sl], nt, preferred_element_type=F32) + bias_ref[var, h]
                m = jnp.max(s, axis=-1, keepdims=True)
                p = jnp.exp(s - m)
                l = jnp.sum(p, axis=-1, keepdims=True)
                outs.append(jnp.dot(p.astype(BF16), vw[:, sl], preferred_element_type=F32) / l)
                lse = m + jnp.log(l)
                lse_tile = jnp.where((lane >= A_DH * h) & (lane < A_DH * (h + 1)), lse, lse_tile)
            cols = slice(c * A_GW, (c + 1) * A_GW)
            o_ref[0, rows, cols] = jnp.concatenate(outs, axis=1).astype(BF16)
            lse_ref[0, rows, cols] = lse_tile


def _attn_group(qkv, bias_tiles, d, B, S):
    L = S // d
    nblk = L // Q_BLK
    wk = min(2 * Q_BLK, L)
    nq = min(ATTN_ITEMS, nblk)
    nc = ATTN_ITEMS // nq

    return pl.pallas_call(
        functools.partial(_attn_kernel, L=L, wk=wk, nc=nc, nq=nq, nblk=nblk),
        grid=(B, d // nc, nblk // nq),
        in_specs=[
            pl.BlockSpec((1, nc, nq * Q_BLK, A_GW), lambda b, r, i: (b, r, i, 0)),
            pl.BlockSpec((1, nc, L, A_GW), lambda b, r, i: (b, r, 0, 1)),
            pl.BlockSpec((1, nc, L, A_GW), lambda b, r, i: (b, r, 0, 2)),
            pl.BlockSpec(bias_tiles.shape, lambda b, r, i: (0, 0, 0, 0)),
        ],
        out_specs=[
            pl.BlockSpec((1, nq * Q_BLK, nc * A_GW), lambda b, r, i: (b, i, r)),
            pl.BlockSpec((1, nq * Q_BLK, nc * A_GW), lambda b, r, i: (b, i, r)),
        ],
        out_shape=[
            jax.ShapeDtypeStruct((B, L, d * A_GW), BF16),
            jax.ShapeDtypeStruct((B, L, d * A_GW), F32),
        ],
        compiler_params=_cparams(("parallel", "parallel", "arbitrary")),
        name=f"attn_d{d}",
    )(qkv, qkv, qkv, bias_tiles)


def _t5_buckets(rel):
    nb = N_REL_BUCKETS // 2
    max_exact = nb // 2
    ret = (rel > 0).astype(np.int64) * nb
    n = np.abs(rel)
    large = max_exact + (np.log(np.maximum(n, 1) / max_exact) / math.log(REL_MAX_DISTANCE / max_exact)
                         * (nb - max_exact)).astype(np.int64)
    large = np.minimum(large, nb - 1)
    return (ret + np.where(n < max_exact, n, large)).astype(np.int32)


def _bias_tiles(rel_bias, g, d, wk):
    rel = d * np.arange(-HALF_WIN, HALF_WIN + 1)
    bucket = _t5_buckets(rel)
    bias = rel_bias[jnp.asarray(bucket)][:, g * A_HPG:(g + 1) * A_HPG].T.astype(F32)
    pad = wk + Q_BLK
    neg = jnp.full((A_HPG, pad), NEG_INF, F32)
    val = jnp.concatenate([neg, bias, neg], axis=1)
    zero = pad + HALF_WIN
    m = wk + Q_BLK
    tiles = []
    for off in (0, HALF_WIN, wk - Q_BLK):
        u = jnp.concatenate([val[:, zero - off:zero - off + wk], val[:, zero - off - Q_BLK:zero - off]], axis=1)
        flat = jnp.tile(u, (1, Q_BLK))[:, :Q_BLK * (m - 1)]
        tiles.append(flat.reshape(A_HPG, Q_BLK, m - 1)[:, :, :wk])
    return jnp.stack(tiles)


def _log_sigmoid(z):
    return jnp.minimum(z, 0.0) - jnp.log1p(jnp.exp(-jnp.abs(z)))


def _gla_kernel(q_ref, k_ref, v_ref, og_ref, g_ref, wgf_ref, wgb_ref, bgf_ref, bgb_ref, gn_ref, o_ref,
                qif, kof, qib, kob, vt, etf, etb, acc, *, S):
    PB = 256
    C = GLA_CHUNK
    NB = S // PB
    rr = lax.broadcasted_iota(jnp.int32, (PB, PB), 0)
    cc = lax.broadcasted_iota(jnp.int32, (PB, PB), 1)
    same = (rr >> 6) == (cc >> 6)
    mask_f = same & (cc <= rr)
    mask_b = same & (cc > rr)
    tl = jnp.where(mask_f, 1.0, 0.0).astype(BF16)
    qscale = GLA_DK ** -0.5
    inv_tau = 1.0 / GATE_TAU
    nt = (((1,), (1,)), ((), ()))

    def split3(x):
        hi = x.astype(BF16)
        r1 = x - hi.astype(F32)
        mid = r1.astype(BF16)
        lo = (r1 - mid.astype(F32)).astype(BF16)
        return [hi, mid, lo]

    def chunk_total(b):
        b4 = b.reshape(PB // C, C, GLA_DK)
        return jnp.broadcast_to(b4[:, C - 1:C, :], b4.shape).reshape(PB, GLA_DK)

    def prep(i, carry):
        rows = pl.ds(pl.multiple_of(i * PB, PB), PB)
        g = g_ref[0, rows, :]
        q = q_ref[0, rows, :].astype(F32) * qscale
        k = k_ref[0, rows, :].astype(F32)
        v = v_ref[0, rows, :]
        lf = _log_sigmoid(jnp.dot(g, wgf_ref[...], preferred_element_type=F32) + bgf_ref[...]) * inv_tau
        lb = _log_sigmoid(jnp.dot(g, wgb_ref[...], preferred_element_type=F32) + bgb_ref[...]) * inv_tau
        cs = jnp.dot(tl, jnp.concatenate(split3(lf) + split3(lb), axis=1), preferred_element_type=F32)
        bf = cs[:, 0:128] + cs[:, 128:256] + cs[:, 256:384]
        pb = cs[:, 384:512] + cs[:, 512:640] + cs[:, 640:768]
        totf = chunk_total(bf)
        totb = chunk_total(pb)
        bs = totb - pb + lb
        qf = (q * jnp.exp(bf)).astype(BF16)
        qb = (q * jnp.exp(bs)).astype(BF16)
        sf = lax.dot_general(qf, (k * jnp.exp(-bf)).astype(BF16), nt, preferred_element_type=F32)
        sb = lax.dot_general(qb, (k * jnp.exp(-bs)).astype(BF16), nt, preferred_element_type=F32)
        p = jnp.where(mask_f, sf, jnp.where(mask_b, sb, 0.0)).astype(BF16)
        acc[rows, :] = jnp.dot(p, v, preferred_element_type=F32)
        qif[rows, :] = qf
        qib[rows, :] = qb
        kof[rows, :] = (k * jnp.exp(totf - bf)).astype(BF16)
        kob[rows, :] = (k * jnp.exp(totb - bs)).astype(BF16)
        etf[rows, :] = jnp.exp(totf)
        etb[rows, :] = jnp.exp(totb)
        vtb = v.astype(F32).T.astype(BF16)
        vt[2 * i] = vtb[:, :128]
        vt[2 * i + 1] = vtb[:, 128:]
        return carry

    lax.fori_loop(0, NB, prep, 0)

    zeros_half = jnp.zeros((C, GLA_DK), BF16)

    def chunk_step(qi_ref, ko_ref, et_ref, blk, j, state):
        r0 = pl.multiple_of(blk * PB + j * C, C)
        rows = pl.ds(r0, C)
        acc[rows, :] += lax.dot_general(qi_ref[rows, :], state.astype(BF16), nt, preferred_element_type=F32)
        ko = ko_ref[rows, :]
        ko_pad = jnp.concatenate([ko, zeros_half] if j % 2 == 0 else [zeros_half, ko], axis=0)
        upd = jnp.dot(vt[2 * blk + j // 2], ko_pad, preferred_element_type=F32)
        return state * et_ref[pl.ds(r0, 1), :] + upd

    def serial(t, carry):
        st_f, st_b = carry
        for j in range(PB // C):
            st_f = chunk_step(qif, kof, etf, t, j, st_f)
            st_b = chunk_step(qib, kob, etb, NB - 1 - t, PB // C - 1 - j, st_b)
        return st_f, st_b

    zero_state = jnp.zeros((GLA_DV, GLA_DK), F32)
    lax.fori_loop(0, NB, serial, (zero_state, zero_state))

    gn = gn_ref[...]

    def finish(i, carry):
        rows = pl.ds(pl.multiple_of(i * PB, PB), PB)
        tot = acc[rows, :]
        nrm = tot * lax.rsqrt(jnp.mean(tot * tot, axis=-1, keepdims=True) + EPS) * gn
        og = og_ref[0, rows, :].astype(F32)
        o_ref[0, rows, :] = (nrm * (og * jax.nn.sigmoid(og))).astype(BF16)
        return carry

    lax.fori_loop(0, NB, finish, 0)


def _gla(proj, wgf, wgb, bgf, bgb, gla_norm, B, S):
    pv = proj.reshape(B, S, PR)
    cq, ck, cv, cog, cg = C_QB // GLA_DK, C_KB // GLA_DK, C_VB // GLA_DV, C_OG // GLA_DV, C_G // 128
    return pl.pallas_call(
        functools.partial(_gla_kernel, S=S),
        grid=(B, GLA_H),
        in_specs=[
            pl.BlockSpec((1, S, GLA_DK), lambda b, h: (b, 0, cq + h)),
            pl.BlockSpec((1, S, GLA_DK), lambda b, h: (b, 0, ck + h)),
            pl.BlockSpec((1, S, GLA_DV), lambda b, h: (b, 0, cv + h)),
            pl.BlockSpec((1, S, GLA_DV), lambda b, h: (b, 0, cog + h)),
            pl.BlockSpec((1, S, 128), lambda b, h: (b, 0, cg)),
            pl.BlockSpec((128, GLA_DK), lambda b, h: (0, h)),
            pl.BlockSpec((128, GLA_DK), lambda b, h: (0, h)),
            pl.BlockSpec((1, GLA_DK), lambda b, h: (0, h)),
            pl.BlockSpec((1, GLA_DK), lambda b, h: (0, h)),
            pl.BlockSpec((1, GLA_DV), lambda b, h: (0, 0)),
        ],
        out_specs=pl.BlockSpec((1, S, GLA_DV), lambda b, h: (b, 0, h)),
        out_shape=jax.ShapeDtypeStruct((B, S, GLA_H * GLA_DV), BF16),
        scratch_shapes=[
            pltpu.VMEM((S, GLA_DK), BF16), pltpu.VMEM((S, GLA_DK), BF16),
            pltpu.VMEM((S, GLA_DK), BF16), pltpu.VMEM((S, GLA_DK), BF16),
            pltpu.VMEM((S // 128, GLA_DV, 128), BF16),
            pltpu.VMEM((S, GLA_DK), F32), pltpu.VMEM((S, GLA_DK), F32),
            pltpu.VMEM((S, GLA_DV), F32),
        ],
        compiler_params=_cparams(("parallel", "arbitrary")),
        name="gla",
    )(pv, pv, pv, pv, pv, wgf, wgb, bgf, bgb, gla_norm)


def _outproj_kernel(x_ref, ma_ref, mb_ref, o0_ref, o1_ref, o2_ref, l0_ref, l1_ref, l2_ref, ob_ref,
                    wpa_ref, wpb_ref, wo_ref, nm_ref, wr_ref, br_ref,
                    x1_ref, hm_ref, gate_ref, gatet_ref):
    l0, l1, l2 = l0_ref[...], l1_ref[...], l2_ref[...]
    m = jnp.maximum(jnp.maximum(l0, l1), l2)
    e0, e1, e2 = jnp.exp(l0 - m), jnp.exp(l1 - m), jnp.exp(l2 - m)
    den = e0 + e1 + e2
    oa = ((e0 / den) * o0_ref[...].astype(F32) + (e1 / den) * o1_ref[...].astype(F32)
          + (e2 / den) * o2_ref[...].astype(F32))
    ya = jnp.dot(oa.astype(BF16), wpa_ref[...], preferred_element_type=F32)
    yb = jnp.dot(ob_ref[...], wpb_ref[...], preferred_element_type=F32)
    mix = jax.nn.sigmoid(ma_ref[...].astype(F32)) * ya + jax.nn.sigmoid(mb_ref[...].astype(F32)) * yb
    x1 = x_ref[...] + jnp.dot(mix.astype(BF16), wo_ref[...], preferred_element_type=F32)
    x1_ref[...] = x1
    hm = x1 * lax.rsqrt(jnp.mean(x1 * x1, axis=-1, keepdims=True) + EPS) * nm_ref[...]
    hm_b = hm.astype(BF16)
    hm_ref[...] = hm_b

    logits = jnp.dot(hm_b, wr_ref[...], preferred_element_type=F32) + br_ref[...]
    tm = logits.shape[0]
    lane = lax.broadcasted_iota(jnp.int32, (tm, 128), 1)
    is_g = (lane >= N_EXPERTS) & (lane < 2 * N_EXPERTS)
    is_e = lane < N_EXPERTS
    grp_of_lane = jnp.where(is_g, (lane - N_EXPERTS) >> 3, lane >> 3)
    gl = jnp.where(is_g, logits, NEG_INF)
    gmax = jnp.max(gl, axis=-1, keepdims=True)
    gsum = jnp.sum(jnp.where(is_g, jnp.exp(gl - gmax), 0.0), axis=-1, keepdims=True) * (1.0 / EPG)
    pg_top = 1.0 / gsum
    g_sel = jnp.min(jnp.where(is_g & (gl == gmax), grp_of_lane, N_GROUPS), axis=-1, keepdims=True)
    in_grp = is_e & (grp_of_lane == g_sel)
    el = jnp.where(in_grp, logits, NEG_INF)
    emax = jnp.max(el, axis=-1, keepdims=True)
    ee = jnp.where(in_grp, jnp.exp(el - emax), 0.0)
    pe = ee / jnp.sum(ee, axis=-1, keepdims=True)
    p1 = jnp.max(pe, axis=-1, keepdims=True)
    i1 = jnp.min(jnp.where(in_grp & (pe == p1), lane, 128), axis=-1, keepdims=True)
    rest = in_grp & (lane != i1)
    pe2 = jnp.where(rest, pe, -1.0)
    p2 = jnp.max(pe2, axis=-1, keepdims=True)
    i2 = jnp.min(jnp.where(rest & (pe2 == p2), lane, 128), axis=-1, keepdims=True)
    psum = p1 + p2
    gate = jnp.where(lane == i1, pg_top * (p1 / psum), jnp.where(lane == i2, pg_top * (p2 / psum), 0.0))
    gate_ref[...] = gate
    gatet_ref[...] = gate.T


def _outproj(x2d, proj, o_list, lse_list, ob2d, wpa, wpb, wo, norm_moe, wr, br, tm=512):
    T = x2d.shape[0]
    row = lambda w: pl.BlockSpec((tm, w), lambda i: (i, 0))
    full = lambda a: pl.BlockSpec(a.shape, lambda i: (0,) * a.ndim)
    return pl.pallas_call(
        _outproj_kernel,
        grid=(T // tm,),
        in_specs=[
            row(D_MODEL),
            pl.BlockSpec((tm, D_MODEL), lambda i: (i, C_MA // D_MODEL)),
            pl.BlockSpec((tm, D_MODEL), lambda i: (i, C_MB // D_MODEL)),
            row(A_GW), row(A_GW), row(A_GW), row(A_GW), row(A_GW), row(A_GW),
            row(D_MODEL),
            full(wpa), full(wpb), full(wo), full(norm_moe), full(wr), full(br),
        ],
        out_specs=[row(D_MODEL), row(D_MODEL), row(128), pl.BlockSpec((128, tm), lambda i: (0, i))],
        out_shape=[
            jax.ShapeDtypeStruct((T, D_MODEL), F32),
            jax.ShapeDtypeStruct((T, D_MODEL), BF16),
            jax.ShapeDtypeStruct((T, 128), F32),
            jax.ShapeDtypeStruct((128, T), F32),
        ],
        compiler_params=_cparams(("parallel",)),
        name="outproj",
    )(x2d, proj, proj, *o_list, *lse_list, ob2d, wpa, wpb, wo, norm_moe, wr, br)


MOE_TB = 1024
MOE_ALIGN = 16
MOE_CH = 128
MOE_R = 2 * MOE_TB + N_EXPERTS * MOE_ALIGN
MOE_RS = MOE_R + MOE_CH
MOE_SUB = 256
MOE_G = 4


def _moe_route(hm_ref, gate_ref, gatet_ref, xs, ys, pt, wrow, rinfo):
    TB, R, SUB = MOE_TB, MOE_R, MOE_SUB
    big = float(4 * TB)
    gate = gate_ref[...]
    gt = gatet_ref[...]
    a = gate > 0.0
    at = gt > 0.0
    a_b = jnp.where(a, 1.0, 0.0).astype(BF16)
    at_b = jnp.where(at, 1.0, 0.0).astype(BF16)

    def pad(c):
        return jnp.floor((c + (MOE_ALIGN - 1)) * (1.0 / MOE_ALIGN)) * MOE_ALIGN

    er = lax.broadcasted_iota(jnp.int32, (128, 128), 0)
    ec = lax.broadcasted_iota(jnp.int32, (128, 128), 1)
    pad_row = pad(jnp.sum(jnp.where(a, 1.0, 0.0), axis=0, keepdims=True))
    off_row = jnp.dot(jnp.broadcast_to(pad_row, (8, 128)).astype(BF16), jnp.where(er < ec, 1.0, 0.0).astype(BF16),
                      preferred_element_type=F32)[0:1]
    pad_col = pad(jnp.sum(jnp.where(at, 1.0, 0.0), axis=1, keepdims=True))
    off_col = jnp.dot(jnp.where(ec < er, 1.0, 0.0).astype(BF16), jnp.broadcast_to(pad_col, (128, 128)).astype(BF16),
                      preferred_element_type=F32)[:, 0:1]

    rl = lax.broadcasted_iota(jnp.int32, (SUB, R), 1).astype(F32)
    for rc in range(TB // SUB):
        rows = slice(rc * SUB, (rc + 1) * SUB)
        tr = lax.broadcasted_iota(jnp.int32, (SUB, TB), 0) + rc * SUB
        tc = lax.broadcasted_iota(jnp.int32, (SUB, TB), 1)
        rank = jnp.dot(jnp.where(tc < tr, 1.0, 0.0).astype(BF16), a_b, preferred_element_type=F32)
        pos = off_row + rank
        a_c = a[rows]
        plo = jnp.min(jnp.where(a_c, pos, big), axis=1, keepdims=True)
        phi = jnp.max(jnp.where(a_c, pos, -1.0), axis=1, keepdims=True)
        phi = jnp.where(phi == plo, -1.0, phi)
        pt[rows, :] = jnp.where((rl == plo) | (rl == phi), 1.0, 0.0).astype(BF16)

    for cc in range(TB // SUB):
        cols = slice(cc * SUB, (cc + 1) * SUB)
        tr = lax.broadcasted_iota(jnp.int32, (TB, SUB), 0)
        tc = lax.broadcasted_iota(jnp.int32, (TB, SUB), 1) + cc * SUB
        rank_t = jnp.dot(at_b, jnp.where(tr < tc, 1.0, 0.0).astype(BF16), preferred_element_type=F32)
        pos_t = off_col + rank_t
        at_c = at[:, cols]
        gt_c = gt[:, cols]
        plo = jnp.min(jnp.where(at_c, pos_t, big), axis=0, keepdims=True)
        phi = jnp.max(jnp.where(at_c, pos_t, -1.0), axis=0, keepdims=True)
        phi = jnp.where(phi == plo, -1.0, phi)
        rinfo[0:1, cols] = plo
        rinfo[1:2, cols] = phi
        rinfo[2:3, cols] = jnp.sum(jnp.where(at_c & (pos_t == plo), gt_c, 0.0), axis=0, keepdims=True)
        rinfo[3:4, cols] = jnp.sum(jnp.where(at_c & (pos_t == phi), gt_c, 0.0), axis=0, keepdims=True)

    def gather(k, carry):
        r0 = pl.multiple_of(k * SUB, SUB)
        ri = (lax.broadcasted_iota(jnp.int32, (SUB, TB), 0) + r0).astype(F32)
        mlo = ri == rinfo[0:1, :]
        mhi = ri == rinfo[1:2, :]
        p = jnp.where(mlo | mhi, 1.0, 0.0).astype(BF16)
        xs[pl.ds(r0, SUB), :] = jnp.dot(p, hm_ref[...], preferred_element_type=F32).astype(BF16)
        w = jnp.sum(jnp.where(mlo, rinfo[2:3, :], 0.0) + jnp.where(mhi, rinfo[3:4, :], 0.0), axis=1, keepdims=True)
        wrow[pl.ds(r0, SUB), :] = jnp.broadcast_to(w, (SUB, 128))
        return carry

    lax.fori_loop(0, R // SUB, gather, 0)
    xs[R:, :] = jnp.zeros((MOE_RS - R, D_MODEL), BF16)
    wrow[R:, :] = jnp.zeros((MOE_RS - R, 128), F32)
    ys[...] = jnp.zeros_like(ys)


def _moe_kernel(meta_ref, hm_ref, gate_ref, gatet_ref, x1_ref, wgu_ref, wd_ref, nf_ref, y_ref,
                xs, ys, pt, wrow, rinfo):
    i = pl.program_id(0)
    s = pl.program_id(1)

    @pl.when(s == 0)
    def _():
        _moe_route(hm_ref, gate_ref, gatet_ref, xs, ys, pt, wrow, rinfo)

    offs = [meta_ref[i, s * MOE_G + g] for g in range(MOE_G)]
    nchs = [meta_ref[i, N_EXPERTS + s * MOE_G + g] for g in range(MOE_G)]

    def chunk(g, r0):
        rows = pl.ds(pl.multiple_of(r0, MOE_ALIGN), MOE_CH)
        gu = jnp.dot(xs[rows, :], wgu_ref[g], preferred_element_type=F32)
        gt, up = gu[:, :EXPERT_FF], gu[:, EXPERT_FF:]
        act = (gt * jax.nn.sigmoid(gt)) * up
        w = wrow[rows, :]
        act = jnp.concatenate([act[:, :128] * w, act[:, 128:] * w], axis=1)
        ys[rows, :] = jnp.dot(act.astype(BF16), wd_ref[g], preferred_element_type=F32).astype(BF16)

    single = nchs[0] <= 1
    for g in range(1, MOE_G):
        single = jnp.logical_and(single, nchs[g] <= 1)

    @pl.when(single)
    def _():
        for g in range(MOE_G):
            chunk(g, offs[g])

    @pl.when(jnp.logical_not(single))
    def _():
        for g in range(MOE_G):
            def body(c, carry, g=g):
                chunk(g, offs[g] + c * MOE_CH)
                return carry
            lax.fori_loop(0, nchs[g], body, 0)

    @pl.when(s == N_EXPERTS // MOE_G - 1)
    def _():
        x2 = x1_ref[...] + jnp.dot(pt[...], ys[0:MOE_R, :], preferred_element_type=F32)
        y_ref[...] = x2 * lax.rsqrt(jnp.mean(x2 * x2, axis=-1, keepdims=True) + EPS) * nf_ref[...]


def _moe(hm, gate, gatet, x1, wgu, wd, norm_final):
    T = hm.shape[0]
    tb = MOE_TB
    nb = T // tb
    cnt = jnp.sum((gate.reshape(nb, tb, 128)[:, :, :N_EXPERTS] > 0.0).astype(jnp.int32), axis=1)
    padded = (cnt + (MOE_ALIGN - 1)) // MOE_ALIGN * MOE_ALIGN
    off = jnp.cumsum(padded, axis=1) - padded
    meta = jnp.concatenate([off, (padded + (MOE_CH - 1)) // MOE_CH], axis=1).astype(jnp.int32)
    grid_spec = pltpu.PrefetchScalarGridSpec(
        num_scalar_prefetch=1,
        grid=(nb, N_EXPERTS // MOE_G),
        in_specs=[
            pl.BlockSpec((tb, D_MODEL), lambda i, e, m: (i, 0)),
            pl.BlockSpec((tb, 128), lambda i, e, m: (i, 0)),
            pl.BlockSpec((128, tb), lambda i, e, m: (0, i)),
            pl.BlockSpec((tb, D_MODEL), lambda i, e, m: (i, 0)),
            pl.BlockSpec((MOE_G, D_MODEL, 2 * EXPERT_FF), lambda i, e, m: (e, 0, 0)),
            pl.BlockSpec((MOE_G, EXPERT_FF, D_MODEL), lambda i, e, m: (e, 0, 0)),
            pl.BlockSpec((1, D_MODEL), lambda i, e, m: (0, 0)),
        ],
        out_specs=pl.BlockSpec((tb, D_MODEL), lambda i, e, m: (i, 0)),
        scratch_shapes=[
            pltpu.VMEM((MOE_RS, D_MODEL), BF16),
            pltpu.VMEM((MOE_RS, D_MODEL), BF16),
            pltpu.VMEM((tb, MOE_R), BF16),
            pltpu.VMEM((MOE_RS, 128), F32),
            pltpu.VMEM((8, tb), F32),
        ],
    )
    return pl.pallas_call(
        _moe_kernel,
        grid_spec=grid_spec,
        out_shape=jax.ShapeDtypeStruct((T, D_MODEL), F32),
        compiler_params=_cparams(("parallel", "arbitrary")),
        name="moe",
    )(meta, hm, gate, gatet, x1, wgu, wd, norm_final)


def _prep_weights(norm_mix, w_in, rel_bias, w_gate_f, b_gate_f, w_gate_b, b_gate_b, gla_norm, w_proj_a,
                  w_proj_b, w_out, norm_moe, w_rg, b_rg, w_re, b_re, w_eg, w_eu, w_ed, norm_final):
    splits = (A_QKV, A_QKV, A_QKV, 512, 512, 1024, 1024, GATE_RANK, GATE_RANK, D_MODEL, D_MODEL)
    qa, ka, va, qb, kb, vb, og, gf, gb, ma, mb = jnp.split(w_in, np.cumsum(splits)[:-1].tolist(), axis=1)
    gpad = jnp.zeros((D_MODEL, PR - C_G - 2 * GATE_RANK), w_in.dtype)
    grp = lambda t, g: t[:, g * A_GW:(g + 1) * A_GW]
    a_cols = [grp(t, g) for g in range(N_DIL) for t in (qa, ka, va)]
    w_all = jnp.concatenate(a_cols + [ma, mb, qb, kb, vb, og, gf, gb, gpad], axis=1).astype(BF16)
    kw = GLA_H * GLA_DK
    wgf = jnp.zeros((128, kw), F32).at[:GATE_RANK].set(w_gate_f).astype(BF16)
    wgb = jnp.zeros((128, kw), F32).at[GATE_RANK:2 * GATE_RANK].set(w_gate_b).astype(BF16)
    wr = jnp.zeros((D_MODEL, 128), F32)
    wr = wr.at[:, :N_EXPERTS].set(w_re).at[:, N_EXPERTS:2 * N_EXPERTS].set(jnp.repeat(w_rg, EPG, axis=1))
    br = jnp.zeros((1, 128), F32)
    br = br.at[0, :N_EXPERTS].set(b_re).at[0, N_EXPERTS:2 * N_EXPERTS].set(jnp.repeat(b_rg, EPG))
    return dict(
        norm_mix=norm_mix.reshape(1, D_MODEL), w_all=w_all, rel_bias=rel_bias,
        wgf=wgf, wgb=wgb, bgf=b_gate_f.reshape(1, kw), bgb=b_gate_b.reshape(1, kw),
        gla_norm=gla_norm.reshape(1, GLA_DV),
        wpa=w_proj_a.astype(BF16), wpb=w_proj_b.astype(BF16), wo=w_out.astype(BF16),
        norm_moe=norm_moe.reshape(1, D_MODEL), wr=wr.astype(BF16), br=br,
        wgu=jnp.concatenate([w_eg, w_eu], axis=-1).astype(BF16), wd=w_ed.astype(BF16),
        norm_final=norm_final.reshape(1, D_MODEL),
    )


def _trunk(x, w):
    B, S, _ = x.shape
    T = B * S
    x2d = x.reshape(T, D_MODEL)
    *qkv, proj = _inproj(x, w["norm_mix"], w["w_all"])
    o_list, lse_list = [], []
    for g, (_, d) in enumerate(DIL_PAIRS):
        L = S // d
        tiles = _bias_tiles(w["rel_bias"], g, d, min(2 * Q_BLK, L))
        o, lse = _attn_group(qkv[g], tiles, d, B, S)
        o_list.append(o.reshape(T, A_GW))
        lse_list.append(lse.reshape(T, A_GW))
    ob = _gla(proj, w["wgf"], w["wgb"], w["bgf"], w["bgb"], w["gla_norm"], B, S)
    x1, hm, gate, gatet = _outproj(x2d, proj, o_list, lse_list, ob.reshape(T, D_MODEL), w["wpa"], w["wpb"],
                                   w["wo"], w["norm_moe"], w["wr"], w["br"])
    y = _moe(hm, gate, gatet, x1, w["wgu"], w["wd"], w["norm_final"])
    return y.reshape(B, S, D_MODEL)


def kernel(x_prompt, x_sample, norm_mix, w_in, rel_bias, w_gate_f, b_gate_f, w_gate_b, b_gate_b, gla_norm,
           w_proj_a, w_proj_b, w_out, norm_moe, w_router_group, b_router_group, w_router_expert,
           b_router_expert, w_exp_gate, w_exp_up, w_exp_down, norm_final):
    w = _prep_weights(norm_mix[0], w_in[0], rel_bias, w_gate_f[0], b_gate_f[0], w_gate_b[0], b_gate_b[0],
                      gla_norm[0], w_proj_a[0], w_proj_b[0], w_out[0], norm_moe[0], w_router_group[0],
                      b_router_group[0], w_router_expert[0], b_router_expert[0], w_exp_gate[0], w_exp_up[0],
                      w_exp_down[0], norm_final)
    return (_trunk(x_prompt, w), _trunk(x_sample, w))
```

```python
import functools
import math

import numpy as np
import jax
import jax.numpy as jnp
from jax import lax
from jax.experimental import pallas as pl
from jax.experimental.pallas import tpu as pltpu

F32 = jnp.float32
BF16 = jnp.bfloat16

D_MODEL = 1024
EPS = 1e-6
NEG_INF = -1e30

DIL_PAIRS = ((128, 1), (512, 4), (2048, 16))
A_HPG = 4
A_DH = 64
A_GW = A_HPG * A_DH
A_QKV = 3 * A_GW
HALF_WIN = 64
Q_BLK = 128
N_REL_BUCKETS = 32
REL_MAX_DISTANCE = 1024
GLA_H = 4
GLA_DK = 128
GLA_DV = 256
GATE_RANK = 16
GATE_TAU = 16.0
GLA_CHUNK = 64
N_GROUPS = 4
EPG = 8
N_EXPERTS = 32
EXPERT_FF = 256

N_DIL = len(DIL_PAIRS)
C_MA, C_MB = 0, 1024
C_QB, C_KB, C_VB, C_OG = 2048, 2560, 3072, 4096
C_G = 5120
PR = 5376
PROJ_TN = A_QKV

VMEM_LIMIT = 58 * 1024 * 1024


def _cparams(sem):
    return pltpu.CompilerParams(dimension_semantics=sem, vmem_limit_bytes=VMEM_LIMIT)


def _inproj_kernel(x_ref, g_ref, w_ref, a0_ref, a1_ref, a2_ref, or_ref, y_scr):
    x = x_ref[...]
    r = x * lax.rsqrt(jnp.mean(x * x, axis=-1, keepdims=True) + EPS)
    h = (r * g_ref[...]).astype(BF16)
    nct = PROJ_TN // 128
    for j in range(N_DIL + PR // PROJ_TN):
        y = jnp.dot(h, w_ref[:, j * PROJ_TN:(j + 1) * PROJ_TN], preferred_element_type=F32)
        if j >= N_DIL:
            or_ref[:, (j - N_DIL) * PROJ_TN:(j - N_DIL + 1) * PROJ_TN] = y.astype(BF16)
            continue
        a_ref = (a0_ref, a1_ref, a2_ref)[j]
        d = DIL_PAIRS[j][1]
        if d == 1:
            a_ref[0, 0] = y.astype(BF16)
            continue
        for c in range(nct):
            y_scr[j - 1, c] = y[:, c * 128:(c + 1) * 128]
        n = y.shape[0] // d
        for cls in range(d):
            cols = [y_scr[j - 1, c, pl.ds(cls, n, stride=d), :] for c in range(nct)]
            a_ref[0, cls] = jnp.concatenate(cols, axis=1).astype(BF16)


def _inproj(x, norm_g, w_all, tm=512):
    B, S, _ = x.shape
    T = B * S
    spb = S // tm
    a_specs, a_shapes = [], []
    for _, d in DIL_PAIRS:
        a_specs.append(pl.BlockSpec((1, d, tm // d, PROJ_TN), lambda i: (i // spb, 0, i % spb, 0)))
        a_shapes.append(jax.ShapeDtypeStruct((B, d, S // d, PROJ_TN), BF16))
    return pl.pallas_call(
        _inproj_kernel,
        grid=(T // tm,),
        in_specs=[
            pl.BlockSpec((tm, D_MODEL), lambda i: (i, 0)),
            pl.BlockSpec((1, D_MODEL), lambda i: (0, 0)),
            pl.BlockSpec(w_all.shape, lambda i: (0, 0), pipeline_mode=pl.Buffered(1)),
        ],
        out_specs=a_specs + [pl.BlockSpec((tm, PR), lambda i: (i, 0))],
        out_shape=a_shapes + [jax.ShapeDtypeStruct((T, PR), BF16)],
        scratch_shapes=[pltpu.VMEM((N_DIL - 1, PROJ_TN // 128, tm, 128), F32)],
        compiler_params=_cparams(("parallel",)),
        name="inproj",
    )(x.reshape(T, D_MODEL), norm_g, w_all)


ATTN_ITEMS = 4


def _attn_kernel(q_ref, k_ref, v_ref, bias_ref, o_ref, lse_ref, *, L, wk, nc, nq, nblk):
    i = pl.program_id(2)
    lane = lax.broadcasted_iota(jnp.int32, (Q_BLK, A_GW), 1)
    qscale = jnp.asarray(A_DH ** -0.5, BF16)
    nt = (((1,), (1,)), ((), ()))
    for c in range(nc):
        for t in range(nq):
            blk = i * nq + t
            start = pl.multiple_of(jnp.clip(blk * Q_BLK - HALF_WIN, 0, L - wk), HALF_WIN)
            var = jnp.where(blk == 0, 0, jnp.where(blk == nblk - 1, 2, 1))
            rows = slice(t * Q_BLK, (t + 1) * Q_BLK)
            q = q_ref[0, c, rows, :] * qscale
            kw = k_ref[0, c, pl.ds(start, wk), :]
            vw = v_ref[0, c, pl.ds(start, wk), :]
            outs = []
            lse_tile = jnp.zeros((Q_BLK, A_GW), F32)
            for h in range(A_HPG):
                sl = slice(h * A_DH, (h + 1) * A_DH)
                s = lax.dot_general(q[:, sl], kw[:, sl], nt, preferred_element_type=F32) + bias_ref[var, h]
                m = jnp.max(s, axis=-1, keepdims=True)
                p = jnp.exp(s - m)
                l = jnp.sum(p, axis=-1, keepdims=True)
                outs.append(jnp.dot(p.astype(BF16), vw[:, sl], preferred_element_type=F32) / l)
                lse = m + jnp.log(l)
                lse_tile = jnp.where((lane >= A_DH * h) & (lane < A_DH * (h + 1)), lse, lse_tile)
            cols = slice(c * A_GW, (c + 1) * A_GW)
            o_ref[0, rows, cols] = jnp.concatenate(outs, axis=1).astype(BF16)
            lse_ref[0, rows, cols] = lse_tile


def _attn_group(qkv, bias_tiles, d, B, S):
    L = S // d
    nblk = L // Q_BLK
    wk = min(2 * Q_BLK, L)
    nq = min(ATTN_ITEMS, nblk)
    nc = ATTN_ITEMS // nq

    return pl.pallas_call(
        functools.partial(_attn_kernel, L=L, wk=wk, nc=nc, nq=nq, nblk=nblk),
        grid=(B, d // nc, nblk // nq),
        in_specs=[
            pl.BlockSpec((1, nc, nq * Q_BLK, A_GW), lambda b, r, i: (b, r, i, 0)),
            pl.BlockSpec((1, nc, L, A_GW), lambda b, r, i: (b, r, 0, 1)),
            pl.BlockSpec((1, nc, L, A_GW), lambda b, r, i: (b, r, 0, 2)),
            pl.BlockSpec(bias_tiles.shape, lambda b, r, i: (0, 0, 0, 0)),
        ],
        out_specs=[
            pl.BlockSpec((1, nq * Q_BLK, nc * A_GW), lambda b, r, i: (b, i, r)),
            pl.BlockSpec((1, nq * Q_BLK, nc * A_GW), lambda b, r, i: (b, i, r)),
        ],
        out_shape=[
            jax.ShapeDtypeStruct((B, L, d * A_GW), BF16),
            jax.ShapeDtypeStruct((B, L, d * A_GW), F32),
        ],
        compiler_params=_cparams(("parallel", "parallel", "arbitrary")),
        name=f"attn_d{d}",
    )(qkv, qkv, qkv, bias_tiles)


def _t5_buckets(rel):
    nb = N_REL_BUCKETS // 2
    max_exact = nb // 2
    ret = (rel > 0).astype(np.int64) * nb
    n = np.abs(rel)
    large = max_exact + (np.log(np.maximum(n, 1) / max_exact) / math.log(REL_MAX_DISTANCE / max_exact)
                         * (nb - max_exact)).astype(np.int64)
    large = np.minimum(large, nb - 1)
    return (ret + np.where(n < max_exact, n, large)).astype(np.int32)


def _bias_tiles(rel_bias, g, d, wk):
    rel = d * np.arange(-HALF_WIN, HALF_WIN + 1)
    bucket = _t5_buckets(rel)
    bias = rel_bias[jnp.asarray(bucket)][:, g * A_HPG:(g + 1) * A_HPG].T.astype(F32)
    pad = wk + Q_BLK
    neg = jnp.full((A_HPG, pad), NEG_INF, F32)
    val = jnp.concatenate([neg, bias, neg], axis=1)
    zero = pad + HALF_WIN
    m = wk + Q_BLK
    tiles = []
    for off in (0, HALF_WIN, wk - Q_BLK):
        u = jnp.concatenate([val[:, zero - off:zero - off + wk], val[:, zero - off - Q_BLK:zero - off]], axis=1)
        flat = jnp.tile(u, (1, Q_BLK))[:, :Q_BLK * (m - 1)]
        tiles.append(flat.reshape(A_HPG, Q_BLK, m - 1)[:, :, :wk])
    return jnp.stack(tiles)


def _log_sigmoid(z):
    return jnp.minimum(z, 0.0) - jnp.log1p(jnp.exp(-jnp.abs(z)))


def _gla_kernel(q_ref, k_ref, v_ref, og_ref, g_ref, wgf_ref, wgb_ref, bgf_ref, bgb_ref, gn_ref, o_ref,
                qif, kof, qib, kob, vt, etf, etb, acc, *, S):
    PB = 256
    C = GLA_CHUNK
    NB = S // PB
    rr = lax.broadcasted_iota(jnp.int32, (PB, PB), 0)
    cc = lax.broadcasted_iota(jnp.int32, (PB, PB), 1)
    same = (rr >> 6) == (cc >> 6)
    mask_f = same & (cc <= rr)
    mask_b = same & (cc > rr)
    tl = jnp.where(mask_f, 1.0, 0.0).astype(BF16)
    qscale = GLA_DK ** -0.5
    inv_tau = 1.0 / GATE_TAU
    nt = (((1,), (1,)), ((), ()))

    def split3(x):
        hi = x.astype(BF16)
        r1 = x - hi.astype(F32)
        mid = r1.astype(BF16)
        lo = (r1 - mid.astype(F32)).astype(BF16)
        return [hi, mid, lo]

    def chunk_total(b):
        b4 = b.reshape(PB // C, C, GLA_DK)
        return jnp.broadcast_to(b4[:, C - 1:C, :], b4.shape).reshape(PB, GLA_DK)

    def prep(i, carry):
        rows = pl.ds(pl.multiple_of(i * PB, PB), PB)
        g = g_ref[0, rows, :]
        q = q_ref[0, rows, :].astype(F32) * qscale
        k = k_ref[0, rows, :].astype(F32)
        v = v_ref[0, rows, :]
        lf = _log_sigmoid(jnp.dot(g, wgf_ref[...], preferred_element_type=F32) + bgf_ref[...]) * inv_tau
        lb = _log_sigmoid(jnp.dot(g, wgb_ref[...], preferred_element_type=F32) + bgb_ref[...]) * inv_tau
        cs = jnp.dot(tl, jnp.concatenate(split3(lf) + split3(lb), axis=1), preferred_element_type=F32)
        bf = cs[:, 0:128] + cs[:, 128:256] + cs[:, 256:384]
        pb = cs[:, 384:512] + cs[:, 512:640] + cs[:, 640:768]
        totf = chunk_total(bf)
        totb = chunk_total(pb)
        bs = totb - pb + lb
        qf = (q * jnp.exp(bf)).astype(BF16)
        qb = (q * jnp.exp(bs)).astype(BF16)
        sf = lax.dot_general(qf, (k * jnp.exp(-bf)).astype(BF16), nt, preferred_element_type=F32)
        sb = lax.dot_general(qb, (k * jnp.exp(-bs)).astype(BF16), nt, preferred_element_type=F32)
        p = jnp.where(mask_f, sf, jnp.where(mask_b, sb, 0.0)).astype(BF16)
        acc[rows, :] = jnp.dot(p, v, preferred_element_type=F32)
        qif[rows, :] = qf
        qib[rows, :] = qb
        kof[rows, :] = (k * jnp.exp(totf - bf)).astype(BF16)
        kob[rows, :] = (k * jnp.exp(totb - bs)).astype(BF16)
        etf[rows, :] = jnp.exp(totf)
        etb[rows, :] = jnp.exp(totb)
        vtb = v.astype(F32).T.astype(BF16)
        vt[2 * i] = vtb[:, :128]
        vt[2 * i + 1] = vtb[:, 128:]
        return carry

    lax.fori_loop(0, NB, prep, 0)

    zeros_half = jnp.zeros((C, GLA_DK), BF16)

    def chunk_step(qi_ref, ko_ref, et_ref, blk, j, state):
        r0 = pl.multiple_of(blk * PB + j * C, C)
        rows = pl.ds(r0, C)
        acc[rows, :] += lax.dot_general(qi_ref[rows, :], state.astype(BF16), nt, preferred_element_type=F32)
        ko = ko_ref[rows, :]
        ko_pad = jnp.concatenate([ko, zeros_half] if j % 2 == 0 else [zeros_half, ko], axis=0)
        upd = jnp.dot(vt[2 * blk + j // 2], ko_pad, preferred_element_type=F32)
        return state * et_ref[pl.ds(r0, 1), :] + upd

    def serial(t, carry):
        st_f, st_b = carry
        for j in range(PB // C):
            st_f = chunk_step(qif, kof, etf, t, j, st_f)
            st_b = chunk_step(qib, kob, etb, NB - 1 - t, PB // C - 1 - j, st_b)
        return st_f, st_b

    zero_state = jnp.zeros((GLA_DV, GLA_DK), F32)
    lax.fori_loop(0, NB, serial, (zero_state, zero_state))

    gn = gn_ref[...]

    def finish(i, carry):
        rows = pl.ds(pl.multiple_of(i * PB, PB), PB)
        tot = acc[rows, :]
        nrm = tot * lax.rsqrt(jnp.mean(tot * tot, axis=-1, keepdims=True) + EPS) * gn
        og = og_ref[0, rows, :].astype(F32)
        o_ref[0, rows, :] = (nrm * (og * jax.nn.sigmoid(og))).astype(BF16)
        return carry

    lax.fori_loop(0, NB, finish, 0)


def _gla(proj, wgf, wgb, bgf, bgb, gla_norm, B, S):
    pv = proj.reshape(B, S, PR)
    cq, ck, cv, cog, cg = C_QB // GLA_DK, C_KB // GLA_DK, C_VB // GLA_DV, C_OG // GLA_DV, C_G // 128
    return pl.pallas_call(
        functools.partial(_gla_kernel, S=S),
        grid=(B, GLA_H),
        in_specs=[
            pl.BlockSpec((1, S, GLA_DK), lambda b, h: (b, 0, cq + h)),
            pl.BlockSpec((1, S, GLA_DK), lambda b, h: (b, 0, ck + h)),
            pl.BlockSpec((1, S, GLA_DV), lambda b, h: (b, 0, cv + h)),
            pl.BlockSpec((1, S, GLA_DV), lambda b, h: (b, 0, cog + h)),
            pl.BlockSpec((1, S, 128), lambda b, h: (b, 0, cg)),
            pl.BlockSpec((128, GLA_DK), lambda b, h: (0, h)),
            pl.BlockSpec((128, GLA_DK), lambda b, h: (0, h)),
            pl.BlockSpec((1, GLA_DK), lambda b, h: (0, h)),
            pl.BlockSpec((1, GLA_DK), lambda b, h: (0, h)),
            pl.BlockSpec((1, GLA_DV), lambda b, h: (0, 0)),
        ],
        out_specs=pl.BlockSpec((1, S, GLA_DV), lambda b, h: (b, 0, h)),
        out_shape=jax.ShapeDtypeStruct((B, S, GLA_H * GLA_DV), BF16),
        scratch_shapes=[
            pltpu.VMEM((S, GLA_DK), BF16), pltpu.VMEM((S, GLA_DK), BF16),
            pltpu.VMEM((S, GLA_DK), BF16), pltpu.VMEM((S, GLA_DK), BF16),
            pltpu.VMEM((S // 128, GLA_DV, 128), BF16),
            pltpu.VMEM((S, GLA_DK), F32), pltpu.VMEM((S, GLA_DK), F32),
            pltpu.VMEM((S, GLA_DV), F32),
        ],
        compiler_params=_cparams(("parallel", "arbitrary")),
        name="gla",
    )(pv, pv, pv, pv, pv, wgf, wgb, bgf, bgb, gla_norm)


def _outproj_kernel(x_ref, ma_ref, mb_ref, o0_ref, o1_ref, o2_ref, l0_ref, l1_ref, l2_ref, ob_ref,
                    wpa_ref, wpb_ref, wo_ref, nm_ref, wr_ref, br_ref,
                    x1_ref, hm_ref, gate_ref, gatet_ref):
    l0, l1, l2 = l0_ref[...], l1_ref[...], l2_ref[...]
    m = jnp.maximum(jnp.maximum(l0, l1), l2)
    e0, e1, e2 = jnp.exp(l0 - m), jnp.exp(l1 - m), jnp.exp(l2 - m)
    den = e0 + e1 + e2
    oa = ((e0 / den) * o0_ref[...].astype(F32) + (e1 / den) * o1_ref[...].astype(F32)
          + (e2 / den) * o2_ref[...].astype(F32))
    ya = jnp.dot(oa.astype(BF16), wpa_ref[...], preferred_element_type=F32)
    yb = jnp.dot(ob_ref[...], wpb_ref[...], preferred_element_type=F32)
    mix = jax.nn.sigmoid(ma_ref[...].astype(F32)) * ya + jax.nn.sigmoid(mb_ref[...].astype(F32)) * yb
    x1 = x_ref[...] + jnp.dot(mix.astype(BF16), wo_ref[...], preferred_element_type=F32)
    x1_ref[...] = x1
    hm = x1 * lax.rsqrt(jnp.mean(x1 * x1, axis=-1, keepdims=True) + EPS) * nm_ref[...]
    hm_b = hm.astype(BF16)
    hm_ref[...] = hm_b

    logits = jnp.dot(hm_b, wr_ref[...], preferred_element_type=F32) + br_ref[...]
    tm = logits.shape[0]
    lane = lax.broadcasted_iota(jnp.int32, (tm, 128), 1)
    is_g = (lane >= N_EXPERTS) & (lane < 2 * N_EXPERTS)
    is_e = lane < N_EXPERTS
    grp_of_lane = jnp.where(is_g, (lane - N_EXPERTS) >> 3, lane >> 3)
    gl = jnp.where(is_g, logits, NEG_INF)
    gmax = jnp.max(gl, axis=-1, keepdims=True)
    gsum = jnp.sum(jnp.where(is_g, jnp.exp(gl - gmax), 0.0), axis=-1, keepdims=True) * (1.0 / EPG)
    pg_top = 1.0 / gsum
    g_sel = jnp.min(jnp.where(is_g & (gl == gmax), grp_of_lane, N_GROUPS), axis=-1, keepdims=True)
    in_grp = is_e & (grp_of_lane == g_sel)
    el = jnp.where(in_grp, logits, NEG_INF)
    emax = jnp.max(el, axis=-1, keepdims=True)
    ee = jnp.where(in_grp, jnp.exp(el - emax), 0.0)
    pe = ee / jnp.sum(ee, axis=-1, keepdims=True)
    p1 = jnp.max(pe, axis=-1, keepdims=True)
    i1 = jnp.min(jnp.where(in_grp & (pe == p1), lane, 128), axis=-1, keepdims=True)
    rest = in_grp & (lane != i1)
    pe2 = jnp.where(rest, pe, -1.0)
    p2 = jnp.max(pe2, axis=-1, keepdims=True)
    i2 = jnp.min(jnp.where(rest & (pe2 == p2), lane, 128), axis=-1, keepdims=True)
    psum = p1 + p2
    gate = jnp.where(lane == i1, pg_top * (p1 / psum), jnp.where(lane == i2, pg_top * (p2 / psum), 0.0))
    gate_ref[...] = gate
    gatet_ref[...] = gate.T


def _outproj(x2d, proj, o_list, lse_list, ob2d, wpa, wpb, wo, norm_moe, wr, br, tm=512):
    T = x2d.shape[0]
    row = lambda w: pl.BlockSpec((tm, w), lambda i: (i, 0))
    full = lambda a: pl.BlockSpec(a.shape, lambda i: (0,) * a.ndim)
    return pl.pallas_call(
        _outproj_kernel,
        grid=(T // tm,),
        in_specs=[
            row(D_MODEL),
            pl.BlockSpec((tm, D_MODEL), lambda i: (i, C_MA // D_MODEL)),
            pl.BlockSpec((tm, D_MODEL), lambda i: (i, C_MB // D_MODEL)),
            row(A_GW), row(A_GW), row(A_GW), row(A_GW), row(A_GW), row(A_GW),
            row(D_MODEL),
            full(wpa), full(wpb), full(wo), full(norm_moe), full(wr), full(br),
        ],
        out_specs=[row(D_MODEL), row(D_MODEL), row(128), pl.BlockSpec((128, tm), lambda i: (0, i))],
        out_shape=[
            jax.ShapeDtypeStruct((T, D_MODEL), F32),
            jax.ShapeDtypeStruct((T, D_MODEL), BF16),
            jax.ShapeDtypeStruct((T, 128), F32),
            jax.ShapeDtypeStruct((128, T), F32),
        ],
        compiler_params=_cparams(("parallel",)),
        name="outproj",
    )(x2d, proj, proj, *o_list, *lse_list, ob2d, wpa, wpb, wo, norm_moe, wr, br)


MOE_TB = 1024
MOE_ALIGN = 16
MOE_CH = 128
MOE_R = 2 * MOE_TB + N_EXPERTS * MOE_ALIGN
MOE_RS = MOE_R + MOE_CH
MOE_SUB = 256
MOE_G = 4


def _moe_route(hm_ref, gate_ref, gatet_ref, xs, ys, pt, wrow, rinfo):
    TB, R, SUB = MOE_TB, MOE_R, MOE_SUB
    big = float(4 * TB)
    gate = gate_ref[...]
    gt = gatet_ref[...]
    a = gate > 0.0
    at = gt > 0.0
    a_b = jnp.where(a, 1.0, 0.0).astype(BF16)
    at_b = jnp.where(at, 1.0, 0.0).astype(BF16)

    def pad(c):
        return jnp.floor((c + (MOE_ALIGN - 1)) * (1.0 / MOE_ALIGN)) * MOE_ALIGN

    er = lax.broadcasted_iota(jnp.int32, (128, 128), 0)
    ec = lax.broadcasted_iota(jnp.int32, (128, 128), 1)
    pad_row = pad(jnp.sum(jnp.where(a, 1.0, 0.0), axis=0, keepdims=True))
    off_row = jnp.dot(jnp.broadcast_to(pad_row, (8, 128)).astype(BF16), jnp.where(er < ec, 1.0, 0.0).astype(BF16),
                      preferred_element_type=F32)[0:1]
    pad_col = pad(jnp.sum(jnp.where(at, 1.0, 0.0), axis=1, keepdims=True))
    off_col = jnp.dot(jnp.where(ec < er, 1.0, 0.0).astype(BF16), jnp.broadcast_to(pad_col, (128, 128)).astype(BF16),
                      preferred_element_type=F32)[:, 0:1]

    rl = lax.broadcasted_iota(jnp.int32, (SUB, R), 1).astype(F32)
    for rc in range(TB // SUB):
        rows = slice(rc * SUB, (rc + 1) * SUB)
        tr = lax.broadcasted_iota(jnp.int32, (SUB, TB), 0) + rc * SUB
        tc = lax.broadcasted_iota(jnp.int32, (SUB, TB), 1)
        rank = jnp.dot(jnp.where(tc < tr, 1.0, 0.0).astype(BF16), a_b, preferred_element_type=F32)
        pos = off_row + rank
        a_c = a[rows]
        plo = jnp.min(jnp.where(a_c, pos, big), axis=1, keepdims=True)
        phi = jnp.max(jnp.where(a_c, pos, -1.0), axis=1, keepdims=True)
        phi = jnp.where(phi == plo, -1.0, phi)
        pt[rows, :] = jnp.where((rl == plo) | (rl == phi), 1.0, 0.0).astype(BF16)

    for cc in range(TB // SUB):
        cols = slice(cc * SUB, (cc + 1) * SUB)
        tr = lax.broadcasted_iota(jnp.int32, (TB, SUB), 0)
        tc = lax.broadcasted_iota(jnp.int32, (TB, SUB), 1) + cc * SUB
        rank_t = jnp.dot(at_b, jnp.where(tr < tc, 1.0, 0.0).astype(BF16), preferred_element_type=F32)
        pos_t = off_col + rank_t
        at_c = at[:, cols]
        gt_c = gt[:, cols]
        plo = jnp.min(jnp.where(at_c, pos_t, big), axis=0, keepdims=True)
        phi = jnp.max(jnp.where(at_c, pos_t, -1.0), axis=0, keepdims=True)
        phi = jnp.where(phi == plo, -1.0, phi)
        rinfo[0:1, cols] = plo
        rinfo[1:2, cols] = phi
        rinfo[2:3, cols] = jnp.sum(jnp.where(at_c & (pos_t == plo), gt_c, 0.0), axis=0, keepdims=True)
        rinfo[3:4, cols] = jnp.sum(jnp.where(at_c & (pos_t == phi), gt_c, 0.0), axis=0, keepdims=True)

    def gather(k, carry):
        r0 = pl.multiple_of(k * SUB, SUB)
        ri = (lax.broadcasted_iota(jnp.int32, (SUB, TB), 0) + r0).astype(F32)
        mlo = ri == rinfo[0:1, :]
        mhi = ri == rinfo[1:2, :]
        p = jnp.where(mlo | mhi, 1.0, 0.0).astype(BF16)
        xs[pl.ds(r0, SUB), :] = jnp.dot(p, hm_ref[...], preferred_element_type=F32).astype(BF16)
        w = jnp.sum(jnp.where(mlo, rinfo[2:3, :], 0.0) + jnp.where(mhi, rinfo[3:4, :], 0.0), axis=1, keepdims=True)
        wrow[pl.ds(r0, SUB), :] = jnp.broadcast_to(w, (SUB, 128))
        return carry

    lax.fori_loop(0, R // SUB, gather, 0)
    xs[R:, :] = jnp.zeros((MOE_RS - R, D_MODEL), BF16)
    wrow[R:, :] = jnp.zeros((MOE_RS - R, 128), F32)
    ys[...] = jnp.zeros_like(ys)


def _moe_kernel(meta_ref, hm_ref, gate_ref, gatet_ref, x1_ref, wgu_ref, wd_ref, nf_ref, y_ref,
                xs, ys, pt, wrow, rinfo):
    i = pl.program_id(0)
    s = pl.program_id(1)

    @pl.when(s == 0)
    def _():
        _moe_route(hm_ref, gate_ref, gatet_ref, xs, ys, pt, wrow, rinfo)

    offs = [meta_ref[i, s * MOE_G + g] for g in range(MOE_G)]
    nchs = [meta_ref[i, N_EXPERTS + s * MOE_G + g] for g in range(MOE_G)]

    def chunk(g, r0):
        rows = pl.ds(pl.multiple_of(r0, MOE_ALIGN), MOE_CH)
        gu = jnp.dot(xs[rows, :], wgu_ref[g], preferred_element_type=F32)
        gt, up = gu[:, :EXPERT_FF], gu[:, EXPERT_FF:]
        act = (gt * jax.nn.sigmoid(gt)) * up
        w = wrow[rows, :]
        act = jnp.concatenate([act[:, :128] * w, act[:, 128:] * w], axis=1)
        ys[rows, :] = jnp.dot(act.astype(BF16), wd_ref[g], preferred_element_type=F32).astype(BF16)

    single = nchs[0] <= 1
    for g in range(1, MOE_G):
        single = jnp.logical_and(single, nchs[g] <= 1)

    @pl.when(single)
    def _():
        for g in range(MOE_G):
            chunk(g, offs[g])

    @pl.when(jnp.logical_not(single))
    def _():
        for g in range(MOE_G):
            def body(c, carry, g=g):
                chunk(g, offs[g] + c * MOE_CH)
                return carry
            lax.fori_loop(0, nchs[g], body, 0)

    @pl.when(s == N_EXPERTS // MOE_G - 1)
    def _():
        x2 = x1_ref[...] + jnp.dot(pt[...], ys[0:MOE_R, :], preferred_element_type=F32)
        y_ref[...] = x2 * lax.rsqrt(jnp.mean(x2 * x2, axis=-1, keepdims=True) + EPS) * nf_ref[...]


def _moe(hm, gate, gatet, x1, wgu, wd, norm_final):
    T = hm.shape[0]
    tb = MOE_TB
    nb = T // tb
    cnt = jnp.sum((gate.reshape(nb, tb, 128)[:, :, :N_EXPERTS] > 0.0).astype(jnp.int32), axis=1)
    padded = (cnt + (MOE_ALIGN - 1)) // MOE_ALIGN * MOE_ALIGN
    off = jnp.cumsum(padded, axis=1) - padded
    meta = jnp.concatenate([off, (padded + (MOE_CH - 1)) // MOE_CH], axis=1).astype(jnp.int32)
    grid_spec = pltpu.PrefetchScalarGridSpec(
        num_scalar_prefetch=1,
        grid=(nb, N_EXPERTS // MOE_G),
        in_specs=[
            pl.BlockSpec((tb, D_MODEL), lambda i, e, m: (i, 0)),
            pl.BlockSpec((tb, 128), lambda i, e, m: (i, 0)),
            pl.BlockSpec((128, tb), lambda i, e, m: (0, i)),
            pl.BlockSpec((tb, D_MODEL), lambda i, e, m: (i, 0)),
            pl.BlockSpec((MOE_G, D_MODEL, 2 * EXPERT_FF), lambda i, e, m: (e, 0, 0)),
            pl.BlockSpec((MOE_G, EXPERT_FF, D_MODEL), lambda i, e, m: (e, 0, 0)),
            pl.BlockSpec((1, D_MODEL), lambda i, e, m: (0, 0)),
        ],
        out_specs=pl.BlockSpec((tb, D_MODEL), lambda i, e, m: (i, 0)),
        scratch_shapes=[
            pltpu.VMEM((MOE_RS, D_MODEL), BF16),
            pltpu.VMEM((MOE_RS, D_MODEL), BF16),
            pltpu.VMEM((tb, MOE_R), BF16),
            pltpu.VMEM((MOE_RS, 128), F32),
            pltpu.VMEM((8, tb), F32),
        ],
    )
    return pl.pallas_call(
        _moe_kernel,
        grid_spec=grid_spec,
        out_shape=jax.ShapeDtypeStruct((T, D_MODEL), F32),
        compiler_params=_cparams(("parallel", "arbitrary")),
        name="moe",
    )(meta, hm, gate, gatet, x1, wgu, wd, norm_final)


def _prep_weights(norm_mix, w_in, rel_bias, w_gate_f, b_gate_f, w_gate_b, b_gate_b, gla_norm, w_proj_a,
                  w_proj_b, w_out, norm_moe, w_rg, b_rg, w_re, b_re, w_eg, w_eu, w_ed, norm_final):
    splits = (A_QKV, A_QKV, A_QKV, 512, 512, 1024, 1024, GATE_RANK, GATE_RANK, D_MODEL, D_MODEL)
    qa, ka, va, qb, kb, vb, og, gf, gb, ma, mb = jnp.split(w_in, np.cumsum(splits)[:-1].tolist(), axis=1)
    gpad = jnp.zeros((D_MODEL, PR - C_G - 2 * GATE_RANK), w_in.dtype)
    grp = lambda t, g: t[:, g * A_GW:(g + 1) * A_GW]
    a_cols = [grp(t, g) for g in range(N_DIL) for t in (qa, ka, va)]
    w_all = jnp.concatenate(a_cols + [ma, mb, qb, kb, vb, og, gf, gb, gpad], axis=1).astype(BF16)
    kw = GLA_H * GLA_DK
    wgf = jnp.zeros((128, kw), F32).at[:GATE_RANK].set(w_gate_f).astype(BF16)
    wgb = jnp.zeros((128, kw), F32).at[GATE_RANK:2 * GATE_RANK].set(w_gate_b).astype(BF16)
    wr = jnp.zeros((D_MODEL, 128), F32)
    wr = wr.at[:, :N_EXPERTS].set(w_re).at[:, N_EXPERTS:2 * N_EXPERTS].set(jnp.repeat(w_rg, EPG, axis=1))
    br = jnp.zeros((1, 128), F32)
    br = br.at[0, :N_EXPERTS].set(b_re).at[0, N_EXPERTS:2 * N_EXPERTS].set(jnp.repeat(b_rg, EPG))
    return dict(
        norm_mix=norm_mix.reshape(1, D_MODEL), w_all=w_all, rel_bias=rel_bias,
        wgf=wgf, wgb=wgb, bgf=b_gate_f.reshape(1, kw), bgb=b_gate_b.reshape(1, kw),
        gla_norm=gla_norm.reshape(1, GLA_DV),
        wpa=w_proj_a.astype(BF16), wpb=w_proj_b.astype(BF16), wo=w_out.astype(BF16),
        norm_moe=norm_moe.reshape(1, D_MODEL), wr=wr.astype(BF16), br=br,
        wgu=jnp.concatenate([w_eg, w_eu], axis=-1).astype(BF16), wd=w_ed.astype(BF16),
        norm_final=norm_final.reshape(1, D_MODEL),
    )


def _trunk(x, w):
    B, S, _ = x.shape
    T = B * S
    x2d = x.reshape(T, D_MODEL)
    *qkv, proj = _inproj(x, w["norm_mix"], w["w_all"])
    o_list, lse_list = [], []
    for g, (_, d) in enumerate(DIL_PAIRS):
        L = S // d
        tiles = _bias_tiles(w["rel_bias"], g, d, min(2 * Q_BLK, L))
        o, lse = _attn_group(qkv[g], tiles, d, B, S)
        o_list.append(o.reshape(T, A_GW))
        lse_list.append(lse.reshape(T, A_GW))
    ob = _gla(proj, w["wgf"], w["wgb"], w["bgf"], w["bgb"], w["gla_norm"], B, S)
    x1, hm, gate, gatet = _outproj(x2d, proj, o_list, lse_list, ob.reshape(T, D_MODEL), w["wpa"], w["wpb"],
                                   w["wo"], w["norm_moe"], w["wr"], w["br"])
    y = _moe(hm, gate, gatet, x1, w["wgu"], w["wd"], w["norm_final"])
    return y.reshape(B, S, D_MODEL)


def kernel(x_prompt, x_sample, norm_mix, w_in, rel_bias, w_gate_f, b_gate_f, w_gate_b, b_gate_b, gla_norm,
           w_proj_a, w_proj_b, w_out, norm_moe, w_router_group, b_router_group, w_router_expert,
           b_router_expert, w_exp_gate, w_exp_up, w_exp_down, norm_final):
    w = _prep_weights(norm_mix[0], w_in[0], rel_bias, w_gate_f[0], b_gate_f[0], w_gate_b[0], b_gate_b[0],
                      gla_norm[0], w_proj_a[0], w_proj_b[0], w_out[0], norm_moe[0], w_router_group[0],
                      b_router_group[0], w_router_expert[0], b_router_expert[0], w_exp_gate[0], w_exp_up[0],
                      w_exp_down[0], norm_final)
    return (_trunk(x_prompt, w), _trunk(x_sample, w))
```

```python
import functools
import math

import numpy as np
import jax
import jax.numpy as jnp
from jax import lax
from jax.experimental import pallas as pl
from jax.experimental.pallas import tpu as pltpu

F32 = jnp.float32
BF16 = jnp.bfloat16

D_MODEL = 1024
EPS = 1e-6
NEG_INF = -1e30

DIL_PAIRS = ((128, 1), (512, 4), (2048, 16))
A_HPG = 4
A_DH = 64
A_GW = A_HPG * A_DH
A_QKV = 3 * A_GW
HALF_WIN = 64
Q_BLK = 128
N_REL_BUCKETS = 32
REL_MAX_DISTANCE = 1024
GLA_H = 4
GLA_DK = 128
GLA_DV = 256
GATE_RANK = 16
GATE_TAU = 16.0
GLA_CHUNK = 64
N_GROUPS = 4
EPG = 8
N_EXPERTS = 32
EXPERT_FF = 256

N_DIL = len(DIL_PAIRS)
C_MA, C_MB = 0, 1024
C_QB, C_KB, C_VB, C_OG = 2048, 2560, 3072, 4096
C_G = 5120
PR = 5376
PROJ_TN = A_QKV

VMEM_LIMIT = 58 * 1024 * 1024


def _cparams(sem):
    return pltpu.CompilerParams(dimension_semantics=sem, vmem_limit_bytes=VMEM_LIMIT)


def _inproj_kernel(x_ref, g_ref, w_ref, a0_ref, a1_ref, a2_ref, or_ref, y_scr):
    x = x_ref[...]
    r = x * lax.rsqrt(jnp.mean(x * x, axis=-1, keepdims=True) + EPS)
    h = (r * g_ref[...]).astype(BF16)
    nct = PROJ_TN // 128
    for j in range(N_DIL + PR // PROJ_TN):
        y = jnp.dot(h, w_ref[:, j * PROJ_TN:(j + 1) * PROJ_TN], preferred_element_type=F32)
        if j >= N_DIL:
            or_ref[:, (j - N_DIL) * PROJ_TN:(j - N_DIL + 1) * PROJ_TN] = y.astype(BF16)
            continue
        a_ref = (a0_ref, a1_ref, a2_ref)[j]
        d = DIL_PAIRS[j][1]
        if d == 1:
            a_ref[0, 0] = y.astype(BF16)
            continue
        for c in range(nct):
            y_scr[j - 1, c] = y[:, c * 128:(c + 1) * 128]
        n = y.shape[0] // d
        for cls in range(d):
            cols = [y_scr[j - 1, c, pl.ds(cls, n, stride=d), :] for c in range(nct)]
            a_ref[0, cls] = jnp.concatenate(cols, axis=1).astype(BF16)


def _inproj(x, norm_g, w_all, tm=512):
    B, S, _ = x.shape
    T = B * S
    spb = S // tm
    a_specs, a_shapes = [], []
    for _, d in DIL_PAIRS:
        a_specs.append(pl.BlockSpec((1, d, tm // d, PROJ_TN), lambda i: (i // spb, 0, i % spb, 0)))
        a_shapes.append(jax.ShapeDtypeStruct((B, d, S // d, PROJ_TN), BF16))
    return pl.pallas_call(
        _inproj_kernel,
        grid=(T // tm,),
        in_specs=[
            pl.BlockSpec((tm, D_MODEL), lambda i: (i, 0)),
            pl.BlockSpec((1, D_MODEL), lambda i: (0, 0)),
            pl.BlockSpec(w_all.shape, lambda i: (0, 0), pipeline_mode=pl.Buffered(1)),
        ],
        out_specs=a_specs + [pl.BlockSpec((tm, PR), lambda i: (i, 0))],
        out_shape=a_shapes + [jax.ShapeDtypeStruct((T, PR), BF16)],
        scratch_shapes=[pltpu.VMEM((N_DIL - 1, PROJ_TN // 128, tm, 128), F32)],
        compiler_params=_cparams(("parallel",)),
        name="inproj",
    )(x.reshape(T, D_MODEL), norm_g, w_all)


ATTN_ITEMS = 4


def _attn_kernel(q_ref, k_ref, v_ref, bias_ref, o_ref, lse_ref, *, L, wk, nc, nq, nblk):
    i = pl.program_id(2)
    lane = lax.broadcasted_iota(jnp.int32, (Q_BLK, A_GW), 1)
    qscale = jnp.asarray(A_DH ** -0.5, BF16)
    nt = (((1,), (1,)), ((), ()))
    for c in range(nc):
        for t in range(nq):
            blk = i * nq + t
            start = pl.multiple_of(jnp.clip(blk * Q_BLK - HALF_WIN, 0, L - wk), HALF_WIN)
            var = jnp.where(blk == 0, 0, jnp.where(blk == nblk - 1, 2, 1))
            rows = slice(t * Q_BLK, (t + 1) * Q_BLK)
            q = q_ref[0, c, rows, :] * qscale
            kw = k_ref[0, c, pl.ds(start, wk), :]
            vw = v_ref[0, c, pl.ds(start, wk), :]
            outs = []
            lse_tile = jnp.zeros((Q_BLK, A_GW), F32)
            for h in range(A_HPG):
                sl = slice(h * A_DH, (h + 1) * A_DH)
                s = lax.dot_general(q[:, sl], kw[:, sl], nt, preferred_element_type=F32) + bias_ref[var, h]
                m = jnp.max(s, axis=-1, keepdims=True)
                p = jnp.exp(s - m)
                l = jnp.sum(p, axis=-1, keepdims=True)
                outs.append(jnp.dot(p.astype(BF16), vw[:, sl], preferred_element_type=F32) / l)
                lse = m + jnp.log(l)
                lse_tile = jnp.where((lane >= A_DH * h) & (lane < A_DH * (h + 1)), lse, lse_tile)
            cols = slice(c * A_GW, (c + 1) * A_GW)
            o_ref[0, rows, cols] = jnp.concatenate(outs, axis=1).astype(BF16)
            lse_ref[0, rows, cols] = lse_tile


def _attn_group(qkv, bias_tiles, d, B, S):
    L = S // d
    nblk = L // Q_BLK
    wk = min(2 * Q_BLK, L)
    nq = min(ATTN_ITEMS, nblk)
    nc = ATTN_ITEMS // nq

    return pl.pallas_call(
        functools.partial(_attn_kernel, L=L, wk=wk, nc=nc, nq=nq, nblk=nblk),
        grid=(B, d // nc, nblk // nq),
        in_specs=[
            pl.BlockSpec((1, nc, nq * Q_BLK, A_GW), lambda b, r, i: (b, r, i, 0)),
            pl.BlockSpec((1, nc, L, A_GW), lambda b, r, i: (b, r, 0, 1)),
            pl.BlockSpec((1, nc, L, A_GW), lambda b, r, i: (b, r, 0, 2)),
            pl.BlockSpec(bias_tiles.shape, lambda b, r, i: (0, 0, 0, 0)),
        ],
        out_specs=[
            pl.BlockSpec((1, nq * Q_BLK, nc * A_GW), lambda b, r, i: (b, i, r)),
            pl.BlockSpec((1, nq * Q_BLK, nc * A_GW), lambda b, r, i: (b, i, r)),
        ],
        out_shape=[
            jax.ShapeDtypeStruct((B, L, d * A_GW), BF16),
            jax.ShapeDtypeStruct((B, L, d * A_GW), F32),
        ],
        compiler_params=_cparams(("parallel", "parallel", "arbitrary")),
        name=f"attn_d{d}",
    )(qkv, qkv, qkv, bias_tiles)


def _t5_buckets(rel):
    nb = N_REL_BUCKETS // 2
    max_exact = nb // 2
    ret = (rel > 0).astype(np.int64) * nb
    n = np.abs(rel)
    large = max_exact + (np.log(np.maximum(n, 1) / max_exact) / math.log(REL_MAX_DISTANCE / max_exact)
                         * (nb - max_exact)).astype(np.int64)
    large = np.minimum(large, nb - 1)
    return (ret + np.where(n < max_exact, n, large)).astype(np.int32)


def _bias_tiles(rel_bias, g, d, wk):
    rel = d * np.arange(-HALF_WIN, HALF_WIN + 1)
    bucket = _t5_buckets(rel)
    bias = rel_bias[jnp.asarray(bucket)][:, g * A_HPG:(g + 1) * A_HPG].T.astype(F32)
    pad = wk + Q_BLK
    neg = jnp.full((A_HPG, pad), NEG_INF, F32)
    val = jnp.concatenate([neg, bias, neg], axis=1)
    zero = pad + HALF_WIN
    m = wk + Q_BLK
    tiles = []
    for off in (0, HALF_WIN, wk - Q_BLK):
        u = jnp.concatenate([val[:, zero - off:zero - off + wk], val[:, zero - off - Q_BLK:zero - off]], axis=1)
        flat = jnp.tile(u, (1, Q_BLK))[:, :Q_BLK * (m - 1)]
        tiles.append(flat.reshape(A_HPG, Q_BLK, m - 1)[:, :, :wk])
    return jnp.stack(tiles)


def _log_sigmoid(z):
    return jnp.minimum(z, 0.0) - jnp.log(1.0 + jnp.exp(-jnp.abs(z)))


def _gla_kernel(q_ref, k_ref, v_ref, og_ref, g_ref, wgf_ref, wgb_ref, bgf_ref, bgb_ref, gn_ref, o_ref,
                qif, kof, qib, kob, vt, etf, etb, acc, *, S):
    PB = 256
    C = GLA_CHUNK
    NB = S // PB
    rr = lax.broadcasted_iota(jnp.int32, (PB, PB), 0)
    cc = lax.broadcasted_iota(jnp.int32, (PB, PB), 1)
    same = (rr >> 6) == (cc >> 6)
    mask_f = same & (cc <= rr)
    mask_b = same & (cc > rr)
    tl = jnp.where(mask_f, 1.0, 0.0).astype(BF16)
    qscale = GLA_DK ** -0.5
    inv_tau = 1.0 / GATE_TAU
    nt = (((1,), (1,)), ((), ()))

    def split3(x):
        hi = x.astype(BF16)
        r1 = x - hi.astype(F32)
        mid = r1.astype(BF16)
        lo = (r1 - mid.astype(F32)).astype(BF16)
        return [hi, mid, lo]

    def chunk_total(b):
        b4 = b.reshape(PB // C, C, GLA_DK)
        return jnp.broadcast_to(b4[:, C - 1:C, :], b4.shape).reshape(PB, GLA_DK)

    def prep(i, carry):
        rows = pl.ds(pl.multiple_of(i * PB, PB), PB)
        g = g_ref[0, rows, :]
        q = q_ref[0, rows, :].astype(F32) * qscale
        k = k_ref[0, rows, :].astype(F32)
        v = v_ref[0, rows, :]
        lf = _log_sigmoid(jnp.dot(g, wgf_ref[...], preferred_element_type=F32) + bgf_ref[...]) * inv_tau
        lb = _log_sigmoid(jnp.dot(g, wgb_ref[...], preferred_element_type=F32) + bgb_ref[...]) * inv_tau
        cs = jnp.dot(tl, jnp.concatenate(split3(lf) + split3(lb), axis=1), preferred_element_type=F32)
        bf = cs[:, 0:128] + cs[:, 128:256] + cs[:, 256:384]
        pb = cs[:, 384:512] + cs[:, 512:640] + cs[:, 640:768]
        totf = chunk_total(bf)
        totb = chunk_total(pb)
        bs = totb - pb + lb
        qf = (q * jnp.exp(bf)).astype(BF16)
        qb = (q * jnp.exp(bs)).astype(BF16)
        sf = lax.dot_general(qf, (k * jnp.exp(-bf)).astype(BF16), nt, preferred_element_type=F32)
        sb = lax.dot_general(qb, (k * jnp.exp(-bs)).astype(BF16), nt, preferred_element_type=F32)
        p = jnp.where(mask_f, sf, jnp.where(mask_b, sb, 0.0)).astype(BF16)
        acc[rows, :] = jnp.dot(p, v, preferred_element_type=F32)
        qif[rows, :] = qf
        qib[rows, :] = qb
        kof[rows, :] = (k * jnp.exp(totf - bf)).astype(BF16)
        kob[rows, :] = (k * jnp.exp(totb - bs)).astype(BF16)
        etf[rows, :] = jnp.exp(totf)
        etb[rows, :] = jnp.exp(totb)
        vtb = v.astype(F32).T.astype(BF16)
        vt[2 * i] = vtb[:, :128]
        vt[2 * i + 1] = vtb[:, 128:]
        return carry

    lax.fori_loop(0, NB, prep, 0, unroll=2)

    zeros_half = jnp.zeros((C, GLA_DK), BF16)

    def chunk_step(qi_ref, ko_ref, et_ref, blk, j, state):
        r0 = pl.multiple_of(blk * PB + j * C, C)
        rows = pl.ds(r0, C)
        acc[rows, :] += lax.dot_general(qi_ref[rows, :], state.astype(BF16), nt, preferred_element_type=F32)
        ko = ko_ref[rows, :]
        ko_pad = jnp.concatenate([ko, zeros_half] if j % 2 == 0 else [zeros_half, ko], axis=0)
        upd = jnp.dot(vt[2 * blk + j // 2], ko_pad, preferred_element_type=F32)
        return state * et_ref[pl.ds(r0, 1), :] + upd

    def serial(t, carry):
        st_f, st_b = carry
        for j in range(PB // C):
            st_f = chunk_step(qif, kof, etf, t, j, st_f)
            st_b = chunk_step(qib, kob, etb, NB - 1 - t, PB // C - 1 - j, st_b)
        return st_f, st_b

    zero_state = jnp.zeros((GLA_DV, GLA_DK), F32)
    lax.fori_loop(0, NB, serial, (zero_state, zero_state), unroll=4)

    gn = gn_ref[...]

    def finish(i, carry):
        rows = pl.ds(pl.multiple_of(i * PB, PB), PB)
        tot = acc[rows, :]
        nrm = tot * lax.rsqrt(jnp.mean(tot * tot, axis=-1, keepdims=True) + EPS) * gn
        og = og_ref[0, rows, :].astype(F32)
        o_ref[0, rows, :] = (nrm * (og * jax.nn.sigmoid(og))).astype(BF16)
        return carry

    lax.fori_loop(0, NB, finish, 0)


def _gla(proj, wgf, wgb, bgf, bgb, gla_norm, B, S):
    pv = proj.reshape(B, S, PR)
    cq, ck, cv, cog, cg = C_QB // GLA_DK, C_KB // GLA_DK, C_VB // GLA_DV, C_OG // GLA_DV, C_G // 128
    return pl.pallas_call(
        functools.partial(_gla_kernel, S=S),
        grid=(B, GLA_H),
        in_specs=[
            pl.BlockSpec((1, S, GLA_DK), lambda b, h: (b, 0, cq + h)),
            pl.BlockSpec((1, S, GLA_DK), lambda b, h: (b, 0, ck + h)),
            pl.BlockSpec((1, S, GLA_DV), lambda b, h: (b, 0, cv + h)),
            pl.BlockSpec((1, S, GLA_DV), lambda b, h: (b, 0, cog + h)),
            pl.BlockSpec((1, S, 128), lambda b, h: (b, 0, cg)),
            pl.BlockSpec((128, GLA_DK), lambda b, h: (0, h)),
            pl.BlockSpec((128, GLA_DK), lambda b, h: (0, h)),
            pl.BlockSpec((1, GLA_DK), lambda b, h: (0, h)),
            pl.BlockSpec((1, GLA_DK), lambda b, h: (0, h)),
            pl.BlockSpec((1, GLA_DV), lambda b, h: (0, 0)),
        ],
        out_specs=pl.BlockSpec((1, S, GLA_DV), lambda b, h: (b, 0, h)),
        out_shape=jax.ShapeDtypeStruct((B, S, GLA_H * GLA_DV), BF16),
        scratch_shapes=[
            pltpu.VMEM((S, GLA_DK), BF16), pltpu.VMEM((S, GLA_DK), BF16),
            pltpu.VMEM((S, GLA_DK), BF16), pltpu.VMEM((S, GLA_DK), BF16),
            pltpu.VMEM((S // 128, GLA_DV, 128), BF16),
            pltpu.VMEM((S, GLA_DK), F32), pltpu.VMEM((S, GLA_DK), F32),
            pltpu.VMEM((S, GLA_DV), F32),
        ],
        compiler_params=_cparams(("parallel", "arbitrary")),
        name="gla",
    )(pv, pv, pv, pv, pv, wgf, wgb, bgf, bgb, gla_norm)


def _outproj_kernel(x_ref, ma_ref, mb_ref, o0_ref, o1_ref, o2_ref, l0_ref, l1_ref, l2_ref, ob_ref,
                    wpa_ref, wpb_ref, wo_ref, nm_ref, wr_ref, br_ref,
                    x1_ref, hm_ref, gate_ref, gatet_ref):
    l0, l1, l2 = l0_ref[...], l1_ref[...], l2_ref[...]
    m = jnp.maximum(jnp.maximum(l0, l1), l2)
    e0, e1, e2 = jnp.exp(l0 - m), jnp.exp(l1 - m), jnp.exp(l2 - m)
    den = e0 + e1 + e2
    oa = ((e0 / den) * o0_ref[...].astype(F32) + (e1 / den) * o1_ref[...].astype(F32)
          + (e2 / den) * o2_ref[...].astype(F32))
    ya = jnp.dot(oa.astype(BF16), wpa_ref[...], preferred_element_type=F32)
    yb = jnp.dot(ob_ref[...], wpb_ref[...], preferred_element_type=F32)
    mix = jax.nn.sigmoid(ma_ref[...].astype(F32)) * ya + jax.nn.sigmoid(mb_ref[...].astype(F32)) * yb
    x1 = x_ref[...] + jnp.dot(mix.astype(BF16), wo_ref[...], preferred_element_type=F32)
    x1_ref[...] = x1
    hm = x1 * lax.rsqrt(jnp.mean(x1 * x1, axis=-1, keepdims=True) + EPS) * nm_ref[...]
    hm_b = hm.astype(BF16)
    hm_ref[...] = hm_b

    logits = jnp.dot(hm_b, wr_ref[...], preferred_element_type=F32) + br_ref[...]
    tm = logits.shape[0]
    lane = lax.broadcasted_iota(jnp.int32, (tm, 128), 1)
    is_g = (lane >= N_EXPERTS) & (lane < 2 * N_EXPERTS)
    is_e = lane < N_EXPERTS
    grp_of_lane = jnp.where(is_g, (lane - N_EXPERTS) >> 3, lane >> 3)
    gl = jnp.where(is_g, logits, NEG_INF)
    gmax = jnp.max(gl, axis=-1, keepdims=True)
    gsum = jnp.sum(jnp.where(is_g, jnp.exp(gl - gmax), 0.0), axis=-1, keepdims=True) * (1.0 / EPG)
    pg_top = 1.0 / gsum
    g_sel = jnp.min(jnp.where(is_g & (gl == gmax), grp_of_lane, N_GROUPS), axis=-1, keepdims=True)
    in_grp = is_e & (grp_of_lane == g_sel)
    el = jnp.where(in_grp, logits, NEG_INF)
    emax = jnp.max(el, axis=-1, keepdims=True)
    ee = jnp.where(in_grp, jnp.exp(el - emax), 0.0)
    pe = ee / jnp.sum(ee, axis=-1, keepdims=True)
    p1 = jnp.max(pe, axis=-1, keepdims=True)
    i1 = jnp.min(jnp.where(in_grp & (pe == p1), lane, 128), axis=-1, keepdims=True)
    rest = in_grp & (lane != i1)
    pe2 = jnp.where(rest, pe, -1.0)
    p2 = jnp.max(pe2, axis=-1, keepdims=True)
    i2 = jnp.min(jnp.where(rest & (pe2 == p2), lane, 128), axis=-1, keepdims=True)
    psum = p1 + p2
    gate = jnp.where(lane == i1, pg_top * (p1 / psum), jnp.where(lane == i2, pg_top * (p2 / psum), 0.0))
    gate_ref[...] = gate
    gatet_ref[...] = gate.T


def _outproj(x2d, proj, o_list, lse_list, ob2d, wpa, wpb, wo, norm_moe, wr, br, tm=512):
    T = x2d.shape[0]
    row = lambda w: pl.BlockSpec((tm, w), lambda i: (i, 0))
    full = lambda a: pl.BlockSpec(a.shape, lambda i: (0,) * a.ndim)
    return pl.pallas_call(
        _outproj_kernel,
        grid=(T // tm,),
        in_specs=[
            row(D_MODEL),
            pl.BlockSpec((tm, D_MODEL), lambda i: (i, C_MA // D_MODEL)),
            pl.BlockSpec((tm, D_MODEL), lambda i: (i, C_MB // D_MODEL)),
            row(A_GW), row(A_GW), row(A_GW), row(A_GW), row(A_GW), row(A_GW),
            row(D_MODEL),
            full(wpa), full(wpb), full(wo), full(norm_moe), full(wr), full(br),
        ],
        out_specs=[row(D_MODEL), row(D_MODEL), row(128), pl.BlockSpec((128, tm), lambda i: (0, i))],
        out_shape=[
            jax.ShapeDtypeStruct((T, D_MODEL), F32),
            jax.ShapeDtypeStruct((T, D_MODEL), BF16),
            jax.ShapeDtypeStruct((T, 128), F32),
            jax.ShapeDtypeStruct((128, T), F32),
        ],
        compiler_params=_cparams(("parallel",)),
        name="outproj",
    )(x2d, proj, proj, *o_list, *lse_list, ob2d, wpa, wpb, wo, norm_moe, wr, br)


MOE_TB = 1024
MOE_ALIGN = 16
MOE_CH = 128
MOE_R = 2 * MOE_TB + N_EXPERTS * MOE_ALIGN
MOE_RS = MOE_R + MOE_CH
MOE_SUB = 256
MOE_G = 4


def _moe_route(hm_ref, gate_ref, gatet_ref, xs, ys, pt, wrow, rinfo):
    TB, R, SUB = MOE_TB, MOE_R, MOE_SUB
    big = float(4 * TB)
    gate = gate_ref[...]
    gt = gatet_ref[...]
    a = gate > 0.0
    at = gt > 0.0
    a_b = jnp.where(a, 1.0, 0.0).astype(BF16)
    at_b = jnp.where(at, 1.0, 0.0).astype(BF16)

    def pad(c):
        return jnp.floor((c + (MOE_ALIGN - 1)) * (1.0 / MOE_ALIGN)) * MOE_ALIGN

    er = lax.broadcasted_iota(jnp.int32, (128, 128), 0)
    ec = lax.broadcasted_iota(jnp.int32, (128, 128), 1)
    pad_row = pad(jnp.sum(jnp.where(a, 1.0, 0.0), axis=0, keepdims=True))
    off_row = jnp.dot(jnp.broadcast_to(pad_row, (8, 128)).astype(BF16), jnp.where(er < ec, 1.0, 0.0).astype(BF16),
                      preferred_element_type=F32)[0:1]
    pad_col = pad(jnp.sum(jnp.where(at, 1.0, 0.0), axis=1, keepdims=True))
    off_col = jnp.dot(jnp.where(ec < er, 1.0, 0.0).astype(BF16), jnp.broadcast_to(pad_col, (128, 128)).astype(BF16),
                      preferred_element_type=F32)[:, 0:1]

    rl = lax.broadcasted_iota(jnp.int32, (SUB, R), 1).astype(F32)
    for rc in range(TB // SUB):
        rows = slice(rc * SUB, (rc + 1) * SUB)
        tr = lax.broadcasted_iota(jnp.int32, (SUB, TB), 0) + rc * SUB
        tc = lax.broadcasted_iota(jnp.int32, (SUB, TB), 1)
        rank = jnp.dot(jnp.where(tc < tr, 1.0, 0.0).astype(BF16), a_b, preferred_element_type=F32)
        pos = off_row + rank
        a_c = a[rows]
        plo = jnp.min(jnp.where(a_c, pos, big), axis=1, keepdims=True)
        phi = jnp.max(jnp.where(a_c, pos, -1.0), axis=1, keepdims=True)
        phi = jnp.where(phi == plo, -1.0, phi)
        pt[rows, :] = jnp.where((rl == plo) | (rl == phi), 1.0, 0.0).astype(BF16)

    for cc in range(TB // SUB):
        cols = slice(cc * SUB, (cc + 1) * SUB)
        tr = lax.broadcasted_iota(jnp.int32, (TB, SUB), 0)
        tc = lax.broadcasted_iota(jnp.int32, (TB, SUB), 1) + cc * SUB
        rank_t = jnp.dot(at_b, jnp.where(tr < tc, 1.0, 0.0).astype(BF16), preferred_element_type=F32)
        pos_t = off_col + rank_t
        at_c = at[:, cols]
        gt_c = gt[:, cols]
        plo = jnp.min(jnp.where(at_c, pos_t, big), axis=0, keepdims=True)
        phi = jnp.max(jnp.where(at_c, pos_t, -1.0), axis=0, keepdims=True)
        phi = jnp.where(phi == plo, -1.0, phi)
        rinfo[0:1, cols] = plo
        rinfo[1:2, cols] = phi
        rinfo[2:3, cols] = jnp.sum(jnp.where(at_c & (pos_t == plo), gt_c, 0.0), axis=0, keepdims=True)
        rinfo[3:4, cols] = jnp.sum(jnp.where(at_c & (pos_t == phi), gt_c, 0.0), axis=0, keepdims=True)

    def gather(k, carry):
        r0 = pl.multiple_of(k * SUB, SUB)
        ri = (lax.broadcasted_iota(jnp.int32, (SUB, TB), 0) + r0).astype(F32)
        mlo = ri == rinfo[0:1, :]
        mhi = ri == rinfo[1:2, :]
        p = jnp.where(mlo | mhi, 1.0, 0.0).astype(BF16)
        xs[pl.ds(r0, SUB), :] = jnp.dot(p, hm_ref[...], preferred_element_type=F32).astype(BF16)
        w = jnp.sum(jnp.where(mlo, rinfo[2:3, :], 0.0) + jnp.where(mhi, rinfo[3:4, :], 0.0), axis=1, keepdims=True)
        wrow[pl.ds(r0, SUB), :] = jnp.broadcast_to(w, (SUB, 128))
        return carry

    lax.fori_loop(0, R // SUB, gather, 0)
    xs[R:, :] = jnp.zeros((MOE_RS - R, D_MODEL), BF16)
    wrow[R:, :] = jnp.zeros((MOE_RS - R, 128), F32)
    ys[...] = jnp.zeros_like(ys)


def _moe_kernel(meta_ref, hm_ref, gate_ref, gatet_ref, x1_ref, wgu_ref, wd_ref, nf_ref, y_ref,
                xs, ys, pt, wrow, rinfo):
    i = pl.program_id(0)
    s = pl.program_id(1)

    @pl.when(s == 0)
    def _():
        _moe_route(hm_ref, gate_ref, gatet_ref, xs, ys, pt, wrow, rinfo)

    offs = [meta_ref[i, s * MOE_G + g] for g in range(MOE_G)]
    nchs = [meta_ref[i, N_EXPERTS + s * MOE_G + g] for g in range(MOE_G)]

    def chunk(g, r0):
        rows = pl.ds(pl.multiple_of(r0, MOE_ALIGN), MOE_CH)
        gu = jnp.dot(xs[rows, :], wgu_ref[g], preferred_element_type=F32)
        gt, up = gu[:, :EXPERT_FF], gu[:, EXPERT_FF:]
        act = (gt * jax.nn.sigmoid(gt)) * up
        w = wrow[rows, :]
        act = jnp.concatenate([act[:, :128] * w, act[:, 128:] * w], axis=1)
        ys[rows, :] = jnp.dot(act.astype(BF16), wd_ref[g], preferred_element_type=F32).astype(BF16)

    single = nchs[0] <= 1
    for g in range(1, MOE_G):
        single = jnp.logical_and(single, nchs[g] <= 1)

    @pl.when(single)
    def _():
        for g in range(MOE_G):
            chunk(g, offs[g])

    @pl.when(jnp.logical_not(single))
    def _():
        for g in range(MOE_G):
            def body(c, carry, g=g):
                chunk(g, offs[g] + c * MOE_CH)
                return carry
            lax.fori_loop(0, nchs[g], body, 0)

    @pl.when(s == N_EXPERTS // MOE_G - 1)
    def _():
        x2 = x1_ref[...] + jnp.dot(pt[...], ys[0:MOE_R, :], preferred_element_type=F32)
        y_ref[...] = x2 * lax.rsqrt(jnp.mean(x2 * x2, axis=-1, keepdims=True) + EPS) * nf_ref[...]


def _moe(hm, gate, gatet, x1, wgu, wd, norm_final):
    T = hm.shape[0]
    tb = MOE_TB
    nb = T // tb
    cnt = jnp.sum((gate.reshape(nb, tb, 128)[:, :, :N_EXPERTS] > 0.0).astype(jnp.int32), axis=1)
    padded = (cnt + (MOE_ALIGN - 1)) // MOE_ALIGN * MOE_ALIGN
    off = jnp.cumsum(padded, axis=1) - padded
    meta = jnp.concatenate([off, (padded + (MOE_CH - 1)) // MOE_CH], axis=1).astype(jnp.int32)
    grid_spec = pltpu.PrefetchScalarGridSpec(
        num_scalar_prefetch=1,
        grid=(nb, N_EXPERTS // MOE_G),
        in_specs=[
            pl.BlockSpec((tb, D_MODEL), lambda i, e, m: (i, 0)),
            pl.BlockSpec((tb, 128), lambda i, e, m: (i, 0)),
            pl.BlockSpec((128, tb), lambda i, e, m: (0, i)),
            pl.BlockSpec((tb, D_MODEL), lambda i, e, m: (i, 0)),
            pl.BlockSpec((MOE_G, D_MODEL, 2 * EXPERT_FF), lambda i, e, m: (e, 0, 0)),
            pl.BlockSpec((MOE_G, EXPERT_FF, D_MODEL), lambda i, e, m: (e, 0, 0)),
            pl.BlockSpec((1, D_MODEL), lambda i, e, m: (0, 0)),
        ],
        out_specs=pl.BlockSpec((tb, D_MODEL), lambda i, e, m: (i, 0)),
        scratch_shapes=[
            pltpu.VMEM((MOE_RS, D_MODEL), BF16),
            pltpu.VMEM((MOE_RS, D_MODEL), BF16),
            pltpu.VMEM((tb, MOE_R), BF16),
            pltpu.VMEM((MOE_RS, 128), F32),
            pltpu.VMEM((8, tb), F32),
        ],
    )
    return pl.pallas_call(
        _moe_kernel,
        grid_spec=grid_spec,
        out_shape=jax.ShapeDtypeStruct((T, D_MODEL), F32),
        compiler_params=_cparams(("parallel", "arbitrary")),
        name="moe",
    )(meta, hm, gate, gatet, x1, wgu, wd, norm_final)


def _prep_weights(norm_mix, w_in, rel_bias, w_gate_f, b_gate_f, w_gate_b, b_gate_b, gla_norm, w_proj_a,
                  w_proj_b, w_out, norm_moe, w_rg, b_rg, w_re, b_re, w_eg, w_eu, w_ed, norm_final):
    splits = (A_QKV, A_QKV, A_QKV, 512, 512, 1024, 1024, GATE_RANK, GATE_RANK, D_MODEL, D_MODEL)
    qa, ka, va, qb, kb, vb, og, gf, gb, ma, mb = jnp.split(w_in, np.cumsum(splits)[:-1].tolist(), axis=1)
    gpad = jnp.zeros((D_MODEL, PR - C_G - 2 * GATE_RANK), w_in.dtype)
    grp = lambda t, g: t[:, g * A_GW:(g + 1) * A_GW]
    a_cols = [grp(t, g) for g in range(N_DIL) for t in (qa, ka, va)]
    w_all = jnp.concatenate(a_cols + [ma, mb, qb, kb, vb, og, gf, gb, gpad], axis=1).astype(BF16)
    kw = GLA_H * GLA_DK
    wgf = jnp.zeros((128, kw), F32).at[:GATE_RANK].set(w_gate_f).astype(BF16)
    wgb = jnp.zeros((128, kw), F32).at[GATE_RANK:2 * GATE_RANK].set(w_gate_b).astype(BF16)
    wr = jnp.zeros((D_MODEL, 128), F32)
    wr = wr.at[:, :N_EXPERTS].set(w_re).at[:, N_EXPERTS:2 * N_EXPERTS].set(jnp.repeat(w_rg, EPG, axis=1))
    br = jnp.zeros((1, 128), F32)
    br = br.at[0, :N_EXPERTS].set(b_re).at[0, N_EXPERTS:2 * N_EXPERTS].set(jnp.repeat(b_rg, EPG))
    return dict(
        norm_mix=norm_mix.reshape(1, D_MODEL), w_all=w_all, rel_bias=rel_bias,
        wgf=wgf, wgb=wgb, bgf=b_gate_f.reshape(1, kw), bgb=b_gate_b.reshape(1, kw),
        gla_norm=gla_norm.reshape(1, GLA_DV),
        wpa=w_proj_a.astype(BF16), wpb=w_proj_b.astype(BF16), wo=w_out.astype(BF16),
        norm_moe=norm_moe.reshape(1, D_MODEL), wr=wr.astype(BF16), br=br,
        wgu=jnp.concatenate([w_eg, w_eu], axis=-1).astype(BF16), wd=w_ed.astype(BF16),
        norm_final=norm_final.reshape(1, D_MODEL),
    )


def _trunk(x, w):
    B, S, _ = x.shape
    T = B * S
    x2d = x.reshape(T, D_MODEL)
    *qkv, proj = _inproj(x, w["norm_mix"], w["w_all"])
    o_list, lse_list = [], []
    for g, (_, d) in enumerate(DIL_PAIRS):
        L = S // d
        tiles = _bias_tiles(w["rel_bias"], g, d, min(2 * Q_BLK, L))
        o, lse = _attn_group(qkv[g], tiles, d, B, S)
        o_list.append(o.reshape(T, A_GW))
        lse_list.append(lse.reshape(T, A_GW))
    ob = _gla(proj, w["wgf"], w["wgb"], w["bgf"], w["bgb"], w["gla_norm"], B, S)
    x1, hm, gate, gatet = _outproj(x2d, proj, o_list, lse_list, ob.reshape(T, D_MODEL), w["wpa"], w["wpb"],
                                   w["wo"], w["norm_moe"], w["wr"], w["br"])
    y = _moe(hm, gate, gatet, x1, w["wgu"], w["wd"], w["norm_final"])
    return y.reshape(B, S, D_MODEL)


def kernel(x_prompt, x_sample, norm_mix, w_in, rel_bias, w_gate_f, b_gate_f, w_gate_b, b_gate_b, gla_norm,
           w_proj_a, w_proj_b, w_out, norm_moe, w_router_group, b_router_group, w_router_expert,
           b_router_expert, w_exp_gate, w_exp_up, w_exp_down, norm_final):
    w = _prep_weights(norm_mix[0], w_in[0], rel_bias, w_gate_f[0], b_gate_f[0], w_gate_b[0], b_gate_b[0],
                      gla_norm[0], w_proj_a[0], w_proj_b[0], w_out[0], norm_moe[0], w_router_group[0],
                      b_router_group[0], w_router_expert[0], b_router_expert[0], w_exp_gate[0], w_exp_up[0],
                      w_exp_down[0], norm_final)
    return (_trunk(x_prompt, w), _trunk(x_sample, w))
```

```python
import functools
import math

import numpy as np
import jax
import jax.numpy as jnp
from jax import lax
from jax.experimental import pallas as pl
from jax.experimental.pallas import tpu as pltpu

F32 = jnp.float32
BF16 = jnp.bfloat16

D_MODEL = 1024
EPS = 1e-6
NEG_INF = -1e30

DIL_PAIRS = ((128, 1), (512, 4), (2048, 16))
A_HPG = 4
A_DH = 64
A_GW = A_HPG * A_DH
A_QKV = 3 * A_GW
HALF_WIN = 64
Q_BLK = 128
N_REL_BUCKETS = 32
REL_MAX_DISTANCE = 1024
GLA_H = 4
GLA_DK = 128
GLA_DV = 256
GATE_RANK = 16
GATE_TAU = 16.0
GLA_CHUNK = 64
N_GROUPS = 4
EPG = 8
N_EXPERTS = 32
EXPERT_FF = 256

N_DIL = len(DIL_PAIRS)
C_MA, C_MB = 0, 1024
C_QB, C_KB, C_VB, C_OG = 2048, 2560, 3072, 4096
C_G = 5120
PR = 5376
PROJ_TN = A_QKV

VMEM_LIMIT = 58 * 1024 * 1024


def _cparams(sem):
    return pltpu.CompilerParams(dimension_semantics=sem, vmem_limit_bytes=VMEM_LIMIT)


def _inproj_kernel(x_ref, g_ref, w_ref, a0_ref, a1_ref, a2_ref, or_ref, y_scr):
    x = x_ref[...]
    r = x * lax.rsqrt(jnp.mean(x * x, axis=-1, keepdims=True) + EPS)
    h = (r * g_ref[...]).astype(BF16)
    nct = PROJ_TN // 128
    for j in range(N_DIL + PR // PROJ_TN):
        y = jnp.dot(h, w_ref[:, j * PROJ_TN:(j + 1) * PROJ_TN], preferred_element_type=F32)
        if j >= N_DIL:
            or_ref[:, (j - N_DIL) * PROJ_TN:(j - N_DIL + 1) * PROJ_TN] = y.astype(BF16)
            continue
        a_ref = (a0_ref, a1_ref, a2_ref)[j]
        d = DIL_PAIRS[j][1]
        if d == 1:
            a_ref[0, 0] = y.astype(BF16)
            continue
        for c in range(nct):
            y_scr[j - 1, c] = y[:, c * 128:(c + 1) * 128]
        n = y.shape[0] // d
        for cls in range(d):
            cols = [y_scr[j - 1, c, pl.ds(cls, n, stride=d), :] for c in range(nct)]
            a_ref[0, cls] = jnp.concatenate(cols, axis=1).astype(BF16)


def _inproj(x, norm_g, w_all, tm=512):
    B, S, _ = x.shape
    T = B * S
    spb = S // tm
    a_specs, a_shapes = [], []
    for _, d in DIL_PAIRS:
        a_specs.append(pl.BlockSpec((1, d, tm // d, PROJ_TN), lambda i: (i // spb, 0, i % spb, 0)))
        a_shapes.append(jax.ShapeDtypeStruct((B, d, S // d, PROJ_TN), BF16))
    return pl.pallas_call(
        _inproj_kernel,
        grid=(T // tm,),
        in_specs=[
            pl.BlockSpec((tm, D_MODEL), lambda i: (i, 0)),
            pl.BlockSpec((1, D_MODEL), lambda i: (0, 0)),
            pl.BlockSpec(w_all.shape, lambda i: (0, 0), pipeline_mode=pl.Buffered(1)),
        ],
        out_specs=a_specs + [pl.BlockSpec((tm, PR), lambda i: (i, 0))],
        out_shape=a_shapes + [jax.ShapeDtypeStruct((T, PR), BF16)],
        scratch_shapes=[pltpu.VMEM((N_DIL - 1, PROJ_TN // 128, tm, 128), F32)],
        compiler_params=_cparams(("parallel",)),
        name="inproj",
    )(x.reshape(T, D_MODEL), norm_g, w_all)


ATTN_ITEMS = 4


def _attn_kernel(q_ref, k_ref, v_ref, bias_ref, o_ref, lse_ref, *, L, wk, nc, nq, nblk):
    i = pl.program_id(2)
    lane = lax.broadcasted_iota(jnp.int32, (Q_BLK, A_GW), 1)
    qscale = jnp.asarray(A_DH ** -0.5, BF16)
    nt = (((1,), (1,)), ((), ()))
    for c in range(nc):
        for t in range(nq):
            blk = i * nq + t
            start = pl.multiple_of(jnp.clip(blk * Q_BLK - HALF_WIN, 0, L - wk), HALF_WIN)
            var = jnp.where(blk == 0, 0, jnp.where(blk == nblk - 1, 2, 1))
            rows = slice(t * Q_BLK, (t + 1) * Q_BLK)
            q = q_ref[0, c, rows, :] * qscale
            kw = k_ref[0, c, pl.ds(start, wk), :]
            vw = v_ref[0, c, pl.ds(start, wk), :]
            outs = []
            lse_tile = jnp.zeros((Q_BLK, A_GW), F32)
            for h in range(A_HPG):
                sl = slice(h * A_DH, (h + 1) * A_DH)
                s = lax.dot_general(q[:, sl], kw[:, sl], nt, preferred_element_type=F32) + bias_ref[var, h]
                m = jnp.max(s, axis=-1, keepdims=True)
                p = jnp.exp(s - m)
                l = jnp.sum(p, axis=-1, keepdims=True)
                outs.append(jnp.dot(p.astype(BF16), vw[:, sl], preferred_element_type=F32) / l)
                lse = m + jnp.log(l)
                lse_tile = jnp.where((lane >= A_DH * h) & (lane < A_DH * (h + 1)), lse, lse_tile)
            cols = slice(c * A_GW, (c + 1) * A_GW)
            o_ref[0, rows, cols] = jnp.concatenate(outs, axis=1).astype(BF16)
            lse_ref[0, rows, cols] = lse_tile


def _attn_group(qkv, bias_tiles, d, B, S):
    L = S // d
    nblk = L // Q_BLK
    wk = min(2 * Q_BLK, L)
    nq = min(ATTN_ITEMS, nblk)
    nc = ATTN_ITEMS // nq

    return pl.pallas_call(
        functools.partial(_attn_kernel, L=L, wk=wk, nc=nc, nq=nq, nblk=nblk),
        grid=(B, d // nc, nblk // nq),
        in_specs=[
            pl.BlockSpec((1, nc, nq * Q_BLK, A_GW), lambda b, r, i: (b, r, i, 0)),
            pl.BlockSpec((1, nc, L, A_GW), lambda b, r, i: (b, r, 0, 1)),
            pl.BlockSpec((1, nc, L, A_GW), lambda b, r, i: (b, r, 0, 2)),
            pl.BlockSpec(bias_tiles.shape, lambda b, r, i: (0, 0, 0, 0)),
        ],
        out_specs=[
            pl.BlockSpec((1, nq * Q_BLK, nc * A_GW), lambda b, r, i: (b, i, r)),
            pl.BlockSpec((1, nq * Q_BLK, nc * A_GW), lambda b, r, i: (b, i, r)),
        ],
        out_shape=[
            jax.ShapeDtypeStruct((B, L, d * A_GW), BF16),
            jax.ShapeDtypeStruct((B, L, d * A_GW), F32),
        ],
        compiler_params=_cparams(("parallel", "parallel", "arbitrary")),
        name=f"attn_d{d}",
    )(qkv, qkv, qkv, bias_tiles)


def _t5_buckets(rel):
    nb = N_REL_BUCKETS // 2
    max_exact = nb // 2
    ret = (rel > 0).astype(np.int64) * nb
    n = np.abs(rel)
    large = max_exact + (np.log(np.maximum(n, 1) / max_exact) / math.log(REL_MAX_DISTANCE / max_exact)
                         * (nb - max_exact)).astype(np.int64)
    large = np.minimum(large, nb - 1)
    return (ret + np.where(n < max_exact, n, large)).astype(np.int32)


def _bias_tiles(rel_bias, g, d, wk):
    rel = d * np.arange(-HALF_WIN, HALF_WIN + 1)
    bucket = _t5_buckets(rel)
    bias = rel_bias[jnp.asarray(bucket)][:, g * A_HPG:(g + 1) * A_HPG].T.astype(F32)
    pad = wk + Q_BLK
    neg = jnp.full((A_HPG, pad), NEG_INF, F32)
    val = jnp.concatenate([neg, bias, neg], axis=1)
    zero = pad + HALF_WIN
    m = wk + Q_BLK
    tiles = []
    for off in (0, HALF_WIN, wk - Q_BLK):
        u = jnp.concatenate([val[:, zero - off:zero - off + wk], val[:, zero - off - Q_BLK:zero - off]], axis=1)
        flat = jnp.tile(u, (1, Q_BLK))[:, :Q_BLK * (m - 1)]
        tiles.append(flat.reshape(A_HPG, Q_BLK, m - 1)[:, :, :wk])
    return jnp.stack(tiles)


def _log_sigmoid(z):
    return jnp.minimum(z, 0.0) - jnp.log(1.0 + jnp.exp(-jnp.abs(z)))


def _gla_kernel(q_ref, k_ref, v_ref, og_ref, g_ref, wgf_ref, wgb_ref, bgf_ref, bgb_ref, gn_ref, o_ref,
                qif, kof, qib, kob, vt, etf, etb, acc, *, S):
    PB = 256
    C = GLA_CHUNK
    NB = S // PB
    rr = lax.broadcasted_iota(jnp.int32, (PB, PB), 0)
    cc = lax.broadcasted_iota(jnp.int32, (PB, PB), 1)
    same = (rr >> 6) == (cc >> 6)
    mask_f = same & (cc <= rr)
    mask_b = same & (cc > rr)
    tl = jnp.where(mask_f, 1.0, 0.0).astype(BF16)
    qscale = GLA_DK ** -0.5
    inv_tau = 1.0 / GATE_TAU
    nt = (((1,), (1,)), ((), ()))

    def split3(x):
        hi = x.astype(BF16)
        r1 = x - hi.astype(F32)
        mid = r1.astype(BF16)
        lo = (r1 - mid.astype(F32)).astype(BF16)
        return [hi, mid, lo]

    def chunk_total(b):
        b4 = b.reshape(PB // C, C, GLA_DK)
        return jnp.broadcast_to(b4[:, C - 1:C, :], b4.shape).reshape(PB, GLA_DK)

    def prep(i, carry):
        rows = pl.ds(pl.multiple_of(i * PB, PB), PB)
        g = g_ref[0, rows, :]
        q = q_ref[0, rows, :].astype(F32) * qscale
        k = k_ref[0, rows, :].astype(F32)
        v = v_ref[0, rows, :]
        lf = _log_sigmoid(jnp.dot(g, wgf_ref[...], preferred_element_type=F32) + bgf_ref[...]) * inv_tau
        lb = _log_sigmoid(jnp.dot(g, wgb_ref[...], preferred_element_type=F32) + bgb_ref[...]) * inv_tau
        cs = jnp.dot(tl, jnp.concatenate(split3(lf) + split3(lb), axis=1), preferred_element_type=F32)
        bf = cs[:, 0:128] + cs[:, 128:256] + cs[:, 256:384]
        pb = cs[:, 384:512] + cs[:, 512:640] + cs[:, 640:768]
        totf = chunk_total(bf)
        totb = chunk_total(pb)
        bs = totb - pb + lb
        qf = (q * jnp.exp(bf)).astype(BF16)
        qb = (q * jnp.exp(bs)).astype(BF16)
        sf = lax.dot_general(qf, (k * jnp.exp(-bf)).astype(BF16), nt, preferred_element_type=F32)
        sb = lax.dot_general(qb, (k * jnp.exp(-bs)).astype(BF16), nt, preferred_element_type=F32)
        p = jnp.where(mask_f, sf, jnp.where(mask_b, sb, 0.0)).astype(BF16)
        acc[rows, :] = jnp.dot(p, v, preferred_element_type=F32)
        qif[rows, :] = qf
        qib[rows, :] = qb
        kof[rows, :] = (k * jnp.exp(totf - bf)).astype(BF16)
        kob[rows, :] = (k * jnp.exp(totb - bs)).astype(BF16)
        etf[rows, :] = jnp.exp(totf)
        etb[rows, :] = jnp.exp(totb)
        vtb = v.astype(F32).T.astype(BF16)
        vt[2 * i] = vtb[:, :128]
        vt[2 * i + 1] = vtb[:, 128:]
        return carry

    lax.fori_loop(0, NB, prep, 0, unroll=4)

    zeros_half = jnp.zeros((C, GLA_DK), BF16)

    def chunk_step(qi_ref, ko_ref, et_ref, blk, j, state):
        r0 = pl.multiple_of(blk * PB + j * C, C)
        rows = pl.ds(r0, C)
        acc[rows, :] += lax.dot_general(qi_ref[rows, :], state.astype(BF16), nt, preferred_element_type=F32)
        ko = ko_ref[rows, :]
        ko_pad = jnp.concatenate([ko, zeros_half] if j % 2 == 0 else [zeros_half, ko], axis=0)
        upd = jnp.dot(vt[2 * blk + j // 2], ko_pad, preferred_element_type=F32)
        return state * et_ref[pl.ds(r0, 1), :] + upd

    def serial(t, carry):
        st_f, st_b = carry
        for j in range(PB // C):
            st_f = chunk_step(qif, kof, etf, t, j, st_f)
            st_b = chunk_step(qib, kob, etb, NB - 1 - t, PB // C - 1 - j, st_b)
        return st_f, st_b

    zero_state = jnp.zeros((GLA_DV, GLA_DK), F32)
    lax.fori_loop(0, NB, serial, (zero_state, zero_state), unroll=4)

    gn = gn_ref[...]

    def finish(i, carry):
        rows = pl.ds(pl.multiple_of(i * PB, PB), PB)
        tot = acc[rows, :]
        nrm = tot * lax.rsqrt(jnp.mean(tot * tot, axis=-1, keepdims=True) + EPS) * gn
        og = og_ref[0, rows, :].astype(F32)
        o_ref[0, rows, :] = (nrm * (og * jax.nn.sigmoid(og))).astype(BF16)
        return carry

    lax.fori_loop(0, NB, finish, 0)


def _gla(proj, wgf, wgb, bgf, bgb, gla_norm, B, S):
    pv = proj.reshape(B, S, PR)
    cq, ck, cv, cog, cg = C_QB // GLA_DK, C_KB // GLA_DK, C_VB // GLA_DV, C_OG // GLA_DV, C_G // 128
    return pl.pallas_call(
        functools.partial(_gla_kernel, S=S),
        grid=(B, GLA_H),
        in_specs=[
            pl.BlockSpec((1, S, GLA_DK), lambda b, h: (b, 0, cq + h)),
            pl.BlockSpec((1, S, GLA_DK), lambda b, h: (b, 0, ck + h)),
            pl.BlockSpec((1, S, GLA_DV), lambda b, h: (b, 0, cv + h)),
            pl.BlockSpec((1, S, GLA_DV), lambda b, h: (b, 0, cog + h)),
            pl.BlockSpec((1, S, 128), lambda b, h: (b, 0, cg)),
            pl.BlockSpec((128, GLA_DK), lambda b, h: (0, h)),
            pl.BlockSpec((128, GLA_DK), lambda b, h: (0, h)),
            pl.BlockSpec((1, GLA_DK), lambda b, h: (0, h)),
            pl.BlockSpec((1, GLA_DK), lambda b, h: (0, h)),
            pl.BlockSpec((1, GLA_DV), lambda b, h: (0, 0)),
        ],
        out_specs=pl.BlockSpec((1, S, GLA_DV), lambda b, h: (b, 0, h)),
        out_shape=jax.ShapeDtypeStruct((B, S, GLA_H * GLA_DV), BF16),
        scratch_shapes=[
            pltpu.VMEM((S, GLA_DK), BF16), pltpu.VMEM((S, GLA_DK), BF16),
            pltpu.VMEM((S, GLA_DK), BF16), pltpu.VMEM((S, GLA_DK), BF16),
            pltpu.VMEM((S // 128, GLA_DV, 128), BF16),
            pltpu.VMEM((S, GLA_DK), F32), pltpu.VMEM((S, GLA_DK), F32),
            pltpu.VMEM((S, GLA_DV), F32),
        ],
        compiler_params=_cparams(("parallel", "arbitrary")),
        name="gla",
    )(pv, pv, pv, pv, pv, wgf, wgb, bgf, bgb, gla_norm)


def _outproj_kernel(x_ref, ma_ref, mb_ref, o0_ref, o1_ref, o2_ref, l0_ref, l1_ref, l2_ref, ob_ref,
                    wpa_ref, wpb_ref, wo_ref, nm_ref, wr_ref, br_ref,
                    x1_ref, hm_ref, gate_ref, gatet_ref):
    l0, l1, l2 = l0_ref[...], l1_ref[...], l2_ref[...]
    m = jnp.maximum(jnp.maximum(l0, l1), l2)
    e0, e1, e2 = jnp.exp(l0 - m), jnp.exp(l1 - m), jnp.exp(l2 - m)
    den = e0 + e1 + e2
    oa = ((e0 / den) * o0_ref[...].astype(F32) + (e1 / den) * o1_ref[...].astype(F32)
          + (e2 / den) * o2_ref[...].astype(F32))
    ya = jnp.dot(oa.astype(BF16), wpa_ref[...], preferred_element_type=F32)
    yb = jnp.dot(ob_ref[...], wpb_ref[...], preferred_element_type=F32)
    mix = jax.nn.sigmoid(ma_ref[...].astype(F32)) * ya + jax.nn.sigmoid(mb_ref[...].astype(F32)) * yb
    x1 = x_ref[...] + jnp.dot(mix.astype(BF16), wo_ref[...], preferred_element_type=F32)
    x1_ref[...] = x1
    hm = x1 * lax.rsqrt(jnp.mean(x1 * x1, axis=-1, keepdims=True) + EPS) * nm_ref[...]
    hm_b = hm.astype(BF16)
    hm_ref[...] = hm_b

    logits = jnp.dot(hm_b, wr_ref[...], preferred_element_type=F32) + br_ref[...]
    tm = logits.shape[0]
    lane = lax.broadcasted_iota(jnp.int32, (tm, 128), 1)
    is_g = (lane >= N_EXPERTS) & (lane < 2 * N_EXPERTS)
    is_e = lane < N_EXPERTS
    grp_of_lane = jnp.where(is_g, (lane - N_EXPERTS) >> 3, lane >> 3)
    gl = jnp.where(is_g, logits, NEG_INF)
    gmax = jnp.max(gl, axis=-1, keepdims=True)
    gsum = jnp.sum(jnp.where(is_g, jnp.exp(gl - gmax), 0.0), axis=-1, keepdims=True) * (1.0 / EPG)
    pg_top = 1.0 / gsum
    g_sel = jnp.min(jnp.where(is_g & (gl == gmax), grp_of_lane, N_GROUPS), axis=-1, keepdims=True)
    in_grp = is_e & (grp_of_lane == g_sel)
    el = jnp.where(in_grp, logits, NEG_INF)
    emax = jnp.max(el, axis=-1, keepdims=True)
    ee = jnp.where(in_grp, jnp.exp(el - emax), 0.0)
    pe = ee / jnp.sum(ee, axis=-1, keepdims=True)
    p1 = jnp.max(pe, axis=-1, keepdims=True)
    i1 = jnp.min(jnp.where(in_grp & (pe == p1), lane, 128), axis=-1, keepdims=True)
    rest = in_grp & (lane != i1)
    pe2 = jnp.where(rest, pe, -1.0)
    p2 = jnp.max(pe2, axis=-1, keepdims=True)
    i2 = jnp.min(jnp.where(rest & (pe2 == p2), lane, 128), axis=-1, keepdims=True)
    psum = p1 + p2
    gate = jnp.where(lane == i1, pg_top * (p1 / psum), jnp.where(lane == i2, pg_top * (p2 / psum), 0.0))
    gate_ref[...] = gate
    gatet_ref[...] = gate.T


def _outproj(x2d, proj, o_list, lse_list, ob2d, wpa, wpb, wo, norm_moe, wr, br, tm=512):
    T = x2d.shape[0]
    row = lambda w: pl.BlockSpec((tm, w), lambda i: (i, 0))
    full = lambda a: pl.BlockSpec(a.shape, lambda i: (0,) * a.ndim)
    return pl.pallas_call(
        _outproj_kernel,
        grid=(T // tm,),
        in_specs=[
            row(D_MODEL),
            pl.BlockSpec((tm, D_MODEL), lambda i: (i, C_MA // D_MODEL)),
            pl.BlockSpec((tm, D_MODEL), lambda i: (i, C_MB // D_MODEL)),
            row(A_GW), row(A_GW), row(A_GW), row(A_GW), row(A_GW), row(A_GW),
            row(D_MODEL),
            full(wpa), full(wpb), full(wo), full(norm_moe), full(wr), full(br),
        ],
        out_specs=[row(D_MODEL), row(D_MODEL), row(128), pl.BlockSpec((128, tm), lambda i: (0, i))],
        out_shape=[
            jax.ShapeDtypeStruct((T, D_MODEL), F32),
            jax.ShapeDtypeStruct((T, D_MODEL), BF16),
            jax.ShapeDtypeStruct((T, 128), F32),
            jax.ShapeDtypeStruct((128, T), F32),
        ],
        compiler_params=_cparams(("parallel",)),
        name="outproj",
    )(x2d, proj, proj, *o_list, *lse_list, ob2d, wpa, wpb, wo, norm_moe, wr, br)


MOE_TB = 1024
MOE_ALIGN = 16
MOE_CH = 128
MOE_R = 2 * MOE_TB + N_EXPERTS * MOE_ALIGN
MOE_RS = MOE_R + MOE_CH
MOE_SUB = 256
MOE_G = 4


def _moe_route(hm_ref, gate_ref, gatet_ref, xs, ys, pt, wrow, rinfo):
    TB, R, SUB = MOE_TB, MOE_R, MOE_SUB
    big = float(4 * TB)
    gate = gate_ref[...]
    gt = gatet_ref[...]
    a = gate > 0.0
    at = gt > 0.0
    a_b = jnp.where(a, 1.0, 0.0).astype(BF16)
    at_b = jnp.where(at, 1.0, 0.0).astype(BF16)

    def pad(c):
        return jnp.floor((c + (MOE_ALIGN - 1)) * (1.0 / MOE_ALIGN)) * MOE_ALIGN

    er = lax.broadcasted_iota(jnp.int32, (128, 128), 0)
    ec = lax.broadcasted_iota(jnp.int32, (128, 128), 1)
    pad_row = pad(jnp.sum(jnp.where(a, 1.0, 0.0), axis=0, keepdims=True))
    off_row = jnp.dot(jnp.broadcast_to(pad_row, (8, 128)).astype(BF16), jnp.where(er < ec, 1.0, 0.0).astype(BF16),
                      preferred_element_type=F32)[0:1]
    pad_col = pad(jnp.sum(jnp.where(at, 1.0, 0.0), axis=1, keepdims=True))
    off_col = jnp.dot(jnp.where(ec < er, 1.0, 0.0).astype(BF16), jnp.broadcast_to(pad_col, (128, 128)).astype(BF16),
                      preferred_element_type=F32)[:, 0:1]

    rl = lax.broadcasted_iota(jnp.int32, (SUB, R), 1).astype(F32)
    for rc in range(TB // SUB):
        rows = slice(rc * SUB, (rc + 1) * SUB)
        tr = lax.broadcasted_iota(jnp.int32, (SUB, TB), 0) + rc * SUB
        tc = lax.broadcasted_iota(jnp.int32, (SUB, TB), 1)
        rank = jnp.dot(jnp.where(tc < tr, 1.0, 0.0).astype(BF16), a_b, preferred_element_type=F32)
        pos = off_row + rank
        a_c = a[rows]
        plo = jnp.min(jnp.where(a_c, pos, big), axis=1, keepdims=True)
        phi = jnp.max(jnp.where(a_c, pos, -1.0), axis=1, keepdims=True)
        phi = jnp.where(phi == plo, -1.0, phi)
        pt[rows, :] = jnp.where((rl == plo) | (rl == phi), 1.0, 0.0).astype(BF16)

    for cc in range(TB // SUB):
        cols = slice(cc * SUB, (cc + 1) * SUB)
        tr = lax.broadcasted_iota(jnp.int32, (TB, SUB), 0)
        tc = lax.broadcasted_iota(jnp.int32, (TB, SUB), 1) + cc * SUB
        rank_t = jnp.dot(at_b, jnp.where(tr < tc, 1.0, 0.0).astype(BF16), preferred_element_type=F32)
        pos_t = off_col + rank_t
        at_c = at[:, cols]
        gt_c = gt[:, cols]
        plo = jnp.min(jnp.where(at_c, pos_t, big), axis=0, keepdims=True)
        phi = jnp.max(jnp.where(at_c, pos_t, -1.0), axis=0, keepdims=True)
        phi = jnp.where(phi == plo, -1.0, phi)
        rinfo[0:1, cols] = plo
        rinfo[1:2, cols] = phi
        rinfo[2:3, cols] = jnp.sum(jnp.where(at_c & (pos_t == plo), gt_c, 0.0), axis=0, keepdims=True)
        rinfo[3:4, cols] = jnp.sum(jnp.where(at_c & (pos_t == phi), gt_c, 0.0), axis=0, keepdims=True)

    def gather(k, carry):
        r0 = pl.multiple_of(k * SUB, SUB)
        ri = (lax.broadcasted_iota(jnp.int32, (SUB, TB), 0) + r0).astype(F32)
        mlo = ri == rinfo[0:1, :]
        mhi = ri == rinfo[1:2, :]
        p = jnp.where(mlo | mhi, 1.0, 0.0).astype(BF16)
        xs[pl.ds(r0, SUB), :] = jnp.dot(p, hm_ref[...], preferred_element_type=F32).astype(BF16)
        w = jnp.sum(jnp.where(mlo, rinfo[2:3, :], 0.0) + jnp.where(mhi, rinfo[3:4, :], 0.0), axis=1, keepdims=True)
        wrow[pl.ds(r0, SUB), :] = jnp.broadcast_to(w, (SUB, 128))
        return carry

    lax.fori_loop(0, R // SUB, gather, 0)
    xs[R:, :] = jnp.zeros((MOE_RS - R, D_MODEL), BF16)
    wrow[R:, :] = jnp.zeros((MOE_RS - R, 128), F32)
    ys[...] = jnp.zeros_like(ys)


def _moe_kernel(meta_ref, hm_ref, gate_ref, gatet_ref, x1_ref, wgu_ref, wd_ref, nf_ref, y_ref,
                xs, ys, pt, wrow, rinfo):
    i = pl.program_id(0)
    s = pl.program_id(1)

    @pl.when(s == 0)
    def _():
        _moe_route(hm_ref, gate_ref, gatet_ref, xs, ys, pt, wrow, rinfo)

    offs = [meta_ref[i, s * MOE_G + g] for g in range(MOE_G)]
    nchs = [meta_ref[i, N_EXPERTS + s * MOE_G + g] for g in range(MOE_G)]

    def chunk(g, r0):
        rows = pl.ds(pl.multiple_of(r0, MOE_ALIGN), MOE_CH)
        gu = jnp.dot(xs[rows, :], wgu_ref[g], preferred_element_type=F32)
        gt, up = gu[:, :EXPERT_FF], gu[:, EXPERT_FF:]
        act = (gt * jax.nn.sigmoid(gt)) * up
        w = wrow[rows, :]
        act = jnp.concatenate([act[:, :128] * w, act[:, 128:] * w], axis=1)
        ys[rows, :] = jnp.dot(act.astype(BF16), wd_ref[g], preferred_element_type=F32).astype(BF16)

    single = nchs[0] <= 1
    for g in range(1, MOE_G):
        single = jnp.logical_and(single, nchs[g] <= 1)

    @pl.when(single)
    def _():
        for g in range(MOE_G):
            chunk(g, offs[g])

    @pl.when(jnp.logical_not(single))
    def _():
        for g in range(MOE_G):
            def body(c, carry, g=g):
                chunk(g, offs[g] + c * MOE_CH)
                return carry
            lax.fori_loop(0, nchs[g], body, 0)

    @pl.when(s == N_EXPERTS // MOE_G - 1)
    def _():
        x2 = x1_ref[...] + jnp.dot(pt[...], ys[0:MOE_R, :], preferred_element_type=F32)
        y_ref[...] = x2 * lax.rsqrt(jnp.mean(x2 * x2, axis=-1, keepdims=True) + EPS) * nf_ref[...]


def _moe(hm, gate, gatet, x1, wgu, wd, norm_final):
    T = hm.shape[0]
    tb = MOE_TB
    nb = T // tb
    cnt = jnp.sum((gate.reshape(nb, tb, 128)[:, :, :N_EXPERTS] > 0.0).astype(jnp.int32), axis=1)
    padded = (cnt + (MOE_ALIGN - 1)) // MOE_ALIGN * MOE_ALIGN
    off = jnp.cumsum(padded, axis=1) - padded
    meta = jnp.concatenate([off, (padded + (MOE_CH - 1)) // MOE_CH], axis=1).astype(jnp.int32)
    grid_spec = pltpu.PrefetchScalarGridSpec(
        num_scalar_prefetch=1,
        grid=(nb, N_EXPERTS // MOE_G),
        in_specs=[
            pl.BlockSpec((tb, D_MODEL), lambda i, e, m: (i, 0)),
            pl.BlockSpec((tb, 128), lambda i, e, m: (i, 0)),
            pl.BlockSpec((128, tb), lambda i, e, m: (0, i)),
            pl.BlockSpec((tb, D_MODEL), lambda i, e, m: (i, 0)),
            pl.BlockSpec((MOE_G, D_MODEL, 2 * EXPERT_FF), lambda i, e, m: (e, 0, 0)),
            pl.BlockSpec((MOE_G, EXPERT_FF, D_MODEL), lambda i, e, m: (e, 0, 0)),
            pl.BlockSpec((1, D_MODEL), lambda i, e, m: (0, 0)),
        ],
        out_specs=pl.BlockSpec((tb, D_MODEL), lambda i, e, m: (i, 0)),
        scratch_shapes=[
            pltpu.VMEM((MOE_RS, D_MODEL), BF16),
            pltpu.VMEM((MOE_RS, D_MODEL), BF16),
            pltpu.VMEM((tb, MOE_R), BF16),
            pltpu.VMEM((MOE_RS, 128), F32),
            pltpu.VMEM((8, tb), F32),
        ],
    )
    return pl.pallas_call(
        _moe_kernel,
        grid_spec=grid_spec,
        out_shape=jax.ShapeDtypeStruct((T, D_MODEL), F32),
        compiler_params=_cparams(("parallel", "arbitrary")),
        name="moe",
    )(meta, hm, gate, gatet, x1, wgu, wd, norm_final)


def _prep_weights(norm_mix, w_in, rel_bias, w_gate_f, b_gate_f, w_gate_b, b_gate_b, gla_norm, w_proj_a,
                  w_proj_b, w_out, norm_moe, w_rg, b_rg, w_re, b_re, w_eg, w_eu, w_ed, norm_final):
    splits = (A_QKV, A_QKV, A_QKV, 512, 512, 1024, 1024, GATE_RANK, GATE_RANK, D_MODEL, D_MODEL)
    qa, ka, va, qb, kb, vb, og, gf, gb, ma, mb = jnp.split(w_in, np.cumsum(splits)[:-1].tolist(), axis=1)
    gpad = jnp.zeros((D_MODEL, PR - C_G - 2 * GATE_RANK), w_in.dtype)
    grp = lambda t, g: t[:, g * A_GW:(g + 1) * A_GW]
    a_cols = [grp(t, g) for g in range(N_DIL) for t in (qa, ka, va)]
    w_all = jnp.concatenate(a_cols + [ma, mb, qb, kb, vb, og, gf, gb, gpad], axis=1).astype(BF16)
    kw = GLA_H * GLA_DK
    wgf = jnp.zeros((128, kw), F32).at[:GATE_RANK].set(w_gate_f).astype(BF16)
    wgb = jnp.zeros((128, kw), F32).at[GATE_RANK:2 * GATE_RANK].set(w_gate_b).astype(BF16)
    wr = jnp.zeros((D_MODEL, 128), F32)
    wr = wr.at[:, :N_EXPERTS].set(w_re).at[:, N_EXPERTS:2 * N_EXPERTS].set(jnp.repeat(w_rg, EPG, axis=1))
    br = jnp.zeros((1, 128), F32)
    br = br.at[0, :N_EXPERTS].set(b_re).at[0, N_EXPERTS:2 * N_EXPERTS].set(jnp.repeat(b_rg, EPG))
    return dict(
        norm_mix=norm_mix.reshape(1, D_MODEL), w_all=w_all, rel_bias=rel_bias,
        wgf=wgf, wgb=wgb, bgf=b_gate_f.reshape(1, kw), bgb=b_gate_b.reshape(1, kw),
        gla_norm=gla_norm.reshape(1, GLA_DV),
        wpa=w_proj_a.astype(BF16), wpb=w_proj_b.astype(BF16), wo=w_out.astype(BF16),
        norm_moe=norm_moe.reshape(1, D_MODEL), wr=wr.astype(BF16), br=br,
        wgu=jnp.concatenate([w_eg, w_eu], axis=-1).astype(BF16), wd=w_ed.astype(BF16),
        norm_final=norm_final.reshape(1, D_MODEL),
    )


def _trunk(x, w):
    B, S, _ = x.shape
    T = B * S
    x2d = x.reshape(T, D_MODEL)
    *qkv, proj = _inproj(x, w["norm_mix"], w["w_all"])
    o_list, lse_list = [], []
    for g, (_, d) in enumerate(DIL_PAIRS):
        L = S // d
        tiles = _bias_tiles(w["rel_bias"], g, d, min(2 * Q_BLK, L))
        o, lse = _attn_group(qkv[g], tiles, d, B, S)
        o_list.append(o.reshape(T, A_GW))
        lse_list.append(lse.reshape(T, A_GW))
    ob = _gla(proj, w["wgf"], w["wgb"], w["bgf"], w["bgb"], w["gla_norm"], B, S)
    x1, hm, gate, gatet = _outproj(x2d, proj, o_list, lse_list, ob.reshape(T, D_MODEL), w["wpa"], w["wpb"],
                                   w["wo"], w["norm_moe"], w["wr"], w["br"])
    y = _moe(hm, gate, gatet, x1, w["wgu"], w["wd"], w["norm_final"])
    return y.reshape(B, S, D_MODEL)


def kernel(x_prompt, x_sample, norm_mix, w_in, rel_bias, w_gate_f, b_gate_f, w_gate_b, b_gate_b, gla_norm,
           w_proj_a, w_proj_b, w_out, norm_moe, w_router_group, b_router_group, w_router_expert,
           b_router_expert, w_exp_gate, w_exp_up, w_exp_down, norm_final):
    w = _prep_weights(norm_mix[0], w_in[0], rel_bias, w_gate_f[0], b_gate_f[0], w_gate_b[0], b_gate_b[0],
                      gla_norm[0], w_proj_a[0], w_proj_b[0], w_out[0], norm_moe[0], w_router_group[0],
                      b_router_group[0], w_router_expert[0], b_router_expert[0], w_exp_gate[0], w_exp_up[0],
                      w_exp_down[0], norm_final)
    return (_trunk(x_prompt, w), _trunk(x_sample, w))
```

```python
import functools
import math

import numpy as np
import jax
import jax.numpy as jnp
from jax import lax
from jax.experimental import pallas as pl
from jax.experimental.pallas import tpu as pltpu

F32 = jnp.float32
BF16 = jnp.bfloat16

D_MODEL = 1024
EPS = 1e-6
NEG_INF = -1e30

DIL_PAIRS = ((128, 1), (512, 4), (2048, 16))
A_HPG = 4
A_DH = 64
A_GW = A_HPG * A_DH
A_QKV = 3 * A_GW
HALF_WIN = 64
Q_BLK = 128
N_REL_BUCKETS = 32
REL_MAX_DISTANCE = 1024
GLA_H = 4
GLA_DK = 128
GLA_DV = 256
GATE_RANK = 16
GATE_TAU = 16.0
GLA_CHUNK = 64
N_GROUPS = 4
EPG = 8
N_EXPERTS = 32
EXPERT_FF = 256

N_DIL = len(DIL_PAIRS)
C_MA, C_MB = 0, 1024
C_QB, C_KB, C_VB, C_OG = 2048, 2560, 3072, 4096
C_G = 5120
PR = 5376
PROJ_TN = A_QKV

VMEM_LIMIT = 58 * 1024 * 1024


def _cparams(sem):
    return pltpu.CompilerParams(dimension_semantics=sem, vmem_limit_bytes=VMEM_LIMIT)


def _inproj_kernel(x_ref, g_ref, w_ref, a0_ref, a1_ref, a2_ref, or_ref, y_scr):
    x = x_ref[...]
    r = x * lax.rsqrt(jnp.mean(x * x, axis=-1, keepdims=True) + EPS)
    h = (r * g_ref[...]).astype(BF16)
    nct = PROJ_TN // 128
    for j in range(N_DIL + PR // PROJ_TN):
        y = jnp.dot(h, w_ref[:, j * PROJ_TN:(j + 1) * PROJ_TN], preferred_element_type=F32)
        if j >= N_DIL:
            or_ref[:, (j - N_DIL) * PROJ_TN:(j - N_DIL + 1) * PROJ_TN] = y.astype(BF16)
            continue
        a_ref = (a0_ref, a1_ref, a2_ref)[j]
        d = DIL_PAIRS[j][1]
        if d == 1:
            a_ref[0, 0] = y.astype(BF16)
            continue
        for c in range(nct):
            y_scr[j - 1, c] = y[:, c * 128:(c + 1) * 128]
        n = y.shape[0] // d
        for cls in range(d):
            cols = [y_scr[j - 1, c, pl.ds(cls, n, stride=d), :] for c in range(nct)]
            a_ref[0, cls] = jnp.concatenate(cols, axis=1).astype(BF16)


def _inproj(x, norm_g, w_all, tm=512):
    B, S, _ = x.shape
    T = B * S
    spb = S // tm
    a_specs, a_shapes = [], []
    for _, d in DIL_PAIRS:
        a_specs.append(pl.BlockSpec((1, d, tm // d, PROJ_TN), lambda i: (i // spb, 0, i % spb, 0)))
        a_shapes.append(jax.ShapeDtypeStruct((B, d, S // d, PROJ_TN), BF16))
    return pl.pallas_call(
        _inproj_kernel,
        grid=(T // tm,),
        in_specs=[
            pl.BlockSpec((tm, D_MODEL), lambda i: (i, 0)),
            pl.BlockSpec((1, D_MODEL), lambda i: (0, 0)),
            pl.BlockSpec(w_all.shape, lambda i: (0, 0), pipeline_mode=pl.Buffered(1)),
        ],
        out_specs=a_specs + [pl.BlockSpec((tm, PR), lambda i: (i, 0))],
        out_shape=a_shapes + [jax.ShapeDtypeStruct((T, PR), BF16)],
        scratch_shapes=[pltpu.VMEM((N_DIL - 1, PROJ_TN // 128, tm, 128), F32)],
        compiler_params=_cparams(("parallel",)),
        name="inproj",
    )(x.reshape(T, D_MODEL), norm_g, w_all)


ATTN_ITEMS = 4


def _attn_kernel(q_ref, k_ref, v_ref, bias_ref, o_ref, lse_ref, *, L, wk, nc, nq, nblk):
    i = pl.program_id(2)
    lane = lax.broadcasted_iota(jnp.int32, (Q_BLK, A_GW), 1)
    qscale = jnp.asarray(A_DH ** -0.5, BF16)
    nt = (((1,), (1,)), ((), ()))
    for c in range(nc):
        for t in range(nq):
            blk = i * nq + t
            start = pl.multiple_of(jnp.clip(blk * Q_BLK - HALF_WIN, 0, L - wk), HALF_WIN)
            var = jnp.where(blk == 0, 0, jnp.where(blk == nblk - 1, 2, 1))
            rows = slice(t * Q_BLK, (t + 1) * Q_BLK)
            q = q_ref[0, c, rows, :] * qscale
            kw = k_ref[0, c, pl.ds(start, wk), :]
            vw = v_ref[0, c, pl.ds(start, wk), :]
            outs = []
            lse_tile = jnp.zeros((Q_BLK, A_GW), F32)
            for h in range(A_HPG):
                sl = slice(h * A_DH, (h + 1) * A_DH)
                s = lax.dot_general(q[:, sl], kw[:, sl], nt, preferred_element_type=F32) + bias_ref[var, h]
                m = jnp.max(s, axis=-1, keepdims=True)
                p = jnp.exp(s - m)
                l = jnp.sum(p, axis=-1, keepdims=True)
                outs.append(jnp.dot(p.astype(BF16), vw[:, sl], preferred_element_type=F32) / l)
                lse = m + jnp.log(l)
                lse_tile = jnp.where((lane >= A_DH * h) & (lane < A_DH * (h + 1)), lse, lse_tile)
            cols = slice(c * A_GW, (c + 1) * A_GW)
            o_ref[0, rows, cols] = jnp.concatenate(outs, axis=1).astype(BF16)
            lse_ref[0, rows, cols] = lse_tile


def _attn_group(qkv, bias_tiles, d, B, S):
    L = S // d
    nblk = L // Q_BLK
    wk = min(2 * Q_BLK, L)
    nq = min(ATTN_ITEMS, nblk)
    nc = ATTN_ITEMS // nq

    return pl.pallas_call(
        functools.partial(_attn_kernel, L=L, wk=wk, nc=nc, nq=nq, nblk=nblk),
        grid=(B, d // nc, nblk // nq),
        in_specs=[
            pl.BlockSpec((1, nc, nq * Q_BLK, A_GW), lambda b, r, i: (b, r, i, 0)),
            pl.BlockSpec((1, nc, L, A_GW), lambda b, r, i: (b, r, 0, 1)),
            pl.BlockSpec((1, nc, L, A_GW), lambda b, r, i: (b, r, 0, 2)),
            pl.BlockSpec(bias_tiles.shape, lambda b, r, i: (0, 0, 0, 0)),
        ],
        out_specs=[
            pl.BlockSpec((1, nq * Q_BLK, nc * A_GW), lambda b, r, i: (b, i, r)),
            pl.BlockSpec((1, nq * Q_BLK, nc * A_GW), lambda b, r, i: (b, i, r)),
        ],
        out_shape=[
            jax.ShapeDtypeStruct((B, L, d * A_GW), BF16),
            jax.ShapeDtypeStruct((B, L, d * A_GW), F32),
        ],
        compiler_params=_cparams(("parallel", "parallel", "arbitrary")),
        name=f"attn_d{d}",
    )(qkv, qkv, qkv, bias_tiles)


def _t5_buckets(rel):
    nb = N_REL_BUCKETS // 2
    max_exact = nb // 2
    ret = (rel > 0).astype(np.int64) * nb
    n = np.abs(rel)
    large = max_exact + (np.log(np.maximum(n, 1) / max_exact) / math.log(REL_MAX_DISTANCE / max_exact)
                         * (nb - max_exact)).astype(np.int64)
    large = np.minimum(large, nb - 1)
    return (ret + np.where(n < max_exact, n, large)).astype(np.int32)


def _bias_tiles(rel_bias, g, d, wk):
    rel = d * np.arange(-HALF_WIN, HALF_WIN + 1)
    bucket = _t5_buckets(rel)
    bias = rel_bias[jnp.asarray(bucket)][:, g * A_HPG:(g + 1) * A_HPG].T.astype(F32)
    pad = wk + Q_BLK
    neg = jnp.full((A_HPG, pad), NEG_INF, F32)
    val = jnp.concatenate([neg, bias, neg], axis=1)
    zero = pad + HALF_WIN
    m = wk + Q_BLK
    tiles = []
    for off in (0, HALF_WIN, wk - Q_BLK):
        u = jnp.concatenate([val[:, zero - off:zero - off + wk], val[:, zero - off - Q_BLK:zero - off]], axis=1)
        flat = jnp.tile(u, (1, Q_BLK))[:, :Q_BLK * (m - 1)]
        tiles.append(flat.reshape(A_HPG, Q_BLK, m - 1)[:, :, :wk])
    return jnp.stack(tiles)


def _log_sigmoid(z):
    return jnp.minimum(z, 0.0) - jnp.log(1.0 + jnp.exp(-jnp.abs(z)))


def _gla_kernel(q_ref, k_ref, v_ref, og_ref, g_ref, wgf_ref, wgb_ref, bgf_ref, bgb_ref, gn_ref, o_ref,
                qif, kof, qib, kob, vt, etf, etb, acc, *, S):
    PB = 256
    C = GLA_CHUNK
    NB = S // PB
    rr = lax.broadcasted_iota(jnp.int32, (PB, PB), 0)
    cc = lax.broadcasted_iota(jnp.int32, (PB, PB), 1)
    same = (rr >> 6) == (cc >> 6)
    mask_f = same & (cc <= rr)
    mask_b = same & (cc > rr)
    tl = jnp.where(mask_f, 1.0, 0.0).astype(BF16)
    qscale = GLA_DK ** -0.5
    inv_tau = 1.0 / GATE_TAU
    nt = (((1,), (1,)), ((), ()))

    def split3(x):
        hi = x.astype(BF16)
        r1 = x - hi.astype(F32)
        mid = r1.astype(BF16)
        lo = (r1 - mid.astype(F32)).astype(BF16)
        return [hi, mid, lo]

    def chunk_total(b):
        b4 = b.reshape(PB // C, C, GLA_DK)
        return jnp.broadcast_to(b4[:, C - 1:C, :], b4.shape).reshape(PB, GLA_DK)

    def prep(i, carry):
        rows = pl.ds(pl.multiple_of(i * PB, PB), PB)
        g = g_ref[0, rows, :]
        q = q_ref[0, rows, :].astype(F32) * qscale
        k = k_ref[0, rows, :].astype(F32)
        v = v_ref[0, rows, :]
        lf = _log_sigmoid(jnp.dot(g, wgf_ref[...], preferred_element_type=F32) + bgf_ref[...]) * inv_tau
        lb = _log_sigmoid(jnp.dot(g, wgb_ref[...], preferred_element_type=F32) + bgb_ref[...]) * inv_tau
        cs = jnp.dot(tl, jnp.concatenate(split3(lf) + split3(lb), axis=1), preferred_element_type=F32)
        bf = cs[:, 0:128] + cs[:, 128:256] + cs[:, 256:384]
        pb = cs[:, 384:512] + cs[:, 512:640] + cs[:, 640:768]
        totf = chunk_total(bf)
        totb = chunk_total(pb)
        bs = totb - pb + lb
        qf = (q * jnp.exp(bf)).astype(BF16)
        qb = (q * jnp.exp(bs)).astype(BF16)
        sf = lax.dot_general(qf, (k * jnp.exp(-bf)).astype(BF16), nt, preferred_element_type=F32)
        sb = lax.dot_general(qb, (k * jnp.exp(-bs)).astype(BF16), nt, preferred_element_type=F32)
        p = jnp.where(mask_f, sf, jnp.where(mask_b, sb, 0.0)).astype(BF16)
        acc[rows, :] = jnp.dot(p, v, preferred_element_type=F32)
        qif[rows, :] = qf
        qib[rows, :] = qb
        kof[rows, :] = (k * jnp.exp(totf - bf)).astype(BF16)
        kob[rows, :] = (k * jnp.exp(totb - bs)).astype(BF16)
        etf[rows, :] = jnp.exp(totf)
        etb[rows, :] = jnp.exp(totb)
        vtb = v.astype(F32).T.astype(BF16)
        vt[2 * i] = vtb[:, :128]
        vt[2 * i + 1] = vtb[:, 128:]
        return carry

    lax.fori_loop(0, NB, prep, 0, unroll=4)

    zeros_half = jnp.zeros((C, GLA_DK), BF16)

    def chunk_step(qi_ref, ko_ref, et_ref, blk, j, state):
        r0 = pl.multiple_of(blk * PB + j * C, C)
        rows = pl.ds(r0, C)
        acc[rows, :] += lax.dot_general(qi_ref[rows, :], state.astype(BF16), nt, preferred_element_type=F32)
        ko = ko_ref[rows, :]
        ko_pad = jnp.concatenate([ko, zeros_half] if j % 2 == 0 else [zeros_half, ko], axis=0)
        upd = jnp.dot(vt[2 * blk + j // 2], ko_pad, preferred_element_type=F32)
        return state * et_ref[pl.ds(r0, 1), :] + upd

    def serial(t, carry):
        st_f, st_b = carry
        for j in range(PB // C):
            st_f = chunk_step(qif, kof, etf, t, j, st_f)
            st_b = chunk_step(qib, kob, etb, NB - 1 - t, PB // C - 1 - j, st_b)
        return st_f, st_b

    zero_state = jnp.zeros((GLA_DV, GLA_DK), F32)
    lax.fori_loop(0, NB, serial, (zero_state, zero_state), unroll=4)

    gn = gn_ref[...]

    def finish(i, carry):
        rows = pl.ds(pl.multiple_of(i * PB, PB), PB)
        tot = acc[rows, :]
        nrm = tot * lax.rsqrt(jnp.mean(tot * tot, axis=-1, keepdims=True) + EPS) * gn
        og = og_ref[0, rows, :].astype(F32)
        o_ref[0, rows, :] = (nrm * (og * jax.nn.sigmoid(og))).astype(BF16)
        return carry

    lax.fori_loop(0, NB, finish, 0)


def _gla(proj, wgf, wgb, bgf, bgb, gla_norm, B, S):
    pv = proj.reshape(B, S, PR)
    cq, ck, cv, cog, cg = C_QB // GLA_DK, C_KB // GLA_DK, C_VB // GLA_DV, C_OG // GLA_DV, C_G // 128
    return pl.pallas_call(
        functools.partial(_gla_kernel, S=S),
        grid=(B, GLA_H),
        in_specs=[
            pl.BlockSpec((1, S, GLA_DK), lambda b, h: (b, 0, cq + h)),
            pl.BlockSpec((1, S, GLA_DK), lambda b, h: (b, 0, ck + h)),
            pl.BlockSpec((1, S, GLA_DV), lambda b, h: (b, 0, cv + h)),
            pl.BlockSpec((1, S, GLA_DV), lambda b, h: (b, 0, cog + h)),
            pl.BlockSpec((1, S, 128), lambda b, h: (b, 0, cg)),
            pl.BlockSpec((128, GLA_DK), lambda b, h: (0, h)),
            pl.BlockSpec((128, GLA_DK), lambda b, h: (0, h)),
            pl.BlockSpec((1, GLA_DK), lambda b, h: (0, h)),
            pl.BlockSpec((1, GLA_DK), lambda b, h: (0, h)),
            pl.BlockSpec((1, GLA_DV), lambda b, h: (0, 0)),
        ],
        out_specs=pl.BlockSpec((1, S, GLA_DV), lambda b, h: (b, 0, h)),
        out_shape=jax.ShapeDtypeStruct((B, S, GLA_H * GLA_DV), BF16),
        scratch_shapes=[
            pltpu.VMEM((S, GLA_DK), BF16), pltpu.VMEM((S, GLA_DK), BF16),
            pltpu.VMEM((S, GLA_DK), BF16), pltpu.VMEM((S, GLA_DK), BF16),
            pltpu.VMEM((S // 128, GLA_DV, 128), BF16),
            pltpu.VMEM((S, GLA_DK), F32), pltpu.VMEM((S, GLA_DK), F32),
            pltpu.VMEM((S, GLA_DV), F32),
        ],
        compiler_params=_cparams(("parallel", "arbitrary")),
        name="gla",
    )(pv, pv, pv, pv, pv, wgf, wgb, bgf, bgb, gla_norm)


OUTPROJ_SPLIT = 2


def _outproj_kernel(x_ref, ma_ref, mb_ref, o0_ref, o1_ref, o2_ref, l0_ref, l1_ref, l2_ref, ob_ref,
                    wpa_ref, wpb_ref, wo_ref, nm_ref, wr_ref, br_ref,
                    x1_ref, hm_ref, gate_ref, gatet_ref):
    tm = x_ref.shape[0] // OUTPROJ_SPLIT
    for part in range(OUTPROJ_SPLIT):
        _outproj_rows(slice(part * tm, (part + 1) * tm), x_ref, ma_ref, mb_ref, o0_ref, o1_ref, o2_ref,
                      l0_ref, l1_ref, l2_ref, ob_ref, wpa_ref, wpb_ref, wo_ref, nm_ref, wr_ref, br_ref,
                      x1_ref, hm_ref, gate_ref, gatet_ref)


def _outproj_rows(rs, x_ref, ma_ref, mb_ref, o0_ref, o1_ref, o2_ref, l0_ref, l1_ref, l2_ref, ob_ref,
                  wpa_ref, wpb_ref, wo_ref, nm_ref, wr_ref, br_ref, x1_ref, hm_ref, gate_ref, gatet_ref):
    l0, l1, l2 = l0_ref[rs, :], l1_ref[rs, :], l2_ref[rs, :]
    m = jnp.maximum(jnp.maximum(l0, l1), l2)
    e0, e1, e2 = jnp.exp(l0 - m), jnp.exp(l1 - m), jnp.exp(l2 - m)
    den = e0 + e1 + e2
    oa = ((e0 / den) * o0_ref[rs, :].astype(F32) + (e1 / den) * o1_ref[rs, :].astype(F32)
          + (e2 / den) * o2_ref[rs, :].astype(F32))
    ya = jnp.dot(oa.astype(BF16), wpa_ref[...], preferred_element_type=F32)
    yb = jnp.dot(ob_ref[rs, :], wpb_ref[...], preferred_element_type=F32)
    mix = jax.nn.sigmoid(ma_ref[rs, :].astype(F32)) * ya + jax.nn.sigmoid(mb_ref[rs, :].astype(F32)) * yb
    x1 = x_ref[rs, :] + jnp.dot(mix.astype(BF16), wo_ref[...], preferred_element_type=F32)
    x1_ref[rs, :] = x1
    hm = x1 * lax.rsqrt(jnp.mean(x1 * x1, axis=-1, keepdims=True) + EPS) * nm_ref[...]
    hm_b = hm.astype(BF16)
    hm_ref[rs, :] = hm_b

    logits = jnp.dot(hm_b, wr_ref[...], preferred_element_type=F32) + br_ref[...]
    tm = logits.shape[0]
    lane = lax.broadcasted_iota(jnp.int32, (tm, 128), 1)
    is_g = (lane >= N_EXPERTS) & (lane < 2 * N_EXPERTS)
    is_e = lane < N_EXPERTS
    grp_of_lane = jnp.where(is_g, (lane - N_EXPERTS) >> 3, lane >> 3)
    gl = jnp.where(is_g, logits, NEG_INF)
    gmax = jnp.max(gl, axis=-1, keepdims=True)
    gsum = jnp.sum(jnp.where(is_g, jnp.exp(gl - gmax), 0.0), axis=-1, keepdims=True) * (1.0 / EPG)
    pg_top = 1.0 / gsum
    g_sel = jnp.min(jnp.where(is_g & (gl == gmax), grp_of_lane, N_GROUPS), axis=-1, keepdims=True)
    in_grp = is_e & (grp_of_lane == g_sel)
    el = jnp.where(in_grp, logits, NEG_INF)
    emax = jnp.max(el, axis=-1, keepdims=True)
    ee = jnp.where(in_grp, jnp.exp(el - emax), 0.0)
    pe = ee / jnp.sum(ee, axis=-1, keepdims=True)
    p1 = jnp.max(pe, axis=-1, keepdims=True)
    i1 = jnp.min(jnp.where(in_grp & (pe == p1), lane, 128), axis=-1, keepdims=True)
    rest = in_grp & (lane != i1)
    pe2 = jnp.where(rest, pe, -1.0)
    p2 = jnp.max(pe2, axis=-1, keepdims=True)
    i2 = jnp.min(jnp.where(rest & (pe2 == p2), lane, 128), axis=-1, keepdims=True)
    psum = p1 + p2
    gate = jnp.where(lane == i1, pg_top * (p1 / psum), jnp.where(lane == i2, pg_top * (p2 / psum), 0.0))
    gate_ref[rs, :] = gate
    gatet_ref[:, rs] = gate.T


def _outproj(x2d, proj, o_list, lse_list, ob2d, wpa, wpb, wo, norm_moe, wr, br, tm=512):
    T = x2d.shape[0]
    row = lambda w: pl.BlockSpec((tm, w), lambda i: (i, 0))
    full = lambda a: pl.BlockSpec(a.shape, lambda i: (0,) * a.ndim)
    return pl.pallas_call(
        _outproj_kernel,
        grid=(T // tm,),
        in_specs=[
            row(D_MODEL),
            pl.BlockSpec((tm, D_MODEL), lambda i: (i, C_MA // D_MODEL)),
            pl.BlockSpec((tm, D_MODEL), lambda i: (i, C_MB // D_MODEL)),
            row(A_GW), row(A_GW), row(A_GW), row(A_GW), row(A_GW), row(A_GW),
            row(D_MODEL),
            full(wpa), full(wpb), full(wo), full(norm_moe), full(wr), full(br),
        ],
        out_specs=[row(D_MODEL), row(D_MODEL), row(128), pl.BlockSpec((128, tm), lambda i: (0, i))],
        out_shape=[
            jax.ShapeDtypeStruct((T, D_MODEL), F32),
            jax.ShapeDtypeStruct((T, D_MODEL), BF16),
            jax.ShapeDtypeStruct((T, 128), F32),
            jax.ShapeDtypeStruct((128, T), F32),
        ],
        compiler_params=_cparams(("parallel",)),
        name="outproj",
    )(x2d, proj, proj, *o_list, *lse_list, ob2d, wpa, wpb, wo, norm_moe, wr, br)


MOE_TB = 1024
MOE_Q = 2
MOE_SB = MOE_TB // MOE_Q
MOE_ALIGN = 16
MOE_CH = 128
MOE_W = MOE_CH // MOE_Q
MOE_RQ = 2 * MOE_SB + N_EXPERTS * MOE_ALIGN
MOE_RQS = MOE_RQ + MOE_W
MOE_SUB = 256
MOE_G = 4


def _moe_route(hm_ref, gate_ref, gatet_ref, xs, ys, pt, wrow, rinfo):
    TB, Q, SB, RQ, RQS, SUB = MOE_TB, MOE_Q, MOE_SB, MOE_RQ, MOE_RQS, MOE_SUB
    big = float(4 * TB)
    gate = gate_ref[...]
    gt = gatet_ref[...]
    a = gate > 0.0
    at = gt > 0.0
    a_f = jnp.where(a, 1.0, 0.0)
    at_f = jnp.where(at, 1.0, 0.0)
    a_b = a_f.astype(BF16)
    at_b = at_f.astype(BF16)

    def pad(c):
        return jnp.floor((c + (MOE_ALIGN - 1)) * (1.0 / MOE_ALIGN)) * MOE_ALIGN

    er = lax.broadcasted_iota(jnp.int32, (128, 128), 0)
    ec = lax.broadcasted_iota(jnp.int32, (128, 128), 1)
    sub8 = lax.broadcasted_iota(jnp.int32, (8, 128), 0)
    pad_rows = jnp.zeros((8, 128), F32)
    pad_cols = jnp.zeros((128, 128), F32)
    for q in range(Q):
        blk = slice(q * SB, (q + 1) * SB)
        pad_rows = jnp.where(sub8 == q, pad(jnp.sum(a_f[blk], axis=0, keepdims=True)), pad_rows)
        pad_cols = jnp.where(ec == q, pad(jnp.sum(at_f[:, blk], axis=1, keepdims=True)), pad_cols)
    off_rows = jnp.dot(pad_rows.astype(BF16), jnp.where(er < ec, 1.0, 0.0).astype(BF16), preferred_element_type=F32)
    off_cols = jnp.dot(jnp.where(ec < er, 1.0, 0.0).astype(BF16), pad_cols.astype(BF16), preferred_element_type=F32)

    tr = lax.broadcasted_iota(jnp.int32, (SUB, SUB), 0)
    tc = lax.broadcasted_iota(jnp.int32, (SUB, SUB), 1)
    tri_l = jnp.where(tc < tr, 1.0, 0.0).astype(BF16)
    tri_u = jnp.where(tr < tc, 1.0, 0.0).astype(BF16)
    rl = lax.broadcasted_iota(jnp.int32, (SUB, RQ), 1).astype(F32)
    for q in range(Q):
        off_r = off_rows[q:q + 1]
        off_c = off_cols[:, q:q + 1]
        seen_r = jnp.zeros((1, 128), F32)
        seen_c = jnp.zeros((128, 1), F32)
        for piece in range(SB // SUB):
            rows = slice(q * SB + piece * SUB, q * SB + (piece + 1) * SUB)
            a_c = a[rows]
            pos = off_r + seen_r + jnp.dot(tri_l, a_b[rows], preferred_element_type=F32)
            plo = jnp.min(jnp.where(a_c, pos, big), axis=1, keepdims=True)
            phi = jnp.max(jnp.where(a_c, pos, -1.0), axis=1, keepdims=True)
            phi = jnp.where(phi == plo, -1.0, phi)
            pt[rows, :] = jnp.where((rl == plo) | (rl == phi), 1.0, 0.0).astype(BF16)
            seen_r = seen_r + jnp.sum(a_f[rows], axis=0, keepdims=True)
            at_c = at[:, rows]
            gt_c = gt[:, rows]
            pos_t = off_c + seen_c + jnp.dot(at_b[:, rows], tri_u, preferred_element_type=F32)
            plo = jnp.min(jnp.where(at_c, pos_t, big), axis=0, keepdims=True)
            phi = jnp.max(jnp.where(at_c, pos_t, -1.0), axis=0, keepdims=True)
            phi = jnp.where(phi == plo, -1.0, phi)
            rinfo[0:1, rows] = plo
            rinfo[1:2, rows] = phi
            rinfo[2:3, rows] = jnp.sum(jnp.where(at_c & (pos_t == plo), gt_c, 0.0), axis=0, keepdims=True)
            rinfo[3:4, rows] = jnp.sum(jnp.where(at_c & (pos_t == phi), gt_c, 0.0), axis=0, keepdims=True)
            seen_c = seen_c + jnp.sum(at_f[:, rows], axis=1, keepdims=True)

    for q in range(Q):
        blk = slice(q * SB, (q + 1) * SB)

        def gather(k, carry, q=q, blk=blk):
            r0 = pl.multiple_of(k * SUB, SUB)
            ri = (lax.broadcasted_iota(jnp.int32, (SUB, SB), 0) + r0).astype(F32)
            mlo = ri == rinfo[0:1, blk]
            mhi = ri == rinfo[1:2, blk]
            p = jnp.where(mlo | mhi, 1.0, 0.0).astype(BF16)
            dst = pl.ds(pl.multiple_of(q * RQS + r0, MOE_ALIGN), SUB)
            xs[dst, :] = jnp.dot(p, hm_ref[blk, :], preferred_element_type=F32).astype(BF16)
            w = jnp.sum(jnp.where(mlo, rinfo[2:3, blk], 0.0) + jnp.where(mhi, rinfo[3:4, blk], 0.0),
                        axis=1, keepdims=True)
            wrow[dst, :] = jnp.broadcast_to(w, (SUB, 128))
            return carry

        lax.fori_loop(0, RQ // SUB, gather, 0)
        xs[q * RQS + RQ:(q + 1) * RQS, :] = jnp.zeros((RQS - RQ, D_MODEL), BF16)
        wrow[q * RQS + RQ:(q + 1) * RQS, :] = jnp.zeros((RQS - RQ, 128), F32)
    ys[...] = jnp.zeros_like(ys)


def _moe_kernel(meta_ref, hm_ref, gate_ref, gatet_ref, x1_ref, wgu_ref, wd_ref, nf_ref, y_ref,
                xs, ys, pt, wrow, rinfo):
    i = pl.program_id(0)
    s = pl.program_id(1)
    Q, W = MOE_Q, MOE_W

    @pl.when(s == 0)
    def _():
        _moe_route(hm_ref, gate_ref, gatet_ref, xs, ys, pt, wrow, rinfo)

    offs = [[meta_ref[i, q * N_EXPERTS + s * MOE_G + g] + q * MOE_RQS for q in range(Q)] for g in range(MOE_G)]
    pads = [[meta_ref[i, (Q + q) * N_EXPERTS + s * MOE_G + g] for q in range(Q)] for g in range(MOE_G)]

    def mlp(g, x, w):
        gu = jnp.dot(x, wgu_ref[g], preferred_element_type=F32)
        gt, up = gu[:, :EXPERT_FF], gu[:, EXPERT_FF:]
        act = (gt * jax.nn.sigmoid(gt)) * up
        act = jnp.concatenate([act[:, :128] * w, act[:, 128:] * w], axis=1)
        return jnp.dot(act.astype(BF16), wd_ref[g], preferred_element_type=F32).astype(BF16)

    single = pads[0][0] <= W
    for g in range(MOE_G):
        for q in range(Q):
            single = jnp.logical_and(single, pads[g][q] <= W)

    @pl.when(single)
    def _():
        for g in range(MOE_G):
            rows = [pl.ds(pl.multiple_of(offs[g][q], MOE_ALIGN), W) for q in range(Q)]
            x = jnp.concatenate([xs[r, :] for r in rows], axis=0)
            w = jnp.concatenate([wrow[r, :] for r in rows], axis=0)
            y = mlp(g, x, w)
            for q, r in enumerate(rows):
                ys[r, :] = y[q * W:(q + 1) * W]

    @pl.when(jnp.logical_not(single))
    def _():
        for g in range(MOE_G):
            for q in range(Q):
                def body(c, carry, g=g, q=q):
                    r = pl.ds(pl.multiple_of(offs[g][q] + c * W, MOE_ALIGN), W)
                    ys[r, :] = mlp(g, xs[r, :], wrow[r, :])
                    return carry
                lax.fori_loop(0, (pads[g][q] + (W - 1)) // W, body, 0)

    @pl.when(s == N_EXPERTS // MOE_G - 1)
    def _():
        for q in range(Q):
            blk = slice(q * MOE_SB, (q + 1) * MOE_SB)
            moe = jnp.dot(pt[blk, :], ys[q * MOE_RQS:q * MOE_RQS + MOE_RQ, :], preferred_element_type=F32)
            x2 = x1_ref[blk, :] + moe
            y_ref[blk, :] = x2 * lax.rsqrt(jnp.mean(x2 * x2, axis=-1, keepdims=True) + EPS) * nf_ref[...]


def _moe(hm, gate, gatet, x1, wgu, wd, norm_final):
    T = hm.shape[0]
    tb = MOE_TB
    nb = T // tb
    used = gate.reshape(nb, MOE_Q, MOE_SB, 128)[..., :N_EXPERTS] > 0.0
    cnt = jnp.sum(used.astype(jnp.int32), axis=2)
    padded = (cnt + (MOE_ALIGN - 1)) // MOE_ALIGN * MOE_ALIGN
    off = jnp.cumsum(padded, axis=2) - padded
    meta = jnp.concatenate([off.reshape(nb, -1), padded.reshape(nb, -1)], axis=1).astype(jnp.int32)
    grid_spec = pltpu.PrefetchScalarGridSpec(
        num_scalar_prefetch=1,
        grid=(nb, N_EXPERTS // MOE_G),
        in_specs=[
            pl.BlockSpec((tb, D_MODEL), lambda i, e, m: (i, 0)),
            pl.BlockSpec((tb, 128), lambda i, e, m: (i, 0)),
            pl.BlockSpec((128, tb), lambda i, e, m: (0, i)),
            pl.BlockSpec((tb, D_MODEL), lambda i, e, m: (i, 0)),
            pl.BlockSpec((MOE_G, D_MODEL, 2 * EXPERT_FF), lambda i, e, m: (e, 0, 0)),
            pl.BlockSpec((MOE_G, EXPERT_FF, D_MODEL), lambda i, e, m: (e, 0, 0)),
            pl.BlockSpec((1, D_MODEL), lambda i, e, m: (0, 0)),
        ],
        out_specs=pl.BlockSpec((tb, D_MODEL), lambda i, e, m: (i, 0)),
        scratch_shapes=[
            pltpu.VMEM((MOE_Q * MOE_RQS, D_MODEL), BF16),
            pltpu.VMEM((MOE_Q * MOE_RQS, D_MODEL), BF16),
            pltpu.VMEM((tb, MOE_RQ), BF16),
            pltpu.VMEM((MOE_Q * MOE_RQS, 128), F32),
            pltpu.VMEM((8, tb), F32),
        ],
    )
    return pl.pallas_call(
        _moe_kernel,
        grid_spec=grid_spec,
        out_shape=jax.ShapeDtypeStruct((T, D_MODEL), F32),
        compiler_params=_cparams(("parallel", "arbitrary")),
        name="moe",
    )(meta, hm, gate, gatet, x1, wgu, wd, norm_final)


def _prep_weights(norm_mix, w_in, rel_bias, w_gate_f, b_gate_f, w_gate_b, b_gate_b, gla_norm, w_proj_a,
                  w_proj_b, w_out, norm_moe, w_rg, b_rg, w_re, b_re, w_eg, w_eu, w_ed, norm_final):
    splits = (A_QKV, A_QKV, A_QKV, 512, 512, 1024, 1024, GATE_RANK, GATE_RANK, D_MODEL, D_MODEL)
    qa, ka, va, qb, kb, vb, og, gf, gb, ma, mb = jnp.split(w_in, np.cumsum(splits)[:-1].tolist(), axis=1)
    gpad = jnp.zeros((D_MODEL, PR - C_G - 2 * GATE_RANK), w_in.dtype)
    grp = lambda t, g: t[:, g * A_GW:(g + 1) * A_GW]
    a_cols = [grp(t, g) for g in range(N_DIL) for t in (qa, ka, va)]
    w_all = jnp.concatenate(a_cols + [ma, mb, qb, kb, vb, og, gf, gb, gpad], axis=1).astype(BF16)
    kw = GLA_H * GLA_DK
    wgf = jnp.zeros((128, kw), F32).at[:GATE_RANK].set(w_gate_f).astype(BF16)
    wgb = jnp.zeros((128, kw), F32).at[GATE_RANK:2 * GATE_RANK].set(w_gate_b).astype(BF16)
    wr = jnp.zeros((D_MODEL, 128), F32)
    wr = wr.at[:, :N_EXPERTS].set(w_re).at[:, N_EXPERTS:2 * N_EXPERTS].set(jnp.repeat(w_rg, EPG, axis=1))
    br = jnp.zeros((1, 128), F32)
    br = br.at[0, :N_EXPERTS].set(b_re).at[0, N_EXPERTS:2 * N_EXPERTS].set(jnp.repeat(b_rg, EPG))
    return dict(
        norm_mix=norm_mix.reshape(1, D_MODEL), w_all=w_all, rel_bias=rel_bias,
        wgf=wgf, wgb=wgb, bgf=b_gate_f.reshape(1, kw), bgb=b_gate_b.reshape(1, kw),
        gla_norm=gla_norm.reshape(1, GLA_DV),
        wpa=w_proj_a.astype(BF16), wpb=w_proj_b.astype(BF16), wo=w_out.astype(BF16),
        norm_moe=norm_moe.reshape(1, D_MODEL), wr=wr.astype(BF16), br=br,
        wgu=jnp.concatenate([w_eg, w_eu], axis=-1).astype(BF16), wd=w_ed.astype(BF16),
        norm_final=norm_final.reshape(1, D_MODEL),
    )


def _trunk(x, w):
    B, S, _ = x.shape
    T = B * S
    x2d = x.reshape(T, D_MODEL)
    *qkv, proj = _inproj(x, w["norm_mix"], w["w_all"])
    o_list, lse_list = [], []
    for g, (_, d) in enumerate(DIL_PAIRS):
        L = S // d
        tiles = _bias_tiles(w["rel_bias"], g, d, min(2 * Q_BLK, L))
        o, lse = _attn_group(qkv[g], tiles, d, B, S)
        o_list.append(o.reshape(T, A_GW))
        lse_list.append(lse.reshape(T, A_GW))
    ob = _gla(proj, w["wgf"], w["wgb"], w["bgf"], w["bgb"], w["gla_norm"], B, S)
    x1, hm, gate, gatet = _outproj(x2d, proj, o_list, lse_list, ob.reshape(T, D_MODEL), w["wpa"], w["wpb"],
                                   w["wo"], w["norm_moe"], w["wr"], w["br"])
    y = _moe(hm, gate, gatet, x1, w["wgu"], w["wd"], w["norm_final"])
    return y.reshape(B, S, D_MODEL)


def kernel(x_prompt, x_sample, norm_mix, w_in, rel_bias, w_gate_f, b_gate_f, w_gate_b, b_gate_b, gla_norm,
           w_proj_a, w_proj_b, w_out, norm_moe, w_router_group, b_router_group, w_router_expert,
           b_router_expert, w_exp_gate, w_exp_up, w_exp_down, norm_final):
    w = _prep_weights(norm_mix[0], w_in[0], rel_bias, w_gate_f[0], b_gate_f[0], w_gate_b[0], b_gate_b[0],
                      gla_norm[0], w_proj_a[0], w_proj_b[0], w_out[0], norm_moe[0], w_router_group[0],
                      b_router_group[0], w_router_expert[0], b_router_expert[0], w_exp_gate[0], w_exp_up[0],
                      w_exp_down[0], norm_final)
    return (_trunk(x_prompt, w), _trunk(x_sample, w))
```

```python
import functools
import math

import numpy as np
import jax
import jax.numpy as jnp
from jax import lax
from jax.experimental import pallas as pl
from jax.experimental.pallas import tpu as pltpu

F32 = jnp.float32
BF16 = jnp.bfloat16

D_MODEL = 1024
EPS = 1e-6
NEG_INF = -1e30

DIL_PAIRS = ((128, 1), (512, 4), (2048, 16))
A_HPG = 4
A_DH = 64
A_GW = A_HPG * A_DH
A_QKV = 3 * A_GW
HALF_WIN = 64
Q_BLK = 128
N_REL_BUCKETS = 32
REL_MAX_DISTANCE = 1024
GLA_H = 4
GLA_DK = 128
GLA_DV = 256
GATE_RANK = 16
GATE_TAU = 16.0
GLA_CHUNK = 64
N_GROUPS = 4
EPG = 8
N_EXPERTS = 32
EXPERT_FF = 256

N_DIL = len(DIL_PAIRS)
C_MA, C_MB = 0, 1024
C_QB, C_KB, C_VB, C_OG = 2048, 2560, 3072, 4096
C_G = 5120
PR = 5376
PROJ_TN = A_QKV

VMEM_LIMIT = 58 * 1024 * 1024


def _cparams(sem):
    return pltpu.CompilerParams(dimension_semantics=sem, vmem_limit_bytes=VMEM_LIMIT)


def _inproj_kernel(x_ref, g_ref, w_ref, a0_ref, a1_ref, a2_ref, or_ref, y_scr):
    x = x_ref[...]
    r = x * lax.rsqrt(jnp.mean(x * x, axis=-1, keepdims=True) + EPS)
    h = (r * g_ref[...]).astype(BF16)
    nct = PROJ_TN // 128
    for j in range(N_DIL + PR // PROJ_TN):
        y = jnp.dot(h, w_ref[:, j * PROJ_TN:(j + 1) * PROJ_TN], preferred_element_type=F32)
        if j >= N_DIL:
            or_ref[:, (j - N_DIL) * PROJ_TN:(j - N_DIL + 1) * PROJ_TN] = y.astype(BF16)
            continue
        a_ref = (a0_ref, a1_ref, a2_ref)[j]
        d = DIL_PAIRS[j][1]
        if d == 1:
            a_ref[0, 0] = y.astype(BF16)
            continue
        for c in range(nct):
            y_scr[j - 1, c] = y[:, c * 128:(c + 1) * 128]
        n = y.shape[0] // d
        for cls in range(d):
            cols = [y_scr[j - 1, c, pl.ds(cls, n, stride=d), :] for c in range(nct)]
            a_ref[0, cls] = jnp.concatenate(cols, axis=1).astype(BF16)


def _inproj(x, norm_g, w_all, tm=512):
    B, S, _ = x.shape
    T = B * S
    spb = S // tm
    a_specs, a_shapes = [], []
    for _, d in DIL_PAIRS:
        a_specs.append(pl.BlockSpec((1, d, tm // d, PROJ_TN), lambda i: (i // spb, 0, i % spb, 0)))
        a_shapes.append(jax.ShapeDtypeStruct((B, d, S // d, PROJ_TN), BF16))
    return pl.pallas_call(
        _inproj_kernel,
        grid=(T // tm,),
        in_specs=[
            pl.BlockSpec((tm, D_MODEL), lambda i: (i, 0)),
            pl.BlockSpec((1, D_MODEL), lambda i: (0, 0)),
            pl.BlockSpec(w_all.shape, lambda i: (0, 0), pipeline_mode=pl.Buffered(1)),
        ],
        out_specs=a_specs + [pl.BlockSpec((tm, PR), lambda i: (i, 0))],
        out_shape=a_shapes + [jax.ShapeDtypeStruct((T, PR), BF16)],
        scratch_shapes=[pltpu.VMEM((N_DIL - 1, PROJ_TN // 128, tm, 128), F32)],
        compiler_params=_cparams(("parallel",)),
        name="inproj",
    )(x.reshape(T, D_MODEL), norm_g, w_all)


ATTN_ITEMS = 4


def _attn_kernel(q_ref, k_ref, v_ref, bias_ref, o_ref, lse_ref, *, L, wk, nc, nq, nblk):
    i = pl.program_id(2)
    lane = lax.broadcasted_iota(jnp.int32, (Q_BLK, A_GW), 1)
    qscale = jnp.asarray(A_DH ** -0.5, BF16)
    nt = (((1,), (1,)), ((), ()))
    for c in range(nc):
        for t in range(nq):
            blk = i * nq + t
            start = pl.multiple_of(jnp.clip(blk * Q_BLK - HALF_WIN, 0, L - wk), HALF_WIN)
            var = jnp.where(blk == 0, 0, jnp.where(blk == nblk - 1, 2, 1))
            rows = slice(t * Q_BLK, (t + 1) * Q_BLK)
            q = q_ref[0, c, rows, :] * qscale
            kw = k_ref[0, c, pl.ds(start, wk), :]
            vw = v_ref[0, c, pl.ds(start, wk), :]
            outs = []
            lse_tile = jnp.zeros((Q_BLK, A_GW), F32)
            for h in range(A_HPG):
                sl = slice(h * A_DH, (h + 1) * A_DH)
                s = lax.dot_general(q[:, sl], kw[:, sl], nt, preferred_element_type=F32) + bias_ref[var, h]
                m = jnp.max(s, axis=-1, keepdims=True)
                p = jnp.exp(s - m)
                l = jnp.sum(p, axis=-1, keepdims=True)
                outs.append(jnp.dot(p.astype(BF16), vw[:, sl], preferred_element_type=F32) / l)
                lse = m + jnp.log(l)
                lse_tile = jnp.where((lane >= A_DH * h) & (lane < A_DH * (h + 1)), lse, lse_tile)
            cols = slice(c * A_GW, (c + 1) * A_GW)
            o_ref[0, rows, cols] = jnp.concatenate(outs, axis=1).astype(BF16)
            lse_ref[0, rows, cols] = lse_tile


def _attn_group(qkv, bias_tiles, d, B, S):
    L = S // d
    nblk = L // Q_BLK
    wk = min(2 * Q_BLK, L)
    nq = min(ATTN_ITEMS, nblk)
    nc = ATTN_ITEMS // nq

    return pl.pallas_call(
        functools.partial(_attn_kernel, L=L, wk=wk, nc=nc, nq=nq, nblk=nblk),
        grid=(B, d // nc, nblk // nq),
        in_specs=[
            pl.BlockSpec((1, nc, nq * Q_BLK, A_GW), lambda b, r, i: (b, r, i, 0)),
            pl.BlockSpec((1, nc, L, A_GW), lambda b, r, i: (b, r, 0, 1)),
            pl.BlockSpec((1, nc, L, A_GW), lambda b, r, i: (b, r, 0, 2)),
            pl.BlockSpec(bias_tiles.shape, lambda b, r, i: (0, 0, 0, 0)),
        ],
        out_specs=[
            pl.BlockSpec((1, nq * Q_BLK, nc * A_GW), lambda b, r, i: (b, i, r)),
            pl.BlockSpec((1, nq * Q_BLK, nc * A_GW), lambda b, r, i: (b, i, r)),
        ],
        out_shape=[
            jax.ShapeDtypeStruct((B, L, d * A_GW), BF16),
            jax.ShapeDtypeStruct((B, L, d * A_GW), F32),
        ],
        compiler_params=_cparams(("parallel", "parallel", "arbitrary")),
        name=f"attn_d{d}",
    )(qkv, qkv, qkv, bias_tiles)


def _t5_buckets(rel):
    nb = N_REL_BUCKETS // 2
    max_exact = nb // 2
    ret = (rel > 0).astype(np.int64) * nb
    n = np.abs(rel)
    large = max_exact + (np.log(np.maximum(n, 1) / max_exact) / math.log(REL_MAX_DISTANCE / max_exact)
                         * (nb - max_exact)).astype(np.int64)
    large = np.minimum(large, nb - 1)
    return (ret + np.where(n < max_exact, n, large)).astype(np.int32)


def _bias_tiles(rel_bias, g, d, wk):
    rel = d * np.arange(-HALF_WIN, HALF_WIN + 1)
    bucket = _t5_buckets(rel)
    bias = rel_bias[jnp.asarray(bucket)][:, g * A_HPG:(g + 1) * A_HPG].T.astype(F32)
    pad = wk + Q_BLK
    neg = jnp.full((A_HPG, pad), NEG_INF, F32)
    val = jnp.concatenate([neg, bias, neg], axis=1)
    zero = pad + HALF_WIN
    m = wk + Q_BLK
    tiles = []
    for off in (0, HALF_WIN, wk - Q_BLK):
        u = jnp.concatenate([val[:, zero - off:zero - off + wk], val[:, zero - off - Q_BLK:zero - off]], axis=1)
        flat = jnp.tile(u, (1, Q_BLK))[:, :Q_BLK * (m - 1)]
        tiles.append(flat.reshape(A_HPG, Q_BLK, m - 1)[:, :, :wk])
    return jnp.stack(tiles)


def _log_sigmoid(z):
    return jnp.minimum(z, 0.0) - jnp.log(1.0 + jnp.exp(-jnp.abs(z)))


def _gla_kernel(q_ref, k_ref, v_ref, og_ref, g_ref, wgf_ref, wgb_ref, bgf_ref, bgb_ref, gn_ref, o_ref,
                qif, kof, qib, kob, vt, etf, etb, acc, *, S):
    PB = 256
    C = GLA_CHUNK
    NB = S // PB
    rr = lax.broadcasted_iota(jnp.int32, (PB, PB), 0)
    cc = lax.broadcasted_iota(jnp.int32, (PB, PB), 1)
    same = (rr >> 6) == (cc >> 6)
    mask_f = same & (cc <= rr)
    mask_b = same & (cc > rr)
    tl = jnp.where(mask_f, 1.0, 0.0).astype(BF16)
    qscale = GLA_DK ** -0.5
    inv_tau = 1.0 / GATE_TAU
    nt = (((1,), (1,)), ((), ()))

    def split3(x):
        hi = x.astype(BF16)
        r1 = x - hi.astype(F32)
        mid = r1.astype(BF16)
        lo = (r1 - mid.astype(F32)).astype(BF16)
        return [hi, mid, lo]

    def chunk_total(b):
        b4 = b.reshape(PB // C, C, GLA_DK)
        return jnp.broadcast_to(b4[:, C - 1:C, :], b4.shape).reshape(PB, GLA_DK)

    def prep(i, carry):
        rows = pl.ds(pl.multiple_of(i * PB, PB), PB)
        g = g_ref[0, rows, :]
        q = q_ref[0, rows, :].astype(F32) * qscale
        k = k_ref[0, rows, :].astype(F32)
        v = v_ref[0, rows, :]
        lf = _log_sigmoid(jnp.dot(g, wgf_ref[...], preferred_element_type=F32) + bgf_ref[...]) * inv_tau
        lb = _log_sigmoid(jnp.dot(g, wgb_ref[...], preferred_element_type=F32) + bgb_ref[...]) * inv_tau
        cs = jnp.dot(tl, jnp.concatenate(split3(lf) + split3(lb), axis=1), preferred_element_type=F32)
        bf = cs[:, 0:128] + cs[:, 128:256] + cs[:, 256:384]
        pb = cs[:, 384:512] + cs[:, 512:640] + cs[:, 640:768]
        totf = chunk_total(bf)
        totb = chunk_total(pb)
        bs = totb - pb + lb
        qf = (q * jnp.exp(bf)).astype(BF16)
        qb = (q * jnp.exp(bs)).astype(BF16)
        sf = lax.dot_general(qf, (k * jnp.exp(-bf)).astype(BF16), nt, preferred_element_type=F32)
        sb = lax.dot_general(qb, (k * jnp.exp(-bs)).astype(BF16), nt, preferred_element_type=F32)
        p = jnp.where(mask_f, sf, jnp.where(mask_b, sb, 0.0)).astype(BF16)
        acc[rows, :] = jnp.dot(p, v, preferred_element_type=F32)
        qif[rows, :] = qf
        qib[rows, :] = qb
        kof[rows, :] = (k * jnp.exp(totf - bf)).astype(BF16)
        kob[rows, :] = (k * jnp.exp(totb - bs)).astype(BF16)
        etf[rows, :] = jnp.exp(totf)
        etb[rows, :] = jnp.exp(totb)
        vtb = v.astype(F32).T.astype(BF16)
        vt[2 * i] = vtb[:, :128]
        vt[2 * i + 1] = vtb[:, 128:]
        return carry

    lax.fori_loop(0, NB, prep, 0, unroll=4)

    zeros_half = jnp.zeros((C, GLA_DK), BF16)

    def chunk_step(qi_ref, ko_ref, et_ref, blk, j, state):
        r0 = pl.multiple_of(blk * PB + j * C, C)
        rows = pl.ds(r0, C)
        acc[rows, :] += lax.dot_general(qi_ref[rows, :], state.astype(BF16), nt, preferred_element_type=F32)
        ko = ko_ref[rows, :]
        ko_pad = jnp.concatenate([ko, zeros_half] if j % 2 == 0 else [zeros_half, ko], axis=0)
        upd = jnp.dot(vt[2 * blk + j // 2], ko_pad, preferred_element_type=F32)
        return state * et_ref[pl.ds(r0, 1), :] + upd

    def serial(t, carry):
        st_f, st_b = carry
        for j in range(PB // C):
            st_f = chunk_step(qif, kof, etf, t, j, st_f)
            st_b = chunk_step(qib, kob, etb, NB - 1 - t, PB // C - 1 - j, st_b)
        return st_f, st_b

    zero_state = jnp.zeros((GLA_DV, GLA_DK), F32)
    lax.fori_loop(0, NB, serial, (zero_state, zero_state), unroll=4)

    gn = gn_ref[...]

    def finish(i, carry):
        rows = pl.ds(pl.multiple_of(i * PB, PB), PB)
        tot = acc[rows, :]
        nrm = tot * lax.rsqrt(jnp.mean(tot * tot, axis=-1, keepdims=True) + EPS) * gn
        og = og_ref[0, rows, :].astype(F32)
        o_ref[0, rows, :] = (nrm * (og * jax.nn.sigmoid(og))).astype(BF16)
        return carry

    lax.fori_loop(0, NB, finish, 0)


def _gla(proj, wgf, wgb, bgf, bgb, gla_norm, B, S):
    pv = proj.reshape(B, S, PR)
    cq, ck, cv, cog, cg = C_QB // GLA_DK, C_KB // GLA_DK, C_VB // GLA_DV, C_OG // GLA_DV, C_G // 128
    return pl.pallas_call(
        functools.partial(_gla_kernel, S=S),
        grid=(B, GLA_H),
        in_specs=[
            pl.BlockSpec((1, S, GLA_DK), lambda b, h: (b, 0, cq + h)),
            pl.BlockSpec((1, S, GLA_DK), lambda b, h: (b, 0, ck + h)),
            pl.BlockSpec((1, S, GLA_DV), lambda b, h: (b, 0, cv + h)),
            pl.BlockSpec((1, S, GLA_DV), lambda b, h: (b, 0, cog + h)),
            pl.BlockSpec((1, S, 128), lambda b, h: (b, 0, cg)),
            pl.BlockSpec((128, GLA_DK), lambda b, h: (0, h)),
            pl.BlockSpec((128, GLA_DK), lambda b, h: (0, h)),
            pl.BlockSpec((1, GLA_DK), lambda b, h: (0, h)),
            pl.BlockSpec((1, GLA_DK), lambda b, h: (0, h)),
            pl.BlockSpec((1, GLA_DV), lambda b, h: (0, 0)),
        ],
        out_specs=pl.BlockSpec((1, S, GLA_DV), lambda b, h: (b, 0, h)),
        out_shape=jax.ShapeDtypeStruct((B, S, GLA_H * GLA_DV), BF16),
        scratch_shapes=[
            pltpu.VMEM((S, GLA_DK), BF16), pltpu.VMEM((S, GLA_DK), BF16),
            pltpu.VMEM((S, GLA_DK), BF16), pltpu.VMEM((S, GLA_DK), BF16),
            pltpu.VMEM((S // 128, GLA_DV, 128), BF16),
            pltpu.VMEM((S, GLA_DK), F32), pltpu.VMEM((S, GLA_DK), F32),
            pltpu.VMEM((S, GLA_DV), F32),
        ],
        compiler_params=_cparams(("parallel", "arbitrary")),
        name="gla",
    )(pv, pv, pv, pv, pv, wgf, wgb, bgf, bgb, gla_norm)


OUTPROJ_SPLIT = 2


def _outproj_kernel(x_ref, ma_ref, mb_ref, o0_ref, o1_ref, o2_ref, l0_ref, l1_ref, l2_ref, ob_ref,
                    wpa_ref, wpb_ref, wo_ref, nm_ref, wr_ref, br_ref,
                    x1_ref, hm_ref, gate_ref, gatet_ref):
    tm = x_ref.shape[0] // OUTPROJ_SPLIT
    for part in range(OUTPROJ_SPLIT):
        _outproj_rows(slice(part * tm, (part + 1) * tm), x_ref, ma_ref, mb_ref, o0_ref, o1_ref, o2_ref,
                      l0_ref, l1_ref, l2_ref, ob_ref, wpa_ref, wpb_ref, wo_ref, nm_ref, wr_ref, br_ref,
                      x1_ref, hm_ref, gate_ref, gatet_ref)


def _outproj_rows(rs, x_ref, ma_ref, mb_ref, o0_ref, o1_ref, o2_ref, l0_ref, l1_ref, l2_ref, ob_ref,
                  wpa_ref, wpb_ref, wo_ref, nm_ref, wr_ref, br_ref, x1_ref, hm_ref, gate_ref, gatet_ref):
    l0, l1, l2 = l0_ref[rs, :], l1_ref[rs, :], l2_ref[rs, :]
    m = jnp.maximum(jnp.maximum(l0, l1), l2)
    e0, e1, e2 = jnp.exp(l0 - m), jnp.exp(l1 - m), jnp.exp(l2 - m)
    den = e0 + e1 + e2
    oa = ((e0 / den) * o0_ref[rs, :].astype(F32) + (e1 / den) * o1_ref[rs, :].astype(F32)
          + (e2 / den) * o2_ref[rs, :].astype(F32))
    ya = jnp.dot(oa.astype(BF16), wpa_ref[...], preferred_element_type=F32)
    yb = jnp.dot(ob_ref[rs, :], wpb_ref[...], preferred_element_type=F32)
    mix = jax.nn.sigmoid(ma_ref[rs, :].astype(F32)) * ya + jax.nn.sigmoid(mb_ref[rs, :].astype(F32)) * yb
    x1 = x_ref[rs, :] + jnp.dot(mix.astype(BF16), wo_ref[...], preferred_element_type=F32)
    x1_ref[rs, :] = x1
    hm = x1 * lax.rsqrt(jnp.mean(x1 * x1, axis=-1, keepdims=True) + EPS) * nm_ref[...]
    hm_b = hm.astype(BF16)
    hm_ref[rs, :] = hm_b

    logits = jnp.dot(hm_b, wr_ref[...], preferred_element_type=F32) + br_ref[...]
    tm = logits.shape[0]
    lane = lax.broadcasted_iota(jnp.int32, (tm, 128), 1)
    is_g = (lane >= N_EXPERTS) & (lane < 2 * N_EXPERTS)
    is_e = lane < N_EXPERTS
    grp_of_lane = jnp.where(is_g, (lane - N_EXPERTS) >> 3, lane >> 3)
    gl = jnp.where(is_g, logits, NEG_INF)
    gmax = jnp.max(gl, axis=-1, keepdims=True)
    gsum = jnp.sum(jnp.where(is_g, jnp.exp(gl - gmax), 0.0), axis=-1, keepdims=True) * (1.0 / EPG)
    pg_top = 1.0 / gsum
    g_sel = jnp.min(jnp.where(is_g & (gl == gmax), grp_of_lane, N_GROUPS), axis=-1, keepdims=True)
    in_grp = is_e & (grp_of_lane == g_sel)
    el = jnp.where(in_grp, logits, NEG_INF)
    emax = jnp.max(el, axis=-1, keepdims=True)
    ee = jnp.where(in_grp, jnp.exp(el - emax), 0.0)
    pe = ee / jnp.sum(ee, axis=-1, keepdims=True)
    p1 = jnp.max(pe, axis=-1, keepdims=True)
    i1 = jnp.min(jnp.where(in_grp & (pe == p1), lane, 128), axis=-1, keepdims=True)
    rest = in_grp & (lane != i1)
    pe2 = jnp.where(rest, pe, -1.0)
    p2 = jnp.max(pe2, axis=-1, keepdims=True)
    i2 = jnp.min(jnp.where(rest & (pe2 == p2), lane, 128), axis=-1, keepdims=True)
    psum = p1 + p2
    gate = jnp.where(lane == i1, pg_top * (p1 / psum), jnp.where(lane == i2, pg_top * (p2 / psum), 0.0))
    gate_ref[rs, :] = gate
    gatet_ref[:, rs] = gate.T


def _outproj(x2d, proj, o_list, lse_list, ob2d, wpa, wpb, wo, norm_moe, wr, br, tm=512):
    T = x2d.shape[0]
    row = lambda w: pl.BlockSpec((tm, w), lambda i: (i, 0))
    full = lambda a: pl.BlockSpec(a.shape, lambda i: (0,) * a.ndim)
    return pl.pallas_call(
        _outproj_kernel,
        grid=(T // tm,),
        in_specs=[
            row(D_MODEL),
            pl.BlockSpec((tm, D_MODEL), lambda i: (i, C_MA // D_MODEL)),
            pl.BlockSpec((tm, D_MODEL), lambda i: (i, C_MB // D_MODEL)),
            row(A_GW), row(A_GW), row(A_GW), row(A_GW), row(A_GW), row(A_GW),
            row(D_MODEL),
            full(wpa), full(wpb), full(wo), full(norm_moe), full(wr), full(br),
        ],
        out_specs=[row(D_MODEL), row(D_MODEL), row(128), pl.BlockSpec((128, tm), lambda i: (0, i))],
        out_shape=[
            jax.ShapeDtypeStruct((T, D_MODEL), F32),
            jax.ShapeDtypeStruct((T, D_MODEL), BF16),
            jax.ShapeDtypeStruct((T, 128), F32),
            jax.ShapeDtypeStruct((128, T), F32),
        ],
        compiler_params=_cparams(("parallel",)),
        name="outproj",
    )(x2d, proj, proj, *o_list, *lse_list, ob2d, wpa, wpb, wo, norm_moe, wr, br)


MOE_TB = 1024
MOE_Q = 4
MOE_SB = MOE_TB // MOE_Q
MOE_ALIGN = 16
MOE_CH = 128
MOE_W = MOE_CH // MOE_Q
MOE_RQ = 2 * MOE_SB + N_EXPERTS * MOE_ALIGN
MOE_RQS = MOE_RQ + MOE_W
MOE_SUB = 256
MOE_G = 4


def _moe_route(hm_ref, gate_ref, gatet_ref, xs, ys, pt, wrow, rinfo):
    TB, Q, SB, RQ, RQS, SUB = MOE_TB, MOE_Q, MOE_SB, MOE_RQ, MOE_RQS, MOE_SUB
    big = float(4 * TB)
    gate = gate_ref[...]
    gt = gatet_ref[...]
    a = gate > 0.0
    at = gt > 0.0
    a_f = jnp.where(a, 1.0, 0.0)
    at_f = jnp.where(at, 1.0, 0.0)
    a_b = a_f.astype(BF16)
    at_b = at_f.astype(BF16)

    def pad(c):
        return jnp.floor((c + (MOE_ALIGN - 1)) * (1.0 / MOE_ALIGN)) * MOE_ALIGN

    er = lax.broadcasted_iota(jnp.int32, (128, 128), 0)
    ec = lax.broadcasted_iota(jnp.int32, (128, 128), 1)
    sub8 = lax.broadcasted_iota(jnp.int32, (8, 128), 0)
    pad_rows = jnp.zeros((8, 128), F32)
    pad_cols = jnp.zeros((128, 128), F32)
    for q in range(Q):
        blk = slice(q * SB, (q + 1) * SB)
        pad_rows = jnp.where(sub8 == q, pad(jnp.sum(a_f[blk], axis=0, keepdims=True)), pad_rows)
        pad_cols = jnp.where(ec == q, pad(jnp.sum(at_f[:, blk], axis=1, keepdims=True)), pad_cols)
    off_rows = jnp.dot(pad_rows.astype(BF16), jnp.where(er < ec, 1.0, 0.0).astype(BF16), preferred_element_type=F32)
    off_cols = jnp.dot(jnp.where(ec < er, 1.0, 0.0).astype(BF16), pad_cols.astype(BF16), preferred_element_type=F32)

    tr = lax.broadcasted_iota(jnp.int32, (SUB, SUB), 0)
    tc = lax.broadcasted_iota(jnp.int32, (SUB, SUB), 1)
    tri_l = jnp.where(tc < tr, 1.0, 0.0).astype(BF16)
    tri_u = jnp.where(tr < tc, 1.0, 0.0).astype(BF16)
    rl = lax.broadcasted_iota(jnp.int32, (SUB, RQ), 1).astype(F32)
    for q in range(Q):
        off_r = off_rows[q:q + 1]
        off_c = off_cols[:, q:q + 1]
        seen_r = jnp.zeros((1, 128), F32)
        seen_c = jnp.zeros((128, 1), F32)
        for piece in range(SB // SUB):
            rows = slice(q * SB + piece * SUB, q * SB + (piece + 1) * SUB)
            a_c = a[rows]
            pos = off_r + seen_r + jnp.dot(tri_l, a_b[rows], preferred_element_type=F32)
            plo = jnp.min(jnp.where(a_c, pos, big), axis=1, keepdims=True)
            phi = jnp.max(jnp.where(a_c, pos, -1.0), axis=1, keepdims=True)
            phi = jnp.where(phi == plo, -1.0, phi)
            pt[rows, :] = jnp.where((rl == plo) | (rl == phi), 1.0, 0.0).astype(BF16)
            seen_r = seen_r + jnp.sum(a_f[rows], axis=0, keepdims=True)
            at_c = at[:, rows]
            gt_c = gt[:, rows]
            pos_t = off_c + seen_c + jnp.dot(at_b[:, rows], tri_u, preferred_element_type=F32)
            plo = jnp.min(jnp.where(at_c, pos_t, big), axis=0, keepdims=True)
            phi = jnp.max(jnp.where(at_c, pos_t, -1.0), axis=0, keepdims=True)
            phi = jnp.where(phi == plo, -1.0, phi)
            rinfo[0:1, rows] = plo
            rinfo[1:2, rows] = phi
            rinfo[2:3, rows] = jnp.sum(jnp.where(at_c & (pos_t == plo), gt_c, 0.0), axis=0, keepdims=True)
            rinfo[3:4, rows] = jnp.sum(jnp.where(at_c & (pos_t == phi), gt_c, 0.0), axis=0, keepdims=True)
            seen_c = seen_c + jnp.sum(at_f[:, rows], axis=1, keepdims=True)

    for q in range(Q):
        blk = slice(q * SB, (q + 1) * SB)

        def gather(k, carry, q=q, blk=blk):
            r0 = pl.multiple_of(k * SUB, SUB)
            ri = (lax.broadcasted_iota(jnp.int32, (SUB, SB), 0) + r0).astype(F32)
            mlo = ri == rinfo[0:1, blk]
            mhi = ri == rinfo[1:2, blk]
            p = jnp.where(mlo | mhi, 1.0, 0.0).astype(BF16)
            dst = pl.ds(pl.multiple_of(q * RQS + r0, MOE_ALIGN), SUB)
            xs[dst, :] = jnp.dot(p, hm_ref[blk, :], preferred_element_type=F32).astype(BF16)
            w = jnp.sum(jnp.where(mlo, rinfo[2:3, blk], 0.0) + jnp.where(mhi, rinfo[3:4, blk], 0.0),
                        axis=1, keepdims=True)
            wrow[dst, :] = jnp.broadcast_to(w, (SUB, 128))
            return carry

        lax.fori_loop(0, RQ // SUB, gather, 0)
        xs[q * RQS + RQ:(q + 1) * RQS, :] = jnp.zeros((RQS - RQ, D_MODEL), BF16)
        wrow[q * RQS + RQ:(q + 1) * RQS, :] = jnp.zeros((RQS - RQ, 128), F32)
    ys[...] = jnp.zeros_like(ys)


def _moe_kernel(meta_ref, hm_ref, gate_ref, gatet_ref, x1_ref, wgu_ref, wd_ref, nf_ref, y_ref,
                xs, ys, pt, wrow, rinfo):
    i = pl.program_id(0)
    s = pl.program_id(1)
    Q, W = MOE_Q, MOE_W

    @pl.when(s == 0)
    def _():
        _moe_route(hm_ref, gate_ref, gatet_ref, xs, ys, pt, wrow, rinfo)

    offs = [[meta_ref[i, q * N_EXPERTS + s * MOE_G + g] + q * MOE_RQS for q in range(Q)] for g in range(MOE_G)]
    pads = [[meta_ref[i, (Q + q) * N_EXPERTS + s * MOE_G + g] for q in range(Q)] for g in range(MOE_G)]

    def mlp(g, x, w):
        gu = jnp.dot(x, wgu_ref[g], preferred_element_type=F32)
        gt, up = gu[:, :EXPERT_FF], gu[:, EXPERT_FF:]
        act = (gt * jax.nn.sigmoid(gt)) * up
        act = jnp.concatenate([act[:, :128] * w, act[:, 128:] * w], axis=1)
        return jnp.dot(act.astype(BF16), wd_ref[g], preferred_element_type=F32).astype(BF16)

    single = pads[0][0] <= W
    for g in range(MOE_G):
        for q in range(Q):
            single = jnp.logical_and(single, pads[g][q] <= W)

    @pl.when(single)
    def _():
        for g in range(MOE_G):
            rows = [pl.ds(pl.multiple_of(offs[g][q], MOE_ALIGN), W) for q in range(Q)]
            x = jnp.concatenate([xs[r, :] for r in rows], axis=0)
            w = jnp.concatenate([wrow[r, :] for r in rows], axis=0)
            y = mlp(g, x, w)
            for q, r in enumerate(rows):
                ys[r, :] = y[q * W:(q + 1) * W]

    @pl.when(jnp.logical_not(single))
    def _():
        for g in range(MOE_G):
            for q in range(Q):
                def body(c, carry, g=g, q=q):
                    r = pl.ds(pl.multiple_of(offs[g][q] + c * W, MOE_ALIGN), W)
                    ys[r, :] = mlp(g, xs[r, :], wrow[r, :])
                    return carry
                lax.fori_loop(0, (pads[g][q] + (W - 1)) // W, body, 0)

    @pl.when(s == N_EXPERTS // MOE_G - 1)
    def _():
        for q in range(Q):
            blk = slice(q * MOE_SB, (q + 1) * MOE_SB)
            moe = jnp.dot(pt[blk, :], ys[q * MOE_RQS:q * MOE_RQS + MOE_RQ, :], preferred_element_type=F32)
            x2 = x1_ref[blk, :] + moe
            y_ref[blk, :] = x2 * lax.rsqrt(jnp.mean(x2 * x2, axis=-1, keepdims=True) + EPS) * nf_ref[...]


def _moe(hm, gate, gatet, x1, wgu, wd, norm_final):
    T = hm.shape[0]
    tb = MOE_TB
    nb = T // tb
    used = gate.reshape(nb, MOE_Q, MOE_SB, 128)[..., :N_EXPERTS] > 0.0
    cnt = jnp.sum(used.astype(jnp.int32), axis=2)
    padded = (cnt + (MOE_ALIGN - 1)) // MOE_ALIGN * MOE_ALIGN
    off = jnp.cumsum(padded, axis=2) - padded
    meta = jnp.concatenate([off.reshape(nb, -1), padded.reshape(nb, -1)], axis=1).astype(jnp.int32)
    grid_spec = pltpu.PrefetchScalarGridSpec(
        num_scalar_prefetch=1,
        grid=(nb, N_EXPERTS // MOE_G),
        in_specs=[
            pl.BlockSpec((tb, D_MODEL), lambda i, e, m: (i, 0)),
            pl.BlockSpec((tb, 128), lambda i, e, m: (i, 0)),
            pl.BlockSpec((128, tb), lambda i, e, m: (0, i)),
            pl.BlockSpec((tb, D_MODEL), lambda i, e, m: (i, 0)),
            pl.BlockSpec((MOE_G, D_MODEL, 2 * EXPERT_FF), lambda i, e, m: (e, 0, 0)),
            pl.BlockSpec((MOE_G, EXPERT_FF, D_MODEL), lambda i, e, m: (e, 0, 0)),
            pl.BlockSpec((1, D_MODEL), lambda i, e, m: (0, 0)),
        ],
        out_specs=pl.BlockSpec((tb, D_MODEL), lambda i, e, m: (i, 0)),
        scratch_shapes=[
            pltpu.VMEM((MOE_Q * MOE_RQS, D_MODEL), BF16),
            pltpu.VMEM((MOE_Q * MOE_RQS, D_MODEL), BF16),
            pltpu.VMEM((tb, MOE_RQ), BF16),
            pltpu.VMEM((MOE_Q * MOE_RQS, 128), F32),
            pltpu.VMEM((8, tb), F32),
        ],
    )
    return pl.pallas_call(
        _moe_kernel,
        grid_spec=grid_spec,
        out_shape=jax.ShapeDtypeStruct((T, D_MODEL), F32),
        compiler_params=_cparams(("parallel", "arbitrary")),
        name="moe",
    )(meta, hm, gate, gatet, x1, wgu, wd, norm_final)


def _prep_weights(norm_mix, w_in, rel_bias, w_gate_f, b_gate_f, w_gate_b, b_gate_b, gla_norm, w_proj_a,
                  w_proj_b, w_out, norm_moe, w_rg, b_rg, w_re, b_re, w_eg, w_eu, w_ed, norm_final):
    splits = (A_QKV, A_QKV, A_QKV, 512, 512, 1024, 1024, GATE_RANK, GATE_RANK, D_MODEL, D_MODEL)
    qa, ka, va, qb, kb, vb, og, gf, gb, ma, mb = jnp.split(w_in, np.cumsum(splits)[:-1].tolist(), axis=1)
    gpad = jnp.zeros((D_MODEL, PR - C_G - 2 * GATE_RANK), w_in.dtype)
    grp = lambda t, g: t[:, g * A_GW:(g + 1) * A_GW]
    a_cols = [grp(t, g) for g in range(N_DIL) for t in (qa, ka, va)]
    w_all = jnp.concatenate(a_cols + [ma, mb, qb, kb, vb, og, gf, gb, gpad], axis=1).astype(BF16)
    kw = GLA_H * GLA_DK
    wgf = jnp.zeros((128, kw), F32).at[:GATE_RANK].set(w_gate_f).astype(BF16)
    wgb = jnp.zeros((128, kw), F32).at[GATE_RANK:2 * GATE_RANK].set(w_gate_b).astype(BF16)
    wr = jnp.zeros((D_MODEL, 128), F32)
    wr = wr.at[:, :N_EXPERTS].set(w_re).at[:, N_EXPERTS:2 * N_EXPERTS].set(jnp.repeat(w_rg, EPG, axis=1))
    br = jnp.zeros((1, 128), F32)
    br = br.at[0, :N_EXPERTS].set(b_re).at[0, N_EXPERTS:2 * N_EXPERTS].set(jnp.repeat(b_rg, EPG))
    return dict(
        norm_mix=norm_mix.reshape(1, D_MODEL), w_all=w_all, rel_bias=rel_bias,
        wgf=wgf, wgb=wgb, bgf=b_gate_f.reshape(1, kw), bgb=b_gate_b.reshape(1, kw),
        gla_norm=gla_norm.reshape(1, GLA_DV),
        wpa=w_proj_a.astype(BF16), wpb=w_proj_b.astype(BF16), wo=w_out.astype(BF16),
        norm_moe=norm_moe.reshape(1, D_MODEL), wr=wr.astype(BF16), br=br,
        wgu=jnp.concatenate([w_eg, w_eu], axis=-1).astype(BF16), wd=w_ed.astype(BF16),
        norm_final=norm_final.reshape(1, D_MODEL),
    )


def _trunk(x, w):
    B, S, _ = x.shape
    T = B * S
    x2d = x.reshape(T, D_MODEL)
    *qkv, proj = _inproj(x, w["norm_mix"], w["w_all"])
    o_list, lse_list = [], []
    for g, (_, d) in enumerate(DIL_PAIRS):
        L = S // d
        tiles = _bias_tiles(w["rel_bias"], g, d, min(2 * Q_BLK, L))
        o, lse = _attn_group(qkv[g], tiles, d, B, S)
        o_list.append(o.reshape(T, A_GW))
        lse_list.append(lse.reshape(T, A_GW))
    ob = _gla(proj, w["wgf"], w["wgb"], w["bgf"], w["bgb"], w["gla_norm"], B, S)
    x1, hm, gate, gatet = _outproj(x2d, proj, o_list, lse_list, ob.reshape(T, D_MODEL), w["wpa"], w["wpb"],
                                   w["wo"], w["norm_moe"], w["wr"], w["br"])
    y = _moe(hm, gate, gatet, x1, w["wgu"], w["wd"], w["norm_final"])
    return y.reshape(B, S, D_MODEL)


def kernel(x_prompt, x_sample, norm_mix, w_in, rel_bias, w_gate_f, b_gate_f, w_gate_b, b_gate_b, gla_norm,
           w_proj_a, w_proj_b, w_out, norm_moe, w_router_group, b_router_group, w_router_expert,
           b_router_expert, w_exp_gate, w_exp_up, w_exp_down, norm_final):
    w = _prep_weights(norm_mix[0], w_in[0], rel_bias, w_gate_f[0], b_gate_f[0], w_gate_b[0], b_gate_b[0],
                      gla_norm[0], w_proj_a[0], w_proj_b[0], w_out[0], norm_moe[0], w_router_group[0],
                      b_router_group[0], w_router_expert[0], b_router_expert[0], w_exp_gate[0], w_exp_up[0],
                      w_exp_down[0], norm_final)
    return (_trunk(x_prompt, w), _trunk(x_sample, w))
```

```python
import functools
import math

import numpy as np
import jax
import jax.numpy as jnp
from jax import lax
from jax.experimental import pallas as pl
from jax.experimental.pallas import tpu as pltpu

F32 = jnp.float32
BF16 = jnp.bfloat16

D_MODEL = 1024
EPS = 1e-6
NEG_INF = -1e30

DIL_PAIRS = ((128, 1), (512, 4), (2048, 16))
A_HPG = 4
A_DH = 64
A_GW = A_HPG * A_DH
A_QKV = 3 * A_GW
HALF_WIN = 64
Q_BLK = 128
N_REL_BUCKETS = 32
REL_MAX_DISTANCE = 1024
GLA_H = 4
GLA_DK = 128
GLA_DV = 256
GATE_RANK = 16
GATE_TAU = 16.0
GLA_CHUNK = 64
N_GROUPS = 4
EPG = 8
N_EXPERTS = 32
EXPERT_FF = 256

N_DIL = len(DIL_PAIRS)
C_MA, C_MB = 0, 1024
C_QB, C_KB, C_VB, C_OG = 2048, 2560, 3072, 4096
C_G = 5120
PR = 5376
PROJ_TN = A_QKV

VMEM_LIMIT = 58 * 1024 * 1024


def _cparams(sem):
    return pltpu.CompilerParams(dimension_semantics=sem, vmem_limit_bytes=VMEM_LIMIT)


def _inproj_kernel(x_ref, g_ref, w_ref, a0_ref, a1_ref, a2_ref, or_ref, y_scr):
    x = x_ref[...]
    r = x * lax.rsqrt(jnp.mean(x * x, axis=-1, keepdims=True) + EPS)
    h = (r * g_ref[...]).astype(BF16)
    nct = PROJ_TN // 128
    for j in range(N_DIL + PR // PROJ_TN):
        y = jnp.dot(h, w_ref[:, j * PROJ_TN:(j + 1) * PROJ_TN], preferred_element_type=F32)
        if j >= N_DIL:
            or_ref[:, (j - N_DIL) * PROJ_TN:(j - N_DIL + 1) * PROJ_TN] = y.astype(BF16)
            continue
        a_ref = (a0_ref, a1_ref, a2_ref)[j]
        d = DIL_PAIRS[j][1]
        if d == 1:
            a_ref[0, 0] = y.astype(BF16)
            continue
        for c in range(nct):
            y_scr[j - 1, c] = y[:, c * 128:(c + 1) * 128]
        n = y.shape[0] // d
        for cls in range(d):
            cols = [y_scr[j - 1, c, pl.ds(cls, n, stride=d), :] for c in range(nct)]
            a_ref[0, cls] = jnp.concatenate(cols, axis=1).astype(BF16)


def _inproj(x, norm_g, w_all, tm=512):
    B, S, _ = x.shape
    T = B * S
    spb = S // tm
    a_specs, a_shapes = [], []
    for _, d in DIL_PAIRS:
        a_specs.append(pl.BlockSpec((1, d, tm // d, PROJ_TN), lambda i: (i // spb, 0, i % spb, 0)))
        a_shapes.append(jax.ShapeDtypeStruct((B, d, S // d, PROJ_TN), BF16))
    return pl.pallas_call(
        _inproj_kernel,
        grid=(T // tm,),
        in_specs=[
            pl.BlockSpec((tm, D_MODEL), lambda i: (i, 0)),
            pl.BlockSpec((1, D_MODEL), lambda i: (0, 0)),
            pl.BlockSpec(w_all.shape, lambda i: (0, 0), pipeline_mode=pl.Buffered(1)),
        ],
        out_specs=a_specs + [pl.BlockSpec((tm, PR), lambda i: (i, 0))],
        out_shape=a_shapes + [jax.ShapeDtypeStruct((T, PR), BF16)],
        scratch_shapes=[pltpu.VMEM((N_DIL - 1, PROJ_TN // 128, tm, 128), F32)],
        compiler_params=_cparams(("parallel",)),
        name="inproj",
    )(x.reshape(T, D_MODEL), norm_g, w_all)


ATTN_ITEMS = 4


def _attn_kernel(q_ref, k_ref, v_ref, bias_ref, o_ref, lse_ref, *, L, wk, nc, nq, nblk):
    i = pl.program_id(2)
    lane = lax.broadcasted_iota(jnp.int32, (Q_BLK, A_GW), 1)
    qscale = jnp.asarray(A_DH ** -0.5, BF16)
    nt = (((1,), (1,)), ((), ()))
    items = [(c, t) for c in range(nc) for t in range(nq)]
    heads = [slice(h * A_DH, (h + 1) * A_DH) for h in range(A_HPG)]
    vws, scores = [], []
    for c, t in items:
        blk = i * nq + t
        start = pl.multiple_of(jnp.clip(blk * Q_BLK - HALF_WIN, 0, L - wk), HALF_WIN)
        var = jnp.where(blk == 0, 0, jnp.where(blk == nblk - 1, 2, 1))
        q = q_ref[0, c, t * Q_BLK:(t + 1) * Q_BLK, :] * qscale
        kw = k_ref[0, c, pl.ds(start, wk), :]
        vws.append(v_ref[0, c, pl.ds(start, wk), :])
        scores.append([lax.dot_general(q[:, sl], kw[:, sl], nt, preferred_element_type=F32) + bias_ref[var, h]
                       for h, sl in enumerate(heads)])
    maxes = [[jnp.max(s, axis=-1, keepdims=True) for s in item] for item in scores]
    probs = [[jnp.exp(s - m) for s, m in zip(si, mi)] for si, mi in zip(scores, maxes)]
    sums = [[jnp.sum(p, axis=-1, keepdims=True) for p in item] for item in probs]
    for (c, t), vw, pi, mi, li in zip(items, vws, probs, maxes, sums):
        outs = [jnp.dot(p.astype(BF16), vw[:, sl], preferred_element_type=F32) / l
                for p, l, sl in zip(pi, li, heads)]
        lse_tile = jnp.zeros((Q_BLK, A_GW), F32)
        for h, (m, l) in enumerate(zip(mi, li)):
            lse_tile = jnp.where((lane >= A_DH * h) & (lane < A_DH * (h + 1)), m + jnp.log(l), lse_tile)
        rows = slice(t * Q_BLK, (t + 1) * Q_BLK)
        cols = slice(c * A_GW, (c + 1) * A_GW)
        o_ref[0, rows, cols] = jnp.concatenate(outs, axis=1).astype(BF16)
        lse_ref[0, rows, cols] = lse_tile


def _attn_group(qkv, bias_tiles, d, B, S):
    L = S // d
    nblk = L // Q_BLK
    wk = min(2 * Q_BLK, L)
    nq = min(ATTN_ITEMS, nblk)
    nc = ATTN_ITEMS // nq

    return pl.pallas_call(
        functools.partial(_attn_kernel, L=L, wk=wk, nc=nc, nq=nq, nblk=nblk),
        grid=(B, d // nc, nblk // nq),
        in_specs=[
            pl.BlockSpec((1, nc, nq * Q_BLK, A_GW), lambda b, r, i: (b, r, i, 0)),
            pl.BlockSpec((1, nc, L, A_GW), lambda b, r, i: (b, r, 0, 1)),
            pl.BlockSpec((1, nc, L, A_GW), lambda b, r, i: (b, r, 0, 2)),
            pl.BlockSpec(bias_tiles.shape, lambda b, r, i: (0, 0, 0, 0)),
        ],
        out_specs=[
            pl.BlockSpec((1, nq * Q_BLK, nc * A_GW), lambda b, r, i: (b, i, r)),
            pl.BlockSpec((1, nq * Q_BLK, nc * A_GW), lambda b, r, i: (b, i, r)),
        ],
        out_shape=[
            jax.ShapeDtypeStruct((B, L, d * A_GW), BF16),
            jax.ShapeDtypeStruct((B, L, d * A_GW), F32),
        ],
        compiler_params=_cparams(("parallel", "parallel", "arbitrary")),
        name=f"attn_d{d}",
    )(qkv, qkv, qkv, bias_tiles)


def _t5_buckets(rel):
    nb = N_REL_BUCKETS // 2
    max_exact = nb // 2
    ret = (rel > 0).astype(np.int64) * nb
    n = np.abs(rel)
    large = max_exact + (np.log(np.maximum(n, 1) / max_exact) / math.log(REL_MAX_DISTANCE / max_exact)
                         * (nb - max_exact)).astype(np.int64)
    large = np.minimum(large, nb - 1)
    return (ret + np.where(n < max_exact, n, large)).astype(np.int32)


def _bias_tiles(rel_bias, g, d, wk):
    rel = d * np.arange(-HALF_WIN, HALF_WIN + 1)
    bucket = _t5_buckets(rel)
    bias = rel_bias[jnp.asarray(bucket)][:, g * A_HPG:(g + 1) * A_HPG].T.astype(F32)
    pad = wk + Q_BLK
    neg = jnp.full((A_HPG, pad), NEG_INF, F32)
    val = jnp.concatenate([neg, bias, neg], axis=1)
    zero = pad + HALF_WIN
    m = wk + Q_BLK
    tiles = []
    for off in (0, HALF_WIN, wk - Q_BLK):
        u = jnp.concatenate([val[:, zero - off:zero - off + wk], val[:, zero - off - Q_BLK:zero - off]], axis=1)
        flat = jnp.tile(u, (1, Q_BLK))[:, :Q_BLK * (m - 1)]
        tiles.append(flat.reshape(A_HPG, Q_BLK, m - 1)[:, :, :wk])
    return jnp.stack(tiles)


def _log_sigmoid(z):
    return jnp.minimum(z, 0.0) - jnp.log(1.0 + jnp.exp(-jnp.abs(z)))


def _gla_kernel(q_ref, k_ref, v_ref, og_ref, g_ref, wgf_ref, wgb_ref, bgf_ref, bgb_ref, gn_ref, o_ref,
                qif, kof, qib, kob, vt, etf, etb, acc, *, S):
    PB = 256
    C = GLA_CHUNK
    NB = S // PB
    rr = lax.broadcasted_iota(jnp.int32, (PB, PB), 0)
    cc = lax.broadcasted_iota(jnp.int32, (PB, PB), 1)
    same = (rr >> 6) == (cc >> 6)
    mask_f = same & (cc <= rr)
    mask_b = same & (cc > rr)
    tl = jnp.where(mask_f, 1.0, 0.0).astype(BF16)
    qscale = GLA_DK ** -0.5
    inv_tau = 1.0 / GATE_TAU
    nt = (((1,), (1,)), ((), ()))

    def split3(x):
        hi = x.astype(BF16)
        r1 = x - hi.astype(F32)
        mid = r1.astype(BF16)
        lo = (r1 - mid.astype(F32)).astype(BF16)
        return [hi, mid, lo]

    def chunk_total(b):
        b4 = b.reshape(PB // C, C, GLA_DK)
        return jnp.broadcast_to(b4[:, C - 1:C, :], b4.shape).reshape(PB, GLA_DK)

    def prep(i, carry):
        rows = pl.ds(pl.multiple_of(i * PB, PB), PB)
        g = g_ref[0, rows, :]
        q = q_ref[0, rows, :].astype(F32) * qscale
        k = k_ref[0, rows, :].astype(F32)
        v = v_ref[0, rows, :]
        lf = _log_sigmoid(jnp.dot(g, wgf_ref[...], preferred_element_type=F32) + bgf_ref[...]) * inv_tau
        lb = _log_sigmoid(jnp.dot(g, wgb_ref[...], preferred_element_type=F32) + bgb_ref[...]) * inv_tau
        cs = jnp.dot(tl, jnp.concatenate(split3(lf) + split3(lb), axis=1), preferred_element_type=F32)
        bf = cs[:, 0:128] + cs[:, 128:256] + cs[:, 256:384]
        pb = cs[:, 384:512] + cs[:, 512:640] + cs[:, 640:768]
        totf = chunk_total(bf)
        totb = chunk_total(pb)
        bs = totb - pb + lb
        qf = (q * jnp.exp(bf)).astype(BF16)
        qb = (q * jnp.exp(bs)).astype(BF16)
        sf = lax.dot_general(qf, (k * jnp.exp(-bf)).astype(BF16), nt, preferred_element_type=F32)
        sb = lax.dot_general(qb, (k * jnp.exp(-bs)).astype(BF16), nt, preferred_element_type=F32)
        p = jnp.where(mask_f, sf, jnp.where(mask_b, sb, 0.0)).astype(BF16)
        acc[rows, :] = jnp.dot(p, v, preferred_element_type=F32)
        qif[rows, :] = qf
        qib[rows, :] = qb
        kof[rows, :] = (k * jnp.exp(totf - bf)).astype(BF16)
        kob[rows, :] = (k * jnp.exp(totb - bs)).astype(BF16)
        etf[rows, :] = jnp.exp(totf)
        etb[rows, :] = jnp.exp(totb)
        vtb = v.astype(F32).T.astype(BF16)
        vt[2 * i] = vtb[:, :128]
        vt[2 * i + 1] = vtb[:, 128:]
        return carry

    lax.fori_loop(0, NB, prep, 0, unroll=4)

    zeros_half = jnp.zeros((C, GLA_DK), BF16)

    def chunk_step(qi_ref, ko_ref, et_ref, blk, j, state):
        r0 = pl.multiple_of(blk * PB + j * C, C)
        rows = pl.ds(r0, C)
        acc[rows, :] += lax.dot_general(qi_ref[rows, :], state.astype(BF16), nt, preferred_element_type=F32)
        ko = ko_ref[rows, :]
        ko_pad = jnp.concatenate([ko, zeros_half] if j % 2 == 0 else [zeros_half, ko], axis=0)
        upd = jnp.dot(vt[2 * blk + j // 2], ko_pad, preferred_element_type=F32)
        return state * et_ref[pl.ds(r0, 1), :] + upd

    def serial(t, carry):
        st_f, st_b = carry
        for j in range(PB // C):
            st_f = chunk_step(qif, kof, etf, t, j, st_f)
            st_b = chunk_step(qib, kob, etb, NB - 1 - t, PB // C - 1 - j, st_b)
        return st_f, st_b

    zero_state = jnp.zeros((GLA_DV, GLA_DK), F32)
    lax.fori_loop(0, NB, serial, (zero_state, zero_state), unroll=4)

    gn = gn_ref[...]

    def finish(i, carry):
        rows = pl.ds(pl.multiple_of(i * PB, PB), PB)
        tot = acc[rows, :]
        nrm = tot * lax.rsqrt(jnp.mean(tot * tot, axis=-1, keepdims=True) + EPS) * gn
        og = og_ref[0, rows, :].astype(F32)
        o_ref[0, rows, :] = (nrm * (og * jax.nn.sigmoid(og))).astype(BF16)
        return carry

    lax.fori_loop(0, NB, finish, 0)


def _gla(proj, wgf, wgb, bgf, bgb, gla_norm, B, S):
    pv = proj.reshape(B, S, PR)
    cq, ck, cv, cog, cg = C_QB // GLA_DK, C_KB // GLA_DK, C_VB // GLA_DV, C_OG // GLA_DV, C_G // 128
    return pl.pallas_call(
        functools.partial(_gla_kernel, S=S),
        grid=(B, GLA_H),
        in_specs=[
            pl.BlockSpec((1, S, GLA_DK), lambda b, h: (b, 0, cq + h)),
            pl.BlockSpec((1, S, GLA_DK), lambda b, h: (b, 0, ck + h)),
            pl.BlockSpec((1, S, GLA_DV), lambda b, h: (b, 0, cv + h)),
            pl.BlockSpec((1, S, GLA_DV), lambda b, h: (b, 0, cog + h)),
            pl.BlockSpec((1, S, 128), lambda b, h: (b, 0, cg)),
            pl.BlockSpec((128, GLA_DK), lambda b, h: (0, h)),
            pl.BlockSpec((128, GLA_DK), lambda b, h: (0, h)),
            pl.BlockSpec((1, GLA_DK), lambda b, h: (0, h)),
            pl.BlockSpec((1, GLA_DK), lambda b, h: (0, h)),
            pl.BlockSpec((1, GLA_DV), lambda b, h: (0, 0)),
        ],
        out_specs=pl.BlockSpec((1, S, GLA_DV), lambda b, h: (b, 0, h)),
        out_shape=jax.ShapeDtypeStruct((B, S, GLA_H * GLA_DV), BF16),
        scratch_shapes=[
            pltpu.VMEM((S, GLA_DK), BF16), pltpu.VMEM((S, GLA_DK), BF16),
            pltpu.VMEM((S, GLA_DK), BF16), pltpu.VMEM((S, GLA_DK), BF16),
            pltpu.VMEM((S // 128, GLA_DV, 128), BF16),
            pltpu.VMEM((S, GLA_DK), F32), pltpu.VMEM((S, GLA_DK), F32),
            pltpu.VMEM((S, GLA_DV), F32),
        ],
        compiler_params=_cparams(("parallel", "arbitrary")),
        name="gla",
    )(pv, pv, pv, pv, pv, wgf, wgb, bgf, bgb, gla_norm)


OUTPROJ_SPLIT = 2


def _outproj_kernel(x_ref, ma_ref, mb_ref, o0_ref, o1_ref, o2_ref, l0_ref, l1_ref, l2_ref, ob_ref,
                    wpa_ref, wpb_ref, wo_ref, nm_ref, wr_ref, br_ref,
                    x1_ref, hm_ref, gate_ref, gatet_ref):
    tm = x_ref.shape[0] // OUTPROJ_SPLIT
    for part in range(OUTPROJ_SPLIT):
        _outproj_rows(slice(part * tm, (part + 1) * tm), x_ref, ma_ref, mb_ref, o0_ref, o1_ref, o2_ref,
                      l0_ref, l1_ref, l2_ref, ob_ref, wpa_ref, wpb_ref, wo_ref, nm_ref, wr_ref, br_ref,
                      x1_ref, hm_ref, gate_ref, gatet_ref)


def _outproj_rows(rs, x_ref, ma_ref, mb_ref, o0_ref, o1_ref, o2_ref, l0_ref, l1_ref, l2_ref, ob_ref,
                  wpa_ref, wpb_ref, wo_ref, nm_ref, wr_ref, br_ref, x1_ref, hm_ref, gate_ref, gatet_ref):
    l0, l1, l2 = l0_ref[rs, :], l1_ref[rs, :], l2_ref[rs, :]
    m = jnp.maximum(jnp.maximum(l0, l1), l2)
    e0, e1, e2 = jnp.exp(l0 - m), jnp.exp(l1 - m), jnp.exp(l2 - m)
    den = e0 + e1 + e2
    oa = ((e0 / den) * o0_ref[rs, :].astype(F32) + (e1 / den) * o1_ref[rs, :].astype(F32)
          + (e2 / den) * o2_ref[rs, :].astype(F32))
    ya = jnp.dot(oa.astype(BF16), wpa_ref[...], preferred_element_type=F32)
    yb = jnp.dot(ob_ref[rs, :], wpb_ref[...], preferred_element_type=F32)
    mix = jax.nn.sigmoid(ma_ref[rs, :].astype(F32)) * ya + jax.nn.sigmoid(mb_ref[rs, :].astype(F32)) * yb
    x1 = x_ref[rs, :] + jnp.dot(mix.astype(BF16), wo_ref[...], preferred_element_type=F32)
    x1_ref[rs, :] = x1
    hm = x1 * lax.rsqrt(jnp.mean(x1 * x1, axis=-1, keepdims=True) + EPS) * nm_ref[...]
    hm_b = hm.astype(BF16)
    hm_ref[rs, :] = hm_b

    logits = jnp.dot(hm_b, wr_ref[...], preferred_element_type=F32) + br_ref[...]
    tm = logits.shape[0]
    lane = lax.broadcasted_iota(jnp.int32, (tm, 128), 1)
    is_g = (lane >= N_EXPERTS) & (lane < 2 * N_EXPERTS)
    is_e = lane < N_EXPERTS
    grp_of_lane = jnp.where(is_g, (lane - N_EXPERTS) >> 3, lane >> 3)
    gl = jnp.where(is_g, logits, NEG_INF)
    gmax = jnp.max(gl, axis=-1, keepdims=True)
    gsum = jnp.sum(jnp.where(is_g, jnp.exp(gl - gmax), 0.0), axis=-1, keepdims=True) * (1.0 / EPG)
    pg_top = 1.0 / gsum
    g_sel = jnp.min(jnp.where(is_g & (gl == gmax), grp_of_lane, N_GROUPS), axis=-1, keepdims=True)
    in_grp = is_e & (grp_of_lane == g_sel)
    el = jnp.where(in_grp, logits, NEG_INF)
    emax = jnp.max(el, axis=-1, keepdims=True)
    ee = jnp.where(in_grp, jnp.exp(el - emax), 0.0)
    pe = ee / jnp.sum(ee, axis=-1, keepdims=True)
    p1 = jnp.max(pe, axis=-1, keepdims=True)
    i1 = jnp.min(jnp.where(in_grp & (pe == p1), lane, 128), axis=-1, keepdims=True)
    rest = in_grp & (lane != i1)
    pe2 = jnp.where(rest, pe, -1.0)
    p2 = jnp.max(pe2, axis=-1, keepdims=True)
    i2 = jnp.min(jnp.where(rest & (pe2 == p2), lane, 128), axis=-1, keepdims=True)
    psum = p1 + p2
    gate = jnp.where(lane == i1, pg_top * (p1 / psum), jnp.where(lane == i2, pg_top * (p2 / psum), 0.0))
    gate_ref[rs, :] = gate
    gatet_ref[:, rs] = gate.T


def _outproj(x2d, proj, o_list, lse_list, ob2d, wpa, wpb, wo, norm_moe, wr, br, tm=512):
    T = x2d.shape[0]
    row = lambda w: pl.BlockSpec((tm, w), lambda i: (i, 0))
    full = lambda a: pl.BlockSpec(a.shape, lambda i: (0,) * a.ndim)
    return pl.pallas_call(
        _outproj_kernel,
        grid=(T // tm,),
        in_specs=[
            row(D_MODEL),
            pl.BlockSpec((tm, D_MODEL), lambda i: (i, C_MA // D_MODEL)),
            pl.BlockSpec((tm, D_MODEL), lambda i: (i, C_MB // D_MODEL)),
            row(A_GW), row(A_GW), row(A_GW), row(A_GW), row(A_GW), row(A_GW),
            row(D_MODEL),
            full(wpa), full(wpb), full(wo), full(norm_moe), full(wr), full(br),
        ],
        out_specs=[row(D_MODEL), row(D_MODEL), row(128), pl.BlockSpec((128, tm), lambda i: (0, i))],
        out_shape=[
            jax.ShapeDtypeStruct((T, D_MODEL), F32),
            jax.ShapeDtypeStruct((T, D_MODEL), BF16),
            jax.ShapeDtypeStruct((T, 128), F32),
            jax.ShapeDtypeStruct((128, T), F32),
        ],
        compiler_params=_cparams(("parallel",)),
        name="outproj",
    )(x2d, proj, proj, *o_list, *lse_list, ob2d, wpa, wpb, wo, norm_moe, wr, br)


MOE_TB = 1024
MOE_Q = 4
MOE_SB = MOE_TB // MOE_Q
MOE_ALIGN = 16
MOE_CH = 128
MOE_W = MOE_CH // MOE_Q
MOE_RQ = 2 * MOE_SB + N_EXPERTS * MOE_ALIGN
MOE_RQS = MOE_RQ + MOE_W
MOE_SUB = 256
MOE_G = 4


def _moe_route(hm_ref, gate_ref, gatet_ref, xs, ys, pt, wrow, rinfo):
    TB, Q, SB, RQ, RQS, SUB = MOE_TB, MOE_Q, MOE_SB, MOE_RQ, MOE_RQS, MOE_SUB
    big = float(4 * TB)
    gate = gate_ref[...]
    gt = gatet_ref[...]
    a = gate > 0.0
    at = gt > 0.0
    a_f = jnp.where(a, 1.0, 0.0)
    at_f = jnp.where(at, 1.0, 0.0)
    a_b = a_f.astype(BF16)
    at_b = at_f.astype(BF16)

    def pad(c):
        return jnp.floor((c + (MOE_ALIGN - 1)) * (1.0 / MOE_ALIGN)) * MOE_ALIGN

    er = lax.broadcasted_iota(jnp.int32, (128, 128), 0)
    ec = lax.broadcasted_iota(jnp.int32, (128, 128), 1)
    sub8 = lax.broadcasted_iota(jnp.int32, (8, 128), 0)
    pad_rows = jnp.zeros((8, 128), F32)
    pad_cols = jnp.zeros((128, 128), F32)
    for q in range(Q):
        blk = slice(q * SB, (q + 1) * SB)
        pad_rows = jnp.where(sub8 == q, pad(jnp.sum(a_f[blk], axis=0, keepdims=True)), pad_rows)
        pad_cols = jnp.where(ec == q, pad(jnp.sum(at_f[:, blk], axis=1, keepdims=True)), pad_cols)
    off_rows = jnp.dot(pad_rows.astype(BF16), jnp.where(er < ec, 1.0, 0.0).astype(BF16), preferred_element_type=F32)
    off_cols = jnp.dot(jnp.where(ec < er, 1.0, 0.0).astype(BF16), pad_cols.astype(BF16), preferred_element_type=F32)

    tr = lax.broadcasted_iota(jnp.int32, (SUB, SUB), 0)
    tc = lax.broadcasted_iota(jnp.int32, (SUB, SUB), 1)
    tri_l = jnp.where(tc < tr, 1.0, 0.0).astype(BF16)
    tri_u = jnp.where(tr < tc, 1.0, 0.0).astype(BF16)
    rl = lax.broadcasted_iota(jnp.int32, (SUB, RQ), 1).astype(F32)
    for q in range(Q):
        off_r = off_rows[q:q + 1]
        off_c = off_cols[:, q:q + 1]
        seen_r = jnp.zeros((1, 128), F32)
        seen_c = jnp.zeros((128, 1), F32)
        for piece in range(SB // SUB):
            rows = slice(q * SB + piece * SUB, q * SB + (piece + 1) * SUB)
            a_c = a[rows]
            pos = off_r + seen_r + jnp.dot(tri_l, a_b[rows], preferred_element_type=F32)
            plo = jnp.min(jnp.where(a_c, pos, big), axis=1, keepdims=True)
            phi = jnp.max(jnp.where(a_c, pos, -1.0), axis=1, keepdims=True)
            phi = jnp.where(phi == plo, -1.0, phi)
            pt[rows, :] = jnp.where((rl == plo) | (rl == phi), 1.0, 0.0).astype(BF16)
            seen_r = seen_r + jnp.sum(a_f[rows], axis=0, keepdims=True)
            at_c = at[:, rows]
            gt_c = gt[:, rows]
            pos_t = off_c + seen_c + jnp.dot(at_b[:, rows], tri_u, preferred_element_type=F32)
            plo = jnp.min(jnp.where(at_c, pos_t, big), axis=0, keepdims=True)
            phi = jnp.max(jnp.where(at_c, pos_t, -1.0), axis=0, keepdims=True)
            phi = jnp.where(phi == plo, -1.0, phi)
            rinfo[0:1, rows] = plo
            rinfo[1:2, rows] = phi
            rinfo[2:3, rows] = jnp.sum(jnp.where(at_c & (pos_t == plo), gt_c, 0.0), axis=0, keepdims=True)
            rinfo[3:4, rows] = jnp.sum(jnp.where(at_c & (pos_t == phi), gt_c, 0.0), axis=0, keepdims=True)
            seen_c = seen_c + jnp.sum(at_f[:, rows], axis=1, keepdims=True)

    for q in range(Q):
        blk = slice(q * SB, (q + 1) * SB)

        def gather(k, carry, q=q, blk=blk):
            r0 = pl.multiple_of(k * SUB, SUB)
            ri = (lax.broadcasted_iota(jnp.int32, (SUB, SB), 0) + r0).astype(F32)
            mlo = ri == rinfo[0:1, blk]
            mhi = ri == rinfo[1:2, blk]
            p = jnp.where(mlo | mhi, 1.0, 0.0).astype(BF16)
            dst = pl.ds(pl.multiple_of(q * RQS + r0, MOE_ALIGN), SUB)
            xs[dst, :] = jnp.dot(p, hm_ref[blk, :], preferred_element_type=F32).astype(BF16)
            w = jnp.sum(jnp.where(mlo, rinfo[2:3, blk], 0.0) + jnp.where(mhi, rinfo[3:4, blk], 0.0),
                        axis=1, keepdims=True)
            wrow[dst, :] = jnp.broadcast_to(w, (SUB, 128))
            return carry

        lax.fori_loop(0, RQ // SUB, gather, 0)
        xs[q * RQS + RQ:(q + 1) * RQS, :] = jnp.zeros((RQS - RQ, D_MODEL), BF16)
        wrow[q * RQS + RQ:(q + 1) * RQS, :] = jnp.zeros((RQS - RQ, 128), F32)
    ys[...] = jnp.zeros_like(ys)


def _moe_kernel(meta_ref, hm_ref, gate_ref, gatet_ref, x1_ref, wgu_ref, wd_ref, nf_ref, y_ref,
                xs, ys, pt, wrow, rinfo):
    i = pl.program_id(0)
    s = pl.program_id(1)
    Q, W = MOE_Q, MOE_W

    @pl.when(s == 0)
    def _():
        _moe_route(hm_ref, gate_ref, gatet_ref, xs, ys, pt, wrow, rinfo)

    offs = [[meta_ref[i, q * N_EXPERTS + s * MOE_G + g] + q * MOE_RQS for q in range(Q)] for g in range(MOE_G)]
    pads = [[meta_ref[i, (Q + q) * N_EXPERTS + s * MOE_G + g] for q in range(Q)] for g in range(MOE_G)]

    def mlp(g, x, w):
        gu = jnp.dot(x, wgu_ref[g], preferred_element_type=F32)
        gt, up = gu[:, :EXPERT_FF], gu[:, EXPERT_FF:]
        act = (gt * jax.nn.sigmoid(gt)) * up
        act = jnp.concatenate([act[:, :128] * w, act[:, 128:] * w], axis=1)
        return jnp.dot(act.astype(BF16), wd_ref[g], preferred_element_type=F32).astype(BF16)

    single = pads[0][0] <= W
    for g in range(MOE_G):
        for q in range(Q):
            single = jnp.logical_and(single, pads[g][q] <= W)

    @pl.when(single)
    def _():
        for g in range(MOE_G):
            rows = [pl.ds(pl.multiple_of(offs[g][q], MOE_ALIGN), W) for q in range(Q)]
            x = jnp.concatenate([xs[r, :] for r in rows], axis=0)
            w = jnp.concatenate([wrow[r, :] for r in rows], axis=0)
            y = mlp(g, x, w)
            for q, r in enumerate(rows):
                ys[r, :] = y[q * W:(q + 1) * W]

    @pl.when(jnp.logical_not(single))
    def _():
        for g in range(MOE_G):
            for q in range(Q):
                def body(c, carry, g=g, q=q):
                    r = pl.ds(pl.multiple_of(offs[g][q] + c * W, MOE_ALIGN), W)
                    ys[r, :] = mlp(g, xs[r, :], wrow[r, :])
                    return carry
                lax.fori_loop(0, (pads[g][q] + (W - 1)) // W, body, 0)

    @pl.when(s == N_EXPERTS // MOE_G - 1)
    def _():
        for q in range(Q):
            blk = slice(q * MOE_SB, (q + 1) * MOE_SB)
            moe = jnp.dot(pt[blk, :], ys[q * MOE_RQS:q * MOE_RQS + MOE_RQ, :], preferred_element_type=F32)
            x2 = x1_ref[blk, :] + moe
            y_ref[blk, :] = x2 * lax.rsqrt(jnp.mean(x2 * x2, axis=-1, keepdims=True) + EPS) * nf_ref[...]


def _moe(hm, gate, gatet, x1, wgu, wd, norm_final):
    T = hm.shape[0]
    tb = MOE_TB
    nb = T // tb
    used = gate.reshape(nb, MOE_Q, MOE_SB, 128)[..., :N_EXPERTS] > 0.0
    cnt = jnp.sum(used.astype(jnp.int32), axis=2)
    padded = (cnt + (MOE_ALIGN - 1)) // MOE_ALIGN * MOE_ALIGN
    off = jnp.cumsum(padded, axis=2) - padded
    meta = jnp.concatenate([off.reshape(nb, -1), padded.reshape(nb, -1)], axis=1).astype(jnp.int32)
    grid_spec = pltpu.PrefetchScalarGridSpec(
        num_scalar_prefetch=1,
        grid=(nb, N_EXPERTS // MOE_G),
        in_specs=[
            pl.BlockSpec((tb, D_MODEL), lambda i, e, m: (i, 0)),
            pl.BlockSpec((tb, 128), lambda i, e, m: (i, 0)),
            pl.BlockSpec((128, tb), lambda i, e, m: (0, i)),
            pl.BlockSpec((tb, D_MODEL), lambda i, e, m: (i, 0)),
            pl.BlockSpec((MOE_G, D_MODEL, 2 * EXPERT_FF), lambda i, e, m: (e, 0, 0)),
            pl.BlockSpec((MOE_G, EXPERT_FF, D_MODEL), lambda i, e, m: (e, 0, 0)),
            pl.BlockSpec((1, D_MODEL), lambda i, e, m: (0, 0)),
        ],
        out_specs=pl.BlockSpec((tb, D_MODEL), lambda i, e, m: (i, 0)),
        scratch_shapes=[
            pltpu.VMEM((MOE_Q * MOE_RQS, D_MODEL), BF16),
            pltpu.VMEM((MOE_Q * MOE_RQS, D_MODEL), BF16),
            pltpu.VMEM((tb, MOE_RQ), BF16),
            pltpu.VMEM((MOE_Q * MOE_RQS, 128), F32),
            pltpu.VMEM((8, tb), F32),
        ],
    )
    return pl.pallas_call(
        _moe_kernel,
        grid_spec=grid_spec,
        out_shape=jax.ShapeDtypeStruct((T, D_MODEL), F32),
        compiler_params=_cparams(("parallel", "arbitrary")),
        name="moe",
    )(meta, hm, gate, gatet, x1, wgu, wd, norm_final)


def _prep_weights(norm_mix, w_in, rel_bias, w_gate_f, b_gate_f, w_gate_b, b_gate_b, gla_norm, w_proj_a,
                  w_proj_b, w_out, norm_moe, w_rg, b_rg, w_re, b_re, w_eg, w_eu, w_ed, norm_final):
    splits = (A_QKV, A_QKV, A_QKV, 512, 512, 1024, 1024, GATE_RANK, GATE_RANK, D_MODEL, D_MODEL)
    qa, ka, va, qb, kb, vb, og, gf, gb, ma, mb = jnp.split(w_in, np.cumsum(splits)[:-1].tolist(), axis=1)
    gpad = jnp.zeros((D_MODEL, PR - C_G - 2 * GATE_RANK), w_in.dtype)
    grp = lambda t, g: t[:, g * A_GW:(g + 1) * A_GW]
    a_cols = [grp(t, g) for g in range(N_DIL) for t in (qa, ka, va)]
    w_all = jnp.concatenate(a_cols + [ma, mb, qb, kb, vb, og, gf, gb, gpad], axis=1).astype(BF16)
    kw = GLA_H * GLA_DK
    wgf = jnp.zeros((128, kw), F32).at[:GATE_RANK].set(w_gate_f).astype(BF16)
    wgb = jnp.zeros((128, kw), F32).at[GATE_RANK:2 * GATE_RANK].set(w_gate_b).astype(BF16)
    wr = jnp.zeros((D_MODEL, 128), F32)
    wr = wr.at[:, :N_EXPERTS].set(w_re).at[:, N_EXPERTS:2 * N_EXPERTS].set(jnp.repeat(w_rg, EPG, axis=1))
    br = jnp.zeros((1, 128), F32)
    br = br.at[0, :N_EXPERTS].set(b_re).at[0, N_EXPERTS:2 * N_EXPERTS].set(jnp.repeat(b_rg, EPG))
    return dict(
        norm_mix=norm_mix.reshape(1, D_MODEL), w_all=w_all, rel_bias=rel_bias,
        wgf=wgf, wgb=wgb, bgf=b_gate_f.reshape(1, kw), bgb=b_gate_b.reshape(1, kw),
        gla_norm=gla_norm.reshape(1, GLA_DV),
        wpa=w_proj_a.astype(BF16), wpb=w_proj_b.astype(BF16), wo=w_out.astype(BF16),
        norm_moe=norm_moe.reshape(1, D_MODEL), wr=wr.astype(BF16), br=br,
        wgu=jnp.concatenate([w_eg, w_eu], axis=-1).astype(BF16), wd=w_ed.astype(BF16),
        norm_final=norm_final.reshape(1, D_MODEL),
    )


def _trunk(x, w):
    B, S, _ = x.shape
    T = B * S
    x2d = x.reshape(T, D_MODEL)
    *qkv, proj = _inproj(x, w["norm_mix"], w["w_all"])
    o_list, lse_list = [], []
    for g, (_, d) in enumerate(DIL_PAIRS):
        L = S // d
        tiles = _bias_tiles(w["rel_bias"], g, d, min(2 * Q_BLK, L))
        o, lse = _attn_group(qkv[g], tiles, d, B, S)
        o_list.append(o.reshape(T, A_GW))
        lse_list.append(lse.reshape(T, A_GW))
    ob = _gla(proj, w["wgf"], w["wgb"], w["bgf"], w["bgb"], w["gla_norm"], B, S)
    x1, hm, gate, gatet = _outproj(x2d, proj, o_list, lse_list, ob.reshape(T, D_MODEL), w["wpa"], w["wpb"],
                                   w["wo"], w["norm_moe"], w["wr"], w["br"])
    y = _moe(hm, gate, gatet, x1, w["wgu"], w["wd"], w["norm_final"])
    return y.reshape(B, S, D_MODEL)


def kernel(x_prompt, x_sample, norm_mix, w_in, rel_bias, w_gate_f, b_gate_f, w_gate_b, b_gate_b, gla_norm,
           w_proj_a, w_proj_b, w_out, norm_moe, w_router_group, b_router_group, w_router_expert,
           b_router_expert, w_exp_gate, w_exp_up, w_exp_down, norm_final):
    w = _prep_weights(norm_mix[0], w_in[0], rel_bias, w_gate_f[0], b_gate_f[0], w_gate_b[0], b_gate_b[0],
                      gla_norm[0], w_proj_a[0], w_proj_b[0], w_out[0], norm_moe[0], w_router_group[0],
                      b_router_group[0], w_router_expert[0], b_router_expert[0], w_exp_gate[0], w_exp_up[0],
                      w_exp_down[0], norm_final)
    return (_trunk(x_prompt, w), _trunk(x_sample, w))
```

```python
import functools
import math

import numpy as np
import jax
import jax.numpy as jnp
from jax import lax
from jax.experimental import pallas as pl
from jax.experimental.pallas import tpu as pltpu

F32 = jnp.float32
BF16 = jnp.bfloat16

D_MODEL = 1024
EPS = 1e-6
NEG_INF = -1e30

DIL_PAIRS = ((128, 1), (512, 4), (2048, 16))
A_HPG = 4
A_DH = 64
A_GW = A_HPG * A_DH
A_QKV = 3 * A_GW
HALF_WIN = 64
Q_BLK = 128
N_REL_BUCKETS = 32
REL_MAX_DISTANCE = 1024
GLA_H = 4
GLA_DK = 128
GLA_DV = 256
GATE_RANK = 16
GATE_TAU = 16.0
GLA_CHUNK = 64
N_GROUPS = 4
EPG = 8
N_EXPERTS = 32
EXPERT_FF = 256

N_DIL = len(DIL_PAIRS)
C_MA, C_MB = 0, 1024
C_QB, C_KB, C_VB, C_OG = 2048, 2560, 3072, 4096
C_G = 5120
PR = 5376
PROJ_TN = A_QKV

VMEM_LIMIT = 58 * 1024 * 1024


def _cparams(sem):
    return pltpu.CompilerParams(dimension_semantics=sem, vmem_limit_bytes=VMEM_LIMIT)


def _inproj_kernel(x_ref, g_ref, w_ref, a0_ref, a1_ref, a2_ref, or_ref, y_scr):
    x = x_ref[...]
    r = x * lax.rsqrt(jnp.mean(x * x, axis=-1, keepdims=True) + EPS)
    h = (r * g_ref[...]).astype(BF16)
    nct = PROJ_TN // 128
    for j in range(N_DIL + PR // PROJ_TN):
        y = jnp.dot(h, w_ref[:, j * PROJ_TN:(j + 1) * PROJ_TN], preferred_element_type=F32)
        if j >= N_DIL:
            or_ref[:, (j - N_DIL) * PROJ_TN:(j - N_DIL + 1) * PROJ_TN] = y.astype(BF16)
            continue
        a_ref = (a0_ref, a1_ref, a2_ref)[j]
        d = DIL_PAIRS[j][1]
        if d == 1:
            a_ref[0, 0] = y.astype(BF16)
            continue
        for c in range(nct):
            y_scr[j - 1, c] = y[:, c * 128:(c + 1) * 128]
        n = y.shape[0] // d
        for cls in range(d):
            cols = [y_scr[j - 1, c, pl.ds(cls, n, stride=d), :] for c in range(nct)]
            a_ref[0, cls] = jnp.concatenate(cols, axis=1).astype(BF16)


def _inproj(x, norm_g, w_all, tm=512):
    B, S, _ = x.shape
    T = B * S
    spb = S // tm
    a_specs, a_shapes = [], []
    for _, d in DIL_PAIRS:
        a_specs.append(pl.BlockSpec((1, d, tm // d, PROJ_TN), lambda i: (i // spb, 0, i % spb, 0)))
        a_shapes.append(jax.ShapeDtypeStruct((B, d, S // d, PROJ_TN), BF16))
    return pl.pallas_call(
        _inproj_kernel,
        grid=(T // tm,),
        in_specs=[
            pl.BlockSpec((tm, D_MODEL), lambda i: (i, 0)),
            pl.BlockSpec((1, D_MODEL), lambda i: (0, 0)),
            pl.BlockSpec(w_all.shape, lambda i: (0, 0), pipeline_mode=pl.Buffered(1)),
        ],
        out_specs=a_specs + [pl.BlockSpec((tm, PR), lambda i: (i, 0))],
        out_shape=a_shapes + [jax.ShapeDtypeStruct((T, PR), BF16)],
        scratch_shapes=[pltpu.VMEM((N_DIL - 1, PROJ_TN // 128, tm, 128), F32)],
        compiler_params=_cparams(("parallel",)),
        name="inproj",
    )(x.reshape(T, D_MODEL), norm_g, w_all)


ATTN_ITEMS = 4


def _attn_kernel(q_ref, k_ref, v_ref, bias_ref, o_ref, lse_ref, *, L, wk, nc, nq, nblk):
    i = pl.program_id(2)
    lane = lax.broadcasted_iota(jnp.int32, (Q_BLK, A_GW), 1)
    qscale = jnp.asarray(A_DH ** -0.5, BF16)
    nt = (((1,), (1,)), ((), ()))
    items = [(c, t) for c in range(nc) for t in range(nq)]
    heads = [slice(h * A_DH, (h + 1) * A_DH) for h in range(A_HPG)]
    vws, scores = [], []
    for c, t in items:
        blk = i * nq + t
        start = pl.multiple_of(jnp.clip(blk * Q_BLK - HALF_WIN, 0, L - wk), HALF_WIN)
        var = jnp.where(blk == 0, 0, jnp.where(blk == nblk - 1, 2, 1))
        q = q_ref[0, c, t * Q_BLK:(t + 1) * Q_BLK, :] * qscale
        kw = k_ref[0, c, pl.ds(start, wk), :]
        vws.append(v_ref[0, c, pl.ds(start, wk), :])
        scores.append([lax.dot_general(q[:, sl], kw[:, sl], nt, preferred_element_type=F32) + bias_ref[var, h]
                       for h, sl in enumerate(heads)])
    maxes = [[jnp.max(s, axis=-1, keepdims=True) for s in item] for item in scores]
    probs = [[jnp.exp(s - m) for s, m in zip(si, mi)] for si, mi in zip(scores, maxes)]
    sums = [[jnp.sum(p, axis=-1, keepdims=True) for p in item] for item in probs]
    for (c, t), vw, pi, mi, li in zip(items, vws, probs, maxes, sums):
        outs = [jnp.dot(p.astype(BF16), vw[:, sl], preferred_element_type=F32) / l
                for p, l, sl in zip(pi, li, heads)]
        lse_tile = jnp.zeros((Q_BLK, A_GW), F32)
        for h, (m, l) in enumerate(zip(mi, li)):
            lse_tile = jnp.where((lane >= A_DH * h) & (lane < A_DH * (h + 1)), m + jnp.log(l), lse_tile)
        rows = slice(t * Q_BLK, (t + 1) * Q_BLK)
        cols = slice(c * A_GW, (c + 1) * A_GW)
        o_ref[0, rows, cols] = jnp.concatenate(outs, axis=1).astype(BF16)
        lse_ref[0, rows, cols] = lse_tile


def _attn_group(qkv, bias_tiles, d, B, S):
    L = S // d
    nblk = L // Q_BLK
    wk = min(2 * Q_BLK, L)
    nq = min(ATTN_ITEMS, nblk)
    nc = ATTN_ITEMS // nq

    return pl.pallas_call(
        functools.partial(_attn_kernel, L=L, wk=wk, nc=nc, nq=nq, nblk=nblk),
        grid=(B, d // nc, nblk // nq),
        in_specs=[
            pl.BlockSpec((1, nc, nq * Q_BLK, A_GW), lambda b, r, i: (b, r, i, 0)),
            pl.BlockSpec((1, nc, L, A_GW), lambda b, r, i: (b, r, 0, 1)),
            pl.BlockSpec((1, nc, L, A_GW), lambda b, r, i: (b, r, 0, 2)),
            pl.BlockSpec(bias_tiles.shape, lambda b, r, i: (0, 0, 0, 0)),
        ],
        out_specs=[
            pl.BlockSpec((1, nq * Q_BLK, nc * A_GW), lambda b, r, i: (b, i, r)),
            pl.BlockSpec((1, nq * Q_BLK, nc * A_GW), lambda b, r, i: (b, i, r)),
        ],
        out_shape=[
            jax.ShapeDtypeStruct((B, L, d * A_GW), BF16),
            jax.ShapeDtypeStruct((B, L, d * A_GW), F32),
        ],
        compiler_params=_cparams(("parallel", "parallel", "arbitrary")),
        name=f"attn_d{d}",
    )(qkv, qkv, qkv, bias_tiles)


def _t5_buckets(rel):
    nb = N_REL_BUCKETS // 2
    max_exact = nb // 2
    ret = (rel > 0).astype(np.int64) * nb
    n = np.abs(rel)
    large = max_exact + (np.log(np.maximum(n, 1) / max_exact) / math.log(REL_MAX_DISTANCE / max_exact)
                         * (nb - max_exact)).astype(np.int64)
    large = np.minimum(large, nb - 1)
    return (ret + np.where(n < max_exact, n, large)).astype(np.int32)


def _bias_tiles(rel_bias, g, d, wk):
    rel = d * np.arange(-HALF_WIN, HALF_WIN + 1)
    bucket = _t5_buckets(rel)
    bias = rel_bias[jnp.asarray(bucket)][:, g * A_HPG:(g + 1) * A_HPG].T.astype(F32)
    pad = wk + Q_BLK
    neg = jnp.full((A_HPG, pad), NEG_INF, F32)
    val = jnp.concatenate([neg, bias, neg], axis=1)
    zero = pad + HALF_WIN
    m = wk + Q_BLK
    tiles = []
    for off in (0, HALF_WIN, wk - Q_BLK):
        u = jnp.concatenate([val[:, zero - off:zero - off + wk], val[:, zero - off - Q_BLK:zero - off]], axis=1)
        flat = jnp.tile(u, (1, Q_BLK))[:, :Q_BLK * (m - 1)]
        tiles.append(flat.reshape(A_HPG, Q_BLK, m - 1)[:, :, :wk])
    return jnp.stack(tiles)


def _log_sigmoid(z):
    return jnp.minimum(z, 0.0) - jnp.log(1.0 + jnp.exp(-jnp.abs(z)))


def _gla_kernel(q_ref, k_ref, v_ref, og_ref, g_ref, wgf_ref, wgb_ref, bgf_ref, bgb_ref, gn_ref, o_ref,
                qif, kof, qib, kob, vt, etf, etb, acc, *, S):
    PB = 256
    C = GLA_CHUNK
    NB = S // PB
    rr = lax.broadcasted_iota(jnp.int32, (PB, PB), 0)
    cc = lax.broadcasted_iota(jnp.int32, (PB, PB), 1)
    same = (rr >> 6) == (cc >> 6)
    mask_f = same & (cc <= rr)
    mask_b = same & (cc > rr)
    tl = jnp.where(mask_f, 1.0, 0.0).astype(BF16)
    qscale = GLA_DK ** -0.5
    inv_tau = 1.0 / GATE_TAU
    nt = (((1,), (1,)), ((), ()))

    def split3(x):
        hi = x.astype(BF16)
        r1 = x - hi.astype(F32)
        mid = r1.astype(BF16)
        lo = (r1 - mid.astype(F32)).astype(BF16)
        return [hi, mid, lo]

    def chunk_total(b):
        b4 = b.reshape(PB // C, C, GLA_DK)
        return jnp.broadcast_to(b4[:, C - 1:C, :], b4.shape).reshape(PB, GLA_DK)

    PREP_BLOCKS = 4

    def prep(t, carry):
        blocks = [t * PREP_BLOCKS + u for u in range(PREP_BLOCKS)]
        rows = [pl.ds(pl.multiple_of(i * PB, PB), PB) for i in blocks]
        gs = [g_ref[0, r, :] for r in rows]
        lfs = [_log_sigmoid(jnp.dot(g, wgf_ref[...], preferred_element_type=F32) + bgf_ref[...]) * inv_tau for g in gs]
        lbs = [_log_sigmoid(jnp.dot(g, wgb_ref[...], preferred_element_type=F32) + bgb_ref[...]) * inv_tau for g in gs]
        css = [jnp.dot(tl, jnp.concatenate(split3(lf) + split3(lb), axis=1), preferred_element_type=F32)
               for lf, lb in zip(lfs, lbs)]
        bfs = [cs[:, 0:128] + cs[:, 128:256] + cs[:, 256:384] for cs in css]
        pbs = [cs[:, 384:512] + cs[:, 512:640] + cs[:, 640:768] for cs in css]
        totfs = [chunk_total(bf) for bf in bfs]
        totbs = [chunk_total(pb) for pb in pbs]
        bss = [totb - pb + lb for totb, pb, lb in zip(totbs, pbs, lbs)]
        qs = [q_ref[0, r, :].astype(F32) * qscale for r in rows]
        ks = [k_ref[0, r, :].astype(F32) for r in rows]
        qfs = [(q * jnp.exp(bf)).astype(BF16) for q, bf in zip(qs, bfs)]
        qbs = [(q * jnp.exp(bs)).astype(BF16) for q, bs in zip(qs, bss)]
        kfs = [(k * jnp.exp(-bf)).astype(BF16) for k, bf in zip(ks, bfs)]
        kbs = [(k * jnp.exp(-bs)).astype(BF16) for k, bs in zip(ks, bss)]
        sfs = [lax.dot_general(qf, kf, nt, preferred_element_type=F32) for qf, kf in zip(qfs, kfs)]
        sbs = [lax.dot_general(qb, kb, nt, preferred_element_type=F32) for qb, kb in zip(qbs, kbs)]
        ps = [jnp.where(mask_f, sf, jnp.where(mask_b, sb, 0.0)).astype(BF16) for sf, sb in zip(sfs, sbs)]
        vs = [v_ref[0, r, :] for r in rows]
        for r, p, v in zip(rows, ps, vs):
            acc[r, :] = jnp.dot(p, v, preferred_element_type=F32)
        for r, qf, qb in zip(rows, qfs, qbs):
            qif[r, :] = qf
            qib[r, :] = qb
        for r, k, totf, bf, totb, bs in zip(rows, ks, totfs, bfs, totbs, bss):
            kof[r, :] = (k * jnp.exp(totf - bf)).astype(BF16)
            kob[r, :] = (k * jnp.exp(totb - bs)).astype(BF16)
            etf[r, :] = jnp.exp(totf)
            etb[r, :] = jnp.exp(totb)
        for i, v in zip(blocks, vs):
            vtb = v.astype(F32).T.astype(BF16)
            vt[2 * i] = vtb[:, :128]
            vt[2 * i + 1] = vtb[:, 128:]
        return carry

    lax.fori_loop(0, NB // PREP_BLOCKS, prep, 0)

    zeros_half = jnp.zeros((C, GLA_DK), BF16)

    def chunk_step(qi_ref, ko_ref, et_ref, blk, j, state):
        r0 = pl.multiple_of(blk * PB + j * C, C)
        rows = pl.ds(r0, C)
        acc[rows, :] += lax.dot_general(qi_ref[rows, :], state.astype(BF16), nt, preferred_element_type=F32)
        ko = ko_ref[rows, :]
        ko_pad = jnp.concatenate([ko, zeros_half] if j % 2 == 0 else [zeros_half, ko], axis=0)
        upd = jnp.dot(vt[2 * blk + j // 2], ko_pad, preferred_element_type=F32)
        return state * et_ref[pl.ds(r0, 1), :] + upd

    def serial(t, carry):
        st_f, st_b = carry
        for j in range(PB // C):
            st_f = chunk_step(qif, kof, etf, t, j, st_f)
            st_b = chunk_step(qib, kob, etb, NB - 1 - t, PB // C - 1 - j, st_b)
        return st_f, st_b

    zero_state = jnp.zeros((GLA_DV, GLA_DK), F32)
    lax.fori_loop(0, NB, serial, (zero_state, zero_state), unroll=4)

    gn = gn_ref[...]

    def finish(i, carry):
        rows = pl.ds(pl.multiple_of(i * PB, PB), PB)
        tot = acc[rows, :]
        nrm = tot * lax.rsqrt(jnp.mean(tot * tot, axis=-1, keepdims=True) + EPS) * gn
        og = og_ref[0, rows, :].astype(F32)
        o_ref[0, rows, :] = (nrm * (og * jax.nn.sigmoid(og))).astype(BF16)
        return carry

    lax.fori_loop(0, NB, finish, 0)


def _gla(proj, wgf, wgb, bgf, bgb, gla_norm, B, S):
    pv = proj.reshape(B, S, PR)
    cq, ck, cv, cog, cg = C_QB // GLA_DK, C_KB // GLA_DK, C_VB // GLA_DV, C_OG // GLA_DV, C_G // 128
    return pl.pallas_call(
        functools.partial(_gla_kernel, S=S),
        grid=(B, GLA_H),
        in_specs=[
            pl.BlockSpec((1, S, GLA_DK), lambda b, h: (b, 0, cq + h)),
            pl.BlockSpec((1, S, GLA_DK), lambda b, h: (b, 0, ck + h)),
            pl.BlockSpec((1, S, GLA_DV), lambda b, h: (b, 0, cv + h)),
            pl.BlockSpec((1, S, GLA_DV), lambda b, h: (b, 0, cog + h)),
            pl.BlockSpec((1, S, 128), lambda b, h: (b, 0, cg)),
            pl.BlockSpec((128, GLA_DK), lambda b, h: (0, h)),
            pl.BlockSpec((128, GLA_DK), lambda b, h: (0, h)),
            pl.BlockSpec((1, GLA_DK), lambda b, h: (0, h)),
            pl.BlockSpec((1, GLA_DK), lambda b, h: (0, h)),
            pl.BlockSpec((1, GLA_DV), lambda b, h: (0, 0)),
        ],
        out_specs=pl.BlockSpec((1, S, GLA_DV), lambda b, h: (b, 0, h)),
        out_shape=jax.ShapeDtypeStruct((B, S, GLA_H * GLA_DV), BF16),
        scratch_shapes=[
            pltpu.VMEM((S, GLA_DK), BF16), pltpu.VMEM((S, GLA_DK), BF16),
            pltpu.VMEM((S, GLA_DK), BF16), pltpu.VMEM((S, GLA_DK), BF16),
            pltpu.VMEM((S // 128, GLA_DV, 128), BF16),
            pltpu.VMEM((S, GLA_DK), F32), pltpu.VMEM((S, GLA_DK), F32),
            pltpu.VMEM((S, GLA_DV), F32),
        ],
        compiler_params=_cparams(("parallel", "arbitrary")),
        name="gla",
    )(pv, pv, pv, pv, pv, wgf, wgb, bgf, bgb, gla_norm)


OUTPROJ_SPLIT = 2


def _outproj_kernel(x_ref, ma_ref, mb_ref, o0_ref, o1_ref, o2_ref, l0_ref, l1_ref, l2_ref, ob_ref,
                    wpa_ref, wpb_ref, wo_ref, nm_ref, wr_ref, br_ref,
                    x1_ref, hm_ref, gate_ref, gatet_ref):
    tm = x_ref.shape[0] // OUTPROJ_SPLIT
    parts = [slice(p * tm, (p + 1) * tm) for p in range(OUTPROJ_SPLIT)]

    def merge_groups(rs):
        l0, l1, l2 = l0_ref[rs, :], l1_ref[rs, :], l2_ref[rs, :]
        m = jnp.maximum(jnp.maximum(l0, l1), l2)
        e0, e1, e2 = jnp.exp(l0 - m), jnp.exp(l1 - m), jnp.exp(l2 - m)
        den = e0 + e1 + e2
        oa = ((e0 / den) * o0_ref[rs, :].astype(F32) + (e1 / den) * o1_ref[rs, :].astype(F32)
              + (e2 / den) * o2_ref[rs, :].astype(F32))
        return oa.astype(BF16)

    def project(rs, oa):
        ya = jnp.dot(oa, wpa_ref[...], preferred_element_type=F32)
        yb = jnp.dot(ob_ref[rs, :], wpb_ref[...], preferred_element_type=F32)
        return ya, yb

    def gate_mix(rs, y):
        ya, yb = y
        mix = jax.nn.sigmoid(ma_ref[rs, :].astype(F32)) * ya + jax.nn.sigmoid(mb_ref[rs, :].astype(F32)) * yb
        return mix.astype(BF16)

    def residual(rs, mix):
        x1 = x_ref[rs, :] + jnp.dot(mix, wo_ref[...], preferred_element_type=F32)
        x1_ref[rs, :] = x1
        return x1

    def moe_norm(rs, x1):
        hm = (x1 * lax.rsqrt(jnp.mean(x1 * x1, axis=-1, keepdims=True) + EPS) * nm_ref[...]).astype(BF16)
        hm_ref[rs, :] = hm
        return hm

    def router_logits(rs, hm):
        return jnp.dot(hm, wr_ref[...], preferred_element_type=F32) + br_ref[...]

    def router(rs, logits):
        gate = _route(logits)
        gate_ref[rs, :] = gate
        gatet_ref[:, rs] = gate.T

    stages = [lambda rs, _: merge_groups(rs), project, gate_mix, residual, moe_norm, router_logits, router]
    vals = [None] * OUTPROJ_SPLIT
    for step in range(len(stages) + OUTPROJ_SPLIT - 1):
        for p, rs in enumerate(parts):
            if 0 <= step - p < len(stages):
                vals[p] = stages[step - p](rs, vals[p])


def _route(logits):
    tm = logits.shape[0]
    lane = lax.broadcasted_iota(jnp.int32, (tm, 128), 1)
    is_g = (lane >= N_EXPERTS) & (lane < 2 * N_EXPERTS)
    is_e = lane < N_EXPERTS
    grp_of_lane = jnp.where(is_g, (lane - N_EXPERTS) >> 3, lane >> 3)
    gl = jnp.where(is_g, logits, NEG_INF)
    gmax = jnp.max(gl, axis=-1, keepdims=True)
    gsum = jnp.sum(jnp.where(is_g, jnp.exp(gl - gmax), 0.0), axis=-1, keepdims=True) * (1.0 / EPG)
    pg_top = 1.0 / gsum
    g_sel = jnp.min(jnp.where(is_g & (gl == gmax), grp_of_lane, N_GROUPS), axis=-1, keepdims=True)
    in_grp = is_e & (grp_of_lane == g_sel)
    el = jnp.where(in_grp, logits, NEG_INF)
    emax = jnp.max(el, axis=-1, keepdims=True)
    ee = jnp.where(in_grp, jnp.exp(el - emax), 0.0)
    pe = ee / jnp.sum(ee, axis=-1, keepdims=True)
    p1 = jnp.max(pe, axis=-1, keepdims=True)
    i1 = jnp.min(jnp.where(in_grp & (pe == p1), lane, 128), axis=-1, keepdims=True)
    rest = in_grp & (lane != i1)
    pe2 = jnp.where(rest, pe, -1.0)
    p2 = jnp.max(pe2, axis=-1, keepdims=True)
    i2 = jnp.min(jnp.where(rest & (pe2 == p2), lane, 128), axis=-1, keepdims=True)
    psum = p1 + p2
    return jnp.where(lane == i1, pg_top * (p1 / psum), jnp.where(lane == i2, pg_top * (p2 / psum), 0.0))


def _outproj(x2d, proj, o_list, lse_list, ob2d, wpa, wpb, wo, norm_moe, wr, br, tm=512):
    T = x2d.shape[0]
    row = lambda w: pl.BlockSpec((tm, w), lambda i: (i, 0))
    full = lambda a: pl.BlockSpec(a.shape, lambda i: (0,) * a.ndim)
    return pl.pallas_call(
        _outproj_kernel,
        grid=(T // tm,),
        in_specs=[
            row(D_MODEL),
            pl.BlockSpec((tm, D_MODEL), lambda i: (i, C_MA // D_MODEL)),
            pl.BlockSpec((tm, D_MODEL), lambda i: (i, C_MB // D_MODEL)),
            row(A_GW), row(A_GW), row(A_GW), row(A_GW), row(A_GW), row(A_GW),
            row(D_MODEL),
            full(wpa), full(wpb), full(wo), full(norm_moe), full(wr), full(br),
        ],
        out_specs=[row(D_MODEL), row(D_MODEL), row(128), pl.BlockSpec((128, tm), lambda i: (0, i))],
        out_shape=[
            jax.ShapeDtypeStruct((T, D_MODEL), F32),
            jax.ShapeDtypeStruct((T, D_MODEL), BF16),
            jax.ShapeDtypeStruct((T, 128), F32),
            jax.ShapeDtypeStruct((128, T), F32),
        ],
        compiler_params=_cparams(("parallel",)),
        name="outproj",
    )(x2d, proj, proj, *o_list, *lse_list, ob2d, wpa, wpb, wo, norm_moe, wr, br)


MOE_TB = 1024
MOE_Q = 4
MOE_SB = MOE_TB // MOE_Q
MOE_ALIGN = 16
MOE_CH = 128
MOE_W = MOE_CH // MOE_Q
MOE_RQ = 2 * MOE_SB + N_EXPERTS * MOE_ALIGN
MOE_RQS = MOE_RQ + MOE_W
MOE_SUB = 256
MOE_G = 4


def _moe_route(hm_ref, gate_ref, gatet_ref, xs, ys, pt, wrow, rinfo):
    TB, Q, SB, RQ, RQS, SUB = MOE_TB, MOE_Q, MOE_SB, MOE_RQ, MOE_RQS, MOE_SUB
    big = float(4 * TB)
    gate = gate_ref[...]
    gt = gatet_ref[...]
    a = gate > 0.0
    at = gt > 0.0
    a_f = jnp.where(a, 1.0, 0.0)
    at_f = jnp.where(at, 1.0, 0.0)
    a_b = a_f.astype(BF16)
    at_b = at_f.astype(BF16)

    def pad(c):
        return jnp.floor((c + (MOE_ALIGN - 1)) * (1.0 / MOE_ALIGN)) * MOE_ALIGN

    er = lax.broadcasted_iota(jnp.int32, (128, 128), 0)
    ec = lax.broadcasted_iota(jnp.int32, (128, 128), 1)
    sub8 = lax.broadcasted_iota(jnp.int32, (8, 128), 0)
    pad_rows = jnp.zeros((8, 128), F32)
    pad_cols = jnp.zeros((128, 128), F32)
    for q in range(Q):
        blk = slice(q * SB, (q + 1) * SB)
        pad_rows = jnp.where(sub8 == q, pad(jnp.sum(a_f[blk], axis=0, keepdims=True)), pad_rows)
        pad_cols = jnp.where(ec == q, pad(jnp.sum(at_f[:, blk], axis=1, keepdims=True)), pad_cols)
    off_rows = jnp.dot(pad_rows.astype(BF16), jnp.where(er < ec, 1.0, 0.0).astype(BF16), preferred_element_type=F32)
    off_cols = jnp.dot(jnp.where(ec < er, 1.0, 0.0).astype(BF16), pad_cols.astype(BF16), preferred_element_type=F32)

    tr = lax.broadcasted_iota(jnp.int32, (SUB, SUB), 0)
    tc = lax.broadcasted_iota(jnp.int32, (SUB, SUB), 1)
    tri_l = jnp.where(tc < tr, 1.0, 0.0).astype(BF16)
    tri_u = jnp.where(tr < tc, 1.0, 0.0).astype(BF16)
    rl = lax.broadcasted_iota(jnp.int32, (SUB, RQ), 1).astype(F32)
    for q in range(Q):
        off_r = off_rows[q:q + 1]
        off_c = off_cols[:, q:q + 1]
        seen_r = jnp.zeros((1, 128), F32)
        seen_c = jnp.zeros((128, 1), F32)
        for piece in range(SB // SUB):
            rows = slice(q * SB + piece * SUB, q * SB + (piece + 1) * SUB)
            a_c = a[rows]
            pos = off_r + seen_r + jnp.dot(tri_l, a_b[rows], preferred_element_type=F32)
            plo = jnp.min(jnp.where(a_c, pos, big), axis=1, keepdims=True)
            phi = jnp.max(jnp.where(a_c, pos, -1.0), axis=1, keepdims=True)
            phi = jnp.where(phi == plo, -1.0, phi)
            pt[rows, :] = jnp.where((rl == plo) | (rl == phi), 1.0, 0.0).astype(BF16)
            seen_r = seen_r + jnp.sum(a_f[rows], axis=0, keepdims=True)
            at_c = at[:, rows]
            gt_c = gt[:, rows]
            pos_t = off_c + seen_c + jnp.dot(at_b[:, rows], tri_u, preferred_element_type=F32)
            plo = jnp.min(jnp.where(at_c, pos_t, big), axis=0, keepdims=True)
            phi = jnp.max(jnp.where(at_c, pos_t, -1.0), axis=0, keepdims=True)
            phi = jnp.where(phi == plo, -1.0, phi)
            rinfo[0:1, rows] = plo
            rinfo[1:2, rows] = phi
            rinfo[2:3, rows] = jnp.sum(jnp.where(at_c & (pos_t == plo), gt_c, 0.0), axis=0, keepdims=True)
            rinfo[3:4, rows] = jnp.sum(jnp.where(at_c & (pos_t == phi), gt_c, 0.0), axis=0, keepdims=True)
            seen_c = seen_c + jnp.sum(at_f[:, rows], axis=1, keepdims=True)

    for q in range(Q):
        blk = slice(q * SB, (q + 1) * SB)

        def gather(k, carry, q=q, blk=blk):
            r0 = pl.multiple_of(k * SUB, SUB)
            ri = (lax.broadcasted_iota(jnp.int32, (SUB, SB), 0) + r0).astype(F32)
            mlo = ri == rinfo[0:1, blk]
            mhi = ri == rinfo[1:2, blk]
            p = jnp.where(mlo | mhi, 1.0, 0.0).astype(BF16)
            dst = pl.ds(pl.multiple_of(q * RQS + r0, MOE_ALIGN), SUB)
            xs[dst, :] = jnp.dot(p, hm_ref[blk, :], preferred_element_type=F32).astype(BF16)
            w = jnp.sum(jnp.where(mlo, rinfo[2:3, blk], 0.0) + jnp.where(mhi, rinfo[3:4, blk], 0.0),
                        axis=1, keepdims=True)
            wrow[dst, :] = jnp.broadcast_to(w, (SUB, 128))
            return carry

        lax.fori_loop(0, RQ // SUB, gather, 0)
        xs[q * RQS + RQ:(q + 1) * RQS, :] = jnp.zeros((RQS - RQ, D_MODEL), BF16)
        wrow[q * RQS + RQ:(q + 1) * RQS, :] = jnp.zeros((RQS - RQ, 128), F32)
    ys[...] = jnp.zeros_like(ys)


def _moe_kernel(meta_ref, hm_ref, gate_ref, gatet_ref, x1_ref, wgu_ref, wd_ref, nf_ref, y_ref,
                xs, ys, pt, wrow, rinfo):
    i = pl.program_id(0)
    s = pl.program_id(1)
    Q, W = MOE_Q, MOE_W

    @pl.when(s == 0)
    def _():
        _moe_route(hm_ref, gate_ref, gatet_ref, xs, ys, pt, wrow, rinfo)

    offs = [[meta_ref[i, q * N_EXPERTS + s * MOE_G + g] + q * MOE_RQS for q in range(Q)] for g in range(MOE_G)]
    pads = [[meta_ref[i, (Q + q) * N_EXPERTS + s * MOE_G + g] for q in range(Q)] for g in range(MOE_G)]

    def up_proj(g, x):
        return jnp.dot(x, wgu_ref[g], preferred_element_type=F32)

    def activate(gu, w):
        gt, up = gu[:, :EXPERT_FF], gu[:, EXPERT_FF:]
        act = (gt * jax.nn.sigmoid(gt)) * up
        return jnp.concatenate([act[:, :128] * w, act[:, 128:] * w], axis=1).astype(BF16)

    def down_proj(g, act):
        return jnp.dot(act, wd_ref[g], preferred_element_type=F32).astype(BF16)

    def mlp(g, x, w):
        return down_proj(g, activate(up_proj(g, x), w))

    single = pads[0][0] <= W
    for g in range(MOE_G):
        for q in range(Q):
            single = jnp.logical_and(single, pads[g][q] <= W)

    @pl.when(single)
    def _():
        rows = [[pl.ds(pl.multiple_of(offs[g][q], MOE_ALIGN), W) for q in range(Q)] for g in range(MOE_G)]
        x = [jnp.concatenate([xs[r, :] for r in rg], axis=0) for rg in rows]
        w = [jnp.concatenate([wrow[r, :] for r in rg], axis=0) for rg in rows]
        gu = [up_proj(g, x[g]) for g in range(MOE_G)]
        act = [activate(gu[g], w[g]) for g in range(MOE_G)]
        y = [down_proj(g, act[g]) for g in range(MOE_G)]
        for g in range(MOE_G):
            for q, r in enumerate(rows[g]):
                ys[r, :] = y[g][q * W:(q + 1) * W]

    @pl.when(jnp.logical_not(single))
    def _():
        for g in range(MOE_G):
            for q in range(Q):
                def body(c, carry, g=g, q=q):
                    r = pl.ds(pl.multiple_of(offs[g][q] + c * W, MOE_ALIGN), W)
                    ys[r, :] = mlp(g, xs[r, :], wrow[r, :])
                    return carry
                lax.fori_loop(0, (pads[g][q] + (W - 1)) // W, body, 0)

    @pl.when(s == N_EXPERTS // MOE_G - 1)
    def _():
        for q in range(Q):
            blk = slice(q * MOE_SB, (q + 1) * MOE_SB)
            moe = jnp.dot(pt[blk, :], ys[q * MOE_RQS:q * MOE_RQS + MOE_RQ, :], preferred_element_type=F32)
            x2 = x1_ref[blk, :] + moe
            y_ref[blk, :] = x2 * lax.rsqrt(jnp.mean(x2 * x2, axis=-1, keepdims=True) + EPS) * nf_ref[...]


def _moe(hm, gate, gatet, x1, wgu, wd, norm_final):
    T = hm.shape[0]
    tb = MOE_TB
    nb = T // tb
    used = gate.reshape(nb, MOE_Q, MOE_SB, 128)[..., :N_EXPERTS] > 0.0
    cnt = jnp.sum(used.astype(jnp.int32), axis=2)
    padded = (cnt + (MOE_ALIGN - 1)) // MOE_ALIGN * MOE_ALIGN
    off = jnp.cumsum(padded, axis=2) - padded
    meta = jnp.concatenate([off.reshape(nb, -1), padded.reshape(nb, -1)], axis=1).astype(jnp.int32)
    grid_spec = pltpu.PrefetchScalarGridSpec(
        num_scalar_prefetch=1,
        grid=(nb, N_EXPERTS // MOE_G),
        in_specs=[
            pl.BlockSpec((tb, D_MODEL), lambda i, e, m: (i, 0)),
            pl.BlockSpec((tb, 128), lambda i, e, m: (i, 0)),
            pl.BlockSpec((128, tb), lambda i, e, m: (0, i)),
            pl.BlockSpec((tb, D_MODEL), lambda i, e, m: (i, 0)),
            pl.BlockSpec((MOE_G, D_MODEL, 2 * EXPERT_FF), lambda i, e, m: (e, 0, 0)),
            pl.BlockSpec((MOE_G, EXPERT_FF, D_MODEL), lambda i, e, m: (e, 0, 0)),
            pl.BlockSpec((1, D_MODEL), lambda i, e, m: (0, 0)),
        ],
        out_specs=pl.BlockSpec((tb, D_MODEL), lambda i, e, m: (i, 0)),
        scratch_shapes=[
            pltpu.VMEM((MOE_Q * MOE_RQS, D_MODEL), BF16),
            pltpu.VMEM((MOE_Q * MOE_RQS, D_MODEL), BF16),
            pltpu.VMEM((tb, MOE_RQ), BF16),
            pltpu.VMEM((MOE_Q * MOE_RQS, 128), F32),
            pltpu.VMEM((8, tb), F32),
        ],
    )
    return pl.pallas_call(
        _moe_kernel,
        grid_spec=grid_spec,
        out_shape=jax.ShapeDtypeStruct((T, D_MODEL), F32),
        compiler_params=_cparams(("parallel", "arbitrary")),
        name="moe",
    )(meta, hm, gate, gatet, x1, wgu, wd, norm_final)


def _prep_weights(norm_mix, w_in, rel_bias, w_gate_f, b_gate_f, w_gate_b, b_gate_b, gla_norm, w_proj_a,
                  w_proj_b, w_out, norm_moe, w_rg, b_rg, w_re, b_re, w_eg, w_eu, w_ed, norm_final):
    splits = (A_QKV, A_QKV, A_QKV, 512, 512, 1024, 1024, GATE_RANK, GATE_RANK, D_MODEL, D_MODEL)
    qa, ka, va, qb, kb, vb, og, gf, gb, ma, mb = jnp.split(w_in, np.cumsum(splits)[:-1].tolist(), axis=1)
    gpad = jnp.zeros((D_MODEL, PR - C_G - 2 * GATE_RANK), w_in.dtype)
    grp = lambda t, g: t[:, g * A_GW:(g + 1) * A_GW]
    a_cols = [grp(t, g) for g in range(N_DIL) for t in (qa, ka, va)]
    w_all = jnp.concatenate(a_cols + [ma, mb, qb, kb, vb, og, gf, gb, gpad], axis=1).astype(BF16)
    kw = GLA_H * GLA_DK
    wgf = jnp.zeros((128, kw), F32).at[:GATE_RANK].set(w_gate_f).astype(BF16)
    wgb = jnp.zeros((128, kw), F32).at[GATE_RANK:2 * GATE_RANK].set(w_gate_b).astype(BF16)
    wr = jnp.zeros((D_MODEL, 128), F32)
    wr = wr.at[:, :N_EXPERTS].set(w_re).at[:, N_EXPERTS:2 * N_EXPERTS].set(jnp.repeat(w_rg, EPG, axis=1))
    br = jnp.zeros((1, 128), F32)
    br = br.at[0, :N_EXPERTS].set(b_re).at[0, N_EXPERTS:2 * N_EXPERTS].set(jnp.repeat(b_rg, EPG))
    return dict(
        norm_mix=norm_mix.reshape(1, D_MODEL), w_all=w_all, rel_bias=rel_bias,
        wgf=wgf, wgb=wgb, bgf=b_gate_f.reshape(1, kw), bgb=b_gate_b.reshape(1, kw),
        gla_norm=gla_norm.reshape(1, GLA_DV),
        wpa=w_proj_a.astype(BF16), wpb=w_proj_b.astype(BF16), wo=w_out.astype(BF16),
        norm_moe=norm_moe.reshape(1, D_MODEL), wr=wr.astype(BF16), br=br,
        wgu=jnp.concatenate([w_eg, w_eu], axis=-1).astype(BF16), wd=w_ed.astype(BF16),
        norm_final=norm_final.reshape(1, D_MODEL),
    )


def _trunk(x, w):
    B, S, _ = x.shape
    T = B * S
    x2d = x.reshape(T, D_MODEL)
    *qkv, proj = _inproj(x, w["norm_mix"], w["w_all"])
    o_list, lse_list = [], []
    for g, (_, d) in enumerate(DIL_PAIRS):
        L = S // d
        tiles = _bias_tiles(w["rel_bias"], g, d, min(2 * Q_BLK, L))
        o, lse = _attn_group(qkv[g], tiles, d, B, S)
        o_list.append(o.reshape(T, A_GW))
        lse_list.append(lse.reshape(T, A_GW))
    ob = _gla(proj, w["wgf"], w["wgb"], w["bgf"], w["bgb"], w["gla_norm"], B, S)
    x1, hm, gate, gatet = _outproj(x2d, proj, o_list, lse_list, ob.reshape(T, D_MODEL), w["wpa"], w["wpb"],
                                   w["wo"], w["norm_moe"], w["wr"], w["br"])
    y = _moe(hm, gate, gatet, x1, w["wgu"], w["wd"], w["norm_final"])
    return y.reshape(B, S, D_MODEL)


def kernel(x_prompt, x_sample, norm_mix, w_in, rel_bias, w_gate_f, b_gate_f, w_gate_b, b_gate_b, gla_norm,
           w_proj_a, w_proj_b, w_out, norm_moe, w_router_group, b_router_group, w_router_expert,
           b_router_expert, w_exp_gate, w_exp_up, w_exp_down, norm_final):
    w = _prep_weights(norm_mix[0], w_in[0], rel_bias, w_gate_f[0], b_gate_f[0], w_gate_b[0], b_gate_b[0],
                      gla_norm[0], w_proj_a[0], w_proj_b[0], w_out[0], norm_moe[0], w_router_group[0],
                      b_router_group[0], w_router_expert[0], b_router_expert[0], w_exp_gate[0], w_exp_up[0],
                      w_exp_down[0], norm_final)
    return (_trunk(x_prompt, w), _trunk(x_sample, w))
```

```python
import functools
import math

import numpy as np
import jax
import jax.numpy as jnp
from jax import lax
from jax.experimental import pallas as pl
from jax.experimental.pallas import tpu as pltpu

F32 = jnp.float32
BF16 = jnp.bfloat16

D_MODEL = 1024
EPS = 1e-6
NEG_INF = -1e30

DIL_PAIRS = ((128, 1), (512, 4), (2048, 16))
A_HPG = 4
A_DH = 64
A_GW = A_HPG * A_DH
A_QKV = 3 * A_GW
HALF_WIN = 64
Q_BLK = 128
N_REL_BUCKETS = 32
REL_MAX_DISTANCE = 1024
GLA_H = 4
GLA_DK = 128
GLA_DV = 256
GATE_RANK = 16
GATE_TAU = 16.0
GLA_CHUNK = 64
N_GROUPS = 4
EPG = 8
N_EXPERTS = 32
EXPERT_FF = 256

N_DIL = len(DIL_PAIRS)
C_MA, C_MB = 0, 1024
C_QB, C_KB, C_VB, C_OG = 2048, 2560, 3072, 4096
C_G = 5120
PR = 5376
PROJ_TN = A_QKV

VMEM_LIMIT = 58 * 1024 * 1024


def _cparams(sem):
    return pltpu.CompilerParams(dimension_semantics=sem, vmem_limit_bytes=VMEM_LIMIT)


def _inproj_kernel(x_ref, g_ref, w_ref, a0_ref, a1_ref, a2_ref, or_ref, y_scr):
    x = x_ref[...]
    r = x * lax.rsqrt(jnp.mean(x * x, axis=-1, keepdims=True) + EPS)
    h = (r * g_ref[...]).astype(BF16)
    nct = PROJ_TN // 128
    for j in range(N_DIL + PR // PROJ_TN):
        y = jnp.dot(h, w_ref[:, j * PROJ_TN:(j + 1) * PROJ_TN], preferred_element_type=F32)
        if j >= N_DIL:
            or_ref[:, (j - N_DIL) * PROJ_TN:(j - N_DIL + 1) * PROJ_TN] = y.astype(BF16)
            continue
        a_ref = (a0_ref, a1_ref, a2_ref)[j]
        d = DIL_PAIRS[j][1]
        if d == 1:
            a_ref[0, 0] = y.astype(BF16)
            continue
        for c in range(nct):
            y_scr[j - 1, c] = y[:, c * 128:(c + 1) * 128]
        n = y.shape[0] // d
        for cls in range(d):
            cols = [y_scr[j - 1, c, pl.ds(cls, n, stride=d), :] for c in range(nct)]
            a_ref[0, cls] = jnp.concatenate(cols, axis=1).astype(BF16)


def _inproj(x, norm_g, w_all, tm=512):
    B, S, _ = x.shape
    T = B * S
    spb = S // tm
    a_specs, a_shapes = [], []
    for _, d in DIL_PAIRS:
        a_specs.append(pl.BlockSpec((1, d, tm // d, PROJ_TN), lambda i: (i // spb, 0, i % spb, 0)))
        a_shapes.append(jax.ShapeDtypeStruct((B, d, S // d, PROJ_TN), BF16))
    return pl.pallas_call(
        _inproj_kernel,
        grid=(T // tm,),
        in_specs=[
            pl.BlockSpec((tm, D_MODEL), lambda i: (i, 0)),
            pl.BlockSpec((1, D_MODEL), lambda i: (0, 0)),
            pl.BlockSpec(w_all.shape, lambda i: (0, 0), pipeline_mode=pl.Buffered(1)),
        ],
        out_specs=a_specs + [pl.BlockSpec((tm, PR), lambda i: (i, 0))],
        out_shape=a_shapes + [jax.ShapeDtypeStruct((T, PR), BF16)],
        scratch_shapes=[pltpu.VMEM((N_DIL - 1, PROJ_TN // 128, tm, 128), F32)],
        compiler_params=_cparams(("parallel",)),
        name="inproj",
    )(x.reshape(T, D_MODEL), norm_g, w_all)


ATTN_ITEMS = 4


def _attn_kernel(q_ref, k_ref, v_ref, bias_ref, o_ref, lse_ref, *, L, wk, nc, nq, nblk):
    i = pl.program_id(2)
    lane = lax.broadcasted_iota(jnp.int32, (Q_BLK, A_GW), 1)
    qscale = jnp.asarray(A_DH ** -0.5, BF16)
    nt = (((1,), (1,)), ((), ()))
    items = [(c, t) for c in range(nc) for t in range(nq)]
    heads = [slice(h * A_DH, (h + 1) * A_DH) for h in range(A_HPG)]
    vws, scores = [], []
    for c, t in items:
        blk = i * nq + t
        start = pl.multiple_of(jnp.clip(blk * Q_BLK - HALF_WIN, 0, L - wk), HALF_WIN)
        var = jnp.where(blk == 0, 0, jnp.where(blk == nblk - 1, 2, 1))
        q = q_ref[0, c, t * Q_BLK:(t + 1) * Q_BLK, :] * qscale
        kw = k_ref[0, c, pl.ds(start, wk), :]
        vws.append(v_ref[0, c, pl.ds(start, wk), :])
        scores.append([lax.dot_general(q[:, sl], kw[:, sl], nt, preferred_element_type=F32) + bias_ref[var, h]
                       for h, sl in enumerate(heads)])
    maxes = [[jnp.max(s, axis=-1, keepdims=True) for s in item] for item in scores]
    probs = [[jnp.exp(s - m) for s, m in zip(si, mi)] for si, mi in zip(scores, maxes)]
    sums = [[jnp.sum(p, axis=-1, keepdims=True) for p in item] for item in probs]
    for (c, t), vw, pi, mi, li in zip(items, vws, probs, maxes, sums):
        outs = [jnp.dot(p.astype(BF16), vw[:, sl], preferred_element_type=F32) / l
                for p, l, sl in zip(pi, li, heads)]
        lse_tile = jnp.zeros((Q_BLK, A_GW), F32)
        for h, (m, l) in enumerate(zip(mi, li)):
            lse_tile = jnp.where((lane >= A_DH * h) & (lane < A_DH * (h + 1)), m + jnp.log(l), lse_tile)
        rows = slice(t * Q_BLK, (t + 1) * Q_BLK)
        cols = slice(c * A_GW, (c + 1) * A_GW)
        o_ref[0, rows, cols] = jnp.concatenate(outs, axis=1).astype(BF16)
        lse_ref[0, rows, cols] = lse_tile


def _attn_group(qkv, bias_tiles, d, B, S):
    L = S // d
    nblk = L // Q_BLK
    wk = min(2 * Q_BLK, L)
    nq = min(ATTN_ITEMS, nblk)
    nc = ATTN_ITEMS // nq

    return pl.pallas_call(
        functools.partial(_attn_kernel, L=L, wk=wk, nc=nc, nq=nq, nblk=nblk),
        grid=(B, d // nc, nblk // nq),
        in_specs=[
            pl.BlockSpec((1, nc, nq * Q_BLK, A_GW), lambda b, r, i: (b, r, i, 0)),
            pl.BlockSpec((1, nc, L, A_GW), lambda b, r, i: (b, r, 0, 1)),
            pl.BlockSpec((1, nc, L, A_GW), lambda b, r, i: (b, r, 0, 2)),
            pl.BlockSpec(bias_tiles.shape, lambda b, r, i: (0, 0, 0, 0)),
        ],
        out_specs=[
            pl.BlockSpec((1, nq * Q_BLK, nc * A_GW), lambda b, r, i: (b, i, r)),
            pl.BlockSpec((1, nq * Q_BLK, nc * A_GW), lambda b, r, i: (b, i, r)),
        ],
        out_shape=[
            jax.ShapeDtypeStruct((B, L, d * A_GW), BF16),
            jax.ShapeDtypeStruct((B, L, d * A_GW), F32),
        ],
        compiler_params=_cparams(("parallel", "parallel", "arbitrary")),
        name=f"attn_d{d}",
    )(qkv, qkv, qkv, bias_tiles)


def _t5_buckets(rel):
    nb = N_REL_BUCKETS // 2
    max_exact = nb // 2
    ret = (rel > 0).astype(np.int64) * nb
    n = np.abs(rel)
    large = max_exact + (np.log(np.maximum(n, 1) / max_exact) / math.log(REL_MAX_DISTANCE / max_exact)
                         * (nb - max_exact)).astype(np.int64)
    large = np.minimum(large, nb - 1)
    return (ret + np.where(n < max_exact, n, large)).astype(np.int32)


def _bias_tiles(rel_bias, g, d, wk):
    rel = d * np.arange(-HALF_WIN, HALF_WIN + 1)
    bucket = _t5_buckets(rel)
    bias = rel_bias[jnp.asarray(bucket)][:, g * A_HPG:(g + 1) * A_HPG].T.astype(F32)
    pad = wk + Q_BLK
    neg = jnp.full((A_HPG, pad), NEG_INF, F32)
    val = jnp.concatenate([neg, bias, neg], axis=1)
    zero = pad + HALF_WIN
    m = wk + Q_BLK
    tiles = []
    for off in (0, HALF_WIN, wk - Q_BLK):
        u = jnp.concatenate([val[:, zero - off:zero - off + wk], val[:, zero - off - Q_BLK:zero - off]], axis=1)
        flat = jnp.tile(u, (1, Q_BLK))[:, :Q_BLK * (m - 1)]
        tiles.append(flat.reshape(A_HPG, Q_BLK, m - 1)[:, :, :wk])
    return jnp.stack(tiles)


def _log_sigmoid(z):
    return jnp.minimum(z, 0.0) - jnp.log(1.0 + jnp.exp(-jnp.abs(z)))


def _gla_kernel(q_ref, k_ref, v_ref, og_ref, g_ref, wgf_ref, wgb_ref, bgf_ref, bgb_ref, gn_ref, o_ref,
                qif, kof, qib, kob, vt, etf, etb, acc, *, S):
    PB = 256
    C = GLA_CHUNK
    NB = S // PB
    rr = lax.broadcasted_iota(jnp.int32, (PB, PB), 0)
    cc = lax.broadcasted_iota(jnp.int32, (PB, PB), 1)
    same = (rr >> 6) == (cc >> 6)
    mask_f = same & (cc <= rr)
    mask_b = same & (cc > rr)
    tl = jnp.where(mask_f, 1.0, 0.0).astype(BF16)
    qscale = GLA_DK ** -0.5
    inv_tau = 1.0 / GATE_TAU
    nt = (((1,), (1,)), ((), ()))

    def split3(x):
        hi = x.astype(BF16)
        r1 = x - hi.astype(F32)
        mid = r1.astype(BF16)
        lo = (r1 - mid.astype(F32)).astype(BF16)
        return [hi, mid, lo]

    def chunk_total(b):
        b4 = b.reshape(PB // C, C, GLA_DK)
        return jnp.broadcast_to(b4[:, C - 1:C, :], b4.shape).reshape(PB, GLA_DK)

    PREP_BLOCKS = 4

    def prep(t, carry):
        blocks = [t * PREP_BLOCKS + u for u in range(PREP_BLOCKS)]
        rows = [pl.ds(pl.multiple_of(i * PB, PB), PB) for i in blocks]
        gs = [g_ref[0, r, :] for r in rows]
        lfs = [_log_sigmoid(jnp.dot(g, wgf_ref[...], preferred_element_type=F32) + bgf_ref[...]) * inv_tau for g in gs]
        lbs = [_log_sigmoid(jnp.dot(g, wgb_ref[...], preferred_element_type=F32) + bgb_ref[...]) * inv_tau for g in gs]
        css = [jnp.dot(tl, jnp.concatenate(split3(lf) + split3(lb), axis=1), preferred_element_type=F32)
               for lf, lb in zip(lfs, lbs)]
        bfs = [cs[:, 0:128] + cs[:, 128:256] + cs[:, 256:384] for cs in css]
        pbs = [cs[:, 384:512] + cs[:, 512:640] + cs[:, 640:768] for cs in css]
        totfs = [chunk_total(bf) for bf in bfs]
        totbs = [chunk_total(pb) for pb in pbs]
        bss = [totb - pb + lb for totb, pb, lb in zip(totbs, pbs, lbs)]
        qs = [q_ref[0, r, :].astype(F32) * qscale for r in rows]
        ks = [k_ref[0, r, :].astype(F32) for r in rows]
        qfs = [(q * jnp.exp(bf)).astype(BF16) for q, bf in zip(qs, bfs)]
        qbs = [(q * jnp.exp(bs)).astype(BF16) for q, bs in zip(qs, bss)]
        kfs = [(k * jnp.exp(-bf)).astype(BF16) for k, bf in zip(ks, bfs)]
        kbs = [(k * jnp.exp(-bs)).astype(BF16) for k, bs in zip(ks, bss)]
        sfs = [lax.dot_general(qf, kf, nt, preferred_element_type=F32) for qf, kf in zip(qfs, kfs)]
        sbs = [lax.dot_general(qb, kb, nt, preferred_element_type=F32) for qb, kb in zip(qbs, kbs)]
        ps = [jnp.where(mask_f, sf, jnp.where(mask_b, sb, 0.0)).astype(BF16) for sf, sb in zip(sfs, sbs)]
        vs = [v_ref[0, r, :] for r in rows]
        for r, p, v in zip(rows, ps, vs):
            acc[r, :] = jnp.dot(p, v, preferred_element_type=F32)
        for r, qf, qb in zip(rows, qfs, qbs):
            qif[r, :] = qf
            qib[r, :] = qb
        for r, k, totf, bf, totb, bs in zip(rows, ks, totfs, bfs, totbs, bss):
            kof[r, :] = (k * jnp.exp(totf - bf)).astype(BF16)
            kob[r, :] = (k * jnp.exp(totb - bs)).astype(BF16)
            etf[r, :] = jnp.exp(totf)
            etb[r, :] = jnp.exp(totb)
        for i, v in zip(blocks, vs):
            vtb = v.astype(F32).T.astype(BF16)
            vt[2 * i] = vtb[:, :128]
            vt[2 * i + 1] = vtb[:, 128:]
        return carry

    lax.fori_loop(0, NB // PREP_BLOCKS, prep, 0)

    zeros_half = jnp.zeros((C, GLA_DK), BF16)

    def chunk_step(qi_ref, ko_ref, et_ref, blk, j, state):
        r0 = pl.multiple_of(blk * PB + j * C, C)
        rows = pl.ds(r0, C)
        acc[rows, :] += lax.dot_general(qi_ref[rows, :], state.astype(BF16), nt, preferred_element_type=F32)
        ko = ko_ref[rows, :]
        ko_pad = jnp.concatenate([ko, zeros_half] if j % 2 == 0 else [zeros_half, ko], axis=0)
        upd = jnp.dot(vt[2 * blk + j // 2], ko_pad, preferred_element_type=F32)
        return state * et_ref[pl.ds(r0, 1), :] + upd

    def serial(t, carry):
        st_f, st_b = carry
        for j in range(PB // C):
            st_f = chunk_step(qif, kof, etf, t, j, st_f)
            st_b = chunk_step(qib, kob, etb, NB - 1 - t, PB // C - 1 - j, st_b)
        return st_f, st_b

    zero_state = jnp.zeros((GLA_DV, GLA_DK), F32)
    lax.fori_loop(0, NB, serial, (zero_state, zero_state), unroll=4)

    gn = gn_ref[...]

    def finish(i, carry):
        rows = pl.ds(pl.multiple_of(i * PB, PB), PB)
        tot = acc[rows, :]
        nrm = tot * lax.rsqrt(jnp.mean(tot * tot, axis=-1, keepdims=True) + EPS) * gn
        og = og_ref[0, rows, :].astype(F32)
        o_ref[0, rows, :] = (nrm * (og * jax.nn.sigmoid(og))).astype(BF16)
        return carry

    lax.fori_loop(0, NB, finish, 0)


def _gla(proj, wgf, wgb, bgf, bgb, gla_norm, B, S):
    pv = proj.reshape(B, S, PR)
    cq, ck, cv, cog, cg = C_QB // GLA_DK, C_KB // GLA_DK, C_VB // GLA_DV, C_OG // GLA_DV, C_G // 128
    return pl.pallas_call(
        functools.partial(_gla_kernel, S=S),
        grid=(B, GLA_H),
        in_specs=[
            pl.BlockSpec((1, S, GLA_DK), lambda b, h: (b, 0, cq + h)),
            pl.BlockSpec((1, S, GLA_DK), lambda b, h: (b, 0, ck + h)),
            pl.BlockSpec((1, S, GLA_DV), lambda b, h: (b, 0, cv + h)),
            pl.BlockSpec((1, S, GLA_DV), lambda b, h: (b, 0, cog + h)),
            pl.BlockSpec((1, S, 128), lambda b, h: (b, 0, cg)),
            pl.BlockSpec((128, GLA_DK), lambda b, h: (0, h)),
            pl.BlockSpec((128, GLA_DK), lambda b, h: (0, h)),
            pl.BlockSpec((1, GLA_DK), lambda b, h: (0, h)),
            pl.BlockSpec((1, GLA_DK), lambda b, h: (0, h)),
            pl.BlockSpec((1, GLA_DV), lambda b, h: (0, 0)),
        ],
        out_specs=pl.BlockSpec((1, S, GLA_DV), lambda b, h: (b, 0, h)),
        out_shape=jax.ShapeDtypeStruct((B, S, GLA_H * GLA_DV), BF16),
        scratch_shapes=[
            pltpu.VMEM((S, GLA_DK), BF16), pltpu.VMEM((S, GLA_DK), BF16),
            pltpu.VMEM((S, GLA_DK), BF16), pltpu.VMEM((S, GLA_DK), BF16),
            pltpu.VMEM((S // 128, GLA_DV, 128), BF16),
            pltpu.VMEM((S, GLA_DK), F32), pltpu.VMEM((S, GLA_DK), F32),
            pltpu.VMEM((S, GLA_DV), F32),
        ],
        compiler_params=_cparams(("parallel", "arbitrary")),
        name="gla",
    )(pv, pv, pv, pv, pv, wgf, wgb, bgf, bgb, gla_norm)


OUTPROJ_SPLIT = 2


def _outproj_kernel(x_ref, ma_ref, mb_ref, o0_ref, o1_ref, o2_ref, l0_ref, l1_ref, l2_ref, ob_ref,
                    wpa_ref, wpb_ref, wo_ref, nm_ref, wr_ref, br_ref,
                    x1_ref, hm_ref, gate_ref, gatet_ref):
    tm = x_ref.shape[0] // OUTPROJ_SPLIT
    parts = [slice(p * tm, (p + 1) * tm) for p in range(OUTPROJ_SPLIT)]

    def merge_groups(rs):
        l0, l1, l2 = l0_ref[rs, :], l1_ref[rs, :], l2_ref[rs, :]
        m = jnp.maximum(jnp.maximum(l0, l1), l2)
        e0, e1, e2 = jnp.exp(l0 - m), jnp.exp(l1 - m), jnp.exp(l2 - m)
        den = e0 + e1 + e2
        oa = ((e0 / den) * o0_ref[rs, :].astype(F32) + (e1 / den) * o1_ref[rs, :].astype(F32)
              + (e2 / den) * o2_ref[rs, :].astype(F32))
        return oa.astype(BF16)

    def project(rs, oa):
        ya = jnp.dot(oa, wpa_ref[...], preferred_element_type=F32)
        yb = jnp.dot(ob_ref[rs, :], wpb_ref[...], preferred_element_type=F32)
        return ya, yb

    def gate_mix(rs, y):
        ya, yb = y
        mix = jax.nn.sigmoid(ma_ref[rs, :].astype(F32)) * ya + jax.nn.sigmoid(mb_ref[rs, :].astype(F32)) * yb
        return mix.astype(BF16)

    def residual(rs, mix):
        x1 = x_ref[rs, :] + jnp.dot(mix, wo_ref[...], preferred_element_type=F32)
        x1_ref[rs, :] = x1
        return x1

    def moe_norm(rs, x1):
        hm = (x1 * lax.rsqrt(jnp.mean(x1 * x1, axis=-1, keepdims=True) + EPS) * nm_ref[...]).astype(BF16)
        hm_ref[rs, :] = hm
        return hm

    def router_logits(rs, hm):
        return jnp.dot(hm, wr_ref[...], preferred_element_type=F32) + br_ref[...]

    def router(rs, logits):
        gate = _route(logits)
        gate_ref[rs, :] = gate
        gatet_ref[:, rs] = gate.T

    stages = [lambda rs, _: merge_groups(rs), project, gate_mix, residual, moe_norm, router_logits, router]
    vals = [None] * OUTPROJ_SPLIT
    for step in range(len(stages) + OUTPROJ_SPLIT - 1):
        for p, rs in enumerate(parts):
            if 0 <= step - p < len(stages):
                vals[p] = stages[step - p](rs, vals[p])


def _route(logits):
    tm = logits.shape[0]
    lane = lax.broadcasted_iota(jnp.int32, (tm, 128), 1)
    is_g = (lane >= N_EXPERTS) & (lane < 2 * N_EXPERTS)
    is_e = lane < N_EXPERTS
    grp_of_lane = jnp.where(is_g, (lane - N_EXPERTS) >> 3, lane >> 3)
    gl = jnp.where(is_g, logits, NEG_INF)
    gmax = jnp.max(gl, axis=-1, keepdims=True)
    gsum = jnp.sum(jnp.where(is_g, jnp.exp(gl - gmax), 0.0), axis=-1, keepdims=True) * (1.0 / EPG)
    pg_top = 1.0 / gsum
    g_sel = jnp.min(jnp.where(is_g & (gl == gmax), grp_of_lane, N_GROUPS), axis=-1, keepdims=True)
    in_grp = is_e & (grp_of_lane == g_sel)
    el = jnp.where(in_grp, logits, NEG_INF)
    emax = jnp.max(el, axis=-1, keepdims=True)
    ee = jnp.where(in_grp, jnp.exp(el - emax), 0.0)
    pe = ee / jnp.sum(ee, axis=-1, keepdims=True)
    p1 = jnp.max(pe, axis=-1, keepdims=True)
    i1 = jnp.min(jnp.where(in_grp & (pe == p1), lane, 128), axis=-1, keepdims=True)
    rest = in_grp & (lane != i1)
    pe2 = jnp.where(rest, pe, -1.0)
    p2 = jnp.max(pe2, axis=-1, keepdims=True)
    i2 = jnp.min(jnp.where(rest & (pe2 == p2), lane, 128), axis=-1, keepdims=True)
    psum = p1 + p2
    return jnp.where(lane == i1, pg_top * (p1 / psum), jnp.where(lane == i2, pg_top * (p2 / psum), 0.0))


def _outproj(x2d, proj, o_list, lse_list, ob2d, wpa, wpb, wo, norm_moe, wr, br, tm=512):
    T = x2d.shape[0]
    row = lambda w: pl.BlockSpec((tm, w), lambda i: (i, 0))
    full = lambda a: pl.BlockSpec(a.shape, lambda i: (0,) * a.ndim)
    return pl.pallas_call(
        _outproj_kernel,
        grid=(T // tm,),
        in_specs=[
            row(D_MODEL),
            pl.BlockSpec((tm, D_MODEL), lambda i: (i, C_MA // D_MODEL)),
            pl.BlockSpec((tm, D_MODEL), lambda i: (i, C_MB // D_MODEL)),
            row(A_GW), row(A_GW), row(A_GW), row(A_GW), row(A_GW), row(A_GW),
            row(D_MODEL),
            full(wpa), full(wpb), full(wo), full(norm_moe), full(wr), full(br),
        ],
        out_specs=[row(D_MODEL), row(D_MODEL), row(128), pl.BlockSpec((128, tm), lambda i: (0, i))],
        out_shape=[
            jax.ShapeDtypeStruct((T, D_MODEL), F32),
            jax.ShapeDtypeStruct((T, D_MODEL), BF16),
            jax.ShapeDtypeStruct((T, 128), F32),
            jax.ShapeDtypeStruct((128, T), F32),
        ],
        compiler_params=_cparams(("parallel",)),
        name="outproj",
    )(x2d, proj, proj, *o_list, *lse_list, ob2d, wpa, wpb, wo, norm_moe, wr, br)


MOE_TB = 1024
MOE_Q = 4
MOE_SB = MOE_TB // MOE_Q
MOE_ALIGN = 16
MOE_CH = 128
MOE_W = MOE_CH // MOE_Q
MOE_RQ = 2 * MOE_SB + N_EXPERTS * MOE_ALIGN
MOE_RQS = MOE_RQ + MOE_W
MOE_SUB = 256
MOE_G = 4


def _moe_route(hm_ref, gate_ref, gatet_ref, xs, ys, pt, wrow, rinfo):
    TB, Q, SB, RQ, RQS, SUB = MOE_TB, MOE_Q, MOE_SB, MOE_RQ, MOE_RQS, MOE_SUB
    big = float(4 * TB)
    gate = gate_ref[...]
    gt = gatet_ref[...]
    a = gate > 0.0
    at = gt > 0.0
    a_f = jnp.where(a, 1.0, 0.0)
    at_f = jnp.where(at, 1.0, 0.0)
    a_b = a_f.astype(BF16)
    at_b = at_f.astype(BF16)

    def pad(c):
        return jnp.floor((c + (MOE_ALIGN - 1)) * (1.0 / MOE_ALIGN)) * MOE_ALIGN

    er = lax.broadcasted_iota(jnp.int32, (128, 128), 0)
    ec = lax.broadcasted_iota(jnp.int32, (128, 128), 1)
    sub8 = lax.broadcasted_iota(jnp.int32, (8, 128), 0)
    pad_rows = jnp.zeros((8, 128), F32)
    pad_cols = jnp.zeros((128, 128), F32)
    for q in range(Q):
        blk = slice(q * SB, (q + 1) * SB)
        pad_rows = jnp.where(sub8 == q, pad(jnp.sum(a_f[blk], axis=0, keepdims=True)), pad_rows)
        pad_cols = jnp.where(ec == q, pad(jnp.sum(at_f[:, blk], axis=1, keepdims=True)), pad_cols)
    off_rows = jnp.dot(pad_rows.astype(BF16), jnp.where(er < ec, 1.0, 0.0).astype(BF16), preferred_element_type=F32)
    off_cols = jnp.dot(jnp.where(ec < er, 1.0, 0.0).astype(BF16), pad_cols.astype(BF16), preferred_element_type=F32)

    tr = lax.broadcasted_iota(jnp.int32, (SUB, SUB), 0)
    tc = lax.broadcasted_iota(jnp.int32, (SUB, SUB), 1)
    tri_l = jnp.where(tc < tr, 1.0, 0.0).astype(BF16)
    tri_u = jnp.where(tr < tc, 1.0, 0.0).astype(BF16)
    rl = lax.broadcasted_iota(jnp.int32, (SUB, RQ), 1).astype(F32)
    for q in range(Q):
        off_r = off_rows[q:q + 1]
        off_c = off_cols[:, q:q + 1]
        seen_r = jnp.zeros((1, 128), F32)
        seen_c = jnp.zeros((128, 1), F32)
        for piece in range(SB // SUB):
            rows = slice(q * SB + piece * SUB, q * SB + (piece + 1) * SUB)
            a_c = a[rows]
            pos = off_r + seen_r + jnp.dot(tri_l, a_b[rows], preferred_element_type=F32)
            plo = jnp.min(jnp.where(a_c, pos, big), axis=1, keepdims=True)
            phi = jnp.max(jnp.where(a_c, pos, -1.0), axis=1, keepdims=True)
            phi = jnp.where(phi == plo, -1.0, phi)
            pt[rows, :] = jnp.where((rl == plo) | (rl == phi), 1.0, 0.0).astype(BF16)
            seen_r = seen_r + jnp.sum(a_f[rows], axis=0, keepdims=True)
            at_c = at[:, rows]
            gt_c = gt[:, rows]
            pos_t = off_c + seen_c + jnp.dot(at_b[:, rows], tri_u, preferred_element_type=F32)
            plo = jnp.min(jnp.where(at_c, pos_t, big), axis=0, keepdims=True)
            phi = jnp.max(jnp.where(at_c, pos_t, -1.0), axis=0, keepdims=True)
            phi = jnp.where(phi == plo, -1.0, phi)
            rinfo[0:1, rows] = plo
            rinfo[1:2, rows] = phi
            rinfo[2:3, rows] = jnp.sum(jnp.where(at_c & (pos_t == plo), gt_c, 0.0), axis=0, keepdims=True)
            rinfo[3:4, rows] = jnp.sum(jnp.where(at_c & (pos_t == phi), gt_c, 0.0), axis=0, keepdims=True)
            seen_c = seen_c + jnp.sum(at_f[:, rows], axis=1, keepdims=True)

    blks = [slice(q * SB, (q + 1) * SB) for q in range(Q)]

    def gather(k, carry):
        r0 = pl.multiple_of(k * SUB, SUB)
        ri = (lax.broadcasted_iota(jnp.int32, (SUB, SB), 0) + r0).astype(F32)
        mlo = [ri == rinfo[0:1, b] for b in blks]
        mhi = [ri == rinfo[1:2, b] for b in blks]
        p = [jnp.where(lo | hi, 1.0, 0.0).astype(BF16) for lo, hi in zip(mlo, mhi)]
        x = [jnp.dot(pq, hm_ref[b, :], preferred_element_type=F32).astype(BF16) for pq, b in zip(p, blks)]
        w = [jnp.sum(jnp.where(lo, rinfo[2:3, b], 0.0) + jnp.where(hi, rinfo[3:4, b], 0.0), axis=1, keepdims=True)
             for lo, hi, b in zip(mlo, mhi, blks)]
        for q in range(Q):
            dst = pl.ds(pl.multiple_of(q * RQS + r0, MOE_ALIGN), SUB)
            xs[dst, :] = x[q]
            wrow[dst, :] = jnp.broadcast_to(w[q], (SUB, 128))
        return carry

    lax.fori_loop(0, RQ // SUB, gather, 0)
    for q in range(Q):
        xs[q * RQS + RQ:(q + 1) * RQS, :] = jnp.zeros((RQS - RQ, D_MODEL), BF16)
        wrow[q * RQS + RQ:(q + 1) * RQS, :] = jnp.zeros((RQS - RQ, 128), F32)
    ys[...] = jnp.zeros_like(ys)


def _moe_kernel(meta_ref, hm_ref, gate_ref, gatet_ref, x1_ref, wgu_ref, wd_ref, nf_ref, y_ref,
                xs, ys, pt, wrow, rinfo):
    i = pl.program_id(0)
    s = pl.program_id(1)
    Q, W = MOE_Q, MOE_W

    @pl.when(s == 0)
    def _():
        _moe_route(hm_ref, gate_ref, gatet_ref, xs, ys, pt, wrow, rinfo)

    offs = [[meta_ref[i, q * N_EXPERTS + s * MOE_G + g] + q * MOE_RQS for q in range(Q)] for g in range(MOE_G)]
    pads = [[meta_ref[i, (Q + q) * N_EXPERTS + s * MOE_G + g] for q in range(Q)] for g in range(MOE_G)]

    def up_proj(g, x):
        return jnp.dot(x, wgu_ref[g], preferred_element_type=F32)

    def activate(gu, w):
        gt, up = gu[:, :EXPERT_FF], gu[:, EXPERT_FF:]
        act = (gt * jax.nn.sigmoid(gt)) * up
        return jnp.concatenate([act[:, :128] * w, act[:, 128:] * w], axis=1).astype(BF16)

    def down_proj(g, act):
        return jnp.dot(act, wd_ref[g], preferred_element_type=F32).astype(BF16)

    def mlp(g, x, w):
        return down_proj(g, activate(up_proj(g, x), w))

    single = pads[0][0] <= W
    for g in range(MOE_G):
        for q in range(Q):
            single = jnp.logical_and(single, pads[g][q] <= W)

    @pl.when(single)
    def _():
        rows = [[pl.ds(pl.multiple_of(offs[g][q], MOE_ALIGN), W) for q in range(Q)] for g in range(MOE_G)]
        x = [jnp.concatenate([xs[r, :] for r in rg], axis=0) for rg in rows]
        w = [jnp.concatenate([wrow[r, :] for r in rg], axis=0) for rg in rows]
        gu = [up_proj(g, x[g]) for g in range(MOE_G)]
        act = [activate(gu[g], w[g]) for g in range(MOE_G)]
        y = [down_proj(g, act[g]) for g in range(MOE_G)]
        for g in range(MOE_G):
            for q, r in enumerate(rows[g]):
                ys[r, :] = y[g][q * W:(q + 1) * W]

    @pl.when(jnp.logical_not(single))
    def _():
        for g in range(MOE_G):
            for q in range(Q):
                def body(c, carry, g=g, q=q):
                    r = pl.ds(pl.multiple_of(offs[g][q] + c * W, MOE_ALIGN), W)
                    ys[r, :] = mlp(g, xs[r, :], wrow[r, :])
                    return carry
                lax.fori_loop(0, (pads[g][q] + (W - 1)) // W, body, 0)

    @pl.when(s == N_EXPERTS // MOE_G - 1)
    def _():
        blks = [slice(q * MOE_SB, (q + 1) * MOE_SB) for q in range(Q)]
        moe = [jnp.dot(pt[b, :], ys[q * MOE_RQS:q * MOE_RQS + MOE_RQ, :], preferred_element_type=F32)
               for q, b in enumerate(blks)]
        x2 = [x1_ref[b, :] + m for b, m in zip(blks, moe)]
        for b, x in zip(blks, x2):
            y_ref[b, :] = x * lax.rsqrt(jnp.mean(x * x, axis=-1, keepdims=True) + EPS) * nf_ref[...]


def _moe(hm, gate, gatet, x1, wgu, wd, norm_final):
    T = hm.shape[0]
    tb = MOE_TB
    nb = T // tb
    used = gate.reshape(nb, MOE_Q, MOE_SB, 128)[..., :N_EXPERTS] > 0.0
    cnt = jnp.sum(used.astype(jnp.int32), axis=2)
    padded = (cnt + (MOE_ALIGN - 1)) // MOE_ALIGN * MOE_ALIGN
    off = jnp.cumsum(padded, axis=2) - padded
    meta = jnp.concatenate([off.reshape(nb, -1), padded.reshape(nb, -1)], axis=1).astype(jnp.int32)
    grid_spec = pltpu.PrefetchScalarGridSpec(
        num_scalar_prefetch=1,
        grid=(nb, N_EXPERTS // MOE_G),
        in_specs=[
            pl.BlockSpec((tb, D_MODEL), lambda i, e, m: (i, 0)),
            pl.BlockSpec((tb, 128), lambda i, e, m: (i, 0)),
            pl.BlockSpec((128, tb), lambda i, e, m: (0, i)),
            pl.BlockSpec((tb, D_MODEL), lambda i, e, m: (i, 0)),
            pl.BlockSpec((MOE_G, D_MODEL, 2 * EXPERT_FF), lambda i, e, m: (e, 0, 0)),
            pl.BlockSpec((MOE_G, EXPERT_FF, D_MODEL), lambda i, e, m: (e, 0, 0)),
            pl.BlockSpec((1, D_MODEL), lambda i, e, m: (0, 0)),
        ],
        out_specs=pl.BlockSpec((tb, D_MODEL), lambda i, e, m: (i, 0)),
        scratch_shapes=[
            pltpu.VMEM((MOE_Q * MOE_RQS, D_MODEL), BF16),
            pltpu.VMEM((MOE_Q * MOE_RQS, D_MODEL), BF16),
            pltpu.VMEM((tb, MOE_RQ), BF16),
            pltpu.VMEM((MOE_Q * MOE_RQS, 128), F32),
            pltpu.VMEM((8, tb), F32),
        ],
    )
    return pl.pallas_call(
        _moe_kernel,
        grid_spec=grid_spec,
        out_shape=jax.ShapeDtypeStruct((T, D_MODEL), F32),
        compiler_params=_cparams(("parallel", "arbitrary")),
        name="moe",
    )(meta, hm, gate, gatet, x1, wgu, wd, norm_final)


def _prep_weights(norm_mix, w_in, rel_bias, w_gate_f, b_gate_f, w_gate_b, b_gate_b, gla_norm, w_proj_a,
                  w_proj_b, w_out, norm_moe, w_rg, b_rg, w_re, b_re, w_eg, w_eu, w_ed, norm_final):
    splits = (A_QKV, A_QKV, A_QKV, 512, 512, 1024, 1024, GATE_RANK, GATE_RANK, D_MODEL, D_MODEL)
    qa, ka, va, qb, kb, vb, og, gf, gb, ma, mb = jnp.split(w_in, np.cumsum(splits)[:-1].tolist(), axis=1)
    gpad = jnp.zeros((D_MODEL, PR - C_G - 2 * GATE_RANK), w_in.dtype)
    grp = lambda t, g: t[:, g * A_GW:(g + 1) * A_GW]
    a_cols = [grp(t, g) for g in range(N_DIL) for t in (qa, ka, va)]
    w_all = jnp.concatenate(a_cols + [ma, mb, qb, kb, vb, og, gf, gb, gpad], axis=1).astype(BF16)
    kw = GLA_H * GLA_DK
    wgf = jnp.zeros((128, kw), F32).at[:GATE_RANK].set(w_gate_f).astype(BF16)
    wgb = jnp.zeros((128, kw), F32).at[GATE_RANK:2 * GATE_RANK].set(w_gate_b).astype(BF16)
    wr = jnp.zeros((D_MODEL, 128), F32)
    wr = wr.at[:, :N_EXPERTS].set(w_re).at[:, N_EXPERTS:2 * N_EXPERTS].set(jnp.repeat(w_rg, EPG, axis=1))
    br = jnp.zeros((1, 128), F32)
    br = br.at[0, :N_EXPERTS].set(b_re).at[0, N_EXPERTS:2 * N_EXPERTS].set(jnp.repeat(b_rg, EPG))
    return dict(
        norm_mix=norm_mix.reshape(1, D_MODEL), w_all=w_all, rel_bias=rel_bias,
        wgf=wgf, wgb=wgb, bgf=b_gate_f.reshape(1, kw), bgb=b_gate_b.reshape(1, kw),
        gla_norm=gla_norm.reshape(1, GLA_DV),
        wpa=w_proj_a.astype(BF16), wpb=w_proj_b.astype(BF16), wo=w_out.astype(BF16),
        norm_moe=norm_moe.reshape(1, D_MODEL), wr=wr.astype(BF16), br=br,
        wgu=jnp.concatenate([w_eg, w_eu], axis=-1).astype(BF16), wd=w_ed.astype(BF16),
        norm_final=norm_final.reshape(1, D_MODEL),
    )


def _trunk(x, w):
    B, S, _ = x.shape
    T = B * S
    x2d = x.reshape(T, D_MODEL)
    *qkv, proj = _inproj(x, w["norm_mix"], w["w_all"])
    o_list, lse_list = [], []
    for g, (_, d) in enumerate(DIL_PAIRS):
        L = S // d
        tiles = _bias_tiles(w["rel_bias"], g, d, min(2 * Q_BLK, L))
        o, lse = _attn_group(qkv[g], tiles, d, B, S)
        o_list.append(o.reshape(T, A_GW))
        lse_list.append(lse.reshape(T, A_GW))
    ob = _gla(proj, w["wgf"], w["wgb"], w["bgf"], w["bgb"], w["gla_norm"], B, S)
    x1, hm, gate, gatet = _outproj(x2d, proj, o_list, lse_list, ob.reshape(T, D_MODEL), w["wpa"], w["wpb"],
                                   w["wo"], w["norm_moe"], w["wr"], w["br"])
    y = _moe(hm, gate, gatet, x1, w["wgu"], w["wd"], w["norm_final"])
    return y.reshape(B, S, D_MODEL)


def kernel(x_prompt, x_sample, norm_mix, w_in, rel_bias, w_gate_f, b_gate_f, w_gate_b, b_gate_b, gla_norm,
           w_proj_a, w_proj_b, w_out, norm_moe, w_router_group, b_router_group, w_router_expert,
           b_router_expert, w_exp_gate, w_exp_up, w_exp_down, norm_final):
    w = _prep_weights(norm_mix[0], w_in[0], rel_bias, w_gate_f[0], b_gate_f[0], w_gate_b[0], b_gate_b[0],
                      gla_norm[0], w_proj_a[0], w_proj_b[0], w_out[0], norm_moe[0], w_router_group[0],
                      b_router_group[0], w_router_expert[0], b_router_expert[0], w_exp_gate[0], w_exp_up[0],
                      w_exp_down[0], norm_final)
    return (_trunk(x_prompt, w), _trunk(x_sample, w))
```

```python
import functools
import math

import numpy as np
import jax
import jax.numpy as jnp
from jax import lax
from jax.experimental import pallas as pl
from jax.experimental.pallas import tpu as pltpu

F32 = jnp.float32
BF16 = jnp.bfloat16

D_MODEL = 1024
EPS = 1e-6
NEG_INF = -1e30

DIL_PAIRS = ((128, 1), (512, 4), (2048, 16))
A_HPG = 4
A_DH = 64
A_GW = A_HPG * A_DH
A_QKV = 3 * A_GW
HALF_WIN = 64
Q_BLK = 128
N_REL_BUCKETS = 32
REL_MAX_DISTANCE = 1024
GLA_H = 4
GLA_DK = 128
GLA_DV = 256
GATE_RANK = 16
GATE_TAU = 16.0
GLA_CHUNK = 64
N_GROUPS = 4
EPG = 8
N_EXPERTS = 32
EXPERT_FF = 256

N_DIL = len(DIL_PAIRS)
C_MA, C_MB = 0, 1024
C_QB, C_KB, C_VB, C_OG = 2048, 2560, 3072, 4096
C_G = 5120
PR = 5376
PROJ_TN = A_QKV

VMEM_LIMIT = 58 * 1024 * 1024


def _cparams(sem):
    return pltpu.CompilerParams(dimension_semantics=sem, vmem_limit_bytes=VMEM_LIMIT)


def _inproj_kernel(x_ref, g_ref, w_ref, a0_ref, a1_ref, a2_ref, or_ref, y_scr):
    x = x_ref[...]
    r = x * lax.rsqrt(jnp.mean(x * x, axis=-1, keepdims=True) + EPS)
    h = (r * g_ref[...]).astype(BF16)
    nct = PROJ_TN // 128
    for j in range(N_DIL + PR // PROJ_TN):
        y = jnp.dot(h, w_ref[:, j * PROJ_TN:(j + 1) * PROJ_TN], preferred_element_type=F32)
        if j >= N_DIL:
            or_ref[:, (j - N_DIL) * PROJ_TN:(j - N_DIL + 1) * PROJ_TN] = y.astype(BF16)
            continue
        a_ref = (a0_ref, a1_ref, a2_ref)[j]
        d = DIL_PAIRS[j][1]
        if d == 1:
            a_ref[0, 0] = y.astype(BF16)
            continue
        for c in range(nct):
            y_scr[j - 1, c] = y[:, c * 128:(c + 1) * 128]
        n = y.shape[0] // d
        for cls in range(d):
            cols = [y_scr[j - 1, c, pl.ds(cls, n, stride=d), :] for c in range(nct)]
            a_ref[0, cls] = jnp.concatenate(cols, axis=1).astype(BF16)


def _inproj(x, norm_g, w_all, tm=512):
    B, S, _ = x.shape
    T = B * S
    spb = S // tm
    a_specs, a_shapes = [], []
    for _, d in DIL_PAIRS:
        a_specs.append(pl.BlockSpec((1, d, tm // d, PROJ_TN), lambda i: (i // spb, 0, i % spb, 0)))
        a_shapes.append(jax.ShapeDtypeStruct((B, d, S // d, PROJ_TN), BF16))
    return pl.pallas_call(
        _inproj_kernel,
        grid=(T // tm,),
        in_specs=[
            pl.BlockSpec((tm, D_MODEL), lambda i: (i, 0)),
            pl.BlockSpec((1, D_MODEL), lambda i: (0, 0)),
            pl.BlockSpec(w_all.shape, lambda i: (0, 0), pipeline_mode=pl.Buffered(1)),
        ],
        out_specs=a_specs + [pl.BlockSpec((tm, PR), lambda i: (i, 0))],
        out_shape=a_shapes + [jax.ShapeDtypeStruct((T, PR), BF16)],
        scratch_shapes=[pltpu.VMEM((N_DIL - 1, PROJ_TN // 128, tm, 128), F32)],
        compiler_params=_cparams(("parallel",)),
        name="inproj",
    )(x.reshape(T, D_MODEL), norm_g, w_all)


ATTN_ITEMS = 4


def _attn_kernel(q_ref, k_ref, v_ref, bias_ref, o_ref, lse_ref, *, L, wk, nc, nq, nblk):
    i = pl.program_id(2)
    lane = lax.broadcasted_iota(jnp.int32, (Q_BLK, A_GW), 1)
    qscale = jnp.asarray(A_DH ** -0.5, BF16)
    nt = (((1,), (1,)), ((), ()))
    items = [(c, t) for c in range(nc) for t in range(nq)]
    heads = [slice(h * A_DH, (h + 1) * A_DH) for h in range(A_HPG)]
    vws, scores = [], []
    for c, t in items:
        blk = i * nq + t
        start = pl.multiple_of(jnp.clip(blk * Q_BLK - HALF_WIN, 0, L - wk), HALF_WIN)
        var = jnp.where(blk == 0, 0, jnp.where(blk == nblk - 1, 2, 1))
        q = q_ref[0, c, t * Q_BLK:(t + 1) * Q_BLK, :] * qscale
        kw = k_ref[0, c, pl.ds(start, wk), :]
        vws.append(v_ref[0, c, pl.ds(start, wk), :])
        scores.append([lax.dot_general(q[:, sl], kw[:, sl], nt, preferred_element_type=F32) + bias_ref[var, h]
                       for h, sl in enumerate(heads)])
    maxes = [[jnp.max(s, axis=-1, keepdims=True) for s in item] for item in scores]
    probs = [[jnp.exp(s - m) for s, m in zip(si, mi)] for si, mi in zip(scores, maxes)]
    sums = [[jnp.sum(p, axis=-1, keepdims=True) for p in item] for item in probs]
    for (c, t), vw, pi, mi, li in zip(items, vws, probs, maxes, sums):
        outs = [jnp.dot(p.astype(BF16), vw[:, sl], preferred_element_type=F32) / l
                for p, l, sl in zip(pi, li, heads)]
        lse_tile = jnp.zeros((Q_BLK, A_GW), F32)
        for h, (m, l) in enumerate(zip(mi, li)):
            lse_tile = jnp.where((lane >= A_DH * h) & (lane < A_DH * (h + 1)), m + jnp.log(l), lse_tile)
        rows = slice(t * Q_BLK, (t + 1) * Q_BLK)
        cols = slice(c * A_GW, (c + 1) * A_GW)
        o_ref[0, rows, cols] = jnp.concatenate(outs, axis=1).astype(BF16)
        lse_ref[0, rows, cols] = lse_tile


def _attn_group(qkv, bias_tiles, d, B, S):
    L = S // d
    nblk = L // Q_BLK
    wk = min(2 * Q_BLK, L)
    nq = min(ATTN_ITEMS, nblk)
    nc = ATTN_ITEMS // nq

    return pl.pallas_call(
        functools.partial(_attn_kernel, L=L, wk=wk, nc=nc, nq=nq, nblk=nblk),
        grid=(B, d // nc, nblk // nq),
        in_specs=[
            pl.BlockSpec((1, nc, nq * Q_BLK, A_GW), lambda b, r, i: (b, r, i, 0)),
            pl.BlockSpec((1, nc, L, A_GW), lambda b, r, i: (b, r, 0, 1)),
            pl.BlockSpec((1, nc, L, A_GW), lambda b, r, i: (b, r, 0, 2)),
            pl.BlockSpec(bias_tiles.shape, lambda b, r, i: (0, 0, 0, 0)),
        ],
        out_specs=[
            pl.BlockSpec((1, nq * Q_BLK, nc * A_GW), lambda b, r, i: (b, i, r)),
            pl.BlockSpec((1, nq * Q_BLK, nc * A_GW), lambda b, r, i: (b, i, r)),
        ],
        out_shape=[
            jax.ShapeDtypeStruct((B, L, d * A_GW), BF16),
            jax.ShapeDtypeStruct((B, L, d * A_GW), F32),
        ],
        compiler_params=_cparams(("parallel", "parallel", "arbitrary")),
        name=f"attn_d{d}",
    )(qkv, qkv, qkv, bias_tiles)


def _t5_buckets(rel):
    nb = N_REL_BUCKETS // 2
    max_exact = nb // 2
    ret = (rel > 0).astype(np.int64) * nb
    n = np.abs(rel)
    large = max_exact + (np.log(np.maximum(n, 1) / max_exact) / math.log(REL_MAX_DISTANCE / max_exact)
                         * (nb - max_exact)).astype(np.int64)
    large = np.minimum(large, nb - 1)
    return (ret + np.where(n < max_exact, n, large)).astype(np.int32)


def _bias_tiles(rel_bias, g, d, wk):
    rel = d * np.arange(-HALF_WIN, HALF_WIN + 1)
    bucket = _t5_buckets(rel)
    bias = rel_bias[jnp.asarray(bucket)][:, g * A_HPG:(g + 1) * A_HPG].T.astype(F32)
    pad = wk + Q_BLK
    neg = jnp.full((A_HPG, pad), NEG_INF, F32)
    val = jnp.concatenate([neg, bias, neg], axis=1)
    zero = pad + HALF_WIN
    m = wk + Q_BLK
    tiles = []
    for off in (0, HALF_WIN, wk - Q_BLK):
        u = jnp.concatenate([val[:, zero - off:zero - off + wk], val[:, zero - off - Q_BLK:zero - off]], axis=1)
        flat = jnp.tile(u, (1, Q_BLK))[:, :Q_BLK * (m - 1)]
        tiles.append(flat.reshape(A_HPG, Q_BLK, m - 1)[:, :, :wk])
    return jnp.stack(tiles)


def _log_sigmoid(z):
    return jnp.minimum(z, 0.0) - jnp.log(1.0 + jnp.exp(-jnp.abs(z)))


def _gla_kernel(q_ref, k_ref, v_ref, og_ref, g_ref, wgf_ref, wgb_ref, bgf_ref, bgb_ref, gn_ref, o_ref,
                qif, kof, qib, kob, vt, etf, etb, acc, *, S):
    PB = 256
    C = GLA_CHUNK
    NB = S // PB
    rr = lax.broadcasted_iota(jnp.int32, (PB, PB), 0)
    cc = lax.broadcasted_iota(jnp.int32, (PB, PB), 1)
    same = (rr >> 6) == (cc >> 6)
    mask_f = same & (cc <= rr)
    mask_b = same & (cc > rr)
    tl = jnp.where(mask_f, 1.0, 0.0).astype(BF16)
    qscale = GLA_DK ** -0.5
    inv_tau = 1.0 / GATE_TAU
    nt = (((1,), (1,)), ((), ()))

    def split3(x):
        hi = x.astype(BF16)
        r1 = x - hi.astype(F32)
        mid = r1.astype(BF16)
        lo = (r1 - mid.astype(F32)).astype(BF16)
        return [hi, mid, lo]

    def chunk_total(b):
        b4 = b.reshape(PB // C, C, GLA_DK)
        return jnp.broadcast_to(b4[:, C - 1:C, :], b4.shape).reshape(PB, GLA_DK)

    PREP_BLOCKS = 4

    def prep(t, carry):
        blocks = [t * PREP_BLOCKS + u for u in range(PREP_BLOCKS)]
        rows = [pl.ds(pl.multiple_of(i * PB, PB), PB) for i in blocks]
        gs = [g_ref[0, r, :] for r in rows]
        lfs = [_log_sigmoid(jnp.dot(g, wgf_ref[...], preferred_element_type=F32) + bgf_ref[...]) * inv_tau for g in gs]
        lbs = [_log_sigmoid(jnp.dot(g, wgb_ref[...], preferred_element_type=F32) + bgb_ref[...]) * inv_tau for g in gs]
        css = [jnp.dot(tl, jnp.concatenate(split3(lf) + split3(lb), axis=1), preferred_element_type=F32)
               for lf, lb in zip(lfs, lbs)]
        bfs = [cs[:, 0:128] + cs[:, 128:256] + cs[:, 256:384] for cs in css]
        pbs = [cs[:, 384:512] + cs[:, 512:640] + cs[:, 640:768] for cs in css]
        totfs = [chunk_total(bf) for bf in bfs]
        totbs = [chunk_total(pb) for pb in pbs]
        bss = [totb - pb + lb for totb, pb, lb in zip(totbs, pbs, lbs)]
        qs = [q_ref[0, r, :].astype(F32) * qscale for r in rows]
        ks = [k_ref[0, r, :].astype(F32) for r in rows]
        qfs = [(q * jnp.exp(bf)).astype(BF16) for q, bf in zip(qs, bfs)]
        qbs = [(q * jnp.exp(bs)).astype(BF16) for q, bs in zip(qs, bss)]
        kfs = [(k * jnp.exp(-bf)).astype(BF16) for k, bf in zip(ks, bfs)]
        kbs = [(k * jnp.exp(-bs)).astype(BF16) for k, bs in zip(ks, bss)]
        sfs = [lax.dot_general(qf, kf, nt, preferred_element_type=F32) for qf, kf in zip(qfs, kfs)]
        sbs = [lax.dot_general(qb, kb, nt, preferred_element_type=F32) for qb, kb in zip(qbs, kbs)]
        ps = [jnp.where(mask_f, sf, jnp.where(mask_b, sb, 0.0)).astype(BF16) for sf, sb in zip(sfs, sbs)]
        vs = [v_ref[0, r, :] for r in rows]
        for r, p, v in zip(rows, ps, vs):
            acc[r, :] = jnp.dot(p, v, preferred_element_type=F32)
        for r, qf, qb in zip(rows, qfs, qbs):
            qif[r, :] = qf
            qib[r, :] = qb
        for r, k, totf, bf, totb, bs in zip(rows, ks, totfs, bfs, totbs, bss):
            kof[r, :] = (k * jnp.exp(totf - bf)).astype(BF16)
            kob[r, :] = (k * jnp.exp(totb - bs)).astype(BF16)
            etf[r, :] = jnp.exp(totf)
            etb[r, :] = jnp.exp(totb)
        for i, v in zip(blocks, vs):
            vtb = v.astype(F32).T.astype(BF16)
            vt[2 * i] = vtb[:, :128]
            vt[2 * i + 1] = vtb[:, 128:]
        return carry

    lax.fori_loop(0, NB // PREP_BLOCKS, prep, 0)

    zeros_half = jnp.zeros((C, GLA_DK), BF16)

    SER_BLOCKS = 4
    NJ = PB // C

    def chunk_rows(blk, j):
        return pl.multiple_of(blk * PB + j * C, C)

    def state_update(ko_ref, blk, j):
        ko = ko_ref[pl.ds(chunk_rows(blk, j), C), :]
        ko_pad = jnp.concatenate([ko, zeros_half] if j % 2 == 0 else [zeros_half, ko], axis=0)
        return jnp.dot(vt[2 * blk + j // 2], ko_pad, preferred_element_type=F32)

    def advance(qi_ref, et_ref, blk, j, state, upd):
        r0 = chunk_rows(blk, j)
        rows = pl.ds(r0, C)
        acc[rows, :] += lax.dot_general(qi_ref[rows, :], state.astype(BF16), nt, preferred_element_type=F32)
        return state * et_ref[pl.ds(r0, 1), :] + upd

    def serial(t, carry):
        st_f, st_b = carry
        steps_f = [(t * SER_BLOCKS + u, j) for u in range(SER_BLOCKS) for j in range(NJ)]
        steps_b = [(NB - 1 - t * SER_BLOCKS - u, NJ - 1 - j) for u in range(SER_BLOCKS) for j in range(NJ)]
        upd_f = [state_update(kof, blk, j) for blk, j in steps_f]
        upd_b = [state_update(kob, blk, j) for blk, j in steps_b]
        for (bf, jf), (bb, jb), uf, ub in zip(steps_f, steps_b, upd_f, upd_b):
            st_f = advance(qif, etf, bf, jf, st_f, uf)
            st_b = advance(qib, etb, bb, jb, st_b, ub)
        return st_f, st_b

    zero_state = jnp.zeros((GLA_DV, GLA_DK), F32)
    lax.fori_loop(0, NB // SER_BLOCKS, serial, (zero_state, zero_state))

    gn = gn_ref[...]

    def finish(i, carry):
        rows = pl.ds(pl.multiple_of(i * PB, PB), PB)
        tot = acc[rows, :]
        nrm = tot * lax.rsqrt(jnp.mean(tot * tot, axis=-1, keepdims=True) + EPS) * gn
        og = og_ref[0, rows, :].astype(F32)
        o_ref[0, rows, :] = (nrm * (og * jax.nn.sigmoid(og))).astype(BF16)
        return carry

    lax.fori_loop(0, NB, finish, 0)


def _gla(proj, wgf, wgb, bgf, bgb, gla_norm, B, S):
    pv = proj.reshape(B, S, PR)
    cq, ck, cv, cog, cg = C_QB // GLA_DK, C_KB // GLA_DK, C_VB // GLA_DV, C_OG // GLA_DV, C_G // 128
    return pl.pallas_call(
        functools.partial(_gla_kernel, S=S),
        grid=(B, GLA_H),
        in_specs=[
            pl.BlockSpec((1, S, GLA_DK), lambda b, h: (b, 0, cq + h)),
            pl.BlockSpec((1, S, GLA_DK), lambda b, h: (b, 0, ck + h)),
            pl.BlockSpec((1, S, GLA_DV), lambda b, h: (b, 0, cv + h)),
            pl.BlockSpec((1, S, GLA_DV), lambda b, h: (b, 0, cog + h)),
            pl.BlockSpec((1, S, 128), lambda b, h: (b, 0, cg)),
            pl.BlockSpec((128, GLA_DK), lambda b, h: (0, h)),
            pl.BlockSpec((128, GLA_DK), lambda b, h: (0, h)),
            pl.BlockSpec((1, GLA_DK), lambda b, h: (0, h)),
            pl.BlockSpec((1, GLA_DK), lambda b, h: (0, h)),
            pl.BlockSpec((1, GLA_DV), lambda b, h: (0, 0)),
        ],
        out_specs=pl.BlockSpec((1, S, GLA_DV), lambda b, h: (b, 0, h)),
        out_shape=jax.ShapeDtypeStruct((B, S, GLA_H * GLA_DV), BF16),
        scratch_shapes=[
            pltpu.VMEM((S, GLA_DK), BF16), pltpu.VMEM((S, GLA_DK), BF16),
            pltpu.VMEM((S, GLA_DK), BF16), pltpu.VMEM((S, GLA_DK), BF16),
            pltpu.VMEM((S // 128, GLA_DV, 128), BF16),
            pltpu.VMEM((S, GLA_DK), F32), pltpu.VMEM((S, GLA_DK), F32),
            pltpu.VMEM((S, GLA_DV), F32),
        ],
        compiler_params=_cparams(("parallel", "arbitrary")),
        name="gla",
    )(pv, pv, pv, pv, pv, wgf, wgb, bgf, bgb, gla_norm)


OUTPROJ_SPLIT = 2


def _outproj_kernel(x_ref, ma_ref, mb_ref, o0_ref, o1_ref, o2_ref, l0_ref, l1_ref, l2_ref, ob_ref,
                    wpa_ref, wpb_ref, wo_ref, nm_ref, wr_ref, br_ref,
                    x1_ref, hm_ref, gate_ref, gatet_ref):
    tm = x_ref.shape[0] // OUTPROJ_SPLIT
    parts = [slice(p * tm, (p + 1) * tm) for p in range(OUTPROJ_SPLIT)]

    def merge_groups(rs):
        l0, l1, l2 = l0_ref[rs, :], l1_ref[rs, :], l2_ref[rs, :]
        m = jnp.maximum(jnp.maximum(l0, l1), l2)
        e0, e1, e2 = jnp.exp(l0 - m), jnp.exp(l1 - m), jnp.exp(l2 - m)
        den = e0 + e1 + e2
        oa = ((e0 / den) * o0_ref[rs, :].astype(F32) + (e1 / den) * o1_ref[rs, :].astype(F32)
              + (e2 / den) * o2_ref[rs, :].astype(F32))
        return oa.astype(BF16)

    def project(rs, oa):
        ya = jnp.dot(oa, wpa_ref[...], preferred_element_type=F32)
        yb = jnp.dot(ob_ref[rs, :], wpb_ref[...], preferred_element_type=F32)
        return ya, yb

    def gate_mix(rs, y):
        ya, yb = y
        mix = jax.nn.sigmoid(ma_ref[rs, :].astype(F32)) * ya + jax.nn.sigmoid(mb_ref[rs, :].astype(F32)) * yb
        return mix.astype(BF16)

    def residual(rs, mix):
        x1 = x_ref[rs, :] + jnp.dot(mix, wo_ref[...], preferred_element_type=F32)
        x1_ref[rs, :] = x1
        return x1

    def moe_norm(rs, x1):
        hm = (x1 * lax.rsqrt(jnp.mean(x1 * x1, axis=-1, keepdims=True) + EPS) * nm_ref[...]).astype(BF16)
        hm_ref[rs, :] = hm
        return hm

    def router_logits(rs, hm):
        return jnp.dot(hm, wr_ref[...], preferred_element_type=F32) + br_ref[...]

    def router(rs, logits):
        gate = _route(logits)
        gate_ref[rs, :] = gate
        gatet_ref[:, rs] = gate.T

    stages = [lambda rs, _: merge_groups(rs), project, gate_mix, residual, moe_norm, router_logits, router]
    vals = [None] * OUTPROJ_SPLIT
    for step in range(len(stages) + OUTPROJ_SPLIT - 1):
        for p, rs in enumerate(parts):
            if 0 <= step - p < len(stages):
                vals[p] = stages[step - p](rs, vals[p])


def _route(logits):
    tm = logits.shape[0]
    lane = lax.broadcasted_iota(jnp.int32, (tm, 128), 1)
    is_g = (lane >= N_EXPERTS) & (lane < 2 * N_EXPERTS)
    is_e = lane < N_EXPERTS
    grp_of_lane = jnp.where(is_g, (lane - N_EXPERTS) >> 3, lane >> 3)
    gl = jnp.where(is_g, logits, NEG_INF)
    gmax = jnp.max(gl, axis=-1, keepdims=True)
    gsum = jnp.sum(jnp.where(is_g, jnp.exp(gl - gmax), 0.0), axis=-1, keepdims=True) * (1.0 / EPG)
    pg_top = 1.0 / gsum
    g_sel = jnp.min(jnp.where(is_g & (gl == gmax), grp_of_lane, N_GROUPS), axis=-1, keepdims=True)
    in_grp = is_e & (grp_of_lane == g_sel)
    el = jnp.where(in_grp, logits, NEG_INF)
    emax = jnp.max(el, axis=-1, keepdims=True)
    ee = jnp.where(in_grp, jnp.exp(el - emax), 0.0)
    pe = ee / jnp.sum(ee, axis=-1, keepdims=True)
    p1 = jnp.max(pe, axis=-1, keepdims=True)
    i1 = jnp.min(jnp.where(in_grp & (pe == p1), lane, 128), axis=-1, keepdims=True)
    rest = in_grp & (lane != i1)
    pe2 = jnp.where(rest, pe, -1.0)
    p2 = jnp.max(pe2, axis=-1, keepdims=True)
    i2 = jnp.min(jnp.where(rest & (pe2 == p2), lane, 128), axis=-1, keepdims=True)
    psum = p1 + p2
    return jnp.where(lane == i1, pg_top * (p1 / psum), jnp.where(lane == i2, pg_top * (p2 / psum), 0.0))


def _outproj(x2d, proj, o_list, lse_list, ob2d, wpa, wpb, wo, norm_moe, wr, br, tm=512):
    T = x2d.shape[0]
    row = lambda w: pl.BlockSpec((tm, w), lambda i: (i, 0))
    full = lambda a: pl.BlockSpec(a.shape, lambda i: (0,) * a.ndim)
    return pl.pallas_call(
        _outproj_kernel,
        grid=(T // tm,),
        in_specs=[
            row(D_MODEL),
            pl.BlockSpec((tm, D_MODEL), lambda i: (i, C_MA // D_MODEL)),
            pl.BlockSpec((tm, D_MODEL), lambda i: (i, C_MB // D_MODEL)),
            row(A_GW), row(A_GW), row(A_GW), row(A_GW), row(A_GW), row(A_GW),
            row(D_MODEL),
            full(wpa), full(wpb), full(wo), full(norm_moe), full(wr), full(br),
        ],
        out_specs=[row(D_MODEL), row(D_MODEL), row(128), pl.BlockSpec((128, tm), lambda i: (0, i))],
        out_shape=[
            jax.ShapeDtypeStruct((T, D_MODEL), F32),
            jax.ShapeDtypeStruct((T, D_MODEL), BF16),
            jax.ShapeDtypeStruct((T, 128), F32),
            jax.ShapeDtypeStruct((128, T), F32),
        ],
        compiler_params=_cparams(("parallel",)),
        name="outproj",
    )(x2d, proj, proj, *o_list, *lse_list, ob2d, wpa, wpb, wo, norm_moe, wr, br)


MOE_TB = 1024
MOE_Q = 4
MOE_SB = MOE_TB // MOE_Q
MOE_ALIGN = 16
MOE_CH = 128
MOE_W = MOE_CH // MOE_Q
MOE_RQ = 2 * MOE_SB + N_EXPERTS * MOE_ALIGN
MOE_RQS = MOE_RQ + MOE_W
MOE_SUB = 256
MOE_G = 4


def _moe_route(hm_ref, gate_ref, gatet_ref, xs, ys, pt, wrow, rinfo):
    TB, Q, SB, RQ, RQS, SUB = MOE_TB, MOE_Q, MOE_SB, MOE_RQ, MOE_RQS, MOE_SUB
    big = float(4 * TB)
    gate = gate_ref[...]
    gt = gatet_ref[...]
    a = gate > 0.0
    at = gt > 0.0
    a_f = jnp.where(a, 1.0, 0.0)
    at_f = jnp.where(at, 1.0, 0.0)
    a_b = a_f.astype(BF16)
    at_b = at_f.astype(BF16)

    def pad(c):
        return jnp.floor((c + (MOE_ALIGN - 1)) * (1.0 / MOE_ALIGN)) * MOE_ALIGN

    er = lax.broadcasted_iota(jnp.int32, (128, 128), 0)
    ec = lax.broadcasted_iota(jnp.int32, (128, 128), 1)
    sub8 = lax.broadcasted_iota(jnp.int32, (8, 128), 0)
    pad_rows = jnp.zeros((8, 128), F32)
    pad_cols = jnp.zeros((128, 128), F32)
    for q in range(Q):
        blk = slice(q * SB, (q + 1) * SB)
        pad_rows = jnp.where(sub8 == q, pad(jnp.sum(a_f[blk], axis=0, keepdims=True)), pad_rows)
        pad_cols = jnp.where(ec == q, pad(jnp.sum(at_f[:, blk], axis=1, keepdims=True)), pad_cols)
    off_rows = jnp.dot(pad_rows.astype(BF16), jnp.where(er < ec, 1.0, 0.0).astype(BF16), preferred_element_type=F32)
    off_cols = jnp.dot(jnp.where(ec < er, 1.0, 0.0).astype(BF16), pad_cols.astype(BF16), preferred_element_type=F32)

    tr = lax.broadcasted_iota(jnp.int32, (SUB, SUB), 0)
    tc = lax.broadcasted_iota(jnp.int32, (SUB, SUB), 1)
    tri_l = jnp.where(tc < tr, 1.0, 0.0).astype(BF16)
    tri_u = jnp.where(tr < tc, 1.0, 0.0).astype(BF16)
    rl = lax.broadcasted_iota(jnp.int32, (SUB, RQ), 1).astype(F32)
    assert SB == SUB
    subs = [slice(q * SB, (q + 1) * SB) for q in range(Q)]
    pos = [off_rows[q:q + 1] + jnp.dot(tri_l, a_b[b], preferred_element_type=F32) for q, b in enumerate(subs)]
    plo = [jnp.min(jnp.where(a[b], p, big), axis=1, keepdims=True) for b, p in zip(subs, pos)]
    phi = [jnp.max(jnp.where(a[b], p, -1.0), axis=1, keepdims=True) for b, p in zip(subs, pos)]
    phi = [jnp.where(hi == lo, -1.0, hi) for lo, hi in zip(plo, phi)]
    for b, lo, hi in zip(subs, plo, phi):
        pt[b, :] = jnp.where((rl == lo) | (rl == hi), 1.0, 0.0).astype(BF16)
    pos_t = [off_cols[:, q:q + 1] + jnp.dot(at_b[:, b], tri_u, preferred_element_type=F32)
             for q, b in enumerate(subs)]
    plo = [jnp.min(jnp.where(at[:, b], p, big), axis=0, keepdims=True) for b, p in zip(subs, pos_t)]
    phi = [jnp.max(jnp.where(at[:, b], p, -1.0), axis=0, keepdims=True) for b, p in zip(subs, pos_t)]
    phi = [jnp.where(hi == lo, -1.0, hi) for lo, hi in zip(plo, phi)]
    wlo = [jnp.sum(jnp.where(at[:, b] & (p == lo), gt[:, b], 0.0), axis=0, keepdims=True)
           for b, p, lo in zip(subs, pos_t, plo)]
    whi = [jnp.sum(jnp.where(at[:, b] & (p == hi), gt[:, b], 0.0), axis=0, keepdims=True)
           for b, p, hi in zip(subs, pos_t, phi)]
    for b, lo, hi, wl, wh in zip(subs, plo, phi, wlo, whi):
        rinfo[0:1, b] = lo
        rinfo[1:2, b] = hi
        rinfo[2:3, b] = wl
        rinfo[3:4, b] = wh

    blks = [slice(q * SB, (q + 1) * SB) for q in range(Q)]

    def gather(k, carry):
        r0 = pl.multiple_of(k * SUB, SUB)
        ri = (lax.broadcasted_iota(jnp.int32, (SUB, SB), 0) + r0).astype(F32)
        mlo = [ri == rinfo[0:1, b] for b in blks]
        mhi = [ri == rinfo[1:2, b] for b in blks]
        p = [jnp.where(lo | hi, 1.0, 0.0).astype(BF16) for lo, hi in zip(mlo, mhi)]
        x = [jnp.dot(pq, hm_ref[b, :], preferred_element_type=F32).astype(BF16) for pq, b in zip(p, blks)]
        w = [jnp.sum(jnp.where(lo, rinfo[2:3, b], 0.0) + jnp.where(hi, rinfo[3:4, b], 0.0), axis=1, keepdims=True)
             for lo, hi, b in zip(mlo, mhi, blks)]
        for q in range(Q):
            dst = pl.ds(pl.multiple_of(q * RQS + r0, MOE_ALIGN), SUB)
            xs[dst, :] = x[q]
            wrow[dst, :] = jnp.broadcast_to(w[q], (SUB, 128))
        return carry

    lax.fori_loop(0, RQ // SUB, gather, 0)
    for q in range(Q):
        xs[q * RQS + RQ:(q + 1) * RQS, :] = jnp.zeros((RQS - RQ, D_MODEL), BF16)
        wrow[q * RQS + RQ:(q + 1) * RQS, :] = jnp.zeros((RQS - RQ, 128), F32)
    ys[...] = jnp.zeros_like(ys)


def _moe_kernel(meta_ref, hm_ref, gate_ref, gatet_ref, x1_ref, wgu_ref, wd_ref, nf_ref, y_ref,
                xs, ys, pt, wrow, rinfo):
    i = pl.program_id(0)
    s = pl.program_id(1)
    Q, W = MOE_Q, MOE_W

    @pl.when(s == 0)
    def _():
        _moe_route(hm_ref, gate_ref, gatet_ref, xs, ys, pt, wrow, rinfo)

    offs = [[meta_ref[i, q * N_EXPERTS + s * MOE_G + g] + q * MOE_RQS for q in range(Q)] for g in range(MOE_G)]
    pads = [[meta_ref[i, (Q + q) * N_EXPERTS + s * MOE_G + g] for q in range(Q)] for g in range(MOE_G)]

    def up_proj(g, x):
        return jnp.dot(x, wgu_ref[g], preferred_element_type=F32)

    def activate(gu, w):
        gt, up = gu[:, :EXPERT_FF], gu[:, EXPERT_FF:]
        act = (gt * jax.nn.sigmoid(gt)) * up
        return jnp.concatenate([act[:, :128] * w, act[:, 128:] * w], axis=1).astype(BF16)

    def down_proj(g, act):
        return jnp.dot(act, wd_ref[g], preferred_element_type=F32).astype(BF16)

    def mlp(g, x, w):
        return down_proj(g, activate(up_proj(g, x), w))

    single = pads[0][0] <= W
    for g in range(MOE_G):
        for q in range(Q):
            single = jnp.logical_and(single, pads[g][q] <= W)

    @pl.when(single)
    def _():
        rows = [[pl.ds(pl.multiple_of(offs[g][q], MOE_ALIGN), W) for q in range(Q)] for g in range(MOE_G)]
        x = [jnp.concatenate([xs[r, :] for r in rg], axis=0) for rg in rows]
        w = [jnp.concatenate([wrow[r, :] for r in rg], axis=0) for rg in rows]
        gu = [up_proj(g, x[g]) for g in range(MOE_G)]
        act = [activate(gu[g], w[g]) for g in range(MOE_G)]
        y = [down_proj(g, act[g]) for g in range(MOE_G)]
        for g in range(MOE_G):
            for q, r in enumerate(rows[g]):
                ys[r, :] = y[g][q * W:(q + 1) * W]

    @pl.when(jnp.logical_not(single))
    def _():
        for g in range(MOE_G):
            for q in range(Q):
                def body(c, carry, g=g, q=q):
                    r = pl.ds(pl.multiple_of(offs[g][q] + c * W, MOE_ALIGN), W)
                    ys[r, :] = mlp(g, xs[r, :], wrow[r, :])
                    return carry
                lax.fori_loop(0, (pads[g][q] + (W - 1)) // W, body, 0)

    @pl.when(s == N_EXPERTS // MOE_G - 1)
    def _():
        blks = [slice(q * MOE_SB, (q + 1) * MOE_SB) for q in range(Q)]
        moe = [jnp.dot(pt[b, :], ys[q * MOE_RQS:q * MOE_RQS + MOE_RQ, :], preferred_element_type=F32)
               for q, b in enumerate(blks)]
        x2 = [x1_ref[b, :] + m for b, m in zip(blks, moe)]
        for b, x in zip(blks, x2):
            y_ref[b, :] = x * lax.rsqrt(jnp.mean(x * x, axis=-1, keepdims=True) + EPS) * nf_ref[...]


def _moe(hm, gate, gatet, x1, wgu, wd, norm_final):
    T = hm.shape[0]
    tb = MOE_TB
    nb = T // tb
    used = gate.reshape(nb, MOE_Q, MOE_SB, 128)[..., :N_EXPERTS] > 0.0
    cnt = jnp.sum(used.astype(jnp.int32), axis=2)
    padded = (cnt + (MOE_ALIGN - 1)) // MOE_ALIGN * MOE_ALIGN
    off = jnp.cumsum(padded, axis=2) - padded
    meta = jnp.concatenate([off.reshape(nb, -1), padded.reshape(nb, -1)], axis=1).astype(jnp.int32)
    grid_spec = pltpu.PrefetchScalarGridSpec(
        num_scalar_prefetch=1,
        grid=(nb, N_EXPERTS // MOE_G),
        in_specs=[
            pl.BlockSpec((tb, D_MODEL), lambda i, e, m: (i, 0)),
            pl.BlockSpec((tb, 128), lambda i, e, m: (i, 0)),
            pl.BlockSpec((128, tb), lambda i, e, m: (0, i)),
            pl.BlockSpec((tb, D_MODEL), lambda i, e, m: (i, 0)),
            pl.BlockSpec((MOE_G, D_MODEL, 2 * EXPERT_FF), lambda i, e, m: (e, 0, 0)),
            pl.BlockSpec((MOE_G, EXPERT_FF, D_MODEL), lambda i, e, m: (e, 0, 0)),
            pl.BlockSpec((1, D_MODEL), lambda i, e, m: (0, 0)),
        ],
        out_specs=pl.BlockSpec((tb, D_MODEL), lambda i, e, m: (i, 0)),
        scratch_shapes=[
            pltpu.VMEM((MOE_Q * MOE_RQS, D_MODEL), BF16),
            pltpu.VMEM((MOE_Q * MOE_RQS, D_MODEL), BF16),
            pltpu.VMEM((tb, MOE_RQ), BF16),
            pltpu.VMEM((MOE_Q * MOE_RQS, 128), F32),
            pltpu.VMEM((8, tb), F32),
        ],
    )
    return pl.pallas_call(
        _moe_kernel,
        grid_spec=grid_spec,
        out_shape=jax.ShapeDtypeStruct((T, D_MODEL), F32),
        compiler_params=_cparams(("parallel", "arbitrary")),
        name="moe",
    )(meta, hm, gate, gatet, x1, wgu, wd, norm_final)


def _prep_weights(norm_mix, w_in, rel_bias, w_gate_f, b_gate_f, w_gate_b, b_gate_b, gla_norm, w_proj_a,
                  w_proj_b, w_out, norm_moe, w_rg, b_rg, w_re, b_re, w_eg, w_eu, w_ed, norm_final):
    splits = (A_QKV, A_QKV, A_QKV, 512, 512, 1024, 1024, GATE_RANK, GATE_RANK, D_MODEL, D_MODEL)
    qa, ka, va, qb, kb, vb, og, gf, gb, ma, mb = jnp.split(w_in, np.cumsum(splits)[:-1].tolist(), axis=1)
    gpad = jnp.zeros((D_MODEL, PR - C_G - 2 * GATE_RANK), w_in.dtype)
    grp = lambda t, g: t[:, g * A_GW:(g + 1) * A_GW]
    a_cols = [grp(t, g) for g in range(N_DIL) for t in (qa, ka, va)]
    w_all = jnp.concatenate(a_cols + [ma, mb, qb, kb, vb, og, gf, gb, gpad], axis=1).astype(BF16)
    kw = GLA_H * GLA_DK
    wgf = jnp.zeros((128, kw), F32).at[:GATE_RANK].set(w_gate_f).astype(BF16)
    wgb = jnp.zeros((128, kw), F32).at[GATE_RANK:2 * GATE_RANK].set(w_gate_b).astype(BF16)
    wr = jnp.zeros((D_MODEL, 128), F32)
    wr = wr.at[:, :N_EXPERTS].set(w_re).at[:, N_EXPERTS:2 * N_EXPERTS].set(jnp.repeat(w_rg, EPG, axis=1))
    br = jnp.zeros((1, 128), F32)
    br = br.at[0, :N_EXPERTS].set(b_re).at[0, N_EXPERTS:2 * N_EXPERTS].set(jnp.repeat(b_rg, EPG))
    return dict(
        norm_mix=norm_mix.reshape(1, D_MODEL), w_all=w_all, rel_bias=rel_bias,
        wgf=wgf, wgb=wgb, bgf=b_gate_f.reshape(1, kw), bgb=b_gate_b.reshape(1, kw),
        gla_norm=gla_norm.reshape(1, GLA_DV),
        wpa=w_proj_a.astype(BF16), wpb=w_proj_b.astype(BF16), wo=w_out.astype(BF16),
        norm_moe=norm_moe.reshape(1, D_MODEL), wr=wr.astype(BF16), br=br,
        wgu=jnp.concatenate([w_eg, w_eu], axis=-1).astype(BF16), wd=w_ed.astype(BF16),
        norm_final=norm_final.reshape(1, D_MODEL),
    )


def _trunk(x, w):
    B, S, _ = x.shape
    T = B * S
    x2d = x.reshape(T, D_MODEL)
    *qkv, proj = _inproj(x, w["norm_mix"], w["w_all"])
    o_list, lse_list = [], []
    for g, (_, d) in enumerate(DIL_PAIRS):
        L = S // d
        tiles = _bias_tiles(w["rel_bias"], g, d, min(2 * Q_BLK, L))
        o, lse = _attn_group(qkv[g], tiles, d, B, S)
        o_list.append(o.reshape(T, A_GW))
        lse_list.append(lse.reshape(T, A_GW))
    ob = _gla(proj, w["wgf"], w["wgb"], w["bgf"], w["bgb"], w["gla_norm"], B, S)
    x1, hm, gate, gatet = _outproj(x2d, proj, o_list, lse_list, ob.reshape(T, D_MODEL), w["wpa"], w["wpb"],
                                   w["wo"], w["norm_moe"], w["wr"], w["br"])
    y = _moe(hm, gate, gatet, x1, w["wgu"], w["wd"], w["norm_final"])
    return y.reshape(B, S, D_MODEL)


def kernel(x_prompt, x_sample, norm_mix, w_in, rel_bias, w_gate_f, b_gate_f, w_gate_b, b_gate_b, gla_norm,
           w_proj_a, w_proj_b, w_out, norm_moe, w_router_group, b_router_group, w_router_expert,
           b_router_expert, w_exp_gate, w_exp_up, w_exp_down, norm_final):
    w = _prep_weights(norm_mix[0], w_in[0], rel_bias, w_gate_f[0], b_gate_f[0], w_gate_b[0], b_gate_b[0],
                      gla_norm[0], w_proj_a[0], w_proj_b[0], w_out[0], norm_moe[0], w_router_group[0],
                      b_router_group[0], w_router_expert[0], b_router_expert[0], w_exp_gate[0], w_exp_up[0],
                      w_exp_down[0], norm_final)
    return (_trunk(x_prompt, w), _trunk(x_sample, w))
```

```python
import functools
import math

import numpy as np
import jax
import jax.numpy as jnp
from jax import lax
from jax.experimental import pallas as pl
from jax.experimental.pallas import tpu as pltpu

F32 = jnp.float32
BF16 = jnp.bfloat16

D_MODEL = 1024
EPS = 1e-6
NEG_INF = -1e30

DIL_PAIRS = ((128, 1), (512, 4), (2048, 16))
A_HPG = 4
A_DH = 64
A_GW = A_HPG * A_DH
A_QKV = 3 * A_GW
HALF_WIN = 64
Q_BLK = 128
N_REL_BUCKETS = 32
REL_MAX_DISTANCE = 1024
GLA_H = 4
GLA_DK = 128
GLA_DV = 256
GATE_RANK = 16
GATE_TAU = 16.0
GLA_CHUNK = 64
N_GROUPS = 4
EPG = 8
N_EXPERTS = 32
EXPERT_FF = 256

N_DIL = len(DIL_PAIRS)
C_MA, C_MB = 0, 1024
C_QB, C_KB, C_VB, C_OG = 2048, 2560, 3072, 4096
C_G = 5120
PR = 5376
PROJ_TN = A_QKV

VMEM_LIMIT = 58 * 1024 * 1024


def _cparams(sem):
    return pltpu.CompilerParams(dimension_semantics=sem, vmem_limit_bytes=VMEM_LIMIT)


def _inproj_kernel(x_ref, g_ref, w_ref, a0_ref, a1_ref, a2_ref, or_ref, y_scr):
    x = x_ref[...]
    r = x * lax.rsqrt(jnp.mean(x * x, axis=-1, keepdims=True) + EPS)
    h = (r * g_ref[...]).astype(BF16)
    nct = PROJ_TN // 128
    for j in range(N_DIL + PR // PROJ_TN):
        y = jnp.dot(h, w_ref[:, j * PROJ_TN:(j + 1) * PROJ_TN], preferred_element_type=F32)
        if j >= N_DIL:
            or_ref[:, (j - N_DIL) * PROJ_TN:(j - N_DIL + 1) * PROJ_TN] = y.astype(BF16)
            continue
        a_ref = (a0_ref, a1_ref, a2_ref)[j]
        d = DIL_PAIRS[j][1]
        if d == 1:
            a_ref[0, 0] = y.astype(BF16)
            continue
        for c in range(nct):
            y_scr[j - 1, c] = y[:, c * 128:(c + 1) * 128]
        n = y.shape[0] // d
        for cls in range(d):
            cols = [y_scr[j - 1, c, pl.ds(cls, n, stride=d), :] for c in range(nct)]
            a_ref[0, cls] = jnp.concatenate(cols, axis=1).astype(BF16)


def _inproj(x, norm_g, w_all, tm=512):
    B, S, _ = x.shape
    T = B * S
    spb = S // tm
    a_specs, a_shapes = [], []
    for _, d in DIL_PAIRS:
        a_specs.append(pl.BlockSpec((1, d, tm // d, PROJ_TN), lambda i: (i // spb, 0, i % spb, 0)))
        a_shapes.append(jax.ShapeDtypeStruct((B, d, S // d, PROJ_TN), BF16))
    return pl.pallas_call(
        _inproj_kernel,
        grid=(T // tm,),
        in_specs=[
            pl.BlockSpec((tm, D_MODEL), lambda i: (i, 0)),
            pl.BlockSpec((1, D_MODEL), lambda i: (0, 0)),
            pl.BlockSpec(w_all.shape, lambda i: (0, 0), pipeline_mode=pl.Buffered(1)),
        ],
        out_specs=a_specs + [pl.BlockSpec((tm, PR), lambda i: (i, 0))],
        out_shape=a_shapes + [jax.ShapeDtypeStruct((T, PR), BF16)],
        scratch_shapes=[pltpu.VMEM((N_DIL - 1, PROJ_TN // 128, tm, 128), F32)],
        compiler_params=_cparams(("parallel",)),
        name="inproj",
    )(x.reshape(T, D_MODEL), norm_g, w_all)


ATTN_ITEMS = 8


def _attn_kernel(q_ref, k_ref, v_ref, bias_ref, o_ref, lse_ref, *, L, wk, nc, nq, nblk):
    i = pl.program_id(2)
    lane = lax.broadcasted_iota(jnp.int32, (Q_BLK, A_GW), 1)
    qscale = jnp.asarray(A_DH ** -0.5, BF16)
    nt = (((1,), (1,)), ((), ()))
    items = [(c, t) for c in range(nc) for t in range(nq)]
    heads = [slice(h * A_DH, (h + 1) * A_DH) for h in range(A_HPG)]
    vws, scores = [], []
    for c, t in items:
        blk = i * nq + t
        start = pl.multiple_of(jnp.clip(blk * Q_BLK - HALF_WIN, 0, L - wk), HALF_WIN)
        var = jnp.where(blk == 0, 0, jnp.where(blk == nblk - 1, 2, 1))
        q = q_ref[0, c, t * Q_BLK:(t + 1) * Q_BLK, :] * qscale
        kw = k_ref[0, c, pl.ds(start, wk), :]
        vws.append(v_ref[0, c, pl.ds(start, wk), :])
        scores.append([lax.dot_general(q[:, sl], kw[:, sl], nt, preferred_element_type=F32) + bias_ref[var, h]
                       for h, sl in enumerate(heads)])
    maxes = [[jnp.max(s, axis=-1, keepdims=True) for s in item] for item in scores]
    probs = [[jnp.exp(s - m) for s, m in zip(si, mi)] for si, mi in zip(scores, maxes)]
    sums = [[jnp.sum(p, axis=-1, keepdims=True) for p in item] for item in probs]
    for (c, t), vw, pi, mi, li in zip(items, vws, probs, maxes, sums):
        outs = [jnp.dot(p.astype(BF16), vw[:, sl], preferred_element_type=F32) / l
                for p, l, sl in zip(pi, li, heads)]
        lse_tile = jnp.zeros((Q_BLK, A_GW), F32)
        for h, (m, l) in enumerate(zip(mi, li)):
            lse_tile = jnp.where((lane >= A_DH * h) & (lane < A_DH * (h + 1)), m + jnp.log(l), lse_tile)
        rows = slice(t * Q_BLK, (t + 1) * Q_BLK)
        cols = slice(c * A_GW, (c + 1) * A_GW)
        o_ref[0, rows, cols] = jnp.concatenate(outs, axis=1).astype(BF16)
        lse_ref[0, rows, cols] = lse_tile


def _attn_group(qkv, bias_tiles, d, B, S):
    L = S // d
    nblk = L // Q_BLK
    wk = min(2 * Q_BLK, L)
    nq = min(ATTN_ITEMS, nblk)
    nc = ATTN_ITEMS // nq

    return pl.pallas_call(
        functools.partial(_attn_kernel, L=L, wk=wk, nc=nc, nq=nq, nblk=nblk),
        grid=(B, d // nc, nblk // nq),
        in_specs=[
            pl.BlockSpec((1, nc, nq * Q_BLK, A_GW), lambda b, r, i: (b, r, i, 0)),
            pl.BlockSpec((1, nc, L, A_GW), lambda b, r, i: (b, r, 0, 1)),
            pl.BlockSpec((1, nc, L, A_GW), lambda b, r, i: (b, r, 0, 2)),
            pl.BlockSpec(bias_tiles.shape, lambda b, r, i: (0, 0, 0, 0)),
        ],
        out_specs=[
            pl.BlockSpec((1, nq * Q_BLK, nc * A_GW), lambda b, r, i: (b, i, r)),
            pl.BlockSpec((1, nq * Q_BLK, nc * A_GW), lambda b, r, i: (b, i, r)),
        ],
        out_shape=[
            jax.ShapeDtypeStruct((B, L, d * A_GW), BF16),
            jax.ShapeDtypeStruct((B, L, d * A_GW), F32),
        ],
        compiler_params=_cparams(("parallel", "parallel", "arbitrary")),
        name=f"attn_d{d}",
    )(qkv, qkv, qkv, bias_tiles)


def _t5_buckets(rel):
    nb = N_REL_BUCKETS // 2
    max_exact = nb // 2
    ret = (rel > 0).astype(np.int64) * nb
    n = np.abs(rel)
    large = max_exact + (np.log(np.maximum(n, 1) / max_exact) / math.log(REL_MAX_DISTANCE / max_exact)
                         * (nb - max_exact)).astype(np.int64)
    large = np.minimum(large, nb - 1)
    return (ret + np.where(n < max_exact, n, large)).astype(np.int32)


def _bias_tiles(rel_bias, g, d, wk):
    rel = d * np.arange(-HALF_WIN, HALF_WIN + 1)
    bucket = _t5_buckets(rel)
    bias = rel_bias[jnp.asarray(bucket)][:, g * A_HPG:(g + 1) * A_HPG].T.astype(F32)
    pad = wk + Q_BLK
    neg = jnp.full((A_HPG, pad), NEG_INF, F32)
    val = jnp.concatenate([neg, bias, neg], axis=1)
    zero = pad + HALF_WIN
    m = wk + Q_BLK
    tiles = []
    for off in (0, HALF_WIN, wk - Q_BLK):
        u = jnp.concatenate([val[:, zero - off:zero - off + wk], val[:, zero - off - Q_BLK:zero - off]], axis=1)
        flat = jnp.tile(u, (1, Q_BLK))[:, :Q_BLK * (m - 1)]
        tiles.append(flat.reshape(A_HPG, Q_BLK, m - 1)[:, :, :wk])
    return jnp.stack(tiles)


def _log_sigmoid(z):
    return jnp.minimum(z, 0.0) - jnp.log(1.0 + jnp.exp(-jnp.abs(z)))


def _gla_kernel(q_ref, k_ref, v_ref, og_ref, g_ref, wgf_ref, wgb_ref, bgf_ref, bgb_ref, gn_ref, o_ref,
                qif, kof, qib, kob, vt, etf, etb, acc, *, S):
    PB = 256
    C = GLA_CHUNK
    NB = S // PB
    rr = lax.broadcasted_iota(jnp.int32, (PB, PB), 0)
    cc = lax.broadcasted_iota(jnp.int32, (PB, PB), 1)
    same = (rr >> 6) == (cc >> 6)
    mask_f = same & (cc <= rr)
    mask_b = same & (cc > rr)
    tl = jnp.where(mask_f, 1.0, 0.0).astype(BF16)
    qscale = GLA_DK ** -0.5
    inv_tau = 1.0 / GATE_TAU
    nt = (((1,), (1,)), ((), ()))

    def split3(x):
        hi = x.astype(BF16)
        r1 = x - hi.astype(F32)
        mid = r1.astype(BF16)
        lo = (r1 - mid.astype(F32)).astype(BF16)
        return [hi, mid, lo]

    def chunk_total(b):
        b4 = b.reshape(PB // C, C, GLA_DK)
        return jnp.broadcast_to(b4[:, C - 1:C, :], b4.shape).reshape(PB, GLA_DK)

    PREP_BLOCKS = 4

    def prep(t, carry):
        blocks = [t * PREP_BLOCKS + u for u in range(PREP_BLOCKS)]
        rows = [pl.ds(pl.multiple_of(i * PB, PB), PB) for i in blocks]
        gs = [g_ref[0, r, :] for r in rows]
        lfs = [_log_sigmoid(jnp.dot(g, wgf_ref[...], preferred_element_type=F32) + bgf_ref[...]) * inv_tau for g in gs]
        lbs = [_log_sigmoid(jnp.dot(g, wgb_ref[...], preferred_element_type=F32) + bgb_ref[...]) * inv_tau for g in gs]
        css = [jnp.dot(tl, jnp.concatenate(split3(lf) + split3(lb), axis=1), preferred_element_type=F32)
               for lf, lb in zip(lfs, lbs)]
        bfs = [cs[:, 0:128] + cs[:, 128:256] + cs[:, 256:384] for cs in css]
        pbs = [cs[:, 384:512] + cs[:, 512:640] + cs[:, 640:768] for cs in css]
        totfs = [chunk_total(bf) for bf in bfs]
        totbs = [chunk_total(pb) for pb in pbs]
        bss = [totb - pb + lb for totb, pb, lb in zip(totbs, pbs, lbs)]
        qs = [q_ref[0, r, :].astype(F32) * qscale for r in rows]
        ks = [k_ref[0, r, :].astype(F32) for r in rows]
        qfs = [(q * jnp.exp(bf)).astype(BF16) for q, bf in zip(qs, bfs)]
        qbs = [(q * jnp.exp(bs)).astype(BF16) for q, bs in zip(qs, bss)]
        kfs = [(k * jnp.exp(-bf)).astype(BF16) for k, bf in zip(ks, bfs)]
        kbs = [(k * jnp.exp(-bs)).astype(BF16) for k, bs in zip(ks, bss)]
        sfs = [lax.dot_general(qf, kf, nt, preferred_element_type=F32) for qf, kf in zip(qfs, kfs)]
        sbs = [lax.dot_general(qb, kb, nt, preferred_element_type=F32) for qb, kb in zip(qbs, kbs)]
        ps = [jnp.where(mask_f, sf, jnp.where(mask_b, sb, 0.0)).astype(BF16) for sf, sb in zip(sfs, sbs)]
        vs = [v_ref[0, r, :] for r in rows]
        for r, p, v in zip(rows, ps, vs):
            acc[r, :] = jnp.dot(p, v, preferred_element_type=F32)
        for r, qf, qb in zip(rows, qfs, qbs):
            qif[r, :] = qf
            qib[r, :] = qb
        for r, k, totf, bf, totb, bs in zip(rows, ks, totfs, bfs, totbs, bss):
            kof[r, :] = (k * jnp.exp(totf - bf)).astype(BF16)
            kob[r, :] = (k * jnp.exp(totb - bs)).astype(BF16)
            etf[r, :] = jnp.exp(totf)
            etb[r, :] = jnp.exp(totb)
        for i, v in zip(blocks, vs):
            vtb = v.astype(F32).T.astype(BF16)
            vt[2 * i] = vtb[:, :128]
            vt[2 * i + 1] = vtb[:, 128:]
        return carry

    lax.fori_loop(0, NB // PREP_BLOCKS, prep, 0)

    zeros_half = jnp.zeros((C, GLA_DK), BF16)

    SER_BLOCKS = 4
    NJ = PB // C

    def chunk_rows(blk, j):
        return pl.multiple_of(blk * PB + j * C, C)

    def state_update(ko_ref, blk, j):
        ko = ko_ref[pl.ds(chunk_rows(blk, j), C), :]
        ko_pad = jnp.concatenate([ko, zeros_half] if j % 2 == 0 else [zeros_half, ko], axis=0)
        return jnp.dot(vt[2 * blk + j // 2], ko_pad, preferred_element_type=F32)

    def advance(qi_ref, et_ref, blk, j, state, upd):
        r0 = chunk_rows(blk, j)
        rows = pl.ds(r0, C)
        acc[rows, :] += lax.dot_general(qi_ref[rows, :], state.astype(BF16), nt, preferred_element_type=F32)
        return state * et_ref[pl.ds(r0, 1), :] + upd

    def serial(t, carry):
        st_f, st_b = carry
        steps_f = [(t * SER_BLOCKS + u, j) for u in range(SER_BLOCKS) for j in range(NJ)]
        steps_b = [(NB - 1 - t * SER_BLOCKS - u, NJ - 1 - j) for u in range(SER_BLOCKS) for j in range(NJ)]
        upd_f = [state_update(kof, blk, j) for blk, j in steps_f]
        upd_b = [state_update(kob, blk, j) for blk, j in steps_b]
        for (bf, jf), (bb, jb), uf, ub in zip(steps_f, steps_b, upd_f, upd_b):
            st_f = advance(qif, etf, bf, jf, st_f, uf)
            st_b = advance(qib, etb, bb, jb, st_b, ub)
        return st_f, st_b

    zero_state = jnp.zeros((GLA_DV, GLA_DK), F32)
    lax.fori_loop(0, NB // SER_BLOCKS, serial, (zero_state, zero_state))

    gn = gn_ref[...]

    def finish(i, carry):
        rows = pl.ds(pl.multiple_of(i * PB, PB), PB)
        tot = acc[rows, :]
        nrm = tot * lax.rsqrt(jnp.mean(tot * tot, axis=-1, keepdims=True) + EPS) * gn
        og = og_ref[0, rows, :].astype(F32)
        o_ref[0, rows, :] = (nrm * (og * jax.nn.sigmoid(og))).astype(BF16)
        return carry

    lax.fori_loop(0, NB, finish, 0)


def _gla(proj, wgf, wgb, bgf, bgb, gla_norm, B, S):
    pv = proj.reshape(B, S, PR)
    cq, ck, cv, cog, cg = C_QB // GLA_DK, C_KB // GLA_DK, C_VB // GLA_DV, C_OG // GLA_DV, C_G // 128
    return pl.pallas_call(
        functools.partial(_gla_kernel, S=S),
        grid=(B, GLA_H),
        in_specs=[
            pl.BlockSpec((1, S, GLA_DK), lambda b, h: (b, 0, cq + h)),
            pl.BlockSpec((1, S, GLA_DK), lambda b, h: (b, 0, ck + h)),
            pl.BlockSpec((1, S, GLA_DV), lambda b, h: (b, 0, cv + h)),
            pl.BlockSpec((1, S, GLA_DV), lambda b, h: (b, 0, cog + h)),
            pl.BlockSpec((1, S, 128), lambda b, h: (b, 0, cg)),
            pl.BlockSpec((128, GLA_DK), lambda b, h: (0, h)),
            pl.BlockSpec((128, GLA_DK), lambda b, h: (0, h)),
            pl.BlockSpec((1, GLA_DK), lambda b, h: (0, h)),
            pl.BlockSpec((1, GLA_DK), lambda b, h: (0, h)),
            pl.BlockSpec((1, GLA_DV), lambda b, h: (0, 0)),
        ],
        out_specs=pl.BlockSpec((1, S, GLA_DV), lambda b, h: (b, 0, h)),
        out_shape=jax.ShapeDtypeStruct((B, S, GLA_H * GLA_DV), BF16),
        scratch_shapes=[
            pltpu.VMEM((S, GLA_DK), BF16), pltpu.VMEM((S, GLA_DK), BF16),
            pltpu.VMEM((S, GLA_DK), BF16), pltpu.VMEM((S, GLA_DK), BF16),
            pltpu.VMEM((S // 128, GLA_DV, 128), BF16),
            pltpu.VMEM((S, GLA_DK), F32), pltpu.VMEM((S, GLA_DK), F32),
            pltpu.VMEM((S, GLA_DV), F32),
        ],
        compiler_params=_cparams(("parallel", "arbitrary")),
        name="gla",
    )(pv, pv, pv, pv, pv, wgf, wgb, bgf, bgb, gla_norm)


OUTPROJ_SPLIT = 2


def _outproj_kernel(x_ref, ma_ref, mb_ref, o0_ref, o1_ref, o2_ref, l0_ref, l1_ref, l2_ref, ob_ref,
                    wpa_ref, wpb_ref, wo_ref, nm_ref, wr_ref, br_ref,
                    x1_ref, hm_ref, gate_ref, gatet_ref):
    tm = x_ref.shape[0] // OUTPROJ_SPLIT
    parts = [slice(p * tm, (p + 1) * tm) for p in range(OUTPROJ_SPLIT)]

    def merge_groups(rs):
        l0, l1, l2 = l0_ref[rs, :], l1_ref[rs, :], l2_ref[rs, :]
        m = jnp.maximum(jnp.maximum(l0, l1), l2)
        e0, e1, e2 = jnp.exp(l0 - m), jnp.exp(l1 - m), jnp.exp(l2 - m)
        den = e0 + e1 + e2
        oa = ((e0 / den) * o0_ref[rs, :].astype(F32) + (e1 / den) * o1_ref[rs, :].astype(F32)
              + (e2 / den) * o2_ref[rs, :].astype(F32))
        return oa.astype(BF16)

    def project(rs, oa):
        ya = jnp.dot(oa, wpa_ref[...], preferred_element_type=F32)
        yb = jnp.dot(ob_ref[rs, :], wpb_ref[...], preferred_element_type=F32)
        return ya, yb

    def gate_mix(rs, y):
        ya, yb = y
        mix = jax.nn.sigmoid(ma_ref[rs, :].astype(F32)) * ya + jax.nn.sigmoid(mb_ref[rs, :].astype(F32)) * yb
        return mix.astype(BF16)

    def residual(rs, mix):
        x1 = x_ref[rs, :] + jnp.dot(mix, wo_ref[...], preferred_element_type=F32)
        x1_ref[rs, :] = x1
        return x1

    def moe_norm(rs, x1):
        hm = (x1 * lax.rsqrt(jnp.mean(x1 * x1, axis=-1, keepdims=True) + EPS) * nm_ref[...]).astype(BF16)
        hm_ref[rs, :] = hm
        return hm

    def router_logits(rs, hm):
        return jnp.dot(hm, wr_ref[...], preferred_element_type=F32) + br_ref[...]

    def router(rs, logits):
        gate = _route(logits)
        gate_ref[rs, :] = gate
        gatet_ref[:, rs] = gate.T

    stages = [lambda rs, _: merge_groups(rs), project, gate_mix, residual, moe_norm, router_logits, router]
    vals = [None] * OUTPROJ_SPLIT
    for step in range(len(stages) + OUTPROJ_SPLIT - 1):
        for p, rs in enumerate(parts):
            if 0 <= step - p < len(stages):
                vals[p] = stages[step - p](rs, vals[p])


def _route(logits):
    tm = logits.shape[0]
    lane = lax.broadcasted_iota(jnp.int32, (tm, 128), 1)
    is_g = (lane >= N_EXPERTS) & (lane < 2 * N_EXPERTS)
    is_e = lane < N_EXPERTS
    grp_of_lane = jnp.where(is_g, (lane - N_EXPERTS) >> 3, lane >> 3)
    gl = jnp.where(is_g, logits, NEG_INF)
    gmax = jnp.max(gl, axis=-1, keepdims=True)
    gsum = jnp.sum(jnp.where(is_g, jnp.exp(gl - gmax), 0.0), axis=-1, keepdims=True) * (1.0 / EPG)
    pg_top = 1.0 / gsum
    g_sel = jnp.min(jnp.where(is_g & (gl == gmax), grp_of_lane, N_GROUPS), axis=-1, keepdims=True)
    in_grp = is_e & (grp_of_lane == g_sel)
    el = jnp.where(in_grp, logits, NEG_INF)
    emax = jnp.max(el, axis=-1, keepdims=True)
    ee = jnp.where(in_grp, jnp.exp(el - emax), 0.0)
    pe = ee / jnp.sum(ee, axis=-1, keepdims=True)
    p1 = jnp.max(pe, axis=-1, keepdims=True)
    i1 = jnp.min(jnp.where(in_grp & (pe == p1), lane, 128), axis=-1, keepdims=True)
    rest = in_grp & (lane != i1)
    pe2 = jnp.where(rest, pe, -1.0)
    p2 = jnp.max(pe2, axis=-1, keepdims=True)
    i2 = jnp.min(jnp.where(rest & (pe2 == p2), lane, 128), axis=-1, keepdims=True)
    psum = p1 + p2
    return jnp.where(lane == i1, pg_top * (p1 / psum), jnp.where(lane == i2, pg_top * (p2 / psum), 0.0))


def _outproj(x2d, proj, o_list, lse_list, ob2d, wpa, wpb, wo, norm_moe, wr, br, tm=512):
    T = x2d.shape[0]
    row = lambda w: pl.BlockSpec((tm, w), lambda i: (i, 0))
    full = lambda a: pl.BlockSpec(a.shape, lambda i: (0,) * a.ndim)
    return pl.pallas_call(
        _outproj_kernel,
        grid=(T // tm,),
        in_specs=[
            row(D_MODEL),
            pl.BlockSpec((tm, D_MODEL), lambda i: (i, C_MA // D_MODEL)),
            pl.BlockSpec((tm, D_MODEL), lambda i: (i, C_MB // D_MODEL)),
            row(A_GW), row(A_GW), row(A_GW), row(A_GW), row(A_GW), row(A_GW),
            row(D_MODEL),
            full(wpa), full(wpb), full(wo), full(norm_moe), full(wr), full(br),
        ],
        out_specs=[row(D_MODEL), row(D_MODEL), row(128), pl.BlockSpec((128, tm), lambda i: (0, i))],
        out_shape=[
            jax.ShapeDtypeStruct((T, D_MODEL), F32),
            jax.ShapeDtypeStruct((T, D_MODEL), BF16),
            jax.ShapeDtypeStruct((T, 128), F32),
            jax.ShapeDtypeStruct((128, T), F32),
        ],
        compiler_params=_cparams(("parallel",)),
        name="outproj",
    )(x2d, proj, proj, *o_list, *lse_list, ob2d, wpa, wpb, wo, norm_moe, wr, br)


MOE_TB = 1024
MOE_Q = 4
MOE_SB = MOE_TB // MOE_Q
MOE_ALIGN = 16
MOE_CH = 128
MOE_W = MOE_CH // MOE_Q
MOE_RQ = 2 * MOE_SB + N_EXPERTS * MOE_ALIGN
MOE_RQS = MOE_RQ + MOE_W
MOE_SUB = 256
MOE_G = 4


def _moe_route(hm_ref, gate_ref, gatet_ref, xs, ys, pt, wrow, rinfo):
    TB, Q, SB, RQ, RQS, SUB = MOE_TB, MOE_Q, MOE_SB, MOE_RQ, MOE_RQS, MOE_SUB
    big = float(4 * TB)
    gate = gate_ref[...]
    gt = gatet_ref[...]
    a = gate > 0.0
    at = gt > 0.0
    a_f = jnp.where(a, 1.0, 0.0)
    at_f = jnp.where(at, 1.0, 0.0)
    a_b = a_f.astype(BF16)
    at_b = at_f.astype(BF16)

    def pad(c):
        return jnp.floor((c + (MOE_ALIGN - 1)) * (1.0 / MOE_ALIGN)) * MOE_ALIGN

    er = lax.broadcasted_iota(jnp.int32, (128, 128), 0)
    ec = lax.broadcasted_iota(jnp.int32, (128, 128), 1)
    sub8 = lax.broadcasted_iota(jnp.int32, (8, 128), 0)
    pad_rows = jnp.zeros((8, 128), F32)
    pad_cols = jnp.zeros((128, 128), F32)
    for q in range(Q):
        blk = slice(q * SB, (q + 1) * SB)
        pad_rows = jnp.where(sub8 == q, pad(jnp.sum(a_f[blk], axis=0, keepdims=True)), pad_rows)
        pad_cols = jnp.where(ec == q, pad(jnp.sum(at_f[:, blk], axis=1, keepdims=True)), pad_cols)
    off_rows = jnp.dot(pad_rows.astype(BF16), jnp.where(er < ec, 1.0, 0.0).astype(BF16), preferred_element_type=F32)
    off_cols = jnp.dot(jnp.where(ec < er, 1.0, 0.0).astype(BF16), pad_cols.astype(BF16), preferred_element_type=F32)

    tr = lax.broadcasted_iota(jnp.int32, (SUB, SUB), 0)
    tc = lax.broadcasted_iota(jnp.int32, (SUB, SUB), 1)
    tri_l = jnp.where(tc < tr, 1.0, 0.0).astype(BF16)
    tri_u = jnp.where(tr < tc, 1.0, 0.0).astype(BF16)
    rl = lax.broadcasted_iota(jnp.int32, (SUB, RQ), 1).astype(F32)
    assert SB == SUB
    subs = [slice(q * SB, (q + 1) * SB) for q in range(Q)]
    pos = [off_rows[q:q + 1] + jnp.dot(tri_l, a_b[b], preferred_element_type=F32) for q, b in enumerate(subs)]
    plo = [jnp.min(jnp.where(a[b], p, big), axis=1, keepdims=True) for b, p in zip(subs, pos)]
    phi = [jnp.max(jnp.where(a[b], p, -1.0), axis=1, keepdims=True) for b, p in zip(subs, pos)]
    phi = [jnp.where(hi == lo, -1.0, hi) for lo, hi in zip(plo, phi)]
    for b, lo, hi in zip(subs, plo, phi):
        pt[b, :] = jnp.where((rl == lo) | (rl == hi), 1.0, 0.0).astype(BF16)
    pos_t = [off_cols[:, q:q + 1] + jnp.dot(at_b[:, b], tri_u, preferred_element_type=F32)
             for q, b in enumerate(subs)]
    plo = [jnp.min(jnp.where(at[:, b], p, big), axis=0, keepdims=True) for b, p in zip(subs, pos_t)]
    phi = [jnp.max(jnp.where(at[:, b], p, -1.0), axis=0, keepdims=True) for b, p in zip(subs, pos_t)]
    phi = [jnp.where(hi == lo, -1.0, hi) for lo, hi in zip(plo, phi)]
    wlo = [jnp.sum(jnp.where(at[:, b] & (p == lo), gt[:, b], 0.0), axis=0, keepdims=True)
           for b, p, lo in zip(subs, pos_t, plo)]
    whi = [jnp.sum(jnp.where(at[:, b] & (p == hi), gt[:, b], 0.0), axis=0, keepdims=True)
           for b, p, hi in zip(subs, pos_t, phi)]
    for b, lo, hi, wl, wh in zip(subs, plo, phi, wlo, whi):
        rinfo[0:1, b] = lo
        rinfo[1:2, b] = hi
        rinfo[2:3, b] = wl
        rinfo[3:4, b] = wh

    blks = [slice(q * SB, (q + 1) * SB) for q in range(Q)]

    def gather(k, carry):
        r0 = pl.multiple_of(k * SUB, SUB)
        ri = (lax.broadcasted_iota(jnp.int32, (SUB, SB), 0) + r0).astype(F32)
        mlo = [ri == rinfo[0:1, b] for b in blks]
        mhi = [ri == rinfo[1:2, b] for b in blks]
        p = [jnp.where(lo | hi, 1.0, 0.0).astype(BF16) for lo, hi in zip(mlo, mhi)]
        x = [jnp.dot(pq, hm_ref[b, :], preferred_element_type=F32).astype(BF16) for pq, b in zip(p, blks)]
        w = [jnp.sum(jnp.where(lo, rinfo[2:3, b], 0.0) + jnp.where(hi, rinfo[3:4, b], 0.0), axis=1, keepdims=True)
             for lo, hi, b in zip(mlo, mhi, blks)]
        for q in range(Q):
            dst = pl.ds(pl.multiple_of(q * RQS + r0, MOE_ALIGN), SUB)
            xs[dst, :] = x[q]
            wrow[dst, :] = jnp.broadcast_to(w[q], (SUB, 128))
        return carry

    lax.fori_loop(0, RQ // SUB, gather, 0)
    for q in range(Q):
        xs[q * RQS + RQ:(q + 1) * RQS, :] = jnp.zeros((RQS - RQ, D_MODEL), BF16)
        wrow[q * RQS + RQ:(q + 1) * RQS, :] = jnp.zeros((RQS - RQ, 128), F32)
    ys[...] = jnp.zeros_like(ys)


def _moe_kernel(meta_ref, hm_ref, gate_ref, gatet_ref, x1_ref, wgu_ref, wd_ref, nf_ref, y_ref,
                xs, ys, pt, wrow, rinfo):
    i = pl.program_id(0)
    s = pl.program_id(1)
    Q, W = MOE_Q, MOE_W

    @pl.when(s == 0)
    def _():
        _moe_route(hm_ref, gate_ref, gatet_ref, xs, ys, pt, wrow, rinfo)

    offs = [[meta_ref[i, q * N_EXPERTS + s * MOE_G + g] + q * MOE_RQS for q in range(Q)] for g in range(MOE_G)]
    pads = [[meta_ref[i, (Q + q) * N_EXPERTS + s * MOE_G + g] for q in range(Q)] for g in range(MOE_G)]

    def up_proj(g, x):
        return jnp.dot(x, wgu_ref[g], preferred_element_type=F32)

    def activate(gu, w):
        gt, up = gu[:, :EXPERT_FF], gu[:, EXPERT_FF:]
        act = (gt * jax.nn.sigmoid(gt)) * up
        return jnp.concatenate([act[:, :128] * w, act[:, 128:] * w], axis=1).astype(BF16)

    def down_proj(g, act):
        return jnp.dot(act, wd_ref[g], preferred_element_type=F32).astype(BF16)

    def mlp(g, x, w):
        return down_proj(g, activate(up_proj(g, x), w))

    single = pads[0][0] <= W
    for g in range(MOE_G):
        for q in range(Q):
            single = jnp.logical_and(single, pads[g][q] <= W)

    @pl.when(single)
    def _():
        rows = [[pl.ds(pl.multiple_of(offs[g][q], MOE_ALIGN), W) for q in range(Q)] for g in range(MOE_G)]
        x = [jnp.concatenate([xs[r, :] for r in rg], axis=0) for rg in rows]
        w = [jnp.concatenate([wrow[r, :] for r in rg], axis=0) for rg in rows]
        gu = [up_proj(g, x[g]) for g in range(MOE_G)]
        act = [activate(gu[g], w[g]) for g in range(MOE_G)]
        y = [down_proj(g, act[g]) for g in range(MOE_G)]
        for g in range(MOE_G):
            for q, r in enumerate(rows[g]):
                ys[r, :] = y[g][q * W:(q + 1) * W]

    @pl.when(jnp.logical_not(single))
    def _():
        for g in range(MOE_G):
            for q in range(Q):
                def body(c, carry, g=g, q=q):
                    r = pl.ds(pl.multiple_of(offs[g][q] + c * W, MOE_ALIGN), W)
                    ys[r, :] = mlp(g, xs[r, :], wrow[r, :])
                    return carry
                lax.fori_loop(0, (pads[g][q] + (W - 1)) // W, body, 0)

    @pl.when(s == N_EXPERTS // MOE_G - 1)
    def _():
        blks = [slice(q * MOE_SB, (q + 1) * MOE_SB) for q in range(Q)]
        moe = [jnp.dot(pt[b, :], ys[q * MOE_RQS:q * MOE_RQS + MOE_RQ, :], preferred_element_type=F32)
               for q, b in enumerate(blks)]
        x2 = [x1_ref[b, :] + m for b, m in zip(blks, moe)]
        for b, x in zip(blks, x2):
            y_ref[b, :] = x * lax.rsqrt(jnp.mean(x * x, axis=-1, keepdims=True) + EPS) * nf_ref[...]


def _moe(hm, gate, gatet, x1, wgu, wd, norm_final):
    T = hm.shape[0]
    tb = MOE_TB
    nb = T // tb
    used = gate.reshape(nb, MOE_Q, MOE_SB, 128)[..., :N_EXPERTS] > 0.0
    cnt = jnp.sum(used.astype(jnp.int32), axis=2)
    padded = (cnt + (MOE_ALIGN - 1)) // MOE_ALIGN * MOE_ALIGN
    off = jnp.cumsum(padded, axis=2) - padded
    meta = jnp.concatenate([off.reshape(nb, -1), padded.reshape(nb, -1)], axis=1).astype(jnp.int32)
    grid_spec = pltpu.PrefetchScalarGridSpec(
        num_scalar_prefetch=1,
        grid=(nb, N_EXPERTS // MOE_G),
        in_specs=[
            pl.BlockSpec((tb, D_MODEL), lambda i, e, m: (i, 0)),
            pl.BlockSpec((tb, 128), lambda i, e, m: (i, 0)),
            pl.BlockSpec((128, tb), lambda i, e, m: (0, i)),
            pl.BlockSpec((tb, D_MODEL), lambda i, e, m: (i, 0)),
            pl.BlockSpec((MOE_G, D_MODEL, 2 * EXPERT_FF), lambda i, e, m: (e, 0, 0)),
            pl.BlockSpec((MOE_G, EXPERT_FF, D_MODEL), lambda i, e, m: (e, 0, 0)),
            pl.BlockSpec((1, D_MODEL), lambda i, e, m: (0, 0)),
        ],
        out_specs=pl.BlockSpec((tb, D_MODEL), lambda i, e, m: (i, 0)),
        scratch_shapes=[
            pltpu.VMEM((MOE_Q * MOE_RQS, D_MODEL), BF16),
            pltpu.VMEM((MOE_Q * MOE_RQS, D_MODEL), BF16),
            pltpu.VMEM((tb, MOE_RQ), BF16),
            pltpu.VMEM((MOE_Q * MOE_RQS, 128), F32),
            pltpu.VMEM((8, tb), F32),
        ],
    )
    return pl.pallas_call(
        _moe_kernel,
        grid_spec=grid_spec,
        out_shape=jax.ShapeDtypeStruct((T, D_MODEL), F32),
        compiler_params=_cparams(("parallel", "arbitrary")),
        name="moe",
    )(meta, hm, gate, gatet, x1, wgu, wd, norm_final)


def _prep_weights(norm_mix, w_in, rel_bias, w_gate_f, b_gate_f, w_gate_b, b_gate_b, gla_norm, w_proj_a,
                  w_proj_b, w_out, norm_moe, w_rg, b_rg, w_re, b_re, w_eg, w_eu, w_ed, norm_final):
    splits = (A_QKV, A_QKV, A_QKV, 512, 512, 1024, 1024, GATE_RANK, GATE_RANK, D_MODEL, D_MODEL)
    qa, ka, va, qb, kb, vb, og, gf, gb, ma, mb = jnp.split(w_in, np.cumsum(splits)[:-1].tolist(), axis=1)
    gpad = jnp.zeros((D_MODEL, PR - C_G - 2 * GATE_RANK), w_in.dtype)
    grp = lambda t, g: t[:, g * A_GW:(g + 1) * A_GW]
    a_cols = [grp(t, g) for g in range(N_DIL) for t in (qa, ka, va)]
    w_all = jnp.concatenate(a_cols + [ma, mb, qb, kb, vb, og, gf, gb, gpad], axis=1).astype(BF16)
    kw = GLA_H * GLA_DK
    wgf = jnp.zeros((128, kw), F32).at[:GATE_RANK].set(w_gate_f).astype(BF16)
    wgb = jnp.zeros((128, kw), F32).at[GATE_RANK:2 * GATE_RANK].set(w_gate_b).astype(BF16)
    wr = jnp.zeros((D_MODEL, 128), F32)
    wr = wr.at[:, :N_EXPERTS].set(w_re).at[:, N_EXPERTS:2 * N_EXPERTS].set(jnp.repeat(w_rg, EPG, axis=1))
    br = jnp.zeros((1, 128), F32)
    br = br.at[0, :N_EXPERTS].set(b_re).at[0, N_EXPERTS:2 * N_EXPERTS].set(jnp.repeat(b_rg, EPG))
    return dict(
        norm_mix=norm_mix.reshape(1, D_MODEL), w_all=w_all, rel_bias=rel_bias,
        wgf=wgf, wgb=wgb, bgf=b_gate_f.reshape(1, kw), bgb=b_gate_b.reshape(1, kw),
        gla_norm=gla_norm.reshape(1, GLA_DV),
        wpa=w_proj_a.astype(BF16), wpb=w_proj_b.astype(BF16), wo=w_out.astype(BF16),
        norm_moe=norm_moe.reshape(1, D_MODEL), wr=wr.astype(BF16), br=br,
        wgu=jnp.concatenate([w_eg, w_eu], axis=-1).astype(BF16), wd=w_ed.astype(BF16),
        norm_final=norm_final.reshape(1, D_MODEL),
    )


def _trunk(x, w):
    B, S, _ = x.shape
    T = B * S
    x2d = x.reshape(T, D_MODEL)
    *qkv, proj = _inproj(x, w["norm_mix"], w["w_all"])
    o_list, lse_list = [], []
    for g, (_, d) in enumerate(DIL_PAIRS):
        L = S // d
        tiles = _bias_tiles(w["rel_bias"], g, d, min(2 * Q_BLK, L))
        o, lse = _attn_group(qkv[g], tiles, d, B, S)
        o_list.append(o.reshape(T, A_GW))
        lse_list.append(lse.reshape(T, A_GW))
    ob = _gla(proj, w["wgf"], w["wgb"], w["bgf"], w["bgb"], w["gla_norm"], B, S)
    x1, hm, gate, gatet = _outproj(x2d, proj, o_list, lse_list, ob.reshape(T, D_MODEL), w["wpa"], w["wpb"],
                                   w["wo"], w["norm_moe"], w["wr"], w["br"])
    y = _moe(hm, gate, gatet, x1, w["wgu"], w["wd"], w["norm_final"])
    return y.reshape(B, S, D_MODEL)


def kernel(x_prompt, x_sample, norm_mix, w_in, rel_bias, w_gate_f, b_gate_f, w_gate_b, b_gate_b, gla_norm,
           w_proj_a, w_proj_b, w_out, norm_moe, w_router_group, b_router_group, w_router_expert,
           b_router_expert, w_exp_gate, w_exp_up, w_exp_down, norm_final):
    w = _prep_weights(norm_mix[0], w_in[0], rel_bias, w_gate_f[0], b_gate_f[0], w_gate_b[0], b_gate_b[0],
                      gla_norm[0], w_proj_a[0], w_proj_b[0], w_out[0], norm_moe[0], w_router_group[0],
                      b_router_group[0], w_router_expert[0], b_router_expert[0], w_exp_gate[0], w_exp_up[0],
                      w_exp_down[0], norm_final)
    return (_trunk(x_prompt, w), _trunk(x_sample, w))
```

```python
import functools
import math

import numpy as np
import jax
import jax.numpy as jnp
from jax import lax
from jax.experimental import pallas as pl
from jax.experimental.pallas import tpu as pltpu

F32 = jnp.float32
BF16 = jnp.bfloat16

D_MODEL = 1024
EPS = 1e-6
NEG_INF = -1e30

DIL_PAIRS = ((128, 1), (512, 4), (2048, 16))
A_HPG = 4
A_DH = 64
A_GW = A_HPG * A_DH
A_QKV = 3 * A_GW
HALF_WIN = 64
Q_BLK = 128
N_REL_BUCKETS = 32
REL_MAX_DISTANCE = 1024
GLA_H = 4
GLA_DK = 128
GLA_DV = 256
GATE_RANK = 16
GATE_TAU = 16.0
GLA_CHUNK = 64
N_GROUPS = 4
EPG = 8
N_EXPERTS = 32
EXPERT_FF = 256

N_DIL = len(DIL_PAIRS)
C_MA, C_MB = 0, 1024
C_QB, C_KB, C_VB, C_OG = 2048, 2560, 3072, 4096
C_G = 5120
PR = 5376
PROJ_TN = A_QKV

VMEM_LIMIT = 58 * 1024 * 1024


def _cparams(sem):
    return pltpu.CompilerParams(dimension_semantics=sem, vmem_limit_bytes=VMEM_LIMIT)


def _inproj_kernel(x_ref, g_ref, w_ref, a0_ref, a1_ref, a2_ref, or_ref, y_scr):
    x = x_ref[...]
    r = x * lax.rsqrt(jnp.mean(x * x, axis=-1, keepdims=True) + EPS)
    h = (r * g_ref[...]).astype(BF16)
    nct = PROJ_TN // 128
    for j in range(N_DIL + PR // PROJ_TN):
        y = jnp.dot(h, w_ref[:, j * PROJ_TN:(j + 1) * PROJ_TN], preferred_element_type=F32)
        if j >= N_DIL:
            or_ref[:, (j - N_DIL) * PROJ_TN:(j - N_DIL + 1) * PROJ_TN] = y.astype(BF16)
            continue
        a_ref = (a0_ref, a1_ref, a2_ref)[j]
        d = DIL_PAIRS[j][1]
        if d == 1:
            a_ref[0, 0] = y.astype(BF16)
            continue
        for c in range(nct):
            y_scr[j - 1, c] = y[:, c * 128:(c + 1) * 128]
        n = y.shape[0] // d
        for cls in range(d):
            cols = [y_scr[j - 1, c, pl.ds(cls, n, stride=d), :] for c in range(nct)]
            a_ref[0, cls] = jnp.concatenate(cols, axis=1).astype(BF16)


def _inproj(x, norm_g, w_all, tm=512):
    B, S, _ = x.shape
    T = B * S
    assert S % tm == 0 and all(tm % (d * MOE_ALIGN) == 0 for _, d in DIL_PAIRS), (S, tm)
    spb = S // tm
    a_specs, a_shapes = [], []
    for _, d in DIL_PAIRS:
        a_specs.append(pl.BlockSpec((1, d, tm // d, PROJ_TN), lambda i: (i // spb, 0, i % spb, 0)))
        a_shapes.append(jax.ShapeDtypeStruct((B, d, S // d, PROJ_TN), BF16))
    return pl.pallas_call(
        _inproj_kernel,
        grid=(T // tm,),
        in_specs=[
            pl.BlockSpec((tm, D_MODEL), lambda i: (i, 0)),
            pl.BlockSpec((1, D_MODEL), lambda i: (0, 0)),
            pl.BlockSpec(w_all.shape, lambda i: (0, 0), pipeline_mode=pl.Buffered(1)),
        ],
        out_specs=a_specs + [pl.BlockSpec((tm, PR), lambda i: (i, 0))],
        out_shape=a_shapes + [jax.ShapeDtypeStruct((T, PR), BF16)],
        scratch_shapes=[pltpu.VMEM((N_DIL - 1, PROJ_TN // 128, tm, 128), F32)],
        compiler_params=_cparams(("parallel",)),
        name="inproj",
    )(x.reshape(T, D_MODEL), norm_g, w_all)


ATTN_ITEMS = 8


def _attn_kernel(q_ref, k_ref, v_ref, bias_ref, o_ref, lse_ref, *, L, wk, nc, nq, nblk):
    i = pl.program_id(2)
    lane = lax.broadcasted_iota(jnp.int32, (Q_BLK, A_GW), 1)
    qscale = jnp.asarray(A_DH ** -0.5, BF16)
    nt = (((1,), (1,)), ((), ()))
    items = [(c, t) for c in range(nc) for t in range(nq)]
    heads = [slice(h * A_DH, (h + 1) * A_DH) for h in range(A_HPG)]
    vws, scores = [], []
    for c, t in items:
        blk = i * nq + t
        start = pl.multiple_of(jnp.clip(blk * Q_BLK - HALF_WIN, 0, L - wk), HALF_WIN)
        var = jnp.where(blk == 0, 0, jnp.where(blk == nblk - 1, 2, 1))
        q = q_ref[0, c, t * Q_BLK:(t + 1) * Q_BLK, :] * qscale
        kw = k_ref[0, c, pl.ds(start, wk), :]
        vws.append(v_ref[0, c, pl.ds(start, wk), :])
        scores.append([lax.dot_general(q[:, sl], kw[:, sl], nt, preferred_element_type=F32) + bias_ref[var, h]
                       for h, sl in enumerate(heads)])
    maxes = [[jnp.max(s, axis=-1, keepdims=True) for s in item] for item in scores]
    probs = [[jnp.exp(s - m) for s, m in zip(si, mi)] for si, mi in zip(scores, maxes)]
    sums = [[jnp.sum(p, axis=-1, keepdims=True) for p in item] for item in probs]
    for (c, t), vw, pi, mi, li in zip(items, vws, probs, maxes, sums):
        outs = [jnp.dot(p.astype(BF16), vw[:, sl], preferred_element_type=F32) / l
                for p, l, sl in zip(pi, li, heads)]
        lse_tile = jnp.zeros((Q_BLK, A_GW), F32)
        for h, (m, l) in enumerate(zip(mi, li)):
            lse_tile = jnp.where((lane >= A_DH * h) & (lane < A_DH * (h + 1)), m + jnp.log(l), lse_tile)
        rows = slice(t * Q_BLK, (t + 1) * Q_BLK)
        cols = slice(c * A_GW, (c + 1) * A_GW)
        o_ref[0, rows, cols] = jnp.concatenate(outs, axis=1).astype(BF16)
        lse_ref[0, rows, cols] = lse_tile


def _attn_group(qkv, bias_tiles, d, B, S):
    L = S // d
    nblk = L // Q_BLK
    wk = min(2 * Q_BLK, L)
    nq = min(ATTN_ITEMS, nblk)
    nc = ATTN_ITEMS // nq
    assert L % Q_BLK == 0 and nblk % nq == 0 and d % nc == 0, (S, d)

    return pl.pallas_call(
        functools.partial(_attn_kernel, L=L, wk=wk, nc=nc, nq=nq, nblk=nblk),
        grid=(B, d // nc, nblk // nq),
        in_specs=[
            pl.BlockSpec((1, nc, nq * Q_BLK, A_GW), lambda b, r, i: (b, r, i, 0)),
            pl.BlockSpec((1, nc, L, A_GW), lambda b, r, i: (b, r, 0, 1)),
            pl.BlockSpec((1, nc, L, A_GW), lambda b, r, i: (b, r, 0, 2)),
            pl.BlockSpec(bias_tiles.shape, lambda b, r, i: (0, 0, 0, 0)),
        ],
        out_specs=[
            pl.BlockSpec((1, nq * Q_BLK, nc * A_GW), lambda b, r, i: (b, i, r)),
            pl.BlockSpec((1, nq * Q_BLK, nc * A_GW), lambda b, r, i: (b, i, r)),
        ],
        out_shape=[
            jax.ShapeDtypeStruct((B, L, d * A_GW), BF16),
            jax.ShapeDtypeStruct((B, L, d * A_GW), F32),
        ],
        compiler_params=_cparams(("parallel", "parallel", "arbitrary")),
        name=f"attn_d{d}",
    )(qkv, qkv, qkv, bias_tiles)


def _t5_buckets(rel):
    nb = N_REL_BUCKETS // 2
    max_exact = nb // 2
    ret = (rel > 0).astype(np.int64) * nb
    n = np.abs(rel)
    large = max_exact + (np.log(np.maximum(n, 1) / max_exact) / math.log(REL_MAX_DISTANCE / max_exact)
                         * (nb - max_exact)).astype(np.int64)
    large = np.minimum(large, nb - 1)
    return (ret + np.where(n < max_exact, n, large)).astype(np.int32)


def _bias_tiles(rel_bias, g, d, wk):
    rel = d * np.arange(-HALF_WIN, HALF_WIN + 1)
    bucket = _t5_buckets(rel)
    bias = rel_bias[jnp.asarray(bucket)][:, g * A_HPG:(g + 1) * A_HPG].T.astype(F32)
    pad = wk + Q_BLK
    neg = jnp.full((A_HPG, pad), NEG_INF, F32)
    val = jnp.concatenate([neg, bias, neg], axis=1)
    zero = pad + HALF_WIN
    m = wk + Q_BLK
    tiles = []
    for off in (0, HALF_WIN, wk - Q_BLK):
        u = jnp.concatenate([val[:, zero - off:zero - off + wk], val[:, zero - off - Q_BLK:zero - off]], axis=1)
        flat = jnp.tile(u, (1, Q_BLK))[:, :Q_BLK * (m - 1)]
        tiles.append(flat.reshape(A_HPG, Q_BLK, m - 1)[:, :, :wk])
    return jnp.stack(tiles)


def _log_sigmoid(z):
    return jnp.minimum(z, 0.0) - jnp.log(1.0 + jnp.exp(-jnp.abs(z)))


def _gla_kernel(q_ref, k_ref, v_ref, og_ref, g_ref, wgf_ref, wgb_ref, bgf_ref, bgb_ref, gn_ref, o_ref,
                qif, kof, qib, kob, vt, etf, etb, acc, *, S):
    PB = 256
    C = GLA_CHUNK
    NB = S // PB
    rr = lax.broadcasted_iota(jnp.int32, (PB, PB), 0)
    cc = lax.broadcasted_iota(jnp.int32, (PB, PB), 1)
    same = (rr >> 6) == (cc >> 6)
    mask_f = same & (cc <= rr)
    mask_b = same & (cc > rr)
    tl = jnp.where(mask_f, 1.0, 0.0).astype(BF16)
    qscale = GLA_DK ** -0.5
    inv_tau = 1.0 / GATE_TAU
    nt = (((1,), (1,)), ((), ()))

    def split3(x):
        hi = x.astype(BF16)
        r1 = x - hi.astype(F32)
        mid = r1.astype(BF16)
        lo = (r1 - mid.astype(F32)).astype(BF16)
        return [hi, mid, lo]

    def chunk_total(b):
        b4 = b.reshape(PB // C, C, GLA_DK)
        return jnp.broadcast_to(b4[:, C - 1:C, :], b4.shape).reshape(PB, GLA_DK)

    PREP_BLOCKS = 4

    def prep(t, carry):
        blocks = [t * PREP_BLOCKS + u for u in range(PREP_BLOCKS)]
        rows = [pl.ds(pl.multiple_of(i * PB, PB), PB) for i in blocks]
        gs = [g_ref[0, r, :] for r in rows]
        lfs = [_log_sigmoid(jnp.dot(g, wgf_ref[...], preferred_element_type=F32) + bgf_ref[...]) * inv_tau for g in gs]
        lbs = [_log_sigmoid(jnp.dot(g, wgb_ref[...], preferred_element_type=F32) + bgb_ref[...]) * inv_tau for g in gs]
        css = [jnp.dot(tl, jnp.concatenate(split3(lf) + split3(lb), axis=1), preferred_element_type=F32)
               for lf, lb in zip(lfs, lbs)]
        bfs = [cs[:, 0:128] + cs[:, 128:256] + cs[:, 256:384] for cs in css]
        pbs = [cs[:, 384:512] + cs[:, 512:640] + cs[:, 640:768] for cs in css]
        totfs = [chunk_total(bf) for bf in bfs]
        totbs = [chunk_total(pb) for pb in pbs]
        bss = [totb - pb + lb for totb, pb, lb in zip(totbs, pbs, lbs)]
        qs = [q_ref[0, r, :].astype(F32) * qscale for r in rows]
        ks = [k_ref[0, r, :].astype(F32) for r in rows]
        qfs = [(q * jnp.exp(bf)).astype(BF16) for q, bf in zip(qs, bfs)]
        qbs = [(q * jnp.exp(bs)).astype(BF16) for q, bs in zip(qs, bss)]
        kfs = [(k * jnp.exp(-bf)).astype(BF16) for k, bf in zip(ks, bfs)]
        kbs = [(k * jnp.exp(-bs)).astype(BF16) for k, bs in zip(ks, bss)]
        sfs = [lax.dot_general(qf, kf, nt, preferred_element_type=F32) for qf, kf in zip(qfs, kfs)]
        sbs = [lax.dot_general(qb, kb, nt, preferred_element_type=F32) for qb, kb in zip(qbs, kbs)]
        ps = [jnp.where(mask_f, sf, jnp.where(mask_b, sb, 0.0)).astype(BF16) for sf, sb in zip(sfs, sbs)]
        vs = [v_ref[0, r, :] for r in rows]
        for r, p, v in zip(rows, ps, vs):
            acc[r, :] = jnp.dot(p, v, preferred_element_type=F32)
        for r, qf, qb in zip(rows, qfs, qbs):
            qif[r, :] = qf
            qib[r, :] = qb
        for r, k, totf, bf, totb, bs in zip(rows, ks, totfs, bfs, totbs, bss):
            kof[r, :] = (k * jnp.exp(totf - bf)).astype(BF16)
            kob[r, :] = (k * jnp.exp(totb - bs)).astype(BF16)
            etf[r, :] = jnp.exp(totf)
            etb[r, :] = jnp.exp(totb)
        for i, v in zip(blocks, vs):
            vtb = v.astype(F32).T.astype(BF16)
            vt[2 * i] = vtb[:, :128]
            vt[2 * i + 1] = vtb[:, 128:]
        return carry

    lax.fori_loop(0, NB // PREP_BLOCKS, prep, 0)

    zeros_half = jnp.zeros((C, GLA_DK), BF16)

    SER_BLOCKS = 4
    NJ = PB // C

    def chunk_rows(blk, j):
        return pl.multiple_of(blk * PB + j * C, C)

    def state_update(ko_ref, blk, j):
        ko = ko_ref[pl.ds(chunk_rows(blk, j), C), :]
        ko_pad = jnp.concatenate([ko, zeros_half] if j % 2 == 0 else [zeros_half, ko], axis=0)
        return jnp.dot(vt[2 * blk + j // 2], ko_pad, preferred_element_type=F32)

    def advance(qi_ref, et_ref, blk, j, state, upd):
        r0 = chunk_rows(blk, j)
        rows = pl.ds(r0, C)
        acc[rows, :] += lax.dot_general(qi_ref[rows, :], state.astype(BF16), nt, preferred_element_type=F32)
        return state * et_ref[pl.ds(r0, 1), :] + upd

    def serial(t, carry):
        st_f, st_b = carry
        steps_f = [(t * SER_BLOCKS + u, j) for u in range(SER_BLOCKS) for j in range(NJ)]
        steps_b = [(NB - 1 - t * SER_BLOCKS - u, NJ - 1 - j) for u in range(SER_BLOCKS) for j in range(NJ)]
        upd_f = [state_update(kof, blk, j) for blk, j in steps_f]
        upd_b = [state_update(kob, blk, j) for blk, j in steps_b]
        for (bf, jf), (bb, jb), uf, ub in zip(steps_f, steps_b, upd_f, upd_b):
            st_f = advance(qif, etf, bf, jf, st_f, uf)
            st_b = advance(qib, etb, bb, jb, st_b, ub)
        return st_f, st_b

    zero_state = jnp.zeros((GLA_DV, GLA_DK), F32)
    lax.fori_loop(0, NB // SER_BLOCKS, serial, (zero_state, zero_state))

    gn = gn_ref[...]

    def finish(i, carry):
        rows = pl.ds(pl.multiple_of(i * PB, PB), PB)
        tot = acc[rows, :]
        nrm = tot * lax.rsqrt(jnp.mean(tot * tot, axis=-1, keepdims=True) + EPS) * gn
        og = og_ref[0, rows, :].astype(F32)
        o_ref[0, rows, :] = (nrm * (og * jax.nn.sigmoid(og))).astype(BF16)
        return carry

    lax.fori_loop(0, NB, finish, 0)


def _gla(proj, wgf, wgb, bgf, bgb, gla_norm, B, S):
    assert S % (8 * 4 * GLA_CHUNK) == 0, S
    pv = proj.reshape(B, S, PR)
    cq, ck, cv, cog, cg = C_QB // GLA_DK, C_KB // GLA_DK, C_VB // GLA_DV, C_OG // GLA_DV, C_G // 128
    return pl.pallas_call(
        functools.partial(_gla_kernel, S=S),
        grid=(B, GLA_H),
        in_specs=[
            pl.BlockSpec((1, S, GLA_DK), lambda b, h: (b, 0, cq + h)),
            pl.BlockSpec((1, S, GLA_DK), lambda b, h: (b, 0, ck + h)),
            pl.BlockSpec((1, S, GLA_DV), lambda b, h: (b, 0, cv + h)),
            pl.BlockSpec((1, S, GLA_DV), lambda b, h: (b, 0, cog + h)),
            pl.BlockSpec((1, S, 128), lambda b, h: (b, 0, cg)),
            pl.BlockSpec((128, GLA_DK), lambda b, h: (0, h)),
            pl.BlockSpec((128, GLA_DK), lambda b, h: (0, h)),
            pl.BlockSpec((1, GLA_DK), lambda b, h: (0, h)),
            pl.BlockSpec((1, GLA_DK), lambda b, h: (0, h)),
            pl.BlockSpec((1, GLA_DV), lambda b, h: (0, 0)),
        ],
        out_specs=pl.BlockSpec((1, S, GLA_DV), lambda b, h: (b, 0, h)),
        out_shape=jax.ShapeDtypeStruct((B, S, GLA_H * GLA_DV), BF16),
        scratch_shapes=[
            pltpu.VMEM((S, GLA_DK), BF16), pltpu.VMEM((S, GLA_DK), BF16),
            pltpu.VMEM((S, GLA_DK), BF16), pltpu.VMEM((S, GLA_DK), BF16),
            pltpu.VMEM((S // 128, GLA_DV, 128), BF16),
            pltpu.VMEM((S, GLA_DK), F32), pltpu.VMEM((S, GLA_DK), F32),
            pltpu.VMEM((S, GLA_DV), F32),
        ],
        compiler_params=_cparams(("parallel", "arbitrary")),
        name="gla",
    )(pv, pv, pv, pv, pv, wgf, wgb, bgf, bgb, gla_norm)


OUTPROJ_SPLIT = 2


def _outproj_kernel(x_ref, ma_ref, mb_ref, o0_ref, o1_ref, o2_ref, l0_ref, l1_ref, l2_ref, ob_ref,
                    wpa_ref, wpb_ref, wo_ref, nm_ref, wr_ref, br_ref,
                    x1_ref, hm_ref, gate_ref, gatet_ref):
    tm = x_ref.shape[0] // OUTPROJ_SPLIT
    parts = [slice(p * tm, (p + 1) * tm) for p in range(OUTPROJ_SPLIT)]

    def merge_groups(rs):
        l0, l1, l2 = l0_ref[rs, :], l1_ref[rs, :], l2_ref[rs, :]
        m = jnp.maximum(jnp.maximum(l0, l1), l2)
        e0, e1, e2 = jnp.exp(l0 - m), jnp.exp(l1 - m), jnp.exp(l2 - m)
        den = e0 + e1 + e2
        oa = ((e0 / den) * o0_ref[rs, :].astype(F32) + (e1 / den) * o1_ref[rs, :].astype(F32)
              + (e2 / den) * o2_ref[rs, :].astype(F32))
        return oa.astype(BF16)

    def project(rs, oa):
        ya = jnp.dot(oa, wpa_ref[...], preferred_element_type=F32)
        yb = jnp.dot(ob_ref[rs, :], wpb_ref[...], preferred_element_type=F32)
        return ya, yb

    def gate_mix(rs, y):
        ya, yb = y
        mix = jax.nn.sigmoid(ma_ref[rs, :].astype(F32)) * ya + jax.nn.sigmoid(mb_ref[rs, :].astype(F32)) * yb
        return mix.astype(BF16)

    def residual(rs, mix):
        x1 = x_ref[rs, :] + jnp.dot(mix, wo_ref[...], preferred_element_type=F32)
        x1_ref[rs, :] = x1
        return x1

    def moe_norm(rs, x1):
        hm = (x1 * lax.rsqrt(jnp.mean(x1 * x1, axis=-1, keepdims=True) + EPS) * nm_ref[...]).astype(BF16)
        hm_ref[rs, :] = hm
        return hm

    def router_logits(rs, hm):
        return jnp.dot(hm, wr_ref[...], preferred_element_type=F32) + br_ref[...]

    def router(rs, logits):
        gate = _route(logits)
        gate_ref[rs, :] = gate
        gatet_ref[:, rs] = gate.T

    stages = [lambda rs, _: merge_groups(rs), project, gate_mix, residual, moe_norm, router_logits, router]
    vals = [None] * OUTPROJ_SPLIT
    for step in range(len(stages) + OUTPROJ_SPLIT - 1):
        for p, rs in enumerate(parts):
            if 0 <= step - p < len(stages):
                vals[p] = stages[step - p](rs, vals[p])


def _route(logits):
    tm = logits.shape[0]
    lane = lax.broadcasted_iota(jnp.int32, (tm, 128), 1)
    is_g = (lane >= N_EXPERTS) & (lane < 2 * N_EXPERTS)
    is_e = lane < N_EXPERTS
    grp_of_lane = jnp.where(is_g, (lane - N_EXPERTS) >> 3, lane >> 3)
    gl = jnp.where(is_g, logits, NEG_INF)
    gmax = jnp.max(gl, axis=-1, keepdims=True)
    gsum = jnp.sum(jnp.where(is_g, jnp.exp(gl - gmax), 0.0), axis=-1, keepdims=True) * (1.0 / EPG)
    pg_top = 1.0 / gsum
    g_sel = jnp.min(jnp.where(is_g & (gl == gmax), grp_of_lane, N_GROUPS), axis=-1, keepdims=True)
    in_grp = is_e & (grp_of_lane == g_sel)
    el = jnp.where(in_grp, logits, NEG_INF)
    emax = jnp.max(el, axis=-1, keepdims=True)
    ee = jnp.where(in_grp, jnp.exp(el - emax), 0.0)
    pe = ee / jnp.sum(ee, axis=-1, keepdims=True)
    p1 = jnp.max(pe, axis=-1, keepdims=True)
    i1 = jnp.min(jnp.where(in_grp & (pe == p1), lane, 128), axis=-1, keepdims=True)
    rest = in_grp & (lane != i1)
    pe2 = jnp.where(rest, pe, -1.0)
    p2 = jnp.max(pe2, axis=-1, keepdims=True)
    i2 = jnp.min(jnp.where(rest & (pe2 == p2), lane, 128), axis=-1, keepdims=True)
    psum = p1 + p2
    return jnp.where(lane == i1, pg_top * (p1 / psum), jnp.where(lane == i2, pg_top * (p2 / psum), 0.0))


def _outproj(x2d, proj, o_list, lse_list, ob2d, wpa, wpb, wo, norm_moe, wr, br, tm=512):
    T = x2d.shape[0]
    row = lambda w: pl.BlockSpec((tm, w), lambda i: (i, 0))
    full = lambda a: pl.BlockSpec(a.shape, lambda i: (0,) * a.ndim)
    return pl.pallas_call(
        _outproj_kernel,
        grid=(T // tm,),
        in_specs=[
            row(D_MODEL),
            pl.BlockSpec((tm, D_MODEL), lambda i: (i, C_MA // D_MODEL)),
            pl.BlockSpec((tm, D_MODEL), lambda i: (i, C_MB // D_MODEL)),
            row(A_GW), row(A_GW), row(A_GW), row(A_GW), row(A_GW), row(A_GW),
            row(D_MODEL),
            full(wpa), full(wpb), full(wo), full(norm_moe), full(wr), full(br),
        ],
        out_specs=[row(D_MODEL), row(D_MODEL), row(128), pl.BlockSpec((128, tm), lambda i: (0, i))],
        out_shape=[
            jax.ShapeDtypeStruct((T, D_MODEL), F32),
            jax.ShapeDtypeStruct((T, D_MODEL), BF16),
            jax.ShapeDtypeStruct((T, 128), F32),
            jax.ShapeDtypeStruct((128, T), F32),
        ],
        compiler_params=_cparams(("parallel",)),
        name="outproj",
    )(x2d, proj, proj, *o_list, *lse_list, ob2d, wpa, wpb, wo, norm_moe, wr, br)


MOE_TB = 1024
MOE_Q = 4
MOE_SB = MOE_TB // MOE_Q
MOE_ALIGN = 16
MOE_CH = 128
MOE_W = MOE_CH // MOE_Q
MOE_RQ = 2 * MOE_SB + N_EXPERTS * MOE_ALIGN
MOE_RQS = MOE_RQ + MOE_W
MOE_SUB = 256
MOE_G = 4


def _moe_route(hm_ref, gate_ref, gatet_ref, xs, ys, pt, wrow, rinfo):
    TB, Q, SB, RQ, RQS, SUB = MOE_TB, MOE_Q, MOE_SB, MOE_RQ, MOE_RQS, MOE_SUB
    big = float(4 * TB)
    gate = gate_ref[...]
    gt = gatet_ref[...]
    a = gate > 0.0
    at = gt > 0.0
    a_f = jnp.where(a, 1.0, 0.0)
    at_f = jnp.where(at, 1.0, 0.0)
    a_b = a_f.astype(BF16)
    at_b = at_f.astype(BF16)

    def pad(c):
        return jnp.floor((c + (MOE_ALIGN - 1)) * (1.0 / MOE_ALIGN)) * MOE_ALIGN

    er = lax.broadcasted_iota(jnp.int32, (128, 128), 0)
    ec = lax.broadcasted_iota(jnp.int32, (128, 128), 1)
    sub8 = lax.broadcasted_iota(jnp.int32, (8, 128), 0)
    pad_rows = jnp.zeros((8, 128), F32)
    pad_cols = jnp.zeros((128, 128), F32)
    for q in range(Q):
        blk = slice(q * SB, (q + 1) * SB)
        pad_rows = jnp.where(sub8 == q, pad(jnp.sum(a_f[blk], axis=0, keepdims=True)), pad_rows)
        pad_cols = jnp.where(ec == q, pad(jnp.sum(at_f[:, blk], axis=1, keepdims=True)), pad_cols)
    off_rows = jnp.dot(pad_rows.astype(BF16), jnp.where(er < ec, 1.0, 0.0).astype(BF16), preferred_element_type=F32)
    off_cols = jnp.dot(jnp.where(ec < er, 1.0, 0.0).astype(BF16), pad_cols.astype(BF16), preferred_element_type=F32)

    tr = lax.broadcasted_iota(jnp.int32, (SUB, SUB), 0)
    tc = lax.broadcasted_iota(jnp.int32, (SUB, SUB), 1)
    tri_l = jnp.where(tc < tr, 1.0, 0.0).astype(BF16)
    tri_u = jnp.where(tr < tc, 1.0, 0.0).astype(BF16)
    rl = lax.broadcasted_iota(jnp.int32, (SUB, RQ), 1).astype(F32)
    assert SB == SUB
    subs = [slice(q * SB, (q + 1) * SB) for q in range(Q)]
    pos = [off_rows[q:q + 1] + jnp.dot(tri_l, a_b[b], preferred_element_type=F32) for q, b in enumerate(subs)]
    plo = [jnp.min(jnp.where(a[b], p, big), axis=1, keepdims=True) for b, p in zip(subs, pos)]
    phi = [jnp.max(jnp.where(a[b], p, -1.0), axis=1, keepdims=True) for b, p in zip(subs, pos)]
    phi = [jnp.where(hi == lo, -1.0, hi) for lo, hi in zip(plo, phi)]
    for b, lo, hi in zip(subs, plo, phi):
        pt[b, :] = jnp.where((rl == lo) | (rl == hi), 1.0, 0.0).astype(BF16)
    pos_t = [off_cols[:, q:q + 1] + jnp.dot(at_b[:, b], tri_u, preferred_element_type=F32)
             for q, b in enumerate(subs)]
    plo = [jnp.min(jnp.where(at[:, b], p, big), axis=0, keepdims=True) for b, p in zip(subs, pos_t)]
    phi = [jnp.max(jnp.where(at[:, b], p, -1.0), axis=0, keepdims=True) for b, p in zip(subs, pos_t)]
    phi = [jnp.where(hi == lo, -1.0, hi) for lo, hi in zip(plo, phi)]
    wlo = [jnp.sum(jnp.where(at[:, b] & (p == lo), gt[:, b], 0.0), axis=0, keepdims=True)
           for b, p, lo in zip(subs, pos_t, plo)]
    whi = [jnp.sum(jnp.where(at[:, b] & (p == hi), gt[:, b], 0.0), axis=0, keepdims=True)
           for b, p, hi in zip(subs, pos_t, phi)]
    for b, lo, hi, wl, wh in zip(subs, plo, phi, wlo, whi):
        rinfo[0:1, b] = lo
        rinfo[1:2, b] = hi
        rinfo[2:3, b] = wl
        rinfo[3:4, b] = wh

    blks = [slice(q * SB, (q + 1) * SB) for q in range(Q)]

    def gather(k, carry):
        r0 = pl.multiple_of(k * SUB, SUB)
        ri = (lax.broadcasted_iota(jnp.int32, (SUB, SB), 0) + r0).astype(F32)
        mlo = [ri == rinfo[0:1, b] for b in blks]
        mhi = [ri == rinfo[1:2, b] for b in blks]
        p = [jnp.where(lo | hi, 1.0, 0.0).astype(BF16) for lo, hi in zip(mlo, mhi)]
        x = [jnp.dot(pq, hm_ref[b, :], preferred_element_type=F32).astype(BF16) for pq, b in zip(p, blks)]
        w = [jnp.sum(jnp.where(lo, rinfo[2:3, b], 0.0) + jnp.where(hi, rinfo[3:4, b], 0.0), axis=1, keepdims=True)
             for lo, hi, b in zip(mlo, mhi, blks)]
        for q in range(Q):
            dst = pl.ds(pl.multiple_of(q * RQS + r0, MOE_ALIGN), SUB)
            xs[dst, :] = x[q]
            wrow[dst, :] = jnp.broadcast_to(w[q], (SUB, 128))
        return carry

    lax.fori_loop(0, RQ // SUB, gather, 0)
    for q in range(Q):
        xs[q * RQS + RQ:(q + 1) * RQS, :] = jnp.zeros((RQS - RQ, D_MODEL), BF16)
        wrow[q * RQS + RQ:(q + 1) * RQS, :] = jnp.zeros((RQS - RQ, 128), F32)
    ys[...] = jnp.zeros_like(ys)


def _moe_kernel(meta_ref, hm_ref, gate_ref, gatet_ref, x1_ref, wgu_ref, wd_ref, nf_ref, y_ref,
                xs, ys, pt, wrow, rinfo):
    i = pl.program_id(0)
    s = pl.program_id(1)
    Q, W = MOE_Q, MOE_W

    @pl.when(s == 0)
    def _():
        _moe_route(hm_ref, gate_ref, gatet_ref, xs, ys, pt, wrow, rinfo)

    offs = [[meta_ref[i, q * N_EXPERTS + s * MOE_G + g] + q * MOE_RQS for q in range(Q)] for g in range(MOE_G)]
    pads = [[meta_ref[i, (Q + q) * N_EXPERTS + s * MOE_G + g] for q in range(Q)] for g in range(MOE_G)]

    def up_proj(g, x):
        return jnp.dot(x, wgu_ref[g], preferred_element_type=F32)

    def activate(gu, w):
        gt, up = gu[:, :EXPERT_FF], gu[:, EXPERT_FF:]
        act = (gt * jax.nn.sigmoid(gt)) * up
        return jnp.concatenate([act[:, :128] * w, act[:, 128:] * w], axis=1).astype(BF16)

    def down_proj(g, act):
        return jnp.dot(act, wd_ref[g], preferred_element_type=F32).astype(BF16)

    def mlp(g, x, w):
        return down_proj(g, activate(up_proj(g, x), w))

    single = pads[0][0] <= W
    for g in range(MOE_G):
        for q in range(Q):
            single = jnp.logical_and(single, pads[g][q] <= W)

    @pl.when(single)
    def _():
        rows = [[pl.ds(pl.multiple_of(offs[g][q], MOE_ALIGN), W) for q in range(Q)] for g in range(MOE_G)]
        x = [jnp.concatenate([xs[r, :] for r in rg], axis=0) for rg in rows]
        w = [jnp.concatenate([wrow[r, :] for r in rg], axis=0) for rg in rows]
        gu = [up_proj(g, x[g]) for g in range(MOE_G)]
        act = [activate(gu[g], w[g]) for g in range(MOE_G)]
        y = [down_proj(g, act[g]) for g in range(MOE_G)]
        for g in range(MOE_G):
            for q, r in enumerate(rows[g]):
                ys[r, :] = y[g][q * W:(q + 1) * W]

    @pl.when(jnp.logical_not(single))
    def _():
        for g in range(MOE_G):
            for q in range(Q):
                def body(c, carry, g=g, q=q):
                    r = pl.ds(pl.multiple_of(offs[g][q] + c * W, MOE_ALIGN), W)
                    ys[r, :] = mlp(g, xs[r, :], wrow[r, :])
                    return carry
                lax.fori_loop(0, (pads[g][q] + (W - 1)) // W, body, 0)

    @pl.when(s == N_EXPERTS // MOE_G - 1)
    def _():
        blks = [slice(q * MOE_SB, (q + 1) * MOE_SB) for q in range(Q)]
        moe = [jnp.dot(pt[b, :], ys[q * MOE_RQS:q * MOE_RQS + MOE_RQ, :], preferred_element_type=F32)
               for q, b in enumerate(blks)]
        x2 = [x1_ref[b, :] + m for b, m in zip(blks, moe)]
        for b, x in zip(blks, x2):
            y_ref[b, :] = x * lax.rsqrt(jnp.mean(x * x, axis=-1, keepdims=True) + EPS) * nf_ref[...]


def _moe(hm, gate, gatet, x1, wgu, wd, norm_final):
    T = hm.shape[0]
    tb = MOE_TB
    assert T % tb == 0, T
    nb = T // tb
    used = gate.reshape(nb, MOE_Q, MOE_SB, 128)[..., :N_EXPERTS] > 0.0
    cnt = jnp.sum(used.astype(jnp.int32), axis=2)
    padded = (cnt + (MOE_ALIGN - 1)) // MOE_ALIGN * MOE_ALIGN
    off = jnp.cumsum(padded, axis=2) - padded
    meta = jnp.concatenate([off.reshape(nb, -1), padded.reshape(nb, -1)], axis=1).astype(jnp.int32)
    grid_spec = pltpu.PrefetchScalarGridSpec(
        num_scalar_prefetch=1,
        grid=(nb, N_EXPERTS // MOE_G),
        in_specs=[
            pl.BlockSpec((tb, D_MODEL), lambda i, e, m: (i, 0)),
            pl.BlockSpec((tb, 128), lambda i, e, m: (i, 0)),
            pl.BlockSpec((128, tb), lambda i, e, m: (0, i)),
            pl.BlockSpec((tb, D_MODEL), lambda i, e, m: (i, 0)),
            pl.BlockSpec((MOE_G, D_MODEL, 2 * EXPERT_FF), lambda i, e, m: (e, 0, 0)),
            pl.BlockSpec((MOE_G, EXPERT_FF, D_MODEL), lambda i, e, m: (e, 0, 0)),
            pl.BlockSpec((1, D_MODEL), lambda i, e, m: (0, 0)),
        ],
        out_specs=pl.BlockSpec((tb, D_MODEL), lambda i, e, m: (i, 0)),
        scratch_shapes=[
            pltpu.VMEM((MOE_Q * MOE_RQS, D_MODEL), BF16),
            pltpu.VMEM((MOE_Q * MOE_RQS, D_MODEL), BF16),
            pltpu.VMEM((tb, MOE_RQ), BF16),
            pltpu.VMEM((MOE_Q * MOE_RQS, 128), F32),
            pltpu.VMEM((8, tb), F32),
        ],
    )
    return pl.pallas_call(
        _moe_kernel,
        grid_spec=grid_spec,
        out_shape=jax.ShapeDtypeStruct((T, D_MODEL), F32),
        compiler_params=_cparams(("parallel", "arbitrary")),
        name="moe",
    )(meta, hm, gate, gatet, x1, wgu, wd, norm_final)


def _prep_weights(norm_mix, w_in, rel_bias, w_gate_f, b_gate_f, w_gate_b, b_gate_b, gla_norm, w_proj_a,
                  w_proj_b, w_out, norm_moe, w_rg, b_rg, w_re, b_re, w_eg, w_eu, w_ed, norm_final):
    splits = (A_QKV, A_QKV, A_QKV, 512, 512, 1024, 1024, GATE_RANK, GATE_RANK, D_MODEL, D_MODEL)
    qa, ka, va, qb, kb, vb, og, gf, gb, ma, mb = jnp.split(w_in, np.cumsum(splits)[:-1].tolist(), axis=1)
    gpad = jnp.zeros((D_MODEL, PR - C_G - 2 * GATE_RANK), w_in.dtype)
    grp = lambda t, g: t[:, g * A_GW:(g + 1) * A_GW]
    a_cols = [grp(t, g) for g in range(N_DIL) for t in (qa, ka, va)]
    w_all = jnp.concatenate(a_cols + [ma, mb, qb, kb, vb, og, gf, gb, gpad], axis=1).astype(BF16)
    kw = GLA_H * GLA_DK
    wgf = jnp.zeros((128, kw), F32).at[:GATE_RANK].set(w_gate_f).astype(BF16)
    wgb = jnp.zeros((128, kw), F32).at[GATE_RANK:2 * GATE_RANK].set(w_gate_b).astype(BF16)
    wr = jnp.zeros((D_MODEL, 128), F32)
    wr = wr.at[:, :N_EXPERTS].set(w_re).at[:, N_EXPERTS:2 * N_EXPERTS].set(jnp.repeat(w_rg, EPG, axis=1))
    br = jnp.zeros((1, 128), F32)
    br = br.at[0, :N_EXPERTS].set(b_re).at[0, N_EXPERTS:2 * N_EXPERTS].set(jnp.repeat(b_rg, EPG))
    return dict(
        norm_mix=norm_mix.reshape(1, D_MODEL), w_all=w_all, rel_bias=rel_bias,
        wgf=wgf, wgb=wgb, bgf=b_gate_f.reshape(1, kw), bgb=b_gate_b.reshape(1, kw),
        gla_norm=gla_norm.reshape(1, GLA_DV),
        wpa=w_proj_a.astype(BF16), wpb=w_proj_b.astype(BF16), wo=w_out.astype(BF16),
        norm_moe=norm_moe.reshape(1, D_MODEL), wr=wr.astype(BF16), br=br,
        wgu=jnp.concatenate([w_eg, w_eu], axis=-1).astype(BF16), wd=w_ed.astype(BF16),
        norm_final=norm_final.reshape(1, D_MODEL),
    )


def _trunk(x, w):
    B, S, _ = x.shape
    T = B * S
    x2d = x.reshape(T, D_MODEL)
    *qkv, proj = _inproj(x, w["norm_mix"], w["w_all"])
    o_list, lse_list = [], []
    for g, (_, d) in enumerate(DIL_PAIRS):
        L = S // d
        tiles = _bias_tiles(w["rel_bias"], g, d, min(2 * Q_BLK, L))
        o, lse = _attn_group(qkv[g], tiles, d, B, S)
        o_list.append(o.reshape(T, A_GW))
        lse_list.append(lse.reshape(T, A_GW))
    ob = _gla(proj, w["wgf"], w["wgb"], w["bgf"], w["bgb"], w["gla_norm"], B, S)
    x1, hm, gate, gatet = _outproj(x2d, proj, o_list, lse_list, ob.reshape(T, D_MODEL), w["wpa"], w["wpb"],
                                   w["wo"], w["norm_moe"], w["wr"], w["br"])
    y = _moe(hm, gate, gatet, x1, w["wgu"], w["wd"], w["norm_final"])
    return y.reshape(B, S, D_MODEL)


def kernel(x_prompt, x_sample, norm_mix, w_in, rel_bias, w_gate_f, b_gate_f, w_gate_b, b_gate_b, gla_norm,
           w_proj_a, w_proj_b, w_out, norm_moe, w_router_group, b_router_group, w_router_expert,
           b_router_expert, w_exp_gate, w_exp_up, w_exp_down, norm_final):
    w = _prep_weights(norm_mix[0], w_in[0], rel_bias, w_gate_f[0], b_gate_f[0], w_gate_b[0], b_gate_b[0],
                      gla_norm[0], w_proj_a[0], w_proj_b[0], w_out[0], norm_moe[0], w_router_group[0],
                      b_router_group[0], w_router_expert[0], b_router_expert[0], w_exp_gate[0], w_exp_up[0],
                      w_exp_down[0], norm_final)
    return (_trunk(x_prompt, w), _trunk(x_sample, w))
```
